```python
import math
import jax, jax.numpy as jnp
from jax import lax
import numpy as np

D_MODEL = 2048
BATCH = 4
SEQ = 2048
DEPTH = 1
DEC_BATCH = 128
DEC_SEQ = 8
PAST_LEN = 16384
PAGE_SIZE = 128

HG_HEAD_V = 128
HG_EXPAND = 128
HG_VAL_DIM = D_MODEL // 2
HG_HEADS = HG_VAL_DIM // HG_HEAD_V
HG_KEY_DIM = HG_HEADS * HG_EXPAND
RET_HEAD_V = 256
RET_HEAD_QK = 128
RET_V_DIM = D_MODEL // 2
RET_HEADS = RET_V_DIM // RET_HEAD_V
RET_QK_DIM = RET_HEADS * RET_HEAD_QK
IN_WIDTH = 2 * HG_KEY_DIM + 2 * HG_VAL_DIM + 2 * RET_QK_DIM + 2 * RET_V_DIM + 2 * D_MODEL
CHUNK = 64
N_GROUPS = 4
EXPERTS_PER_GROUP = 8
N_EXPERTS = N_GROUPS * EXPERTS_PER_GROUP
TOP_K = 2
EXPERT_FF = D_MODEL // 4
ROPE_BASE = 10000.0
EPS = 1e-6

kernel_name = "hgrn2_retention_gated_hier_moe_step"


def _rmsnorm(x, gain=None):
    x32 = x.astype(jnp.float32)
    y = x32 * lax.rsqrt(jnp.mean(x32 * x32, axis=-1, keepdims=True) + EPS)
    if gain is not None:
        y = y * gain.astype(jnp.float32)
    return y


def _to_chunks(a, c):
    B, L, H, d = a.shape
    return a.reshape(B, L // c, c, H, d).transpose(1, 0, 3, 2, 4)


def _from_chunks(a):
    n, B, H, c, d = a.shape
    return a.transpose(1, 0, 3, 2, 4).reshape(B, n * c, H, d)


def _hgrn2_scan(q, k, v, logf, s0):
    c = math.gcd(q.shape[1], CHUNK)
    causal = jnp.tril(jnp.ones((c, c), dtype=bool))[None, None, :, :, None]

    def step(S, inp):
        qc, kc, vc, gc = inp
        b = jnp.cumsum(gc, axis=2)
        diff = b[:, :, :, None, :] - b[:, :, None, :, :]
        decay = jnp.exp(jnp.where(causal, diff, -jnp.inf))
        att = jnp.einsum('bhtk,bhsk,bhtsk->bhts', qc, kc, decay)
        o = (jnp.einsum('bhts,bhsv->bhtv', att, vc)
             + jnp.einsum('bhtk,bhkv->bhtv', qc * jnp.exp(b), S))
        b_end = b[:, :, -1:, :]
        S = (jnp.exp(b_end[:, :, 0, :])[..., None] * S
             + jnp.einsum('bhsk,bhsv->bhkv', kc * jnp.exp(b_end - b), vc))
        return S, o

    S, o = lax.scan(step, s0, (_to_chunks(q, c), _to_chunks(k, c), _to_chunks(v, c), _to_chunks(logf, c)))
    return _from_chunks(o), S


def _retention_scan(q, k, v, s0):
    c = math.gcd(q.shape[1], CHUNK)
    log_gamma = jnp.log(1.0 - jnp.exp2(-5.0 - jnp.arange(RET_HEADS, dtype=jnp.float32)))
    idx = jnp.arange(c, dtype=jnp.float32)
    rel = idx[:, None] - idx[None, :]
    tri = jnp.tril(jnp.ones((c, c), dtype=bool))
    intra = jnp.exp(jnp.where(tri[None], log_gamma[:, None, None] * rel[None], -jnp.inf))
    q_w = jnp.exp(log_gamma[:, None] * (idx[None, :] + 1.0))[..., None]
    k_w = jnp.exp(log_gamma[:, None] * (c - 1.0 - idx[None, :]))[..., None]
    s_decay = jnp.exp(log_gamma * c)[:, None, None]

    def step(S, inp):
        qc, kc, vc = inp
        att = jnp.einsum('bhtd,bhsd->bhts', qc, kc) * intra
        o = jnp.einsum('bhts,bhsv->bhtv', att, vc) + jnp.einsum('bhtd,bhdv->bhtv', qc * q_w, S)
        S = s_decay * S + jnp.einsum('bhsd,bhsv->bhdv', kc * k_w, vc)
        return S, o

    S, o = lax.scan(step, s0, (_to_chunks(q, c), _to_chunks(k, c), _to_chunks(v, c)))
    return _from_chunks(o), S


def _rotary(x, pos):
    half = x.shape[-1] // 2
    inv = 1.0 / (ROPE_BASE ** jnp.linspace(0.0, 1.0, half, dtype=jnp.float32))
    ang = jnp.repeat(pos[:, None] * inv[None, :], 2, axis=-1)[None, :, None, :]
    rot = jnp.stack((-x[..., 1::2], x[..., 0::2]), axis=-1).reshape(x.shape)
    return x * jnp.cos(ang) + rot * jnp.sin(ang)


def _hier_moe(x, w_rg, b_rg, w_re, b_re, w_ge, w_ue, w_de):
    B, L, D = x.shape
    t = x.reshape(B * L, D)
    g_logits = (t @ w_rg).astype(jnp.float32) + b_rg.astype(jnp.float32)
    g_prob = jax.nn.softmax(g_logits, axis=-1)
    g_w, g_i = lax.top_k(g_prob, 1)
    e_logits = ((t @ w_re).astype(jnp.float32) + b_re.astype(jnp.float32)).reshape(-1, N_GROUPS, EXPERTS_PER_GROUP)
    e_sel = jnp.take_along_axis(e_logits, g_i[:, :, None], axis=1)[:, 0]
    e_v, e_i = lax.top_k(e_sel, TOP_K)
    e_w = jax.nn.softmax(e_v, axis=-1) * g_w
    e_global = g_i * EXPERTS_PER_GROUP + e_i
    combine = jnp.sum(jax.nn.one_hot(e_global, N_EXPERTS, dtype=jnp.float32) * e_w[..., None], axis=1)
    h = jax.nn.silu(jnp.einsum('td,edf->tef', t, w_ge)) * jnp.einsum('td,edf->tef', t, w_ue)
    y = jnp.einsum('tef,efd->td', h * combine[..., None].astype(h.dtype), w_de)
    return y.reshape(B, L, D)


def _layer(x, s_hg, s_ret, pos, lb, norm_mix, w_in, hg_norm, w_branch_hg, w_branch_ret, b_gate, w_out,
           norm_ffn, w_rg, b_rg, w_re, b_re, w_ge, w_ue, w_de):
    f32 = jnp.float32
    dt = x.dtype
    B, L, _ = x.shape
    xn = _rmsnorm(x, norm_mix).astype(dt)
    proj = xn @ w_in
    widths = (HG_KEY_DIM, HG_KEY_DIM, HG_VAL_DIM, HG_VAL_DIM,
              RET_QK_DIM, RET_QK_DIM, RET_V_DIM, RET_V_DIM, D_MODEL, D_MODEL)
    cuts = []
    acc = 0
    for w in widths[:-1]:
        acc += w
        cuts.append(acc)
    hq, hf, hi, hg, rq, rk, rv, rg, ga, gb = jnp.split(proj, cuts, axis=-1)

    def heads(a, h):
        return a.astype(f32).reshape(B, L, h, -1)

    f = lb + (1.0 - lb) * jax.nn.sigmoid(hf.astype(f32))
    q_hg = jax.nn.silu(hq.astype(f32)) * (HG_EXPAND ** -0.5)
    o_hg, s_hg_new = _hgrn2_scan(heads(q_hg, HG_HEADS), heads(1.0 - f, HG_HEADS),
                                 heads(hi, HG_HEADS), heads(jnp.log(f), HG_HEADS), s_hg)
    o_hg = _rmsnorm(o_hg.reshape(B, L, HG_VAL_DIM), hg_norm) * jax.nn.silu(hg.astype(f32))

    q_r = _rotary(heads(rq, RET_HEADS), pos)
    k_r = _rotary(heads(rk, RET_HEADS), pos) * (RET_HEAD_QK ** -0.5)
    o_r, s_ret_new = _retention_scan(q_r, k_r, heads(rv, RET_HEADS), s_ret)
    o_r = _rmsnorm(o_r).reshape(B, L, RET_V_DIM) * jax.nn.silu(rg.astype(f32))

    p_hg = o_hg.astype(dt) @ w_branch_hg
    p_r = o_r.astype(dt) @ w_branch_ret
    gate_hg = jax.nn.sigmoid(ga + b_gate[:D_MODEL])
    gate_r = jax.nn.sigmoid(gb + b_gate[D_MODEL:])
    h = x + (gate_hg * p_hg + gate_r * p_r) @ w_out

    h = h + _hier_moe(_rmsnorm(h, norm_ffn).astype(dt), w_rg, b_rg, w_re, b_re, w_ge, w_ue, w_de)
    return h, s_hg_new, s_ret_new


def setup_inputs(seed: int = 0) -> dict:
    key = jax.random.key(seed)
    ks = jax.random.split(key, 21)

    def nrm(k, shape, scale):
        return jax.random.normal(k, shape, jnp.float32) * scale

    return {
        "x_prompt": nrm(ks[0], (BATCH, SEQ, D_MODEL), 1.0),
        "x_sample": nrm(ks[1], (DEC_BATCH, DEC_SEQ, D_MODEL), 1.0),
        "state_hgrn": nrm(ks[2], (DEPTH, DEC_BATCH, HG_HEADS, HG_EXPAND, HG_HEAD_V), 0.5),
        "state_ret": nrm(ks[3], (DEPTH, DEC_BATCH, RET_HEADS, RET_HEAD_QK, RET_HEAD_V), 0.5),
        "norm_mix": 1.0 + nrm(ks[4], (DEPTH, D_MODEL), 0.02),
        "w_in": nrm(ks[5], (DEPTH, D_MODEL, IN_WIDTH), D_MODEL ** -0.5),
        "hg_lb_logits": nrm(ks[6], (DEPTH + 1, HG_KEY_DIM), 0.5),
        "hg_norm": 1.0 + nrm(ks[7], (DEPTH, HG_VAL_DIM), 0.02),
        "w_branch_hg": nrm(ks[8], (DEPTH, HG_VAL_DIM, D_MODEL), HG_VAL_DIM ** -0.5),
        "w_branch_ret": nrm(ks[9], (DEPTH, RET_V_DIM, D_MODEL), RET_V_DIM ** -0.5),
        "b_gate": nrm(ks[10], (DEPTH, 2 * D_MODEL), 0.02),
        "w_out": nrm(ks[11], (DEPTH, D_MODEL, D_MODEL), D_MODEL ** -0.5),
        "norm_ffn": 1.0 + nrm(ks[12], (DEPTH, D_MODEL), 0.02),
        "w_router_group": nrm(ks[13], (DEPTH, D_MODEL, N_GROUPS), D_MODEL ** -0.5),
        "b_router_group": nrm(ks[14], (DEPTH, N_GROUPS), 0.01),
        "w_router_expert": nrm(ks[15], (DEPTH, D_MODEL, N_EXPERTS), D_MODEL ** -0.5),
        "b_router_expert": nrm(ks[16], (DEPTH, N_EXPERTS), 0.01),
        "w_expert_gate": nrm(ks[17], (DEPTH, N_EXPERTS, D_MODEL, EXPERT_FF), D_MODEL ** -0.5),
        "w_expert_up": nrm(ks[18], (DEPTH, N_EXPERTS, D_MODEL, EXPERT_FF), D_MODEL ** -0.5),
        "w_expert_down": nrm(ks[19], (DEPTH, N_EXPERTS, EXPERT_FF, D_MODEL), EXPERT_FF ** -0.5),
        "norm_final": 1.0 + nrm(ks[20], (D_MODEL,), 0.02),
    }


def reference(x_prompt, x_sample, state_hgrn, state_ret, norm_mix, w_in, hg_lb_logits, hg_norm,
              w_branch_hg, w_branch_ret, b_gate, w_out, norm_ffn, w_router_group, b_router_group,
              w_router_expert, b_router_expert, w_expert_gate, w_expert_up, w_expert_down, norm_final):
    f32 = jnp.float32
    Bp, Lp, _ = x_prompt.shape
    Ls = x_sample.shape[1]
    pos_p = jnp.arange(Lp, dtype=f32)
    pos_s = jnp.arange(Ls, dtype=f32) + jnp.float32(PAST_LEN)
    lower = jnp.cumsum(jax.nn.softmax(hg_lb_logits.astype(f32), axis=0), axis=0)

    hp, hs = x_prompt, x_sample
    hg_p, ret_p, hg_s, ret_s = [], [], [], []
    for l in range(DEPTH):
        lw = (norm_mix[l], w_in[l], hg_norm[l], w_branch_hg[l], w_branch_ret[l], b_gate[l], w_out[l],
              norm_ffn[l], w_router_group[l], b_router_group[l], w_router_expert[l], b_router_expert[l],
              w_expert_gate[l], w_expert_up[l], w_expert_down[l])
        s_hg0 = jnp.zeros((Bp, HG_HEADS, HG_EXPAND, HG_HEAD_V), f32)
        s_ret0 = jnp.zeros((Bp, RET_HEADS, RET_HEAD_QK, RET_HEAD_V), f32)
        hp, shp, srp = _layer(hp, s_hg0, s_ret0, pos_p, lower[l], *lw)
        hs, shs, srs = _layer(hs, state_hgrn[l].astype(f32), state_ret[l].astype(f32), pos_s, lower[l], *lw)
        hg_p.append(shp.astype(x_prompt.dtype))
        ret_p.append(srp.astype(x_prompt.dtype))
        hg_s.append(shs.astype(state_hgrn.dtype))
        ret_s.append(srs.astype(state_ret.dtype))

    y_prompt = _rmsnorm(hp, norm_final).astype(x_prompt.dtype)
    y_sample = _rmsnorm(hs, norm_final).astype(x_sample.dtype)
    new_state_hgrn_prompt = jnp.stack(hg_p, axis=0)
    new_state_ret_prompt = jnp.stack(ret_p, axis=0)
    new_state_hgrn_sample = jnp.stack(hg_s, axis=0)
    new_state_ret_sample = jnp.stack(ret_s, axis=0)
    return (y_prompt, y_sample, new_state_hgrn_prompt, new_state_ret_prompt, new_state_hgrn_sample, new_state_ret_sample)
```

```python
import functools
import math

import numpy as np
import jax
import jax.numpy as jnp
from jax import lax
from jax.experimental import pallas as pl
from jax.experimental.pallas import tpu as pltpu

F32 = jnp.float32
BF16 = jnp.bfloat16

D_MODEL = 2048
PAST_LEN = 16384
HG_HEAD_V = 128
HG_EXPAND = 128
HG_VAL_DIM = D_MODEL // 2
HG_HEADS = HG_VAL_DIM // HG_HEAD_V
HG_KEY_DIM = HG_HEADS * HG_EXPAND
RET_HEAD_V = 256
RET_HEAD_QK = 128
RET_V_DIM = D_MODEL // 2
RET_HEADS = RET_V_DIM // RET_HEAD_V
RET_QK_DIM = RET_HEADS * RET_HEAD_QK
IN_WIDTH = 2 * HG_KEY_DIM + 2 * HG_VAL_DIM + 2 * RET_QK_DIM + 2 * RET_V_DIM + 2 * D_MODEL
CHUNK = 64
N_GROUPS = 4
EXPERTS_PER_GROUP = 8
N_EXPERTS = N_GROUPS * EXPERTS_PER_GROUP
EXPERT_FF = D_MODEL // 4
ROPE_BASE = 10000.0
EPS = 1e-6

LANES = 128
VMEM_LIMIT = 56 * 1024 * 1024
PROJ_TN = 512
SEG_HQ, SEG_HF, SEG_HI, SEG_HG, SEG_RQ, SEG_RK, SEG_RV, SEG_RG, SEG_GA = 0, 2, 4, 6, 8, 9, 10, 12, 14
SAFE_EXP_SPAN = 80.0
ROUTE_GROUP_LANE = 0
ROUTE_EXPERT_LANE = N_GROUPS


def _params(sem):
    return pltpu.CompilerParams(dimension_semantics=sem, vmem_limit_bytes=VMEM_LIMIT)


def _silu(z):
    return z * jax.nn.sigmoid(z)


def _dot(a, b):
    return jnp.dot(a, b, preferred_element_type=F32)


def _dot_nt(a, b):
    return lax.dot_general(a, b, (((1,), (1,)), ((), ())), preferred_element_type=F32)


def _dot_tn(a, b):
    return lax.dot_general(a, b, (((0,), (0,)), ((), ())), preferred_element_type=F32)


def _in_proj_kernel(x_ref, gain_ref, w_ref, colp_ref, cos_ref, sa_ref, sb_ref, o_ref, xn_ref):
    j = pl.program_id(1)

    @pl.when(j == 0)
    def _():
        x = x_ref[...]
        ms = jnp.mean(x * x, axis=-1, keepdims=True)
        xn_ref[...] = (x * lax.rsqrt(ms + EPS) * gain_ref[...]).astype(BF16)

    p = _dot(xn_ref[...], w_ref[...].astype(BF16))

    @pl.when(j < SEG_HF)
    def _():
        o_ref[...] = _silu(p) * (HG_EXPAND ** -0.5)

    @pl.when((j >= SEG_HF) & (j < SEG_HI))
    def _():
        lb = colp_ref[...]
        o_ref[...] = lb + (1.0 - lb) * jax.nn.sigmoid(p)

    @pl.when(((j >= SEG_HI) & (j < SEG_HG)) | ((j >= SEG_RV) & (j < SEG_RG)))
    def _():
        o_ref[...] = p

    @pl.when(((j >= SEG_HG) & (j < SEG_RQ)) | ((j >= SEG_RG) & (j < SEG_GA)))
    def _():
        o_ref[...] = _silu(p)

    @pl.when((j >= SEG_RQ) & (j < SEG_RV))
    def _():
        scale = jnp.where(j == SEG_RK, RET_HEAD_QK ** -0.5, 1.0).astype(F32)
        cos, sa, sb = cos_ref[...], sa_ref[...], sb_ref[...]
        for hh in range(PROJ_TN // LANES):
            xs = p[:, hh * LANES:(hh + 1) * LANES]
            r = xs * cos + pltpu.roll(xs, LANES - 1, 1) * sa + pltpu.roll(xs, 1, 1) * sb
            o_ref[:, hh * LANES:(hh + 1) * LANES] = r * scale

    @pl.when(j >= SEG_GA)
    def _():
        o_ref[...] = jax.nn.sigmoid(p + colp_ref[...])


def _in_proj(x, gain, w_in, colp, cos, sa, sb, tm):
    T = x.shape[0]
    nb = cos.shape[0] // tm
    tab = pl.BlockSpec((tm, LANES), lambda i, j: (i % nb, 0))
    return pl.pallas_call(
        _in_proj_kernel,
        grid=(T // tm, IN_WIDTH // PROJ_TN),
        in_specs=[
            pl.BlockSpec((tm, D_MODEL), lambda i, j: (i, 0)),
            pl.BlockSpec((1, D_MODEL), lambda i, j: (0, 0)),
            pl.BlockSpec((D_MODEL, PROJ_TN), lambda i, j: (0, j)),
            pl.BlockSpec((1, PROJ_TN), lambda i, j: (0, j)),
            tab, tab, tab,
        ],
        out_specs=pl.BlockSpec((tm, PROJ_TN), lambda i, j: (i, j)),
        out_shape=jax.ShapeDtypeStruct((T, IN_WIDTH), F32),
        scratch_shapes=[pltpu.VMEM((tm, D_MODEL), BF16)],
        compiler_params=_params(("parallel", "arbitrary")),
        name="in_proj",
    )(x, gain, w_in, colp, cos, sa, sb)


def _cumsum_rows(g, C):
    row = lax.broadcasted_iota(jnp.int32, g.shape, 0)
    b = g
    sh = 1
    while sh < C:
        b = b + jnp.where(row >= sh, pltpu.roll(b, sh, 0), 0.0)
        sh *= 2
    return b


def _hgrn_chunk(q, f, v, S, tmp_ref, C):
    g = jnp.log(f)
    k = 1.0 - f
    b = _cumsum_rows(g, C)
    b_end = b[C - 1:C, :]
    b_mid = b[C // 2 - 1:C // 2, :]
    qb = (q * jnp.exp(b)).astype(BF16)
    kd = (k * jnp.exp(b_end - b)).astype(BF16)
    o_inter = _dot(qb, S.astype(BF16))
    span = jnp.max(jnp.maximum(-b_mid, b_mid - b_end))

    def fast():
        qm = (q * jnp.exp(b - b_mid)).astype(BF16)
        km = (k * jnp.exp(b_mid - b)).astype(BF16)
        att = _dot_nt(qm, km)
        row = lax.broadcasted_iota(jnp.int32, (C, C), 0)
        col = lax.broadcasted_iota(jnp.int32, (C, C), 1)
        att = jnp.where(row >= col, att, 0.0)
        return _dot(att.astype(BF16), v.astype(BF16))

    def slow():
        tmp_ref[0] = b
        tmp_ref[1] = k
        tmp_ref[2] = v
        trow = lax.broadcasted_iota(jnp.int32, (C, LANES), 0)

        def body(s, acc):
            bs = tmp_ref[0, pl.ds(s, 1), :]
            ks = tmp_ref[1, pl.ds(s, 1), :]
            vs = tmp_ref[2, pl.ds(s, 1), :]
            w = jnp.where(trow >= s, jnp.exp(jnp.minimum(b - bs, 0.0)), 0.0)
            a = jnp.sum(q * ks * w, axis=1, keepdims=True)
            return acc + a * vs

        return lax.fori_loop(0, C, body, jnp.zeros((C, LANES), F32))

    o_intra = lax.cond(span <= SAFE_EXP_SPAN, fast, slow)
    r = lax.broadcasted_iota(jnp.int32, (LANES, LANES), 0)
    c = lax.broadcasted_iota(jnp.int32, (LANES, LANES), 1)
    d_col = jnp.sum(jnp.where(r == c, jnp.broadcast_to(jnp.exp(b_end), (LANES, LANES)), 0.0),
                    axis=1, keepdims=True)
    S_new = d_col * S + _dot_tn(kd, v.astype(BF16))
    return o_inter + o_intra, S_new


def _hgrn_prompt_kernel(q_ref, f_ref, v_ref, o_ref, st_ref, s_scr, tmp_ref):
    L = q_ref.shape[0]
    C = math.gcd(L, CHUNK)
    s_scr[...] = jnp.zeros_like(s_scr)

    def body(ci, carry):
        r0 = pl.multiple_of(ci * C, C)
        o, S_new = _hgrn_chunk(q_ref[pl.ds(r0, C), :], f_ref[pl.ds(r0, C), :],
                               v_ref[pl.ds(r0, C), :], s_scr[...], tmp_ref, C)
        o_ref[pl.ds(r0, C), :] = o
        s_scr[...] = S_new
        return carry

    lax.fori_loop(0, L // C, body, 0)
    st_ref[...] = s_scr[...]


def _hgrn_prompt(P, B, L):
    C = math.gcd(L, CHUNK)
    H = HG_HEADS

    def col(seg):
        return pl.BlockSpec((L, LANES), lambda b, h, seg=seg: (b, seg * H + h))

    return pl.pallas_call(
        _hgrn_prompt_kernel,
        grid=(B, H),
        in_specs=[col(0), col(1), col(2)],
        out_specs=[
            pl.BlockSpec((L, LANES), lambda b, h: (b, h)),
            pl.BlockSpec((None, None, HG_EXPAND, HG_HEAD_V), lambda b, h: (b, h, 0, 0)),
        ],
        out_shape=[
            jax.ShapeDtypeStruct((B * L, HG_VAL_DIM), F32),
            jax.ShapeDtypeStruct((B, H, HG_EXPAND, HG_HEAD_V), F32),
        ],
        scratch_shapes=[pltpu.VMEM((HG_EXPAND, HG_HEAD_V), F32), pltpu.VMEM((3, C, LANES), F32)],
        compiler_params=_params(("parallel", "parallel")),
        name="hgrn_prompt",
    )(P, P, P)


def _hgrn_sample_kernel(q_ref, f_ref, v_ref, s_ref, o_ref, so_ref, tmp_ref, *, C):
    Bb = s_ref.shape[0]

    def body(bb, carry):
        r0 = pl.multiple_of(bb * C, C)
        for h in range(HG_HEADS):
            sl = slice(h * LANES, (h + 1) * LANES)
            o, S_new = _hgrn_chunk(q_ref[pl.ds(r0, C), sl], f_ref[pl.ds(r0, C), sl],
                                   v_ref[pl.ds(r0, C), sl], s_ref[bb, h], tmp_ref, C)
            o_ref[pl.ds(r0, C), sl] = o
            so_ref[bb, h] = S_new
        return carry

    lax.fori_loop(0, Bb, body, 0)


def _hgrn_sample(P, state, Bs, Ls, Bb):
    H = HG_HEADS
    W = HG_KEY_DIM

    def col(seg):
        return pl.BlockSpec((Bb * Ls, W), lambda i, seg=seg: (i, seg))

    st = pl.BlockSpec((Bb, H, HG_EXPAND, HG_HEAD_V), lambda i: (i, 0, 0, 0))
    return pl.pallas_call(
        functools.partial(_hgrn_sample_kernel, C=Ls),
        grid=(Bs // Bb,),
        in_specs=[col(0), col(1), col(2), st],
        out_specs=[pl.BlockSpec((Bb * Ls, W), lambda i: (i, 0)), st],
        out_shape=[
            jax.ShapeDtypeStruct((Bs * Ls, HG_VAL_DIM), F32),
            jax.ShapeDtypeStruct((Bs, H, HG_EXPAND, HG_HEAD_V), F32),
        ],
        scratch_shapes=[pltpu.VMEM((3, Ls, LANES), F32)],
        compiler_params=_params(("parallel",)),
        name="hgrn_sample",
    )(P, P, P, state)


def _ret_chunk(q, k, v, gate, S, dmat, qw, kw, sdec):
    att = _dot_nt(q.astype(BF16), k.astype(BF16)) * dmat
    o = _dot(att.astype(BF16), v.astype(BF16)) + _dot((q * qw).astype(BF16), S.astype(BF16))
    S_new = sdec * S + _dot_tn((k * kw).astype(BF16), v.astype(BF16))
    on = o * lax.rsqrt(jnp.mean(o * o, axis=-1, keepdims=True) + EPS)
    return on * gate, S_new


def _ret_prompt_kernel(q_ref, k_ref, v_ref, g_ref, dm_ref, qw_ref, kw_ref, sd_ref,
                       o_ref, st_ref, s_scr):
    L = q_ref.shape[0]
    C = dm_ref.shape[0]
    s_scr[...] = jnp.zeros_like(s_scr)
    dmat, qw, kw, sdec = dm_ref[...], qw_ref[...], kw_ref[...], sd_ref[0:1, 0:1]

    def body(ci, carry):
        r0 = pl.multiple_of(ci * C, C)
        rows = pl.ds(r0, C)
        o, S_new = _ret_chunk(q_ref[rows, :], k_ref[rows, :], v_ref[rows, :], g_ref[rows, :],
                              s_scr[...], dmat, qw, kw, sdec)
        o_ref[rows, :] = o.astype(o_ref.dtype)
        s_scr[...] = S_new
        return carry

    lax.fori_loop(0, L // C, body, 0)
    st_ref[...] = s_scr[...]


def _ret_tables(C):
    log_gamma = jnp.log(1.0 - jnp.exp2(-5.0 - jnp.arange(RET_HEADS, dtype=F32)))
    idx = jnp.arange(C, dtype=F32)
    rel = idx[:, None] - idx[None, :]
    tri = jnp.tril(jnp.ones((C, C), dtype=bool))
    dmat = jnp.exp(jnp.where(tri[None], log_gamma[:, None, None] * rel[None], -jnp.inf))
    qw = jnp.exp(log_gamma[:, None] * (idx[None, :] + 1.0))[..., None]
    kw = jnp.exp(log_gamma[:, None] * (C - 1.0 - idx[None, :]))[..., None]
    sdec = jnp.exp(log_gamma * C)[:, None, None]
    bc = lambda a: jnp.broadcast_to(a, (RET_HEADS, a.shape[1], LANES))
    return dmat, bc(qw), bc(kw), jnp.broadcast_to(sdec, (RET_HEADS, 8, LANES))


def _ret_prompt(P, B, L):
    C = math.gcd(L, CHUNK)
    H = RET_HEADS
    dmat, qw, kw, sdec = _ret_tables(C)
    qk0 = (2 * HG_KEY_DIM + 2 * HG_VAL_DIM) // RET_HEAD_QK
    v0 = (2 * HG_KEY_DIM + 2 * HG_VAL_DIM + 2 * RET_QK_DIM) // RET_HEAD_V

    def hd(shape):
        return pl.BlockSpec((None,) + shape, lambda b, h: (h, 0, 0))

    return pl.pallas_call(
        _ret_prompt_kernel,
        grid=(B, H),
        in_specs=[
            pl.BlockSpec((L, RET_HEAD_QK), lambda b, h: (b, qk0 + h)),
            pl.BlockSpec((L, RET_HEAD_QK), lambda b, h: (b, qk0 + H + h)),
            pl.BlockSpec((L, RET_HEAD_V), lambda b, h: (b, v0 + h)),
            pl.BlockSpec((L, RET_HEAD_V), lambda b, h: (b, v0 + H + h)),
            hd((C, C)), hd((C, LANES)), hd((C, LANES)), hd((8, LANES)),
        ],
        out_specs=[
            pl.BlockSpec((L, RET_HEAD_V), lambda b, h: (b, h)),
            pl.BlockSpec((None, None, RET_HEAD_QK, RET_HEAD_V), lambda b, h: (b, h, 0, 0)),
        ],
        out_shape=[
            jax.ShapeDtypeStruct((B * L, RET_V_DIM), BF16),
            jax.ShapeDtypeStruct((B, H, RET_HEAD_QK, RET_HEAD_V), F32),
        ],
        scratch_shapes=[pltpu.VMEM((RET_HEAD_QK, RET_HEAD_V), F32)],
        compiler_params=_params(("parallel", "parallel")),
        name="ret_prompt",
    )(P, P, P, P, dmat, qw, kw, sdec)


def _ret_sample_kernel(q_ref, k_ref, v_ref, g_ref, s_ref, dm_ref, qw_ref, kw_ref, sd_ref,
                       o_ref, so_ref, o_scr, *, C):
    Bb = s_ref.shape[0]

    def body(bb, carry):
        rows = pl.ds(pl.multiple_of(bb * C, C), C)
        for h in range(RET_HEADS):
            qs = slice(h * RET_HEAD_QK, (h + 1) * RET_HEAD_QK)
            vs = slice(h * RET_HEAD_V, (h + 1) * RET_HEAD_V)
            o, S_new = _ret_chunk(q_ref[rows, qs], k_ref[rows, qs], v_ref[rows, vs], g_ref[rows, vs],
                                  s_ref[bb, h], dm_ref[h], qw_ref[h], kw_ref[h], sd_ref[h, 0:1, 0:1])
            o_scr[rows, vs] = o
            so_ref[bb, h] = S_new
        return carry

    lax.fori_loop(0, Bb, body, 0)
    o_ref[...] = o_scr[...].astype(o_ref.dtype)


def _ret_sample(P, state, Bs, Ls, Bb):
    H = RET_HEADS
    dmat, qw, kw, sdec = _ret_tables(Ls)
    qk0 = (2 * HG_KEY_DIM + 2 * HG_VAL_DIM) // RET_QK_DIM
    v0 = (2 * HG_KEY_DIM + 2 * HG_VAL_DIM + 2 * RET_QK_DIM) // RET_V_DIM
    rows = Bb * Ls
    st = pl.BlockSpec((Bb, H, RET_HEAD_QK, RET_HEAD_V), lambda i: (i, 0, 0, 0))

    def full(a):
        return pl.BlockSpec(a.shape, lambda i: (0, 0, 0))

    return pl.pallas_call(
        functools.partial(_ret_sample_kernel, C=Ls),
        grid=(Bs // Bb,),
        in_specs=[
            pl.BlockSpec((rows, RET_QK_DIM), lambda i: (i, qk0)),
            pl.BlockSpec((rows, RET_QK_DIM), lambda i: (i, qk0 + 1)),
            pl.BlockSpec((rows, RET_V_DIM), lambda i: (i, v0)),
            pl.BlockSpec((rows, RET_V_DIM), lambda i: (i, v0 + 1)),
            st, full(dmat), full(qw), full(kw), full(sdec),
        ],
        out_specs=[pl.BlockSpec((rows, RET_V_DIM), lambda i: (i, 0)), st],
        out_shape=[
            jax.ShapeDtypeStruct((Bs * Ls, RET_V_DIM), BF16),
            jax.ShapeDtypeStruct((Bs, H, RET_HEAD_QK, RET_HEAD_V), F32),
        ],
        scratch_shapes=[pltpu.VMEM((rows, RET_V_DIM), F32)],
        compiler_params=_params(("parallel",)),
        name="ret_sample",
    )(P, P, P, P, state, dmat, qw, kw, sdec)


def _split_dot(a, w_hi, w_lo):
    a_hi = a.astype(BF16)
    a_lo = (a - a_hi.astype(F32)).astype(BF16)
    return _dot(a_hi, w_hi) + (_dot(a_hi, w_lo) + _dot(a_lo, w_hi))


def _route(logits):
    lane = lax.broadcasted_iota(jnp.int32, logits.shape, 1)
    neg = jnp.float32(-jnp.inf)
    big = jnp.int32(LANES)
    gmask = lane < N_GROUPS
    gl = jnp.where(gmask, logits, neg)
    gmax = jnp.max(gl, axis=1, keepdims=True)
    gu = jnp.where(gmask, jnp.exp(gl - gmax), 0.0)
    gp = gu / jnp.sum(gu, axis=1, keepdims=True)
    g_w = jnp.max(gp, axis=1, keepdims=True)
    g_i = jnp.min(jnp.where(gmask & (gp == g_w), lane, big), axis=1, keepdims=True)
    lo = ROUTE_EXPERT_LANE + g_i * EXPERTS_PER_GROUP
    emask = (lane >= lo) & (lane < lo + EXPERTS_PER_GROUP)
    el = jnp.where(emask, logits, neg)
    e1 = jnp.max(el, axis=1, keepdims=True)
    i1 = jnp.min(jnp.where(emask & (el == e1), lane, big), axis=1, keepdims=True)
    el2 = jnp.where(lane == i1, neg, el)
    e2 = jnp.max(el2, axis=1, keepdims=True)
    i2 = jnp.min(jnp.where(emask & (lane != i1) & (el2 == e2), lane, big), axis=1, keepdims=True)
    u2 = jnp.exp(e2 - e1)
    den = 1.0 + u2
    w1 = (1.0 / den) * g_w
    w2 = (u2 / den) * g_w
    comb = jnp.where(lane == i1, w1, 0.0) + jnp.where(lane == i2, w2, 0.0)
    return pltpu.roll(comb, LANES - ROUTE_EXPERT_LANE, 1)


def _mix_kernel(ohg_ref, hgate_ref, or_ref, ga0_ref, ga1_ref, gb0_ref, gb1_ref, x_ref,
                wbh_ref, wbr_ref, wout_ref, hgn_ref, nffn_ref, wr_hi_ref, wr_lo_ref, br_ref,
                h_ref, hn_ref, route_ref):
    o = ohg_ref[...]
    on = o * lax.rsqrt(jnp.mean(o * o, axis=-1, keepdims=True) + EPS) * hgn_ref[...]
    p_hg = _dot((on * hgate_ref[...]).astype(BF16), wbh_ref[...])
    p_r = _dot(or_ref[...], wbr_ref[...])
    half = D_MODEL // 2
    m0 = ga0_ref[...] * p_hg[:, :half] + gb0_ref[...] * p_r[:, :half]
    m1 = ga1_ref[...] * p_hg[:, half:] + gb1_ref[...] * p_r[:, half:]
    mixed = jnp.concatenate([m0, m1], axis=1).astype(BF16)
    h = x_ref[...] + _dot(mixed, wout_ref[...])
    h_ref[...] = h
    hn = h * lax.rsqrt(jnp.mean(h * h, axis=-1, keepdims=True) + EPS) * nffn_ref[...]
    hn_ref[...] = hn.astype(BF16)
    route_ref[...] = _route(_split_dot(hn, wr_hi_ref[...], wr_lo_ref[...]) + br_ref[...])


def _mix(o_hg, P, o_r, x, wbh, wbr, wout, hg_norm, norm_ffn, wr_hi, wr_lo, br, tm):
    T = x.shape[0]
    half = D_MODEL // 2
    ga = (2 * HG_KEY_DIM + 2 * HG_VAL_DIM + 2 * RET_QK_DIM + 2 * RET_V_DIM) // half

    def pcol(c):
        return pl.BlockSpec((tm, half), lambda i, c=c: (i, c))

    def const(a):
        return pl.BlockSpec(a.shape, lambda i: (0, 0), pipeline_mode=pl.Buffered(1))

    return pl.pallas_call(
        _mix_kernel,
        grid=(T // tm,),
        in_specs=[
            pl.BlockSpec((tm, half), lambda i: (i, 0)),
            pcol((2 * HG_KEY_DIM + HG_VAL_DIM) // half),
            pl.BlockSpec((tm, half), lambda i: (i, 0)),
            pcol(ga), pcol(ga + 1), pcol(ga + 2), pcol(ga + 3),
            pl.BlockSpec((tm, D_MODEL), lambda i: (i, 0)),
            const(wbh), const(wbr), const(wout), const(hg_norm), const(norm_ffn),
            const(wr_hi), const(wr_lo), const(br),
        ],
        out_specs=[
            pl.BlockSpec((tm, D_MODEL), lambda i: (i, 0)),
            pl.BlockSpec((tm, D_MODEL), lambda i: (i, 0)),
            pl.BlockSpec((tm, LANES), lambda i: (i, 0)),
        ],
        out_shape=[
            jax.ShapeDtypeStruct((T, D_MODEL), F32),
            jax.ShapeDtypeStruct((T, D_MODEL), BF16),
            jax.ShapeDtypeStruct((T, LANES), F32),
        ],
        compiler_params=_params(("parallel",)),
        name="mix",
    )(o_hg, P, o_r, P, P, P, P, x, wbh, wbr, wout, hg_norm, norm_ffn, wr_hi, wr_lo, br)


def _moe_kernel(x_ref, route_ref, wg_ref, wu_ref, wd_ref, o_ref):
    e = pl.program_id(1)
    x = x_ref[...]
    g = _dot(x, wg_ref[...].astype(BF16))
    u = _dot(x, wu_ref[...].astype(BF16))
    route = route_ref[...]
    lane = lax.broadcasted_iota(jnp.int32, route.shape, 1)
    c = jnp.sum(jnp.where(lane == e, route, 0.0), axis=1, keepdims=True)
    hmid = (_silu(g) * u * c).astype(BF16)
    y = _dot(hmid, wd_ref[...].astype(BF16))

    @pl.when(e == 0)
    def _():
        o_ref[...] = y

    @pl.when(e > 0)
    def _():
        o_ref[...] += y


def _moe(hn, route, w_ge, w_ue, w_de, tm):
    T = hn.shape[0]
    return pl.pallas_call(
        _moe_kernel,
        grid=(T // tm, N_EXPERTS),
        in_specs=[
            pl.BlockSpec((tm, D_MODEL), lambda i, e: (i, 0)),
            pl.BlockSpec((tm, LANES), lambda i, e: (i, 0)),
            pl.BlockSpec((None, D_MODEL, EXPERT_FF), lambda i, e: (e, 0, 0)),
            pl.BlockSpec((None, D_MODEL, EXPERT_FF), lambda i, e: (e, 0, 0)),
            pl.BlockSpec((None, EXPERT_FF, D_MODEL), lambda i, e: (e, 0, 0)),
        ],
        out_specs=pl.BlockSpec((tm, D_MODEL), lambda i, e: (i, 0)),
        out_shape=jax.ShapeDtypeStruct((T, D_MODEL), F32),
        compiler_params=_params(("parallel", "arbitrary")),
        name="moe",
    )(hn, route, w_ge, w_ue, w_de)


def _final_kernel(h_ref, m_ref, gain_ref, y_ref):
    h = h_ref[...] + m_ref[...]
    y_ref[...] = h * lax.rsqrt(jnp.mean(h * h, axis=-1, keepdims=True) + EPS) * gain_ref[...]


def _final(h, m, gain, tm):
    T = h.shape[0]
    blk = pl.BlockSpec((tm, D_MODEL), lambda i: (i, 0))
    return pl.pallas_call(
        _final_kernel,
        grid=(T // tm,),
        in_specs=[blk, blk, pl.BlockSpec((1, D_MODEL), lambda i: (0, 0))],
        out_specs=blk,
        out_shape=jax.ShapeDtypeStruct((T, D_MODEL), F32),
        compiler_params=_params(("parallel",)),
        name="final",
    )(h, m, gain)


def _rope_tables(pos, reps):
    half = RET_HEAD_QK // 2
    inv = 1.0 / (ROPE_BASE ** jnp.linspace(0.0, 1.0, half, dtype=F32))
    ang = jnp.repeat(pos[:, None] * inv[None, :], 2, axis=-1)
    cos, sin = jnp.cos(ang), jnp.sin(ang)
    even = (jnp.arange(RET_HEAD_QK) % 2 == 0)[None, :]
    sa = jnp.where(even, -sin, 0.0)
    sb = jnp.where(even, 0.0, sin)
    return tuple(jnp.tile(t, (reps, 1)) for t in (cos, sa, sb))


def _tile(T, pref):
    return pref if T % pref == 0 else T


def kernel(x_prompt, x_sample, state_hgrn, state_ret, norm_mix, w_in, hg_lb_logits, hg_norm,
           w_branch_hg, w_branch_ret, b_gate, w_out, norm_ffn, w_router_group, b_router_group,
           w_router_expert, b_router_expert, w_expert_gate, w_expert_up, w_expert_down, norm_final):
    assert w_in.shape == (1, D_MODEL, IN_WIDTH), "single-layer trunk only"
    Bp, Lp, _ = x_prompt.shape
    Bs, Ls, _ = x_sample.shape

    lower = jnp.cumsum(jax.nn.softmax(hg_lb_logits.astype(F32), axis=0), axis=0)[0]
    colp = jnp.zeros((IN_WIDTH,), F32)
    colp = colp.at[HG_KEY_DIM:2 * HG_KEY_DIM].set(lower)
    colp = colp.at[IN_WIDTH - 2 * D_MODEL:].set(b_gate[0])[None, :]

    wbh = w_branch_hg[0].astype(BF16)
    wbr = w_branch_ret[0].astype(BF16)
    wout = w_out[0].astype(BF16)
    pad = LANES - N_GROUPS - N_EXPERTS
    wr = jnp.concatenate([w_router_group[0], w_router_expert[0], jnp.zeros((D_MODEL, pad), F32)], axis=1)
    wr_hi = wr.astype(BF16)
    wr_lo = (wr - wr_hi.astype(F32)).astype(BF16)
    br = jnp.concatenate([b_router_group[0], b_router_expert[0], jnp.zeros((pad,), F32)])[None, :]

    def trunk(x3, pos, scan_hg, scan_ret):
        B, L, _ = x3.shape
        T = B * L
        x = x3.reshape(T, D_MODEL)
        tm = _tile(T, 1024)
        cos, sa, sb = _rope_tables(pos, max(tm // L, 1))
        P = _in_proj(x, norm_mix, w_in[0], colp, cos, sa, sb, tm)
        o_hg, s_hg = scan_hg(P)
        o_r, s_r = scan_ret(P)
        h, hn, route = _mix(o_hg, P, o_r, x, wbh, wbr, wout, hg_norm, norm_ffn, wr_hi, wr_lo, br,
                            _tile(T, 256))
        m = _moe(hn, route, w_expert_gate[0], w_expert_up[0], w_expert_down[0], _tile(T, 512))
        y = _final(h, m, norm_final[None, :], _tile(T, 512))
        return y.reshape(B, L, D_MODEL), s_hg, s_r

    bb = 8 if Bs % 8 == 0 else Bs
    y_p, shp, srp = trunk(
        x_prompt, jnp.arange(Lp, dtype=F32),
        lambda P: _hgrn_prompt(P, Bp, Lp), lambda P: _ret_prompt(P, Bp, Lp))
    y_s, shs, srs = trunk(
        x_sample, jnp.arange(Ls, dtype=F32) + jnp.float32(PAST_LEN),
        lambda P: _hgrn_sample(P, state_hgrn[0].astype(F32), Bs, Ls, bb),
        lambda P: _ret_sample(P, state_ret[0].astype(F32), Bs, Ls, bb))
    return (y_p, y_s, shp[None], srp[None], shs[None], srs[None])
```

```python
import functools
import math

import jax
import jax.numpy as jnp
from jax import lax
from jax.experimental import pallas as pl
from jax.experimental.pallas import tpu as pltpu

F32 = jnp.float32
BF16 = jnp.bfloat16
I32 = jnp.int32

D_MODEL = 2048
PAST_LEN = 16384
HG_HEAD_V = 128
HG_EXPAND = 128
HG_VAL_DIM = D_MODEL // 2
HG_HEADS = HG_VAL_DIM // HG_HEAD_V
HG_KEY_DIM = HG_HEADS * HG_EXPAND
RET_HEAD_V = 256
RET_HEAD_QK = 128
RET_V_DIM = D_MODEL // 2
RET_HEADS = RET_V_DIM // RET_HEAD_V
RET_QK_DIM = RET_HEADS * RET_HEAD_QK
IN_WIDTH = 2 * HG_KEY_DIM + 2 * HG_VAL_DIM + 2 * RET_QK_DIM + 2 * RET_V_DIM + 2 * D_MODEL
CHUNK = 64
N_GROUPS = 4
EXPERTS_PER_GROUP = 8
N_EXPERTS = N_GROUPS * EXPERTS_PER_GROUP
TOP_K = 2
EXPERT_FF = D_MODEL // 4
ROPE_BASE = 10000.0
EPS = 1e-6

LANES = 128
SUBLANES = 8
VMEM_LIMIT = 56 * 1024 * 1024
PROJ_TN = 512
SEG_HQ, SEG_HF, SEG_HI, SEG_HG, SEG_RQ, SEG_RK, SEG_RV, SEG_RG, SEG_GA = 0, 2, 4, 6, 8, 9, 10, 12, 14
SAFE_EXP_SPAN = 80.0
ROW_GROUP = D_MODEL // LANES
MOE_TILE = 256
LANE_E1, LANE_E2, LANE_W1, LANE_W2 = 0, 1, 2, 3
ROUTE_EXPERT_LANE = N_GROUPS


def _params(sem):
    return pltpu.CompilerParams(dimension_semantics=sem, vmem_limit_bytes=VMEM_LIMIT)


def _silu(z):
    return z * jax.nn.sigmoid(z)


def _dot(a, b):
    return jnp.dot(a, b, preferred_element_type=F32)


def _dot_nt(a, b):
    return lax.dot_general(a, b, (((1,), (1,)), ((), ())), preferred_element_type=F32)


def _dot_tn(a, b):
    return lax.dot_general(a, b, (((0,), (0,)), ((), ())), preferred_element_type=F32)


def _rms(x):
    return x * lax.rsqrt(jnp.mean(x * x, axis=-1, keepdims=True) + EPS)


def _in_proj_kernel(xp_ref, xs_ref, gain_ref, w_ref, colp_ref, cos_ref, sa_ref, sb_ref, o_ref, xn_ref,
                    *, n_prompt_tiles):
    i = pl.program_id(0)
    j = pl.program_id(1)

    def norm(x_ref):
        xn_ref[...] = (_rms(x_ref[...]) * gain_ref[...]).astype(BF16)

    pl.when((j == 0) & (i < n_prompt_tiles))(lambda: norm(xp_ref))
    pl.when((j == 0) & (i >= n_prompt_tiles))(lambda: norm(xs_ref))

    p = _dot(xn_ref[...], w_ref[...].astype(BF16))

    @pl.when(j < SEG_HF)
    def _():
        o_ref[...] = _silu(p) * (HG_EXPAND ** -0.5)

    @pl.when((j >= SEG_HF) & (j < SEG_HI))
    def _():
        lb = colp_ref[...]
        o_ref[...] = lb + (1.0 - lb) * jax.nn.sigmoid(p)

    @pl.when(((j >= SEG_HI) & (j < SEG_HG)) | ((j >= SEG_RV) & (j < SEG_RG)))
    def _():
        o_ref[...] = p

    @pl.when(((j >= SEG_HG) & (j < SEG_RQ)) | ((j >= SEG_RG) & (j < SEG_GA)))
    def _():
        o_ref[...] = _silu(p)

    @pl.when((j >= SEG_RQ) & (j < SEG_RV))
    def _():
        scale = jnp.where(j == SEG_RK, RET_HEAD_QK ** -0.5, 1.0).astype(F32)
        cos, sa, sb = cos_ref[...], sa_ref[...], sb_ref[...]
        for hh in range(PROJ_TN // LANES):
            xs = p[:, hh * LANES:(hh + 1) * LANES]
            r = xs * cos + pltpu.roll(xs, LANES - 1, 1) * sa + pltpu.roll(xs, 1, 1) * sb
            o_ref[:, hh * LANES:(hh + 1) * LANES] = r * scale

    @pl.when(j >= SEG_GA)
    def _():
        o_ref[...] = jax.nn.sigmoid(p + colp_ref[...])


def _in_proj(xp, xs, gain, w_in, colp, tabs, tm, nbp, nbs):
    npt, nst = xp.shape[0] // tm, xs.shape[0] // tm
    tab = pl.BlockSpec((tm, LANES), lambda i, j: (jnp.where(i < npt, i % nbp, nbp + (i - npt) % nbs), 0))
    return pl.pallas_call(
        functools.partial(_in_proj_kernel, n_prompt_tiles=npt),
        grid=(npt + nst, IN_WIDTH // PROJ_TN),
        in_specs=[
            pl.BlockSpec((tm, D_MODEL), lambda i, j: (jnp.minimum(i, npt - 1), 0)),
            pl.BlockSpec((tm, D_MODEL), lambda i, j: (jnp.maximum(i - npt, 0), 0)),
            pl.BlockSpec((1, D_MODEL), lambda i, j: (0, 0)),
            pl.BlockSpec((D_MODEL, PROJ_TN), lambda i, j: (0, j)),
            pl.BlockSpec((1, PROJ_TN), lambda i, j: (0, j)),
            tab, tab, tab,
        ],
        out_specs=pl.BlockSpec((tm, PROJ_TN), lambda i, j: (i, j)),
        out_shape=jax.ShapeDtypeStruct(((npt + nst) * tm, IN_WIDTH), F32),
        scratch_shapes=[pltpu.VMEM((tm, D_MODEL), BF16)],
        compiler_params=_params(("parallel", "arbitrary")),
        name="in_proj",
    )(xp, xs, gain, w_in, colp, *tabs)


def _cumsum_rows(g, C):
    row = lax.broadcasted_iota(I32, g.shape, 0)
    b = g
    sh = 1
    while sh < C:
        b = b + jnp.where(row >= sh, pltpu.roll(b, sh, 0), 0.0)
        sh *= 2
    return b


def _hgrn_chunk(q, f, v, states, tmp_ref, C):
    H = len(states)
    hs = [slice(h * LANES, (h + 1) * LANES) for h in range(H)]
    g = jnp.log(f)
    k = 1.0 - f
    b = _cumsum_rows(g, C)
    b_end = b[C - 1:C, :]
    b_mid = b[C // 2 - 1:C // 2, :]
    qb = (q * jnp.exp(b)).astype(BF16)
    kd = (k * jnp.exp(b_end - b)).astype(BF16)
    vb = v.astype(BF16)
    o_inter = [_dot(qb[:, hs[h]], states[h].astype(BF16)) for h in range(H)]
    span = jnp.max(jnp.maximum(-b_mid, b_mid - b_end))

    def fast():
        qm = (q * jnp.exp(b - b_mid)).astype(BF16)
        km = (k * jnp.exp(b_mid - b)).astype(BF16)
        row = lax.broadcasted_iota(I32, (C, C), 0)
        col = lax.broadcasted_iota(I32, (C, C), 1)
        outs = []
        for h in range(H):
            att = jnp.where(row >= col, _dot_nt(qm[:, hs[h]], km[:, hs[h]]), 0.0)
            outs.append(_dot(att.astype(BF16), vb[:, hs[h]]))
        return tuple(outs)

    def slow():
        trow = lax.broadcasted_iota(I32, (C, LANES), 0)
        outs = []
        for h in range(H):
            bh, qh = b[:, hs[h]], q[:, hs[h]]
            tmp_ref[0] = bh
            tmp_ref[1] = k[:, hs[h]]
            tmp_ref[2] = v[:, hs[h]]

            def body(s, acc, bh=bh, qh=qh):
                bs = tmp_ref[0, pl.ds(s, 1), :]
                ks = tmp_ref[1, pl.ds(s, 1), :]
                vs = tmp_ref[2, pl.ds(s, 1), :]
                w = jnp.where(trow >= s, jnp.exp(jnp.minimum(bh - bs, 0.0)), 0.0)
                a = jnp.sum(qh * ks * w, axis=1, keepdims=True)
                return acc + a * vs

            outs.append(lax.fori_loop(0, C, body, jnp.zeros((C, LANES), F32)))
        return tuple(outs)

    o_intra = lax.cond(span <= SAFE_EXP_SPAN, fast, slow)
    r = lax.broadcasted_iota(I32, (LANES, LANES), 0)
    c = lax.broadcasted_iota(I32, (LANES, LANES), 1)
    d = jnp.exp(b_end)
    new_states = []
    for h in range(H):
        d_col = jnp.sum(jnp.where(r == c, jnp.broadcast_to(d[:, hs[h]], (LANES, LANES)), 0.0),
                        axis=1, keepdims=True)
        new_states.append(d_col * states[h] + _dot_tn(kd[:, hs[h]], vb[:, hs[h]]))
    o = jnp.concatenate([o_inter[h] + o_intra[h] for h in range(H)], axis=1)
    return o, new_states


def _hgrn_prompt_kernel(q_ref, f_ref, v_ref, gate_ref, gain_ref, o_ref, st_ref, s_scr, o_scr, tmp_ref, *, C):
    l = pl.program_id(1)
    Lb = q_ref.shape[0]

    @pl.when(l == 0)
    def _():
        s_scr[...] = jnp.zeros_like(s_scr)

    def body(ci, carry):
        rows = pl.ds(pl.multiple_of(ci * C, C), C)
        o, new = _hgrn_chunk(q_ref[rows, :], f_ref[rows, :], v_ref[rows, :],
                             [s_scr[h] for h in range(HG_HEADS)], tmp_ref, C)
        o_scr[rows, :] = o
        for h in range(HG_HEADS):
            s_scr[h] = new[h]
        return carry

    lax.fori_loop(0, Lb // C, body, 0)
    o_ref[...] = (_rms(o_scr[...]) * gain_ref[...] * gate_ref[...]).astype(o_ref.dtype)

    @pl.when(l == pl.num_programs(1) - 1)
    def _():
        st_ref[...] = s_scr[...]


def _hgrn_prompt(P, gain, B, L, Lb):
    C = math.gcd(L, CHUNK)
    nl = L // Lb
    W = HG_KEY_DIM

    def col(seg):
        return pl.BlockSpec((Lb, W), lambda b, l, seg=seg: (b * nl + l, seg))

    return pl.pallas_call(
        functools.partial(_hgrn_prompt_kernel, C=C),
        grid=(B, nl),
        in_specs=[col(0), col(1), col(2), col(3), pl.BlockSpec((1, W), lambda b, l: (0, 0))],
        out_specs=[
            pl.BlockSpec((Lb, W), lambda b, l: (b * nl + l, 0)),
            pl.BlockSpec((None, HG_HEADS, HG_EXPAND, HG_HEAD_V), lambda b, l: (b, 0, 0, 0)),
        ],
        out_shape=[
            jax.ShapeDtypeStruct((B * L, HG_VAL_DIM), BF16),
            jax.ShapeDtypeStruct((B, HG_HEADS, HG_EXPAND, HG_HEAD_V), F32),
        ],
        scratch_shapes=[pltpu.VMEM((HG_HEADS, HG_EXPAND, HG_HEAD_V), F32), pltpu.VMEM((Lb, W), F32),
                        pltpu.VMEM((3, C, LANES), F32)],
        compiler_params=_params(("parallel", "arbitrary")),
        name="hgrn_prompt",
    )(P, P, P, P, gain)


def _hgrn_sample_kernel(q_ref, f_ref, v_ref, gate_ref, gain_ref, s_ref, o_ref, so_ref,
                        o_scr, tmp_ref, *, C):
    Bb = s_ref.shape[0]

    def body(bb, carry):
        rows = pl.ds(pl.multiple_of(bb * C, C), C)
        o, new = _hgrn_chunk(q_ref[rows, :], f_ref[rows, :], v_ref[rows, :],
                             [s_ref[bb, h] for h in range(HG_HEADS)], tmp_ref, C)
        o_scr[rows, :] = o
        for h in range(HG_HEADS):
            so_ref[bb, h] = new[h]
        return carry

    lax.fori_loop(0, Bb, body, 0)
    o_ref[...] = (_rms(o_scr[...]) * gain_ref[...] * gate_ref[...]).astype(o_ref.dtype)


def _hgrn_sample(P, gain, state, Bs, Ls, Bb, row0):
    W = HG_KEY_DIM
    rows = Bb * Ls
    blk0 = row0 // rows

    def col(seg):
        return pl.BlockSpec((rows, W), lambda i, seg=seg: (blk0 + i, seg))

    st = pl.BlockSpec((Bb, HG_HEADS, HG_EXPAND, HG_HEAD_V), lambda i: (i, 0, 0, 0))
    return pl.pallas_call(
        functools.partial(_hgrn_sample_kernel, C=Ls),
        grid=(Bs // Bb,),
        in_specs=[col(0), col(1), col(2), col(3), pl.BlockSpec((1, W), lambda i: (0, 0)), st],
        out_specs=[pl.BlockSpec((rows, W), lambda i: (i, 0)), st],
        out_shape=[
            jax.ShapeDtypeStruct((Bs * Ls, HG_VAL_DIM), BF16),
            jax.ShapeDtypeStruct((Bs, HG_HEADS, HG_EXPAND, HG_HEAD_V), F32),
        ],
        scratch_shapes=[pltpu.VMEM((rows, W), F32), pltpu.VMEM((3, Ls, LANES), F32)],
        compiler_params=_params(("parallel",)),
        name="hgrn_sample",
    )(P, P, P, P, gain, state)


def _ret_chunk(q, k, v, states, dm_ref, qw_ref, kw_ref, sd_ref):
    outs, new_states = [], []
    qb, kb, vb = q.astype(BF16), k.astype(BF16), v.astype(BF16)
    for h in range(RET_HEADS):
        qs = slice(h * RET_HEAD_QK, (h + 1) * RET_HEAD_QK)
        vs = slice(h * RET_HEAD_V, (h + 1) * RET_HEAD_V)
        att = _dot_nt(qb[:, qs], kb[:, qs]) * dm_ref[h]
        o = _dot(att.astype(BF16), vb[:, vs]) + _dot((q[:, qs] * qw_ref[h]).astype(BF16), states[h].astype(BF16))
        new_states.append(sd_ref[h, 0:1, 0:1] * states[h] + _dot_tn((k[:, qs] * kw_ref[h]).astype(BF16), vb[:, vs]))
        outs.append(o)
    return jnp.concatenate(outs, axis=1), new_states


def _ret_epilogue(o_scr, gate_ref, o_ref):
    for h in range(RET_HEADS):
        vs = slice(h * RET_HEAD_V, (h + 1) * RET_HEAD_V)
        o_ref[:, vs] = (_rms(o_scr[:, vs]) * gate_ref[:, vs]).astype(o_ref.dtype)


def _ret_prompt_kernel(q_ref, k_ref, v_ref, gate_ref, dm_ref, qw_ref, kw_ref, sd_ref,
                       o_ref, st_ref, s_scr, o_scr):
    l = pl.program_id(1)
    Lb = q_ref.shape[0]
    C = dm_ref.shape[1]

    @pl.when(l == 0)
    def _():
        s_scr[...] = jnp.zeros_like(s_scr)

    def body(ci, carry):
        rows = pl.ds(pl.multiple_of(ci * C, C), C)
        o, new = _ret_chunk(q_ref[rows, :], k_ref[rows, :], v_ref[rows, :],
                            [s_scr[h] for h in range(RET_HEADS)], dm_ref, qw_ref, kw_ref, sd_ref)
        o_scr[rows, :] = o
        for h in range(RET_HEADS):
            s_scr[h] = new[h]
        return carry

    lax.fori_loop(0, Lb // C, body, 0)
    _ret_epilogue(o_scr, gate_ref, o_ref)

    @pl.when(l == pl.num_programs(1) - 1)
    def _():
        st_ref[...] = s_scr[...]


def _ret_tables(C):
    log_gamma = jnp.log(1.0 - jnp.exp2(-5.0 - jnp.arange(RET_HEADS, dtype=F32)))
    idx = jnp.arange(C, dtype=F32)
    rel = idx[:, None] - idx[None, :]
    tri = jnp.tril(jnp.ones((C, C), dtype=bool))
    dmat = jnp.exp(jnp.where(tri[None], log_gamma[:, None, None] * rel[None], -jnp.inf))
    qw = jnp.exp(log_gamma[:, None] * (idx[None, :] + 1.0))[..., None]
    kw = jnp.exp(log_gamma[:, None] * (C - 1.0 - idx[None, :]))[..., None]
    sdec = jnp.exp(log_gamma * C)[:, None, None]
    bc = lambda a: jnp.broadcast_to(a, (RET_HEADS, a.shape[1], LANES))
    return dmat, bc(qw), bc(kw), jnp.broadcast_to(sdec, (RET_HEADS, SUBLANES, LANES))


_QK0 = (2 * HG_KEY_DIM + 2 * HG_VAL_DIM) // RET_QK_DIM
_V0 = (2 * HG_KEY_DIM + 2 * HG_VAL_DIM + 2 * RET_QK_DIM) // RET_V_DIM


def _full3(a):
    return pl.BlockSpec(a.shape, lambda *_: (0, 0, 0))


def _ret_prompt(P, B, L, Lb):
    C = math.gcd(L, CHUNK)
    nl = L // Lb
    tabs = _ret_tables(C)
    st_shape = (RET_HEADS, RET_HEAD_QK, RET_HEAD_V)
    return pl.pallas_call(
        _ret_prompt_kernel,
        grid=(B, nl),
        in_specs=[
            pl.BlockSpec((Lb, RET_QK_DIM), lambda b, l: (b * nl + l, _QK0)),
            pl.BlockSpec((Lb, RET_QK_DIM), lambda b, l: (b * nl + l, _QK0 + 1)),
            pl.BlockSpec((Lb, RET_V_DIM), lambda b, l: (b * nl + l, _V0)),
            pl.BlockSpec((Lb, RET_V_DIM), lambda b, l: (b * nl + l, _V0 + 1)),
        ] + [_full3(t) for t in tabs],
        out_specs=[
            pl.BlockSpec((Lb, RET_V_DIM), lambda b, l: (b * nl + l, 0)),
            pl.BlockSpec((None,) + st_shape, lambda b, l: (b, 0, 0, 0)),
        ],
        out_shape=[
            jax.ShapeDtypeStruct((B * L, RET_V_DIM), BF16),
            jax.ShapeDtypeStruct((B,) + st_shape, F32),
        ],
        scratch_shapes=[pltpu.VMEM(st_shape, F32), pltpu.VMEM((Lb, RET_V_DIM), F32)],
        compiler_params=_params(("parallel", "arbitrary")),
        name="ret_prompt",
    )(P, P, P, P, *tabs)


def _ret_sample_kernel(q_ref, k_ref, v_ref, gate_ref, s_ref, dm_ref, qw_ref, kw_ref, sd_ref,
                       o_ref, so_ref, o_scr, *, C):
    Bb = s_ref.shape[0]

    def body(bb, carry):
        rows = pl.ds(pl.multiple_of(bb * C, C), C)
        o, new = _ret_chunk(q_ref[rows, :], k_ref[rows, :], v_ref[rows, :],
                            [s_ref[bb, h] for h in range(RET_HEADS)], dm_ref, qw_ref, kw_ref, sd_ref)
        o_scr[rows, :] = o
        for h in range(RET_HEADS):
            so_ref[bb, h] = new[h]
        return carry

    lax.fori_loop(0, Bb, body, 0)
    _ret_epilogue(o_scr, gate_ref, o_ref)


def _ret_sample(P, state, Bs, Ls, Bb, row0):
    tabs = _ret_tables(Ls)
    rows = Bb * Ls
    blk0 = row0 // rows
    st = pl.BlockSpec((Bb, RET_HEADS, RET_HEAD_QK, RET_HEAD_V), lambda i: (i, 0, 0, 0))
    return pl.pallas_call(
        functools.partial(_ret_sample_kernel, C=Ls),
        grid=(Bs // Bb,),
        in_specs=[
            pl.BlockSpec((rows, RET_QK_DIM), lambda i: (blk0 + i, _QK0)),
            pl.BlockSpec((rows, RET_QK_DIM), lambda i: (blk0 + i, _QK0 + 1)),
            pl.BlockSpec((rows, RET_V_DIM), lambda i: (blk0 + i, _V0)),
            pl.BlockSpec((rows, RET_V_DIM), lambda i: (blk0 + i, _V0 + 1)),
            st,
        ] + [_full3(t) for t in tabs],
        out_specs=[pl.BlockSpec((rows, RET_V_DIM), lambda i: (i, 0)), st],
        out_shape=[
            jax.ShapeDtypeStruct((Bs * Ls, RET_V_DIM), BF16),
            jax.ShapeDtypeStruct((Bs, RET_HEADS, RET_HEAD_QK, RET_HEAD_V), F32),
        ],
        scratch_shapes=[pltpu.VMEM((rows, RET_V_DIM), F32)],
        compiler_params=_params(("parallel",)),
        name="ret_sample",
    )(P, P, P, P, state, *tabs)


def _split_dot(a, w_hi, w_lo):
    a_hi = a.astype(BF16)
    a_lo = (a - a_hi.astype(F32)).astype(BF16)
    return _dot(a_hi, w_hi) + (_dot(a_hi, w_lo) + _dot(a_lo, w_hi))


def _route(logits):
    lane = lax.broadcasted_iota(I32, logits.shape, 1)
    neg = jnp.float32(-jnp.inf)
    big = jnp.int32(LANES)
    gmask = lane < N_GROUPS
    gl = jnp.where(gmask, logits, neg)
    gmax = jnp.max(gl, axis=1, keepdims=True)
    gu = jnp.where(gmask, jnp.exp(gl - gmax), 0.0)
    gp = gu / jnp.sum(gu, axis=1, keepdims=True)
    g_w = jnp.max(gp, axis=1, keepdims=True)
    g_i = jnp.min(jnp.where(gmask & (gp == g_w), lane, big), axis=1, keepdims=True)
    lo = ROUTE_EXPERT_LANE + g_i * EXPERTS_PER_GROUP
    emask = (lane >= lo) & (lane < lo + EXPERTS_PER_GROUP)
    el = jnp.where(emask, logits, neg)
    e1 = jnp.max(el, axis=1, keepdims=True)
    i1 = jnp.min(jnp.where(emask & (el == e1), lane, big), axis=1, keepdims=True)
    el2 = jnp.where(lane == i1, neg, el)
    e2 = jnp.max(el2, axis=1, keepdims=True)
    i2 = jnp.min(jnp.where(emask & (lane != i1) & (el2 == e2), lane, big), axis=1, keepdims=True)
    u2 = jnp.exp(e2 - e1)
    den = 1.0 + u2
    w1 = (1.0 / den) * g_w
    w2 = (u2 / den) * g_w
    id1 = (i1 - ROUTE_EXPERT_LANE).astype(F32)
    id2 = (i2 - ROUTE_EXPERT_LANE).astype(F32)
    return (jnp.where(lane == LANE_E1, id1, 0.0) + jnp.where(lane == LANE_E2, id2, 0.0)
            + jnp.where(lane == LANE_W1, w1, 0.0) + jnp.where(lane == LANE_W2, w2, 0.0))


def _mix_kernel(ohg_p_ref, ohg_s_ref, or_p_ref, or_s_ref, ga0_ref, ga1_ref, gb0_ref, gb1_ref, xp_ref, xs_ref,
                wbh_ref, wbr_ref, wout_ref, nffn_ref, wr_hi_ref, wr_lo_ref, br_ref,
                h_ref, hn_ref, route_ref, *, n_prompt_tiles):
    i = pl.program_id(0)
    tm = h_ref.shape[0]
    is_prompt = i < n_prompt_tiles
    p_hg = _dot(jnp.where(is_prompt, ohg_p_ref[...], ohg_s_ref[...]), wbh_ref[...])
    p_r = _dot(jnp.where(is_prompt, or_p_ref[...], or_s_ref[...]), wbr_ref[...])
    half = D_MODEL // 2
    m0 = ga0_ref[...] * p_hg[:, :half] + gb0_ref[...] * p_r[:, :half]
    m1 = ga1_ref[...] * p_hg[:, half:] + gb1_ref[...] * p_r[:, half:]
    mixed = jnp.concatenate([m0, m1], axis=1).astype(BF16)
    x = jnp.where(is_prompt, xp_ref[...], xs_ref[...])
    h = x + _dot(mixed, wout_ref[...])
    h_ref[...] = h
    hn = _rms(h) * nffn_ref[...]
    for s in range(ROW_GROUP):
        hn_ref[pl.ds(s, tm, stride=ROW_GROUP), :] = hn[:, s * LANES:(s + 1) * LANES]
    route_ref[...] = _route(_split_dot(hn, wr_hi_ref[...], wr_lo_ref[...]) + br_ref[...])


def _mix(o_hg_p, o_hg_s, o_r_p, o_r_s, P, xp, xs, wbh, wbr, wout, norm_ffn, wr_hi, wr_lo, br, tm):
    npt, nst = xp.shape[0] // tm, xs.shape[0] // tm
    T = (npt + nst) * tm
    half = D_MODEL // 2
    ga = (IN_WIDTH - 2 * D_MODEL) // half

    def pcol(c):
        return pl.BlockSpec((tm, half), lambda i, c=c: (i, c))

    def const(a):
        return pl.BlockSpec(a.shape, lambda i: (0, 0), pipeline_mode=pl.Buffered(1))

    def prompt(w):
        return pl.BlockSpec((tm, w), lambda i: (jnp.minimum(i, npt - 1), 0))

    def sample(w):
        return pl.BlockSpec((tm, w), lambda i: (jnp.maximum(i - npt, 0), 0))

    return pl.pallas_call(
        functools.partial(_mix_kernel, n_prompt_tiles=npt),
        grid=(npt + nst,),
        in_specs=[
            prompt(half), sample(half), prompt(half), sample(half),
            pcol(ga), pcol(ga + 1), pcol(ga + 2), pcol(ga + 3),
            prompt(D_MODEL), sample(D_MODEL),
            const(wbh), const(wbr), const(wout), const(norm_ffn), const(wr_hi), const(wr_lo), const(br),
        ],
        out_specs=[
            pl.BlockSpec((tm, D_MODEL), lambda i: (i, 0)),
            pl.BlockSpec((tm * ROW_GROUP, LANES), lambda i: (i, 0)),
            pl.BlockSpec((tm, LANES), lambda i: (i, 0)),
        ],
        out_shape=[
            jax.ShapeDtypeStruct((T, D_MODEL), F32),
            jax.ShapeDtypeStruct((T * ROW_GROUP, LANES), F32),
            jax.ShapeDtypeStruct((T, LANES), F32),
        ],
        compiler_params=_params(("parallel",)),
        name="mix",
    )(o_hg_p, o_hg_s, o_r_p, o_r_s, P, P, P, P, xp, xs, wbh, wbr, wout, norm_ffn, wr_hi, wr_lo, br)


def _plan_kernel(route_ref, pos_ref, te_ref, tail_ref, cum_scr, *, tb):
    T = route_ref.shape[0]
    lane_t = lax.broadcasted_iota(I32, (tb, LANES), 1).astype(F32)
    r = lax.broadcasted_iota(I32, (tb, tb), 0)
    c = lax.broadcasted_iota(I32, (tb, tb), 1)
    strict_lower = jnp.where(r > c, 1.0, 0.0).astype(BF16)

    def onehots(rows):
        rt = route_ref[rows, :]
        a1 = jnp.where(lane_t == rt[:, LANE_E1:LANE_E1 + 1], 1.0, 0.0)
        a2 = jnp.where(lane_t == rt[:, LANE_E2:LANE_E2 + 1], 1.0, 0.0)
        return a1, a2

    def rank_body(bi, carry):
        rows = pl.ds(pl.multiple_of(bi * tb, tb), tb)
        a1, a2 = onehots(rows)
        m = a1 + a2
        cum_scr[rows, :] = _dot(strict_lower, m.astype(BF16)) + carry
        return carry + jnp.sum(m, axis=0, keepdims=True)

    counts = lax.fori_loop(0, T // tb, rank_body, jnp.zeros((1, LANES), F32))
    ntiles = jnp.floor((counts + (MOE_TILE - 1)) * (1.0 / MOE_TILE))
    rr = lax.broadcasted_iota(I32, (LANES, LANES), 0)
    cc = lax.broadcasted_iota(I32, (LANES, LANES), 1)
    upper = jnp.where(rr < cc, 1.0, 0.0).astype(BF16)
    tile0 = _dot(jnp.broadcast_to(ntiles, (SUBLANES, LANES)).astype(BF16), upper)[0:1, :]
    slot0 = tile0 * MOE_TILE

    def pos_body(bi, carry):
        rows = pl.ds(pl.multiple_of(bi * tb, tb), tb)
        a1, a2 = onehots(rows)
        base = cum_scr[rows, :] + slot0
        p1 = jnp.sum(a1 * base, axis=1, keepdims=True)
        p2 = jnp.sum(a2 * base, axis=1, keepdims=True)
        pos_ref[rows, :] = (jnp.where(lane_t == 0.0, p1, 0.0) + jnp.where(lane_t == 1.0, p2, 0.0)).astype(I32)
        return carry

    lax.fori_loop(0, T // tb, pos_body, 0)
    tend = tile0 + ntiles
    lane = lax.broadcasted_iota(I32, (LANES, LANES), 1)
    tile_i = lax.broadcasted_iota(I32, (LANES, LANES), 0).astype(F32)
    is_expert = lane < N_EXPERTS
    te = jnp.sum(jnp.where(is_expert & (jnp.broadcast_to(tend, (LANES, LANES)) <= tile_i), 1.0, 0.0),
                 axis=1, keepdims=True)
    te_ref[...] = jnp.broadcast_to(jnp.minimum(te, N_EXPERTS - 1.0), (LANES, LANES)).astype(I32)
    lane1 = lax.broadcasted_iota(I32, (1, LANES), 1)
    total = jnp.sum(jnp.where(lane1 < N_EXPERTS, ntiles, 0.0), axis=1, keepdims=True)
    tail = jnp.where(ntiles > 0.0, (tend - 1.0) * MOE_TILE, -1.0)
    tail = jnp.where(lane1 < N_EXPERTS, tail, jnp.where(lane1 == N_EXPERTS, total, 0.0))
    tail_ref[...] = jnp.broadcast_to(tail, (SUBLANES, LANES)).astype(I32)


def _plan(route, tb):
    T = route.shape[0]
    return pl.pallas_call(
        functools.partial(_plan_kernel, tb=tb),
        grid=(1,),
        in_specs=[pl.BlockSpec((T, LANES), lambda i: (0, 0))],
        out_specs=[
            pl.BlockSpec((T, LANES), lambda i: (0, 0)),
            pl.BlockSpec((LANES, LANES), lambda i: (0, 0)),
            pl.BlockSpec((SUBLANES, LANES), lambda i: (0, 0)),
        ],
        out_shape=[
            jax.ShapeDtypeStruct((T, LANES), I32),
            jax.ShapeDtypeStruct((LANES, LANES), I32),
            jax.ShapeDtypeStruct((SUBLANES, LANES), I32),
        ],
        scratch_shapes=[pltpu.VMEM((T, LANES), F32)],
        compiler_params=_params(("arbitrary",)),
        name="moe_plan",
    )(route)


def _row_group(ref, idx):
    return ref.at[pl.ds(pl.multiple_of(idx * ROW_GROUP, ROW_GROUP), ROW_GROUP), :]


def _dispatch_kernel(pos_ref, tail_ref, hn_ref, xs_ref, zero_scr, zsem, sem):
    i = pl.program_id(0)
    tm = hn_ref.shape[0] // ROW_GROUP
    tile_rows = MOE_TILE * ROW_GROUP

    @pl.when(i == 0)
    def _():
        zero_scr[...] = jnp.zeros_like(zero_scr)
        n_used = tail_ref[N_EXPERTS]

        def zero_copy(first_slot):
            start = pl.multiple_of(first_slot * ROW_GROUP, tile_rows)
            return pltpu.make_async_copy(zero_scr, xs_ref.at[pl.ds(start, tile_rows), :], zsem)

        def tails(fn):
            def body(e, carry):
                pl.when(tail_ref[e] >= 0)(lambda: fn(zero_copy(jnp.maximum(tail_ref[e], 0))))
                return carry
            lax.fori_loop(0, N_EXPERTS, body, 0)

        def unused(fn):
            def body(t, carry):
                fn(zero_copy(t * MOE_TILE))
                return carry
            lax.fori_loop(n_used, xs_ref.shape[0] // tile_rows, body, 0)

        tails(lambda c: c.start())
        unused(lambda c: c.start())
        tails(lambda c: c.wait())
        unused(lambda c: c.wait())

    def copy(r, k):
        slot = pos_ref[(i * tm + r) * TOP_K + k]
        return pltpu.make_async_copy(_row_group(hn_ref, r), _row_group(xs_ref, slot), sem)

    def start(r, carry):
        copy(r, 0).start()
        copy(r, 1).start()
        return carry

    def wait(r, carry):
        copy(r, 0).wait()
        copy(r, 1).wait()
        return carry

    lax.fori_loop(0, tm, start, 0, unroll=8)
    lax.fori_loop(0, tm, wait, 0, unroll=8)


def _dispatch(pos, tails, hn_rows, n_slots, tm):
    T = hn_rows.shape[0] // ROW_GROUP
    return pl.pallas_call(
        _dispatch_kernel,
        grid_spec=pltpu.PrefetchScalarGridSpec(
            num_scalar_prefetch=2,
            grid=(T // tm,),
            in_specs=[pl.BlockSpec((tm * ROW_GROUP, LANES), lambda i, *_: (i, 0))],
            out_specs=pl.BlockSpec(memory_space=pl.ANY),
            scratch_shapes=[pltpu.VMEM((MOE_TILE * ROW_GROUP, LANES), F32),
                            pltpu.SemaphoreType.DMA, pltpu.SemaphoreType.DMA],
        ),
        out_shape=jax.ShapeDtypeStruct((n_slots * ROW_GROUP, LANES), F32),
        compiler_params=_params(("arbitrary",)),
        name="moe_dispatch",
    )(pos, tails, hn_rows)


def _experts_kernel(te_ref, nt_ref, xs_ref, wg_ref, wu_ref, wd_ref, ys_ref, wg_scr, wu_scr, wd_scr):
    i = pl.program_id(0)
    M = MOE_TILE
    prev = te_ref[jnp.maximum(i - 1, 0)]

    @pl.when((i == 0) | (te_ref[i] != prev))
    def _():
        wg_scr[...] = wg_ref[...].astype(BF16)
        wu_scr[...] = wu_ref[...].astype(BF16)
        wd_scr[...] = wd_ref[...].astype(BF16)

    @pl.when(i < nt_ref[0])
    def _():
        x = jnp.concatenate([xs_ref[pl.ds(s, M, stride=ROW_GROUP), :] for s in range(ROW_GROUP)],
                            axis=1).astype(BF16)
        hmid = (_silu(_dot(x, wg_scr[...])) * _dot(x, wu_scr[...])).astype(BF16)
        y = _dot(hmid, wd_scr[...])
        for s in range(ROW_GROUP):
            ys_ref[pl.ds(s, M, stride=ROW_GROUP), :] = y[:, s * LANES:(s + 1) * LANES]


def _experts(te, nt, xs_rows, w_ge, w_ue, w_de, n_tiles):
    M = MOE_TILE

    def tile(i, te_ref, nt_ref):
        return jnp.minimum(i, nt_ref[0] - 1)

    def expert(i, te_ref, nt_ref):
        return te_ref[tile(i, te_ref, nt_ref)]

    rows = pl.BlockSpec((M * ROW_GROUP, LANES), lambda i, te_ref, nt_ref: (tile(i, te_ref, nt_ref), 0))
    return pl.pallas_call(
        _experts_kernel,
        grid_spec=pltpu.PrefetchScalarGridSpec(
            num_scalar_prefetch=2,
            grid=(n_tiles,),
            in_specs=[
                rows,
                pl.BlockSpec((None, D_MODEL, EXPERT_FF), lambda i, a, b: (expert(i, a, b), 0, 0)),
                pl.BlockSpec((None, D_MODEL, EXPERT_FF), lambda i, a, b: (expert(i, a, b), 0, 0)),
                pl.BlockSpec((None, EXPERT_FF, D_MODEL), lambda i, a, b: (expert(i, a, b), 0, 0)),
            ],
            out_specs=rows,
            scratch_shapes=[pltpu.VMEM((D_MODEL, EXPERT_FF), BF16), pltpu.VMEM((D_MODEL, EXPERT_FF), BF16),
                            pltpu.VMEM((EXPERT_FF, D_MODEL), BF16)],
        ),
        out_shape=jax.ShapeDtypeStruct(xs_rows.shape, F32),
        input_output_aliases={2: 0},
        compiler_params=_params(("arbitrary",)),
        name="moe_experts",
    )(te, nt, xs_rows, w_ge, w_ue, w_de)


def _final_kernel(pos_ref, h_ref, route_ref, gain_ref, ys_ref, y_ref, buf, sem, *, tile0):
    i = pl.program_id(0)
    n = pl.num_programs(0)
    tm = h_ref.shape[0]

    def copy(step, r, k, slot_buf):
        slot = pos_ref[((tile0 + step) * tm + r) * TOP_K + k]
        return pltpu.make_async_copy(_row_group(ys_ref, slot), _row_group(buf.at[slot_buf], r * TOP_K + k),
                                     sem.at[slot_buf])

    def start_all(step, slot_buf):
        def body(r, carry):
            copy(step, r, 0, slot_buf).start()
            copy(step, r, 1, slot_buf).start()
            return carry
        lax.fori_loop(0, tm, body, 0, unroll=8)

    def wait_all(step, slot_buf):
        def body(r, carry):
            copy(step, r, 0, slot_buf).wait()
            copy(step, r, 1, slot_buf).wait()
            return carry
        lax.fori_loop(0, tm, body, 0, unroll=8)

    cur = i % 2
    pl.when(i == 0)(lambda: start_all(0, 0))
    pl.when(i + 1 < n)(lambda: start_all(i + 1, 1 - cur))
    wait_all(i, cur)

    rt = route_ref[...]
    w1 = rt[:, LANE_W1:LANE_W1 + 1]
    w2 = rt[:, LANE_W2:LANE_W2 + 1]
    ssq = jnp.zeros((tm, 1), F32)
    for s in range(ROW_GROUP):
        cols = slice(s * LANES, (s + 1) * LANES)
        o1 = buf[cur, pl.ds(s, tm, stride=TOP_K * ROW_GROUP), :]
        o2 = buf[cur, pl.ds(ROW_GROUP + s, tm, stride=TOP_K * ROW_GROUP), :]
        v = h_ref[:, cols] + (w1 * o1 + w2 * o2)
        y_ref[:, cols] = v
        ssq = ssq + jnp.sum(v * v, axis=1, keepdims=True)
    y_ref[...] = y_ref[...] * lax.rsqrt(ssq * (1.0 / D_MODEL) + EPS) * gain_ref[...]


def _final(pos, h, route, gain, ys_rows, tile0, n_tiles, tm):
    return pl.pallas_call(
        functools.partial(_final_kernel, tile0=tile0),
        grid_spec=pltpu.PrefetchScalarGridSpec(
            num_scalar_prefetch=1,
            grid=(n_tiles,),
            in_specs=[
                pl.BlockSpec((tm, D_MODEL), lambda i, *_: (tile0 + i, 0)),
                pl.BlockSpec((tm, LANES), lambda i, *_: (tile0 + i, 0)),
                pl.BlockSpec((1, D_MODEL), lambda i, *_: (0, 0)),
                pl.BlockSpec(memory_space=pl.ANY),
            ],
            out_specs=pl.BlockSpec((tm, D_MODEL), lambda i, *_: (i, 0)),
            scratch_shapes=[pltpu.VMEM((2, tm * TOP_K * ROW_GROUP, LANES), F32), pltpu.SemaphoreType.DMA((2,))],
        ),
        out_shape=jax.ShapeDtypeStruct((n_tiles * tm, D_MODEL), F32),
        compiler_params=_params(("arbitrary",)),
        name="final",
    )(pos, h, route, gain, ys_rows)


def _rope_tables(pos, reps):
    half = RET_HEAD_QK // 2
    inv = 1.0 / (ROPE_BASE ** jnp.linspace(0.0, 1.0, half, dtype=F32))
    ang = jnp.repeat(pos[:, None] * inv[None, :], 2, axis=-1)
    cos, sin = jnp.cos(ang), jnp.sin(ang)
    even = (jnp.arange(RET_HEAD_QK) % 2 == 0)[None, :]
    sa = jnp.where(even, -sin, 0.0)
    sb = jnp.where(even, 0.0, sin)
    return tuple(jnp.tile(t, (reps, 1)) for t in (cos, sa, sb))


def _tile_sizes(Tp, Ts, Lp):
    g = math.gcd(Tp, Ts)
    return dict(
        proj=math.gcd(g, 1024),
        mix=math.gcd(g, 256),
        plan=math.gcd(Tp + Ts, 256),
        dispatch=math.gcd(g, 256),
        final=math.gcd(g, 256),
        scan=math.gcd(Lp, 512),
    )


def kernel(x_prompt, x_sample, state_hgrn, state_ret, norm_mix, w_in, hg_lb_logits, hg_norm,
           w_branch_hg, w_branch_ret, b_gate, w_out, norm_ffn, w_router_group, b_router_group,
           w_router_expert, b_router_expert, w_expert_gate, w_expert_up, w_expert_down, norm_final):
    assert w_in.shape == (1, D_MODEL, IN_WIDTH), "single-layer trunk only"
    Bp, Lp, _ = x_prompt.shape
    Bs, Ls, _ = x_sample.shape
    Tp, Ts = Bp * Lp, Bs * Ls
    T = Tp + Ts
    ts = _tile_sizes(Tp, Ts, Lp)
    xp = x_prompt.reshape(Tp, D_MODEL)
    xs = x_sample.reshape(Ts, D_MODEL)

    lower = jnp.cumsum(jax.nn.softmax(hg_lb_logits.astype(F32), axis=0), axis=0)[0]
    colp = jnp.zeros((IN_WIDTH,), F32)
    colp = colp.at[HG_KEY_DIM:2 * HG_KEY_DIM].set(lower)
    colp = colp.at[IN_WIDTH - 2 * D_MODEL:].set(b_gate[0])[None, :]

    tm = ts["proj"]
    tab_p = _rope_tables(jnp.arange(Lp, dtype=F32), max(tm // Lp, 1))
    tab_s = _rope_tables(jnp.arange(Ls, dtype=F32) + jnp.float32(PAST_LEN), max(tm // Ls, 1))
    tabs = tuple(jnp.concatenate([a, b], axis=0) for a, b in zip(tab_p, tab_s))
    P = _in_proj(xp, xs, norm_mix, w_in[0], colp, tabs, tm, tab_p[0].shape[0] // tm, tab_s[0].shape[0] // tm)

    bb = math.gcd(Bs, 8)
    o_hg_p, shp = _hgrn_prompt(P, hg_norm, Bp, Lp, ts["scan"])
    o_hg_s, shs = _hgrn_sample(P, hg_norm, state_hgrn[0].astype(F32), Bs, Ls, bb, Tp)
    o_r_p, srp = _ret_prompt(P, Bp, Lp, ts["scan"])
    o_r_s, srs = _ret_sample(P, state_ret[0].astype(F32), Bs, Ls, bb, Tp)

    wbh = w_branch_hg[0].astype(BF16)
    wbr = w_branch_ret[0].astype(BF16)
    wout = w_out[0].astype(BF16)
    pad = LANES - N_GROUPS - N_EXPERTS
    wr = jnp.concatenate([w_router_group[0], w_router_expert[0], jnp.zeros((D_MODEL, pad), F32)], axis=1)
    wr_hi = wr.astype(BF16)
    wr_lo = (wr - wr_hi.astype(F32)).astype(BF16)
    br = jnp.concatenate([b_router_group[0], b_router_expert[0], jnp.zeros((pad,), F32)])[None, :]
    h, hn_rows, route = _mix(o_hg_p, o_hg_s, o_r_p, o_r_s, P, xp, xs, wbh, wbr, wout, norm_ffn,
                             wr_hi, wr_lo, br, ts["mix"])

    n_tiles = -(-T * TOP_K // MOE_TILE) + N_EXPERTS
    assert n_tiles <= LANES
    pos_slab, te_slab, tail_slab = _plan(route, ts["plan"])
    pos = pos_slab[:, :TOP_K].reshape(-1)
    te = te_slab[:n_tiles, 0]
    tails = tail_slab[0, :N_EXPERTS + 1]
    nt = tail_slab[0, N_EXPERTS:N_EXPERTS + 1]
    xs_rows = _dispatch(pos, tails, hn_rows, n_tiles * MOE_TILE, ts["dispatch"])
    ys_rows = _experts(te, nt, xs_rows, w_expert_gate[0], w_expert_up[0], w_expert_down[0], n_tiles)

    tf = ts["final"]
    gain = norm_final[None, :]
    y_p = _final(pos, h, route, gain, ys_rows, 0, Tp // tf, tf)
    y_s = _final(pos, h, route, gain, ys_rows, Tp // tf, Ts // tf, tf)
    return (y_p.reshape(Bp, Lp, D_MODEL), y_s.reshape(Bs, Ls, D_MODEL),
            shp[None], srp[None], shs[None], srs[None])
```

```python
import functools
import math

import jax
import jax.numpy as jnp
from jax import lax
from jax.experimental import pallas as pl
from jax.experimental.pallas import tpu as pltpu

F32 = jnp.float32
BF16 = jnp.bfloat16
I32 = jnp.int32

D_MODEL = 2048
PAST_LEN = 16384
HG_HEAD_V = 128
HG_EXPAND = 128
HG_VAL_DIM = D_MODEL // 2
HG_HEADS = HG_VAL_DIM // HG_HEAD_V
HG_KEY_DIM = HG_HEADS * HG_EXPAND
RET_HEAD_V = 256
RET_HEAD_QK = 128
RET_V_DIM = D_MODEL // 2
RET_HEADS = RET_V_DIM // RET_HEAD_V
RET_QK_DIM = RET_HEADS * RET_HEAD_QK
IN_WIDTH = 2 * HG_KEY_DIM + 2 * HG_VAL_DIM + 2 * RET_QK_DIM + 2 * RET_V_DIM + 2 * D_MODEL
CHUNK = 64
N_GROUPS = 4
EXPERTS_PER_GROUP = 8
N_EXPERTS = N_GROUPS * EXPERTS_PER_GROUP
TOP_K = 2
EXPERT_FF = D_MODEL // 4
ROPE_BASE = 10000.0
EPS = 1e-6

LANES = 128
SUBLANES = 8
VMEM_LIMIT = 56 * 1024 * 1024
PROJ_MC = 256
PROJ_TN = 512
SEG_HQ, SEG_HF, SEG_HI, SEG_HG, SEG_RQ, SEG_RK, SEG_RV, SEG_RG, SEG_GA = 0, 2, 4, 6, 8, 9, 10, 12, 14
SAFE_EXP_SPAN = 80.0
ROW_GROUP = D_MODEL // LANES
MOE_TILE = 256
LANE_E1, LANE_E2, LANE_W1, LANE_W2 = 0, 1, 2, 3
ROUTE_EXPERT_LANE = N_GROUPS


def _params(sem):
    return pltpu.CompilerParams(dimension_semantics=sem, vmem_limit_bytes=VMEM_LIMIT)


def _silu(z):
    return z * jax.nn.sigmoid(z)


def _dot(a, b):
    return jnp.dot(a, b, preferred_element_type=F32)


def _dot_nt(a, b):
    return lax.dot_general(a, b, (((1,), (1,)), ((), ())), preferred_element_type=F32)


def _dot_tn(a, b):
    return lax.dot_general(a, b, (((0,), (0,)), ((), ())), preferred_element_type=F32)


def _rms(x):
    return x * lax.rsqrt(jnp.mean(x * x, axis=-1, keepdims=True) + EPS)


def _xnorm_kernel(xp_ref, xs_ref, gain_ref, o_ref, *, n_prompt_tiles):
    i = pl.program_id(0)

    def norm(x_ref):
        o_ref[...] = (_rms(x_ref[...]) * gain_ref[...]).astype(BF16)

    pl.when(i < n_prompt_tiles)(lambda: norm(xp_ref))
    pl.when(i >= n_prompt_tiles)(lambda: norm(xs_ref))


def _xnorm(xp, xs, gain, tm):
    npt, nst = xp.shape[0] // tm, xs.shape[0] // tm
    return pl.pallas_call(
        functools.partial(_xnorm_kernel, n_prompt_tiles=npt),
        grid=(npt + nst,),
        in_specs=[
            pl.BlockSpec((tm, D_MODEL), lambda i: (jnp.minimum(i, npt - 1), 0)),
            pl.BlockSpec((tm, D_MODEL), lambda i: (jnp.maximum(i - npt, 0), 0)),
            pl.BlockSpec((1, D_MODEL), lambda i: (0, 0)),
        ],
        out_specs=pl.BlockSpec((tm, D_MODEL), lambda i: (i, 0)),
        out_shape=jax.ShapeDtypeStruct(((npt + nst) * tm, D_MODEL), BF16),
        compiler_params=_params(("parallel",)),
        name="xnorm",
    )(xp, xs, gain)


ACT_ALPHA, ACT_BETA, ACT_GAMMA, ACT_DELTA, ACT_EPS = 0, 1, 2, 3, 4


def _act_coefficients(lower, b_gate):
    z = jnp.zeros((IN_WIDTH,), F32)
    seg = lambda a, b: slice(a * PROJ_TN, b * PROJ_TN)
    alpha = z.at[seg(SEG_HI, SEG_HG)].set(1.0).at[seg(SEG_RQ, SEG_RG)].set(1.0)
    beta = z.at[seg(SEG_GA, IN_WIDTH // PROJ_TN)].set(b_gate)
    gamma = (z.at[seg(SEG_HQ, SEG_HF)].set(HG_EXPAND ** -0.5)
             .at[seg(SEG_HG, SEG_RQ)].set(1.0).at[seg(SEG_RG, SEG_GA)].set(1.0))
    delta = z.at[seg(SEG_HF, SEG_HI)].set(1.0 - lower).at[seg(SEG_GA, IN_WIDTH // PROJ_TN)].set(1.0)
    eps = z.at[seg(SEG_HF, SEG_HI)].set(lower)
    return jnp.stack([alpha, beta, gamma, delta, eps, z, z, z])


def _in_proj_kernel(x_ref, w_ref, act_ref, cos_ref, sa_ref, sb_ref, o_ref):
    j = pl.program_id(1)
    w = w_ref[...].astype(BF16)
    row = lambda r: act_ref[r:r + 1, :]
    tm = x_ref.shape[0]
    mc = math.gcd(tm, PROJ_MC)
    for m in range(0, tm, mc):
        p = _dot(x_ref[m:m + mc, :], w)
        o_ref[m:m + mc, :] = (row(ACT_ALPHA) * p + row(ACT_EPS)
                              + jax.nn.sigmoid(p + row(ACT_BETA)) * (row(ACT_GAMMA) * p + row(ACT_DELTA)))

    @pl.when((j >= SEG_RQ) & (j < SEG_RV))
    def _():
        scale = jnp.where(j == SEG_RK, RET_HEAD_QK ** -0.5, 1.0).astype(F32)
        cos, sa, sb = cos_ref[...], sa_ref[...], sb_ref[...]
        for hh in range(PROJ_TN // LANES):
            cols = slice(hh * LANES, (hh + 1) * LANES)
            xs = o_ref[:, cols]
            r = xs * cos + pltpu.roll(xs, LANES - 1, 1) * sa + pltpu.roll(xs, 1, 1) * sb
            o_ref[:, cols] = r * scale


def _in_proj(xn, w_in, act, tabs, tm):
    T = xn.shape[0]
    once = dict(pipeline_mode=pl.Buffered(1))
    tab = pl.BlockSpec((tm, LANES), lambda i, j: (i, 0), **once)
    return pl.pallas_call(
        _in_proj_kernel,
        grid=(T // tm, IN_WIDTH // PROJ_TN),
        in_specs=[
            pl.BlockSpec((tm, D_MODEL), lambda i, j: (i, 0), **once),
            pl.BlockSpec((D_MODEL, PROJ_TN), lambda i, j: (0, j)),
            pl.BlockSpec((SUBLANES, PROJ_TN), lambda i, j: (0, j)),
            tab, tab, tab,
        ],
        out_specs=pl.BlockSpec((tm, PROJ_TN), lambda i, j: (i, j)),
        out_shape=jax.ShapeDtypeStruct((T, IN_WIDTH), F32),
        compiler_params=_params(("parallel", "arbitrary")),
        name="in_proj",
    )(xn, w_in, act, *tabs)


def _prefix_matrix(C, G):
    r = lax.broadcasted_iota(I32, (C, 3 * C), 0)
    c = lax.broadcasted_iota(I32, (C, 3 * C), 1) % C
    return jnp.where((c <= r) & (c // G == r // G), 1.0, 0.0).astype(BF16)


def _prefix_sum(g, pm):
    g1 = g.astype(BF16)
    r1 = g - g1.astype(F32)
    g2 = r1.astype(BF16)
    g3 = (r1 - g2.astype(F32)).astype(BF16)
    return _dot(pm, jnp.concatenate([g1, g2, g3], axis=0))


def _state_decay_column(d_row):
    r = lax.broadcasted_iota(I32, (LANES, LANES), 0)
    c = lax.broadcasted_iota(I32, (LANES, LANES), 1)
    return jnp.sum(jnp.where(r == c, jnp.broadcast_to(d_row, (LANES, LANES)), 0.0), axis=1, keepdims=True)


def _hgrn_chunk(q, f, g, v, states, pm, tmp_ref, NG, factorised):
    C, W = q.shape
    G, H = C // NG, W // LANES
    hs = [slice(h * LANES, (h + 1) * LANES) for h in range(H)]
    gs = [slice(n * G, (n + 1) * G) for n in range(NG)]
    k = 1.0 - f
    b = _prefix_sum(g, pm)
    b3 = b.reshape(NG, G, W)
    b_end = b3[:, G - 1:G, :]
    q3, k3 = q.reshape(NG, G, W), k.reshape(NG, G, W)
    vb = v.astype(BF16)
    if factorised:
        b_mid = b3[:, G // 2 - 1:G // 2, :]
        qm3 = q3 * jnp.exp(b3 - b_mid)
        km3 = k3 * jnp.exp(b_mid - b3)
        qb = (qm3 * jnp.exp(b_mid)).reshape(C, W)
        kd = (km3 * jnp.exp(b_end - b_mid)).reshape(C, W)
        qm = qm3.reshape(C, W).astype(BF16)
        km = km3.reshape(C, W).astype(BF16)
        row = lax.broadcasted_iota(I32, (C, C), 0)
        col = lax.broadcasted_iota(I32, (C, C), 1)
        amask = (row >= col) & (row // G == col // G)
        o_intra = []
        for h in range(H):
            att = jnp.where(amask, _dot_nt(qm[:, hs[h]], km[:, hs[h]]), 0.0)
            o_intra.append(_dot(att.astype(BF16), vb[:, hs[h]]))
    else:
        qb = (q3 * jnp.exp(b3)).reshape(C, W)
        kd = (k3 * jnp.exp(b_end - b3)).reshape(C, W)
        trow = lax.broadcasted_iota(I32, (G, LANES), 0)
        o_intra = []
        for h in range(H):
            parts = []
            for n in range(NG):
                bh, qh = b[gs[n], hs[h]], q[gs[n], hs[h]]
                tmp_ref[0] = bh
                tmp_ref[1] = k[gs[n], hs[h]]
                tmp_ref[2] = v[gs[n], hs[h]]

                def body(s, acc, bh=bh, qh=qh):
                    bs = tmp_ref[0, pl.ds(s, 1), :]
                    ks = tmp_ref[1, pl.ds(s, 1), :]
                    vs = tmp_ref[2, pl.ds(s, 1), :]
                    w = jnp.where(trow >= s, jnp.exp(jnp.minimum(bh - bs, 0.0)), 0.0)
                    return acc + jnp.sum(qh * ks * w, axis=1, keepdims=True) * vs

                parts.append(lax.fori_loop(0, G, body, jnp.zeros((G, LANES), F32)))
            o_intra.append(parts[0] if NG == 1 else jnp.concatenate(parts, axis=0))
    d = jnp.exp(b_end)
    outs, new_states = [], [[None] * H for _ in range(NG)]
    for h in range(H):
        inter = []
        for n in range(NG):
            S = states[n][h]
            inter.append(_dot(qb[gs[n], hs[h]].astype(BF16), S.astype(BF16)))
            upd = _dot_tn(kd[gs[n], hs[h]].astype(BF16), v[gs[n], hs[h]].astype(BF16))
            new_states[n][h] = _state_decay_column(d[n, :, hs[h]]) * S + upd
        inter = inter[0] if NG == 1 else jnp.concatenate(inter, axis=0)
        outs.append(inter + o_intra[h])
    return jnp.concatenate(outs, axis=1), new_states


def _decay_is_safe(g, G):
    R, W = g.shape
    return jnp.min(jnp.sum(g.reshape(R // G, G, W), axis=1)) >= -SAFE_EXP_SPAN


def _hgrn_epilogue(o_scr, gain_ref, gate_ref, o_ref):
    o_ref[...] = (_rms(o_scr[...]) * gain_ref[...] * gate_ref[...]).astype(o_ref.dtype)


def _hgrn_prompt_kernel(q_ref, f_ref, v_ref, gate_ref, gain_ref, o_ref, st_ref, s_scr, g_scr, o_scr, tmp_ref,
                        *, C):
    l = pl.program_id(1)
    Lb = q_ref.shape[0]

    @pl.when(l == 0)
    def _():
        s_scr[...] = jnp.zeros_like(s_scr)

    g_scr[...] = jnp.log(f_ref[...])
    pm = _prefix_matrix(C, C)

    def run(factorised):
        def body(ci, carry):
            rows = pl.ds(pl.multiple_of(ci * C, C), C)
            o, new = _hgrn_chunk(q_ref[rows, :], f_ref[rows, :], g_scr[rows, :], v_ref[rows, :],
                                 [[s_scr[h] for h in range(HG_HEADS)]], pm, tmp_ref, 1, factorised)
            o_scr[rows, :] = o
            for h in range(HG_HEADS):
                s_scr[h] = new[0][h]
            return carry
        lax.fori_loop(0, Lb // C, body, 0)

    safe = _decay_is_safe(g_scr[...], C)
    pl.when(safe)(lambda: run(True))
    pl.when(jnp.logical_not(safe))(lambda: run(False))
    _hgrn_epilogue(o_scr, gain_ref, gate_ref, o_ref)

    @pl.when(l == pl.num_programs(1) - 1)
    def _():
        st_ref[...] = s_scr[...]


def _hgrn_prompt(P, gain, B, L, Lb):
    C = math.gcd(L, CHUNK)
    nl = L // Lb
    W = HG_KEY_DIM

    def col(seg):
        return pl.BlockSpec((Lb, W), lambda b, l, seg=seg: (b * nl + l, seg))

    return pl.pallas_call(
        functools.partial(_hgrn_prompt_kernel, C=C),
        grid=(B, nl),
        in_specs=[col(0), col(1), col(2), col(3), pl.BlockSpec((1, W), lambda b, l: (0, 0))],
        out_specs=[
            pl.BlockSpec((Lb, W), lambda b, l: (b * nl + l, 0)),
            pl.BlockSpec((None, HG_HEADS, HG_EXPAND, HG_HEAD_V), lambda b, l: (b, 0, 0, 0)),
        ],
        out_shape=[
            jax.ShapeDtypeStruct((B * L, HG_VAL_DIM), BF16),
            jax.ShapeDtypeStruct((B, HG_HEADS, HG_EXPAND, HG_HEAD_V), F32),
        ],
        scratch_shapes=[pltpu.VMEM((HG_HEADS, HG_EXPAND, HG_HEAD_V), F32), pltpu.VMEM((Lb, W), F32),
                        pltpu.VMEM((Lb, W), F32), pltpu.VMEM((3, C, LANES), F32)],
        compiler_params=_params(("parallel", "arbitrary")),
        name="hgrn_prompt",
    )(P, P, P, P, gain)


def _hgrn_sample_kernel(q_ref, f_ref, v_ref, gate_ref, gain_ref, s_ref, o_ref, so_ref, o_scr, tmp_ref, *, G):
    NG = s_ref.shape[0]
    g = jnp.log(f_ref[...])
    pm = _prefix_matrix(NG * G, G)

    def run(factorised):
        states = [[s_ref[n, h] for h in range(HG_HEADS)] for n in range(NG)]
        o, new = _hgrn_chunk(q_ref[...], f_ref[...], g, v_ref[...], states, pm, tmp_ref, NG, factorised)
        o_scr[...] = o
        for n in range(NG):
            for h in range(HG_HEADS):
                so_ref[n, h] = new[n][h]

    safe = _decay_is_safe(g, G)
    pl.when(safe)(lambda: run(True))
    pl.when(jnp.logical_not(safe))(lambda: run(False))
    _hgrn_epilogue(o_scr, gain_ref, gate_ref, o_ref)


def _hgrn_sample(P, gain, state, Bs, Ls, Bb, row0):
    W = HG_KEY_DIM
    rows = Bb * Ls
    blk0 = row0 // rows

    def col(seg):
        return pl.BlockSpec((rows, W), lambda i, seg=seg: (blk0 + i, seg))

    st = pl.BlockSpec((Bb, HG_HEADS, HG_EXPAND, HG_HEAD_V), lambda i: (i, 0, 0, 0))
    return pl.pallas_call(
        functools.partial(_hgrn_sample_kernel, G=Ls),
        grid=(Bs // Bb,),
        in_specs=[col(0), col(1), col(2), col(3), pl.BlockSpec((1, W), lambda i: (0, 0)), st],
        out_specs=[pl.BlockSpec((rows, W), lambda i: (i, 0)), st],
        out_shape=[
            jax.ShapeDtypeStruct((Bs * Ls, HG_VAL_DIM), BF16),
            jax.ShapeDtypeStruct((Bs, HG_HEADS, HG_EXPAND, HG_HEAD_V), F32),
        ],
        scratch_shapes=[pltpu.VMEM((rows, W), F32), pltpu.VMEM((3, Ls, LANES), F32)],
        compiler_params=_params(("parallel",)),
        name="hgrn_sample",
    )(P, P, P, P, gain, state)


def _ret_chunk(q, k, v, states, dm_ref, qw_ref, kw_ref, sd_ref):
    outs, new_states = [], []
    qb, kb, vb = q.astype(BF16), k.astype(BF16), v.astype(BF16)
    for h in range(RET_HEADS):
        qs = slice(h * RET_HEAD_QK, (h + 1) * RET_HEAD_QK)
        vs = slice(h * RET_HEAD_V, (h + 1) * RET_HEAD_V)
        att = _dot_nt(qb[:, qs], kb[:, qs]) * dm_ref[h]
        o = _dot(att.astype(BF16), vb[:, vs]) + _dot((q[:, qs] * qw_ref[h]).astype(BF16), states[h].astype(BF16))
        new_states.append(sd_ref[h, 0:1, 0:1] * states[h] + _dot_tn((k[:, qs] * kw_ref[h]).astype(BF16), vb[:, vs]))
        outs.append(o)
    return jnp.concatenate(outs, axis=1), new_states


def _ret_epilogue(o_scr, gate_ref, o_ref):
    for h in range(RET_HEADS):
        vs = slice(h * RET_HEAD_V, (h + 1) * RET_HEAD_V)
        o_ref[:, vs] = (_rms(o_scr[:, vs]) * gate_ref[:, vs]).astype(o_ref.dtype)


def _ret_prompt_kernel(q_ref, k_ref, v_ref, gate_ref, dm_ref, qw_ref, kw_ref, sd_ref,
                       o_ref, st_ref, s_scr, o_scr):
    l = pl.program_id(1)
    Lb = q_ref.shape[0]
    C = dm_ref.shape[1]

    @pl.when(l == 0)
    def _():
        s_scr[...] = jnp.zeros_like(s_scr)

    def body(ci, carry):
        rows = pl.ds(pl.multiple_of(ci * C, C), C)
        o, new = _ret_chunk(q_ref[rows, :], k_ref[rows, :], v_ref[rows, :],
                            [s_scr[h] for h in range(RET_HEADS)], dm_ref, qw_ref, kw_ref, sd_ref)
        o_scr[rows, :] = o
        for h in range(RET_HEADS):
            s_scr[h] = new[h]
        return carry

    lax.fori_loop(0, Lb // C, body, 0)
    _ret_epilogue(o_scr, gate_ref, o_ref)

    @pl.when(l == pl.num_programs(1) - 1)
    def _():
        st_ref[...] = s_scr[...]


def _ret_tables(C):
    log_gamma = jnp.log(1.0 - jnp.exp2(-5.0 - jnp.arange(RET_HEADS, dtype=F32)))
    idx = jnp.arange(C, dtype=F32)
    rel = idx[:, None] - idx[None, :]
    tri = jnp.tril(jnp.ones((C, C), dtype=bool))
    dmat = jnp.exp(jnp.where(tri[None], log_gamma[:, None, None] * rel[None], -jnp.inf))
    qw = jnp.exp(log_gamma[:, None] * (idx[None, :] + 1.0))[..., None]
    kw = jnp.exp(log_gamma[:, None] * (C - 1.0 - idx[None, :]))[..., None]
    sdec = jnp.exp(log_gamma * C)[:, None, None]
    bc = lambda a: jnp.broadcast_to(a, (RET_HEADS, a.shape[1], LANES))
    return dmat, bc(qw), bc(kw), jnp.broadcast_to(sdec, (RET_HEADS, SUBLANES, LANES))


_QK0 = (2 * HG_KEY_DIM + 2 * HG_VAL_DIM) // RET_QK_DIM
_V0 = (2 * HG_KEY_DIM + 2 * HG_VAL_DIM + 2 * RET_QK_DIM) // RET_V_DIM


def _full3(a):
    return pl.BlockSpec(a.shape, lambda *_: (0, 0, 0))


def _ret_prompt(P, B, L, Lb):
    C = math.gcd(L, CHUNK)
    nl = L // Lb
    tabs = _ret_tables(C)
    st_shape = (RET_HEADS, RET_HEAD_QK, RET_HEAD_V)
    return pl.pallas_call(
        _ret_prompt_kernel,
        grid=(B, nl),
        in_specs=[
            pl.BlockSpec((Lb, RET_QK_DIM), lambda b, l: (b * nl + l, _QK0)),
            pl.BlockSpec((Lb, RET_QK_DIM), lambda b, l: (b * nl + l, _QK0 + 1)),
            pl.BlockSpec((Lb, RET_V_DIM), lambda b, l: (b * nl + l, _V0)),
            pl.BlockSpec((Lb, RET_V_DIM), lambda b, l: (b * nl + l, _V0 + 1)),
        ] + [_full3(t) for t in tabs],
        out_specs=[
            pl.BlockSpec((Lb, RET_V_DIM), lambda b, l: (b * nl + l, 0)),
            pl.BlockSpec((None,) + st_shape, lambda b, l: (b, 0, 0, 0)),
        ],
        out_shape=[
            jax.ShapeDtypeStruct((B * L, RET_V_DIM), BF16),
            jax.ShapeDtypeStruct((B,) + st_shape, F32),
        ],
        scratch_shapes=[pltpu.VMEM(st_shape, F32), pltpu.VMEM((Lb, RET_V_DIM), F32)],
        compiler_params=_params(("parallel", "arbitrary")),
        name="ret_prompt",
    )(P, P, P, P, *tabs)


def _ret_sample_kernel(q_ref, k_ref, v_ref, gate_ref, s_ref, dm_ref, qw_ref, kw_ref, sd_ref,
                       o_ref, so_ref, o_scr, *, C):
    Bb = s_ref.shape[0]

    def body(bb, carry):
        rows = pl.ds(pl.multiple_of(bb * C, C), C)
        o, new = _ret_chunk(q_ref[rows, :], k_ref[rows, :], v_ref[rows, :],
                            [s_ref[bb, h] for h in range(RET_HEADS)], dm_ref, qw_ref, kw_ref, sd_ref)
        o_scr[rows, :] = o
        for h in range(RET_HEADS):
            so_ref[bb, h] = new[h]
        return carry

    lax.fori_loop(0, Bb, body, 0)
    _ret_epilogue(o_scr, gate_ref, o_ref)


def _ret_sample(P, state, Bs, Ls, Bb, row0):
    tabs = _ret_tables(Ls)
    rows = Bb * Ls
    blk0 = row0 // rows
    st = pl.BlockSpec((Bb, RET_HEADS, RET_HEAD_QK, RET_HEAD_V), lambda i: (i, 0, 0, 0))
    return pl.pallas_call(
        functools.partial(_ret_sample_kernel, C=Ls),
        grid=(Bs // Bb,),
        in_specs=[
            pl.BlockSpec((rows, RET_QK_DIM), lambda i: (blk0 + i, _QK0)),
            pl.BlockSpec((rows, RET_QK_DIM), lambda i: (blk0 + i, _QK0 + 1)),
            pl.BlockSpec((rows, RET_V_DIM), lambda i: (blk0 + i, _V0)),
            pl.BlockSpec((rows, RET_V_DIM), lambda i: (blk0 + i, _V0 + 1)),
            st,
        ] + [_full3(t) for t in tabs],
        out_specs=[pl.BlockSpec((rows, RET_V_DIM), lambda i: (i, 0)), st],
        out_shape=[
            jax.ShapeDtypeStruct((Bs * Ls, RET_V_DIM), BF16),
            jax.ShapeDtypeStruct((Bs, RET_HEADS, RET_HEAD_QK, RET_HEAD_V), F32),
        ],
        scratch_shapes=[pltpu.VMEM((rows, RET_V_DIM), F32)],
        compiler_params=_params(("parallel",)),
        name="ret_sample",
    )(P, P, P, P, state, *tabs)


def _split_dot(a, w_hi, w_lo):
    a_hi = a.astype(BF16)
    a_lo = (a - a_hi.astype(F32)).astype(BF16)
    return _dot(a_hi, w_hi) + (_dot(a_hi, w_lo) + _dot(a_lo, w_hi))


def _route(logits):
    lane = lax.broadcasted_iota(I32, logits.shape, 1)
    neg = jnp.float32(-jnp.inf)
    big = jnp.int32(LANES)
    gmask = lane < N_GROUPS
    gl = jnp.where(gmask, logits, neg)
    gmax = jnp.max(gl, axis=1, keepdims=True)
    gu = jnp.where(gmask, jnp.exp(gl - gmax), 0.0)
    gp = gu / jnp.sum(gu, axis=1, keepdims=True)
    g_w = jnp.max(gp, axis=1, keepdims=True)
    g_i = jnp.min(jnp.where(gmask & (gp == g_w), lane, big), axis=1, keepdims=True)
    lo = ROUTE_EXPERT_LANE + g_i * EXPERTS_PER_GROUP
    emask = (lane >= lo) & (lane < lo + EXPERTS_PER_GROUP)
    el = jnp.where(emask, logits, neg)
    e1 = jnp.max(el, axis=1, keepdims=True)
    i1 = jnp.min(jnp.where(emask & (el == e1), lane, big), axis=1, keepdims=True)
    el2 = jnp.where(lane == i1, neg, el)
    e2 = jnp.max(el2, axis=1, keepdims=True)
    i2 = jnp.min(jnp.where(emask & (lane != i1) & (el2 == e2), lane, big), axis=1, keepdims=True)
    u2 = jnp.exp(e2 - e1)
    den = 1.0 + u2
    w1 = (1.0 / den) * g_w
    w2 = (u2 / den) * g_w
    id1 = (i1 - ROUTE_EXPERT_LANE).astype(F32)
    id2 = (i2 - ROUTE_EXPERT_LANE).astype(F32)
    return (jnp.where(lane == LANE_E1, id1, 0.0) + jnp.where(lane == LANE_E2, id2, 0.0)
            + jnp.where(lane == LANE_W1, w1, 0.0) + jnp.where(lane == LANE_W2, w2, 0.0))


def _mix_kernel(ohg_p_ref, ohg_s_ref, or_p_ref, or_s_ref, ga0_ref, ga1_ref, gb0_ref, gb1_ref, xp_ref, xs_ref,
                wbh_ref, wbr_ref, wout_ref, nffn_ref, wr_hi_ref, wr_lo_ref, br_ref,
                h_ref, hn_ref, route_ref, *, n_prompt_tiles):
    i = pl.program_id(0)
    tm = h_ref.shape[0]
    is_prompt = i < n_prompt_tiles
    p_hg = _dot(jnp.where(is_prompt, ohg_p_ref[...], ohg_s_ref[...]), wbh_ref[...])
    p_r = _dot(jnp.where(is_prompt, or_p_ref[...], or_s_ref[...]), wbr_ref[...])
    half = D_MODEL // 2
    m0 = ga0_ref[...] * p_hg[:, :half] + gb0_ref[...] * p_r[:, :half]
    m1 = ga1_ref[...] * p_hg[:, half:] + gb1_ref[...] * p_r[:, half:]
    mixed = jnp.concatenate([m0, m1], axis=1).astype(BF16)
    x = jnp.where(is_prompt, xp_ref[...], xs_ref[...])
    h = x + _dot(mixed, wout_ref[...])
    h_ref[...] = h
    hn = _rms(h) * nffn_ref[...]
    for s in range(ROW_GROUP):
        hn_ref[pl.ds(s, tm, stride=ROW_GROUP), :] = hn[:, s * LANES:(s + 1) * LANES]
    route_ref[...] = _route(_split_dot(hn, wr_hi_ref[...], wr_lo_ref[...]) + br_ref[...])


def _mix(o_hg_p, o_hg_s, o_r_p, o_r_s, P, xp, xs, wbh, wbr, wout, norm_ffn, wr_hi, wr_lo, br, tm):
    npt, nst = xp.shape[0] // tm, xs.shape[0] // tm
    T = (npt + nst) * tm
    half = D_MODEL // 2
    ga = (IN_WIDTH - 2 * D_MODEL) // half

    def pcol(c):
        return pl.BlockSpec((tm, half), lambda i, c=c: (i, c))

    def const(a):
        return pl.BlockSpec(a.shape, lambda i: (0, 0), pipeline_mode=pl.Buffered(1))

    def prompt(w):
        return pl.BlockSpec((tm, w), lambda i: (jnp.minimum(i, npt - 1), 0))

    def sample(w):
        return pl.BlockSpec((tm, w), lambda i: (jnp.maximum(i - npt, 0), 0))

    return pl.pallas_call(
        functools.partial(_mix_kernel, n_prompt_tiles=npt),
        grid=(npt + nst,),
        in_specs=[
            prompt(half), sample(half), prompt(half), sample(half),
            pcol(ga), pcol(ga + 1), pcol(ga + 2), pcol(ga + 3),
            prompt(D_MODEL), sample(D_MODEL),
            const(wbh), const(wbr), const(wout), const(norm_ffn), const(wr_hi), const(wr_lo), const(br),
        ],
        out_specs=[
            pl.BlockSpec((tm, D_MODEL), lambda i: (i, 0)),
            pl.BlockSpec((tm * ROW_GROUP, LANES), lambda i: (i, 0)),
            pl.BlockSpec((tm, LANES), lambda i: (i, 0)),
        ],
        out_shape=[
            jax.ShapeDtypeStruct((T, D_MODEL), F32),
            jax.ShapeDtypeStruct((T * ROW_GROUP, LANES), F32),
            jax.ShapeDtypeStruct((T, LANES), F32),
        ],
        compiler_params=_params(("parallel",)),
        name="mix",
    )(o_hg_p, o_hg_s, o_r_p, o_r_s, P, P, P, P, xp, xs, wbh, wbr, wout, norm_ffn, wr_hi, wr_lo, br)


def _plan_kernel(route_ref, pos_ref, te_ref, tail_ref, cum_scr, *, tb):
    T = route_ref.shape[0]
    lane_t = lax.broadcasted_iota(I32, (tb, LANES), 1).astype(F32)
    r = lax.broadcasted_iota(I32, (tb, tb), 0)
    c = lax.broadcasted_iota(I32, (tb, tb), 1)
    strict_lower = jnp.where(r > c, 1.0, 0.0).astype(BF16)

    def onehots(rows):
        rt = route_ref[rows, :]
        a1 = jnp.where(lane_t == rt[:, LANE_E1:LANE_E1 + 1], 1.0, 0.0)
        a2 = jnp.where(lane_t == rt[:, LANE_E2:LANE_E2 + 1], 1.0, 0.0)
        return a1, a2

    def rank_body(bi, carry):
        rows = pl.ds(pl.multiple_of(bi * tb, tb), tb)
        a1, a2 = onehots(rows)
        m = a1 + a2
        cum_scr[rows, :] = _dot(strict_lower, m.astype(BF16)) + carry
        return carry + jnp.sum(m, axis=0, keepdims=True)

    counts = lax.fori_loop(0, T // tb, rank_body, jnp.zeros((1, LANES), F32))
    ntiles = jnp.floor((counts + (MOE_TILE - 1)) * (1.0 / MOE_TILE))
    rr = lax.broadcasted_iota(I32, (LANES, LANES), 0)
    cc = lax.broadcasted_iota(I32, (LANES, LANES), 1)
    upper = jnp.where(rr < cc, 1.0, 0.0).astype(BF16)
    tile0 = _dot(jnp.broadcast_to(ntiles, (SUBLANES, LANES)).astype(BF16), upper)[0:1, :]
    slot0 = tile0 * MOE_TILE

    def pos_body(bi, carry):
        rows = pl.ds(pl.multiple_of(bi * tb, tb), tb)
        a1, a2 = onehots(rows)
        base = cum_scr[rows, :] + slot0
        p1 = jnp.sum(a1 * base, axis=1, keepdims=True)
        p2 = jnp.sum(a2 * base, axis=1, keepdims=True)
        pos_ref[rows, :] = (jnp.where(lane_t == 0.0, p1, 0.0) + jnp.where(lane_t == 1.0, p2, 0.0)).astype(I32)
        return carry

    lax.fori_loop(0, T // tb, pos_body, 0)
    tend = tile0 + ntiles
    lane = lax.broadcasted_iota(I32, (LANES, LANES), 1)
    tile_i = lax.broadcasted_iota(I32, (LANES, LANES), 0).astype(F32)
    is_expert = lane < N_EXPERTS
    te = jnp.sum(jnp.where(is_expert & (jnp.broadcast_to(tend, (LANES, LANES)) <= tile_i), 1.0, 0.0),
                 axis=1, keepdims=True)
    te_ref[...] = jnp.broadcast_to(jnp.minimum(te, N_EXPERTS - 1.0), (LANES, LANES)).astype(I32)
    lane1 = lax.broadcasted_iota(I32, (1, LANES), 1)
    total = jnp.sum(jnp.where(lane1 < N_EXPERTS, ntiles, 0.0), axis=1, keepdims=True)
    tail = jnp.where(ntiles > 0.0, (tend - 1.0) * MOE_TILE, -1.0)
    tail = jnp.where(lane1 < N_EXPERTS, tail, jnp.where(lane1 == N_EXPERTS, total, 0.0))
    tail_ref[...] = jnp.broadcast_to(tail, (SUBLANES, LANES)).astype(I32)


def _plan(route, tb):
    T = route.shape[0]
    return pl.pallas_call(
        functools.partial(_plan_kernel, tb=tb),
        grid=(1,),
        in_specs=[pl.BlockSpec((T, LANES), lambda i: (0, 0))],
        out_specs=[
            pl.BlockSpec((T, LANES), lambda i: (0, 0)),
            pl.BlockSpec((LANES, LANES), lambda i: (0, 0)),
            pl.BlockSpec((SUBLANES, LANES), lambda i: (0, 0)),
        ],
        out_shape=[
            jax.ShapeDtypeStruct((T, LANES), I32),
            jax.ShapeDtypeStruct((LANES, LANES), I32),
            jax.ShapeDtypeStruct((SUBLANES, LANES), I32),
        ],
        scratch_shapes=[pltpu.VMEM((T, LANES), F32)],
        compiler_params=_params(("arbitrary",)),
        name="moe_plan",
    )(route)


def _row_group(ref, idx):
    return ref.at[pl.ds(pl.multiple_of(idx * ROW_GROUP, ROW_GROUP), ROW_GROUP), :]


def _dispatch_kernel(pos_ref, tail_ref, hn_ref, xs_ref, zero_scr, zsem, sem):
    i = pl.program_id(0)
    tm = hn_ref.shape[0] // ROW_GROUP
    tile_rows = MOE_TILE * ROW_GROUP

    @pl.when(i == 0)
    def _():
        zero_scr[...] = jnp.zeros_like(zero_scr)
        n_used = tail_ref[N_EXPERTS]

        def zero_copy(first_slot):
            start = pl.multiple_of(first_slot * ROW_GROUP, tile_rows)
            return pltpu.make_async_copy(zero_scr, xs_ref.at[pl.ds(start, tile_rows), :], zsem)

        def tails(fn):
            def body(e, carry):
                pl.when(tail_ref[e] >= 0)(lambda: fn(zero_copy(jnp.maximum(tail_ref[e], 0))))
                return carry
            lax.fori_loop(0, N_EXPERTS, body, 0)

        def unused(fn):
            def body(t, carry):
                fn(zero_copy(t * MOE_TILE))
                return carry
            lax.fori_loop(n_used, xs_ref.shape[0] // tile_rows, body, 0)

        tails(lambda c: c.start())
        unused(lambda c: c.start())
        tails(lambda c: c.wait())
        unused(lambda c: c.wait())

    def copy(r, k):
        slot = pos_ref[(i * tm + r) * TOP_K + k]
        return pltpu.make_async_copy(_row_group(hn_ref, r), _row_group(xs_ref, slot), sem)

    def start(r, carry):
        copy(r, 0).start()
        copy(r, 1).start()
        return carry

    def wait(r, carry):
        copy(r, 0).wait()
        copy(r, 1).wait()
        return carry

    lax.fori_loop(0, tm, start, 0, unroll=8)
    lax.fori_loop(0, tm, wait, 0, unroll=8)


def _dispatch(pos, tails, hn_rows, n_slots, tm):
    T = hn_rows.shape[0] // ROW_GROUP
    return pl.pallas_call(
        _dispatch_kernel,
        grid_spec=pltpu.PrefetchScalarGridSpec(
            num_scalar_prefetch=2,
            grid=(T // tm,),
            in_specs=[pl.BlockSpec((tm * ROW_GROUP, LANES), lambda i, *_: (i, 0))],
            out_specs=pl.BlockSpec(memory_space=pl.ANY),
            scratch_shapes=[pltpu.VMEM((MOE_TILE * ROW_GROUP, LANES), F32),
                            pltpu.SemaphoreType.DMA, pltpu.SemaphoreType.DMA],
        ),
        out_shape=jax.ShapeDtypeStruct((n_slots * ROW_GROUP, LANES), F32),
        compiler_params=_params(("arbitrary",)),
        name="moe_dispatch",
    )(pos, tails, hn_rows)


def _experts_kernel(te_ref, nt_ref, xs_ref, wg_hbm, wu_hbm, wd_hbm, ys_ref,
                    wg_buf, wu_buf, wd_buf, wg_scr, wu_scr, wd_scr, slot_ref, sem):
    i = pl.program_id(0)
    nt = nt_ref[0]
    M = MOE_TILE
    e = te_ref[i]

    def fetch(expert, slot):
        return [pltpu.make_async_copy(hbm.at[expert], buf.at[slot], sem.at[slot, n])
                for n, (hbm, buf) in enumerate(((wg_hbm, wg_buf), (wu_hbm, wu_buf), (wd_hbm, wd_buf)))]

    @pl.when(i == 0)
    def _():
        slot_ref[0] = 0
        for c in fetch(e, 0):
            c.start()

    @pl.when((i < nt) & ((i == 0) | (te_ref[jnp.maximum(i - 1, 0)] != e)))
    def _():
        slot = slot_ref[0]
        for c in fetch(e, slot):
            c.wait()
        wg_scr[...] = wg_buf[slot].astype(BF16)
        wu_scr[...] = wu_buf[slot].astype(BF16)
        wd_scr[...] = wd_buf[slot].astype(BF16)
        nxt = lax.while_loop(lambda j: (j < nt) & (te_ref[jnp.minimum(j, nt - 1)] == e), lambda j: j + 1, i + 1)

        @pl.when(nxt < nt)
        def _():
            for c in fetch(te_ref[jnp.minimum(nxt, nt - 1)], 1 - slot):
                c.start()

        slot_ref[0] = 1 - slot

    @pl.when(i < nt)
    def _():
        x = jnp.concatenate([xs_ref[pl.ds(s, M, stride=ROW_GROUP), :] for s in range(ROW_GROUP)],
                            axis=1).astype(BF16)
        hmid = (_silu(_dot(x, wg_scr[...])) * _dot(x, wu_scr[...])).astype(BF16)
        y = _dot(hmid, wd_scr[...])
        for s in range(ROW_GROUP):
            ys_ref[pl.ds(s, M, stride=ROW_GROUP), :] = y[:, s * LANES:(s + 1) * LANES]


def _experts(te, nt, xs_rows, w_ge, w_ue, w_de, n_tiles):
    M = MOE_TILE
    rows = pl.BlockSpec((M * ROW_GROUP, LANES), lambda i, te_ref, nt_ref: (jnp.minimum(i, nt_ref[0] - 1), 0))
    hbm = pl.BlockSpec(memory_space=pl.ANY)
    up, down = (D_MODEL, EXPERT_FF), (EXPERT_FF, D_MODEL)
    return pl.pallas_call(
        _experts_kernel,
        grid_spec=pltpu.PrefetchScalarGridSpec(
            num_scalar_prefetch=2,
            grid=(n_tiles,),
            in_specs=[rows, hbm, hbm, hbm],
            out_specs=rows,
            scratch_shapes=[pltpu.VMEM((2,) + up, F32), pltpu.VMEM((2,) + up, F32), pltpu.VMEM((2,) + down, F32),
                            pltpu.VMEM(up, BF16), pltpu.VMEM(up, BF16), pltpu.VMEM(down, BF16),
                            pltpu.SMEM((1,), I32), pltpu.SemaphoreType.DMA((2, 3))],
        ),
        out_shape=jax.ShapeDtypeStruct(xs_rows.shape, F32),
        input_output_aliases={2: 0},
        compiler_params=_params(("arbitrary",)),
        name="moe_experts",
    )(te, nt, xs_rows, w_ge, w_ue, w_de)


def _final_kernel(pos_ref, h_ref, route_ref, gain_ref, ys_ref, y_ref, buf, sem, *, tile0):
    i = pl.program_id(0)
    n = pl.num_programs(0)
    tm = h_ref.shape[0]

    def copy(step, r, k, slot_buf):
        slot = pos_ref[((tile0 + step) * tm + r) * TOP_K + k]
        return pltpu.make_async_copy(_row_group(ys_ref, slot), _row_group(buf.at[slot_buf], r * TOP_K + k),
                                     sem.at[slot_buf])

    def start_all(step, slot_buf):
        def body(r, carry):
            copy(step, r, 0, slot_buf).start()
            copy(step, r, 1, slot_buf).start()
            return carry
        lax.fori_loop(0, tm, body, 0, unroll=8)

    def wait_all(step, slot_buf):
        def body(r, carry):
            copy(step, r, 0, slot_buf).wait()
            copy(step, r, 1, slot_buf).wait()
            return carry
        lax.fori_loop(0, tm, body, 0, unroll=8)

    cur = i % 2
    pl.when(i == 0)(lambda: start_all(0, 0))
    pl.when(i + 1 < n)(lambda: start_all(i + 1, 1 - cur))
    wait_all(i, cur)

    rt = route_ref[...]
    w1 = rt[:, LANE_W1:LANE_W1 + 1]
    w2 = rt[:, LANE_W2:LANE_W2 + 1]
    ssq = jnp.zeros((tm, 1), F32)
    for s in range(ROW_GROUP):
        cols = slice(s * LANES, (s + 1) * LANES)
        o1 = buf[cur, pl.ds(s, tm, stride=TOP_K * ROW_GROUP), :]
        o2 = buf[cur, pl.ds(ROW_GROUP + s, tm, stride=TOP_K * ROW_GROUP), :]
        v = h_ref[:, cols] + (w1 * o1 + w2 * o2)
        y_ref[:, cols] = v
        ssq = ssq + jnp.sum(v * v, axis=1, keepdims=True)
    y_ref[...] = y_ref[...] * lax.rsqrt(ssq * (1.0 / D_MODEL) + EPS) * gain_ref[...]


def _final(pos, h, route, gain, ys_rows, tile0, n_tiles, tm):
    return pl.pallas_call(
        functools.partial(_final_kernel, tile0=tile0),
        grid_spec=pltpu.PrefetchScalarGridSpec(
            num_scalar_prefetch=1,
            grid=(n_tiles,),
            in_specs=[
                pl.BlockSpec((tm, D_MODEL), lambda i, *_: (tile0 + i, 0)),
                pl.BlockSpec((tm, LANES), lambda i, *_: (tile0 + i, 0)),
                pl.BlockSpec((1, D_MODEL), lambda i, *_: (0, 0)),
                pl.BlockSpec(memory_space=pl.ANY),
            ],
            out_specs=pl.BlockSpec((tm, D_MODEL), lambda i, *_: (i, 0)),
            scratch_shapes=[pltpu.VMEM((2, tm * TOP_K * ROW_GROUP, LANES), F32), pltpu.SemaphoreType.DMA((2,))],
        ),
        out_shape=jax.ShapeDtypeStruct((n_tiles * tm, D_MODEL), F32),
        compiler_params=_params(("arbitrary",)),
        name="final",
    )(pos, h, route, gain, ys_rows)


def _rope_tables(pos):
    half = RET_HEAD_QK // 2
    inv = 1.0 / (ROPE_BASE ** jnp.linspace(0.0, 1.0, half, dtype=F32))
    ang = jnp.repeat(pos[:, None] * inv[None, :], 2, axis=-1)
    cos, sin = jnp.cos(ang), jnp.sin(ang)
    even = (jnp.arange(RET_HEAD_QK) % 2 == 0)[None, :]
    sa = jnp.where(even, -sin, 0.0)
    sb = jnp.where(even, 0.0, sin)
    return cos, sa, sb


def _tile_sizes(Tp, Ts, Lp):
    g = math.gcd(Tp, Ts)
    return dict(
        xnorm=math.gcd(g, 512),
        proj=(Tp + Ts) // 4,
        mix=math.gcd(g, 256),
        plan=math.gcd(Tp + Ts, 256),
        dispatch=math.gcd(g, 256),
        final=math.gcd(g, 256),
        scan=math.gcd(Lp, 512),
    )


def kernel(x_prompt, x_sample, state_hgrn, state_ret, norm_mix, w_in, hg_lb_logits, hg_norm,
           w_branch_hg, w_branch_ret, b_gate, w_out, norm_ffn, w_router_group, b_router_group,
           w_router_expert, b_router_expert, w_expert_gate, w_expert_up, w_expert_down, norm_final):
    assert w_in.shape == (1, D_MODEL, IN_WIDTH), "single-layer trunk only"
    Bp, Lp, _ = x_prompt.shape
    Bs, Ls, _ = x_sample.shape
    Tp, Ts = Bp * Lp, Bs * Ls
    T = Tp + Ts
    ts = _tile_sizes(Tp, Ts, Lp)
    xp = x_prompt.reshape(Tp, D_MODEL)
    xs = x_sample.reshape(Ts, D_MODEL)

    lower = jnp.cumsum(jax.nn.softmax(hg_lb_logits.astype(F32), axis=0), axis=0)[0]
    act = _act_coefficients(lower, b_gate[0])
    pos_all = jnp.concatenate([jnp.tile(jnp.arange(Lp, dtype=F32), Bp),
                               jnp.tile(jnp.arange(Ls, dtype=F32) + jnp.float32(PAST_LEN), Bs)])
    xn = _xnorm(xp, xs, norm_mix, ts["xnorm"])
    P = _in_proj(xn, w_in[0], act, _rope_tables(pos_all), ts["proj"])

    bb = math.gcd(Bs, 8)
    o_hg_p, shp = _hgrn_prompt(P, hg_norm, Bp, Lp, ts["scan"])
    o_hg_s, shs = _hgrn_sample(P, hg_norm, state_hgrn[0].astype(F32), Bs, Ls, bb, Tp)
    o_r_p, srp = _ret_prompt(P, Bp, Lp, ts["scan"])
    o_r_s, srs = _ret_sample(P, state_ret[0].astype(F32), Bs, Ls, bb, Tp)

    wbh = w_branch_hg[0].astype(BF16)
    wbr = w_branch_ret[0].astype(BF16)
    wout = w_out[0].astype(BF16)
    pad = LANES - N_GROUPS - N_EXPERTS
    wr = jnp.concatenate([w_router_group[0], w_router_expert[0], jnp.zeros((D_MODEL, pad), F32)], axis=1)
    wr_hi = wr.astype(BF16)
    wr_lo = (wr - wr_hi.astype(F32)).astype(BF16)
    br = jnp.concatenate([b_router_group[0], b_router_expert[0], jnp.zeros((pad,), F32)])[None, :]
    h, hn_rows, route = _mix(o_hg_p, o_hg_s, o_r_p, o_r_s, P, xp, xs, wbh, wbr, wout, norm_ffn,
                             wr_hi, wr_lo, br, ts["mix"])

    n_tiles = -(-T * TOP_K // MOE_TILE) + N_EXPERTS
    assert n_tiles <= LANES
    pos_slab, te_slab, tail_slab = _plan(route, ts["plan"])
    pos = pos_slab[:, :TOP_K].reshape(-1)
    te = te_slab[:n_tiles, 0]
    tails = tail_slab[0, :N_EXPERTS + 1]
    nt = tail_slab[0, N_EXPERTS:N_EXPERTS + 1]
    xs_rows = _dispatch(pos, tails, hn_rows, n_tiles * MOE_TILE, ts["dispatch"])
    ys_rows = _experts(te, nt, xs_rows, w_expert_gate[0], w_expert_up[0], w_expert_down[0], n_tiles)

    tf = ts["final"]
    gain = norm_final[None, :]
    y_p = _final(pos, h, route, gain, ys_rows, 0, Tp // tf, tf)
    y_s = _final(pos, h, route, gain, ys_rows, Tp // tf, Ts // tf, tf)
    return (y_p.reshape(Bp, Lp, D_MODEL), y_s.reshape(Bs, Ls, D_MODEL),
            shp[None], srp[None], shs[None], srs[None])
```

```python
import functools
import math

import jax
import jax.numpy as jnp
from jax import lax
from jax.experimental import pallas as pl
from jax.experimental.pallas import tpu as pltpu

F32 = jnp.float32
BF16 = jnp.bfloat16
I32 = jnp.int32

D_MODEL = 2048
PAST_LEN = 16384
HG_HEAD_V = 128
HG_EXPAND = 128
HG_VAL_DIM = D_MODEL // 2
HG_HEADS = HG_VAL_DIM // HG_HEAD_V
HG_KEY_DIM = HG_HEADS * HG_EXPAND
RET_HEAD_V = 256
RET_HEAD_QK = 128
RET_V_DIM = D_MODEL // 2
RET_HEADS = RET_V_DIM // RET_HEAD_V
RET_QK_DIM = RET_HEADS * RET_HEAD_QK
IN_WIDTH = 2 * HG_KEY_DIM + 2 * HG_VAL_DIM + 2 * RET_QK_DIM + 2 * RET_V_DIM + 2 * D_MODEL
CHUNK = 64
SCAN_CHUNK = 128
N_GROUPS = 4
EXPERTS_PER_GROUP = 8
N_EXPERTS = N_GROUPS * EXPERTS_PER_GROUP
TOP_K = 2
EXPERT_FF = D_MODEL // 4
ROPE_BASE = 10000.0
EPS = 1e-6

LANES = 128
SUBLANES = 8
VMEM_LIMIT = 56 * 1024 * 1024
PROJ_MC = 256
PROJ_TN = 512
SEG_HQ, SEG_HF, SEG_HI, SEG_HG, SEG_RQ, SEG_RK, SEG_RV, SEG_RG, SEG_GA = 0, 2, 4, 6, 8, 9, 10, 12, 14
SAFE_EXP_SPAN = 80.0
ROW_GROUP = D_MODEL // LANES
MOE_TILE = 256
LANE_E1, LANE_E2, LANE_W1, LANE_W2 = 0, 1, 2, 3
ROUTE_EXPERT_LANE = N_GROUPS


def _params(sem):
    return pltpu.CompilerParams(dimension_semantics=sem, vmem_limit_bytes=VMEM_LIMIT)


def _silu(z):
    return z * jax.nn.sigmoid(z)


def _dot(a, b):
    return jnp.dot(a, b, preferred_element_type=F32)


def _dot_nt(a, b):
    return lax.dot_general(a, b, (((1,), (1,)), ((), ())), preferred_element_type=F32)


def _dot_tn(a, b):
    return lax.dot_general(a, b, (((0,), (0,)), ((), ())), preferred_element_type=F32)


def _rms(x):
    return x * lax.rsqrt(jnp.mean(x * x, axis=-1, keepdims=True) + EPS)


def _xnorm_kernel(xp_ref, xs_ref, gain_ref, o_ref, *, n_prompt_tiles):
    i = pl.program_id(0)

    def norm(x_ref):
        o_ref[...] = (_rms(x_ref[...]) * gain_ref[...]).astype(BF16)

    pl.when(i < n_prompt_tiles)(lambda: norm(xp_ref))
    pl.when(i >= n_prompt_tiles)(lambda: norm(xs_ref))


def _xnorm(xp, xs, gain, tm):
    npt, nst = xp.shape[0] // tm, xs.shape[0] // tm
    return pl.pallas_call(
        functools.partial(_xnorm_kernel, n_prompt_tiles=npt),
        grid=(npt + nst,),
        in_specs=[
            pl.BlockSpec((tm, D_MODEL), lambda i: (jnp.minimum(i, npt - 1), 0)),
            pl.BlockSpec((tm, D_MODEL), lambda i: (jnp.maximum(i - npt, 0), 0)),
            pl.BlockSpec((1, D_MODEL), lambda i: (0, 0)),
        ],
        out_specs=pl.BlockSpec((tm, D_MODEL), lambda i: (i, 0)),
        out_shape=jax.ShapeDtypeStruct(((npt + nst) * tm, D_MODEL), BF16),
        compiler_params=_params(("parallel",)),
        name="xnorm",
    )(xp, xs, gain)


ACT_ALPHA, ACT_BETA, ACT_GAMMA, ACT_DELTA, ACT_EPS = 0, 1, 2, 3, 4


def _act_coefficients(lower, b_gate):
    z = jnp.zeros((IN_WIDTH,), F32)
    seg = lambda a, b: slice(a * PROJ_TN, b * PROJ_TN)
    alpha = z.at[seg(SEG_HI, SEG_HG)].set(1.0).at[seg(SEG_RQ, SEG_RG)].set(1.0)
    beta = z.at[seg(SEG_GA, IN_WIDTH // PROJ_TN)].set(b_gate)
    gamma = (z.at[seg(SEG_HQ, SEG_HF)].set(HG_EXPAND ** -0.5)
             .at[seg(SEG_HG, SEG_RQ)].set(1.0).at[seg(SEG_RG, SEG_GA)].set(1.0))
    delta = z.at[seg(SEG_HF, SEG_HI)].set(1.0 - lower).at[seg(SEG_GA, IN_WIDTH // PROJ_TN)].set(1.0)
    eps = z.at[seg(SEG_HF, SEG_HI)].set(lower)
    return jnp.stack([alpha, beta, gamma, delta, eps, z, z, z])


def _in_proj_kernel(x_ref, w_ref, act_ref, cos_ref, sa_ref, sb_ref, o_ref):
    j = pl.program_id(1)
    w = w_ref[...].astype(BF16)
    row = lambda r: act_ref[r:r + 1, :]
    tm = x_ref.shape[0]
    mc = math.gcd(tm, PROJ_MC)
    for m in range(0, tm, mc):
        p = _dot(x_ref[m:m + mc, :], w)
        o_ref[m:m + mc, :] = (row(ACT_ALPHA) * p + row(ACT_EPS)
                              + jax.nn.sigmoid(p + row(ACT_BETA)) * (row(ACT_GAMMA) * p + row(ACT_DELTA)))

    @pl.when((j >= SEG_RQ) & (j < SEG_RV))
    def _():
        scale = jnp.where(j == SEG_RK, RET_HEAD_QK ** -0.5, 1.0).astype(F32)
        cos, sa, sb = cos_ref[...], sa_ref[...], sb_ref[...]
        for hh in range(PROJ_TN // LANES):
            cols = slice(hh * LANES, (hh + 1) * LANES)
            xs = o_ref[:, cols]
            r = xs * cos + pltpu.roll(xs, LANES - 1, 1) * sa + pltpu.roll(xs, 1, 1) * sb
            o_ref[:, cols] = r * scale


def _in_proj(xn, w_in, act, tabs, tm):
    T = xn.shape[0]
    once = dict(pipeline_mode=pl.Buffered(1))
    tab = pl.BlockSpec((tm, LANES), lambda i, j: (i, 0), **once)
    return pl.pallas_call(
        _in_proj_kernel,
        grid=(T // tm, IN_WIDTH // PROJ_TN),
        in_specs=[
            pl.BlockSpec((tm, D_MODEL), lambda i, j: (i, 0), **once),
            pl.BlockSpec((D_MODEL, PROJ_TN), lambda i, j: (0, j)),
            pl.BlockSpec((SUBLANES, PROJ_TN), lambda i, j: (0, j)),
            tab, tab, tab,
        ],
        out_specs=pl.BlockSpec((tm, PROJ_TN), lambda i, j: (i, j)),
        out_shape=jax.ShapeDtypeStruct((T, IN_WIDTH), F32),
        compiler_params=_params(("parallel", "arbitrary")),
        name="in_proj",
    )(xn, w_in, act, *tabs)


def _prefix_matrix(C, G):
    r = lax.broadcasted_iota(I32, (C, 3 * C), 0)
    c = lax.broadcasted_iota(I32, (C, 3 * C), 1) % C
    return jnp.where((c <= r) & (c // G == r // G), 1.0, 0.0).astype(BF16)


def _prefix_sum(g, pm):
    g1 = g.astype(BF16)
    r1 = g - g1.astype(F32)
    g2 = r1.astype(BF16)
    g3 = (r1 - g2.astype(F32)).astype(BF16)
    return _dot(pm, jnp.concatenate([g1, g2, g3], axis=0))


def _state_decay_column(d_row):
    r = lax.broadcasted_iota(I32, (LANES, LANES), 0)
    c = lax.broadcasted_iota(I32, (LANES, LANES), 1)
    return jnp.sum(jnp.where(r == c, jnp.broadcast_to(d_row, (LANES, LANES)), 0.0), axis=1, keepdims=True)


def _hgrn_chunk(q, f, g, v, states, pm, tmp_ref, NG, factorised):
    C, W = q.shape
    G, H = C // NG, W // LANES
    hs = [slice(h * LANES, (h + 1) * LANES) for h in range(H)]
    gs = [slice(n * G, (n + 1) * G) for n in range(NG)]
    k = 1.0 - f
    b = _prefix_sum(g, pm)
    b3 = b.reshape(NG, G, W)
    b_end = b3[:, G - 1:G, :]
    q3, k3 = q.reshape(NG, G, W), k.reshape(NG, G, W)
    vb = v.astype(BF16)
    if factorised:
        b_mid = b3[:, G // 2 - 1:G // 2, :]
        qm3 = q3 * jnp.exp(b3 - b_mid)
        km3 = k3 * jnp.exp(b_mid - b3)
        qb = (qm3 * jnp.exp(b_mid)).reshape(C, W)
        kd = (km3 * jnp.exp(b_end - b_mid)).reshape(C, W)
        qm = qm3.reshape(C, W).astype(BF16)
        km = km3.reshape(C, W).astype(BF16)
        row = lax.broadcasted_iota(I32, (C, C), 0)
        col = lax.broadcasted_iota(I32, (C, C), 1)
        amask = (row >= col) & (row // G == col // G)
        o_intra = []
        for h in range(H):
            att = jnp.where(amask, _dot_nt(qm[:, hs[h]], km[:, hs[h]]), 0.0)
            o_intra.append(_dot(att.astype(BF16), vb[:, hs[h]]))
    else:
        qb = (q3 * jnp.exp(b3)).reshape(C, W)
        kd = (k3 * jnp.exp(b_end - b3)).reshape(C, W)
        trow = lax.broadcasted_iota(I32, (G, LANES), 0)
        o_intra = []
        for h in range(H):
            parts = []
            for n in range(NG):
                bh, qh = b[gs[n], hs[h]], q[gs[n], hs[h]]
                tmp_ref[0] = bh
                tmp_ref[1] = k[gs[n], hs[h]]
                tmp_ref[2] = v[gs[n], hs[h]]

                def body(s, acc, bh=bh, qh=qh):
                    bs = tmp_ref[0, pl.ds(s, 1), :]
                    ks = tmp_ref[1, pl.ds(s, 1), :]
                    vs = tmp_ref[2, pl.ds(s, 1), :]
                    w = jnp.where(trow >= s, jnp.exp(jnp.minimum(bh - bs, 0.0)), 0.0)
                    return acc + jnp.sum(qh * ks * w, axis=1, keepdims=True) * vs

                parts.append(lax.fori_loop(0, G, body, jnp.zeros((G, LANES), F32)))
            o_intra.append(parts[0] if NG == 1 else jnp.concatenate(parts, axis=0))
    d = jnp.exp(b_end)
    outs, new_states = [], [[None] * H for _ in range(NG)]
    for h in range(H):
        inter = []
        for n in range(NG):
            S = states[n][h]
            inter.append(_dot(qb[gs[n], hs[h]].astype(BF16), S.astype(BF16)))
            upd = _dot_tn(kd[gs[n], hs[h]].astype(BF16), v[gs[n], hs[h]].astype(BF16))
            new_states[n][h] = _state_decay_column(d[n, :, hs[h]]) * S + upd
        inter = inter[0] if NG == 1 else jnp.concatenate(inter, axis=0)
        outs.append(inter + o_intra[h])
    return jnp.concatenate(outs, axis=1), new_states


def _decay_is_safe(g, G):
    R, W = g.shape
    return jnp.min(jnp.sum(g.reshape(R // G, G, W), axis=1)) >= -SAFE_EXP_SPAN


def _hgrn_epilogue(o_scr, gain_ref, gate_ref, o_ref):
    o_ref[...] = (_rms(o_scr[...]) * gain_ref[...] * gate_ref[...]).astype(o_ref.dtype)


def _hgrn_prompt_kernel(*refs, C, NB):
    ins, (gain_ref,), rest = refs[:4 * NB], refs[4 * NB:4 * NB + 1], refs[4 * NB + 1:]
    o_refs, (st_ref, s_scr, g_scr, o_scr, tmp_ref) = rest[:NB], rest[NB:]
    seq = [ins[4 * n:4 * n + 4] for n in range(NB)]
    l = pl.program_id(0)
    Lb = o_refs[0].shape[0]

    @pl.when(l == 0)
    def _():
        s_scr[...] = jnp.zeros_like(s_scr)

    for n in range(NB):
        g_scr[n] = jnp.log(seq[n][1][...])
    pm = _prefix_matrix(C, C)

    def run(factorised):
        def body(ci, carry):
            rows = pl.ds(pl.multiple_of(ci * C, C), C)
            args = [(seq[n][0][rows, :], seq[n][1][rows, :], g_scr[n, rows, :], seq[n][2][rows, :],
                     [[s_scr[n, h] for h in range(HG_HEADS)]]) for n in range(NB)]
            outs = [_hgrn_chunk(*a, pm, tmp_ref, 1, factorised) for a in args]
            for n, (o, new) in enumerate(outs):
                o_scr[n, rows, :] = o
                for h in range(HG_HEADS):
                    s_scr[n, h] = new[0][h]
            return carry
        lax.fori_loop(0, Lb // C, body, 0)

    safe = _decay_is_safe(g_scr[...].reshape(NB * Lb, -1), C // 2)
    pl.when(safe)(lambda: run(True))
    pl.when(jnp.logical_not(safe))(lambda: run(False))
    for n in range(NB):
        o_refs[n][...] = (_rms(o_scr[n]) * gain_ref[...] * seq[n][3][...]).astype(o_refs[n].dtype)

    @pl.when(l == pl.num_programs(0) - 1)
    def _():
        st_ref[...] = s_scr[...]


def _hgrn_prompt(P, gain, B, L, Lb):
    C = math.gcd(L, SCAN_CHUNK)
    nl = L // Lb
    W = HG_KEY_DIM
    blk = lambda b, seg: pl.BlockSpec((Lb, W), lambda l, b=b, seg=seg: (b * nl + l, seg))
    outs = pl.pallas_call(
        functools.partial(_hgrn_prompt_kernel, C=C, NB=B),
        grid=(nl,),
        in_specs=[blk(b, seg) for b in range(B) for seg in range(4)] + [pl.BlockSpec((1, W), lambda l: (0, 0))],
        out_specs=[pl.BlockSpec((Lb, W), lambda l: (l, 0)) for _ in range(B)]
        + [pl.BlockSpec((B, HG_HEADS, HG_EXPAND, HG_HEAD_V), lambda l: (0, 0, 0, 0))],
        out_shape=[jax.ShapeDtypeStruct((L, HG_VAL_DIM), BF16) for _ in range(B)]
        + [jax.ShapeDtypeStruct((B, HG_HEADS, HG_EXPAND, HG_HEAD_V), F32)],
        scratch_shapes=[pltpu.VMEM((B, HG_HEADS, HG_EXPAND, HG_HEAD_V), F32), pltpu.VMEM((B, Lb, W), F32),
                        pltpu.VMEM((B, Lb, W), F32), pltpu.VMEM((3, C, LANES), F32)],
        compiler_params=_params(("arbitrary",)),
        name="hgrn_prompt",
    )(*([P] * (4 * B)), gain)
    return outs[:B], outs[B]


def _hgrn_sample_kernel(q_ref, f_ref, v_ref, gate_ref, gain_ref, s_ref, o_ref, so_ref, o_scr, tmp_ref, *, G):
    NG = s_ref.shape[0]
    g = jnp.log(f_ref[...])
    pm = _prefix_matrix(NG * G, G)

    def run(factorised):
        states = [[s_ref[n, h] for h in range(HG_HEADS)] for n in range(NG)]
        o, new = _hgrn_chunk(q_ref[...], f_ref[...], g, v_ref[...], states, pm, tmp_ref, NG, factorised)
        o_scr[...] = o
        for n in range(NG):
            for h in range(HG_HEADS):
                so_ref[n, h] = new[n][h]

    safe = _decay_is_safe(g, G // 2)
    pl.when(safe)(lambda: run(True))
    pl.when(jnp.logical_not(safe))(lambda: run(False))
    _hgrn_epilogue(o_scr, gain_ref, gate_ref, o_ref)


def _hgrn_sample(P, gain, state, Bs, Ls, Bb, row0):
    W = HG_KEY_DIM
    rows = Bb * Ls
    blk0 = row0 // rows

    def col(seg):
        return pl.BlockSpec((rows, W), lambda i, seg=seg: (blk0 + i, seg))

    st = pl.BlockSpec((Bb, HG_HEADS, HG_EXPAND, HG_HEAD_V), lambda i: (i, 0, 0, 0))
    return pl.pallas_call(
        functools.partial(_hgrn_sample_kernel, G=Ls),
        grid=(Bs // Bb,),
        in_specs=[col(0), col(1), col(2), col(3), pl.BlockSpec((1, W), lambda i: (0, 0)), st],
        out_specs=[pl.BlockSpec((rows, W), lambda i: (i, 0)), st],
        out_shape=[
            jax.ShapeDtypeStruct((Bs * Ls, HG_VAL_DIM), BF16),
            jax.ShapeDtypeStruct((Bs, HG_HEADS, HG_EXPAND, HG_HEAD_V), F32),
        ],
        scratch_shapes=[pltpu.VMEM((rows, W), F32), pltpu.VMEM((3, Ls, LANES), F32)],
        compiler_params=_params(("parallel",)),
        name="hgrn_sample",
    )(P, P, P, P, gain, state)


def _ret_chunk(q, k, v, states, dm_ref, qw_ref, kw_ref, sd_ref):
    outs, new_states = [], []
    qb, kb, vb = q.astype(BF16), k.astype(BF16), v.astype(BF16)
    for h in range(RET_HEADS):
        qs = slice(h * RET_HEAD_QK, (h + 1) * RET_HEAD_QK)
        vs = slice(h * RET_HEAD_V, (h + 1) * RET_HEAD_V)
        att = _dot_nt(qb[:, qs], kb[:, qs]) * dm_ref[h]
        o = _dot(att.astype(BF16), vb[:, vs]) + _dot((q[:, qs] * qw_ref[h]).astype(BF16), states[h].astype(BF16))
        new_states.append(sd_ref[h, 0:1, 0:1] * states[h] + _dot_tn((k[:, qs] * kw_ref[h]).astype(BF16), vb[:, vs]))
        outs.append(o)
    return jnp.concatenate(outs, axis=1), new_states


def _ret_epilogue(o_scr, gate_ref, o_ref):
    for h in range(RET_HEADS):
        vs = slice(h * RET_HEAD_V, (h + 1) * RET_HEAD_V)
        o_ref[:, vs] = (_rms(o_scr[:, vs]) * gate_ref[:, vs]).astype(o_ref.dtype)


def _ret_prompt_kernel(*refs, NB):
    ins, (dm_ref, qw_ref, kw_ref, sd_ref), rest = refs[:4 * NB], refs[4 * NB:4 * NB + 4], refs[4 * NB + 4:]
    o_refs, (st_ref, s_scr, o_scr) = rest[:NB], rest[NB:]
    seq = [ins[4 * n:4 * n + 4] for n in range(NB)]
    l = pl.program_id(0)
    Lb = o_refs[0].shape[0]
    C = dm_ref.shape[1]

    @pl.when(l == 0)
    def _():
        s_scr[...] = jnp.zeros_like(s_scr)

    def body(ci, carry):
        rows = pl.ds(pl.multiple_of(ci * C, C), C)
        args = [(seq[n][0][rows, :], seq[n][1][rows, :], seq[n][2][rows, :],
                 [s_scr[n, h] for h in range(RET_HEADS)]) for n in range(NB)]
        outs = [_ret_chunk(*a, dm_ref, qw_ref, kw_ref, sd_ref) for a in args]
        for n, (o, new) in enumerate(outs):
            o_scr[n, rows, :] = o
            for h in range(RET_HEADS):
                s_scr[n, h] = new[h]
        return carry

    lax.fori_loop(0, Lb // C, body, 0)
    for n in range(NB):
        _ret_epilogue(o_scr.at[n], seq[n][3], o_refs[n])

    @pl.when(l == pl.num_programs(0) - 1)
    def _():
        st_ref[...] = s_scr[...]


def _ret_tables(C):
    log_gamma = jnp.log(1.0 - jnp.exp2(-5.0 - jnp.arange(RET_HEADS, dtype=F32)))
    idx = jnp.arange(C, dtype=F32)
    rel = idx[:, None] - idx[None, :]
    tri = jnp.tril(jnp.ones((C, C), dtype=bool))
    dmat = jnp.exp(jnp.where(tri[None], log_gamma[:, None, None] * rel[None], -jnp.inf))
    qw = jnp.exp(log_gamma[:, None] * (idx[None, :] + 1.0))[..., None]
    kw = jnp.exp(log_gamma[:, None] * (C - 1.0 - idx[None, :]))[..., None]
    sdec = jnp.exp(log_gamma * C)[:, None, None]
    bc = lambda a: jnp.broadcast_to(a, (RET_HEADS, a.shape[1], LANES))
    return dmat, bc(qw), bc(kw), jnp.broadcast_to(sdec, (RET_HEADS, SUBLANES, LANES))


_QK0 = (2 * HG_KEY_DIM + 2 * HG_VAL_DIM) // RET_QK_DIM
_V0 = (2 * HG_KEY_DIM + 2 * HG_VAL_DIM + 2 * RET_QK_DIM) // RET_V_DIM


def _full3(a):
    return pl.BlockSpec(a.shape, lambda *_: (0, 0, 0))


def _ret_prompt(P, B, L, Lb):
    C = math.gcd(L, SCAN_CHUNK)
    nl = L // Lb
    tabs = _ret_tables(C)
    st_shape = (B, RET_HEADS, RET_HEAD_QK, RET_HEAD_V)
    blk = lambda b, w, c: pl.BlockSpec((Lb, w), lambda l, b=b, c=c: (b * nl + l, c))
    per_seq = lambda b: [blk(b, RET_QK_DIM, _QK0), blk(b, RET_QK_DIM, _QK0 + 1),
                         blk(b, RET_V_DIM, _V0), blk(b, RET_V_DIM, _V0 + 1)]
    outs = pl.pallas_call(
        functools.partial(_ret_prompt_kernel, NB=B),
        grid=(nl,),
        in_specs=[spec for b in range(B) for spec in per_seq(b)] + [_full3(t) for t in tabs],
        out_specs=[pl.BlockSpec((Lb, RET_V_DIM), lambda l: (l, 0)) for _ in range(B)]
        + [pl.BlockSpec(st_shape, lambda l: (0, 0, 0, 0))],
        out_shape=[jax.ShapeDtypeStruct((L, RET_V_DIM), BF16) for _ in range(B)]
        + [jax.ShapeDtypeStruct(st_shape, F32)],
        scratch_shapes=[pltpu.VMEM(st_shape, F32), pltpu.VMEM((B, Lb, RET_V_DIM), F32)],
        compiler_params=_params(("arbitrary",)),
        name="ret_prompt",
    )(*([P] * (4 * B)), *tabs)
    return outs[:B], outs[B]


def _ret_sample_kernel(q_ref, k_ref, v_ref, gate_ref, s_ref, dm_ref, qw_ref, kw_ref, sd_ref,
                       o_ref, so_ref, o_scr, *, C):
    Bb = s_ref.shape[0]
    args = [(q_ref[bb * C:(bb + 1) * C, :], k_ref[bb * C:(bb + 1) * C, :], v_ref[bb * C:(bb + 1) * C, :],
             [s_ref[bb, h] for h in range(RET_HEADS)]) for bb in range(Bb)]
    outs = [_ret_chunk(*a, dm_ref, qw_ref, kw_ref, sd_ref) for a in args]
    for bb, (o, new) in enumerate(outs):
        o_scr[bb * C:(bb + 1) * C, :] = o
        for h in range(RET_HEADS):
            so_ref[bb, h] = new[h]
    _ret_epilogue(o_scr, gate_ref, o_ref)


def _ret_sample(P, state, Bs, Ls, Bb, row0):
    tabs = _ret_tables(Ls)
    rows = Bb * Ls
    blk0 = row0 // rows
    st = pl.BlockSpec((Bb, RET_HEADS, RET_HEAD_QK, RET_HEAD_V), lambda i: (i, 0, 0, 0))
    return pl.pallas_call(
        functools.partial(_ret_sample_kernel, C=Ls),
        grid=(Bs // Bb,),
        in_specs=[
            pl.BlockSpec((rows, RET_QK_DIM), lambda i: (blk0 + i, _QK0)),
            pl.BlockSpec((rows, RET_QK_DIM), lambda i: (blk0 + i, _QK0 + 1)),
            pl.BlockSpec((rows, RET_V_DIM), lambda i: (blk0 + i, _V0)),
            pl.BlockSpec((rows, RET_V_DIM), lambda i: (blk0 + i, _V0 + 1)),
            st,
        ] + [_full3(t) for t in tabs],
        out_specs=[pl.BlockSpec((rows, RET_V_DIM), lambda i: (i, 0)), st],
        out_shape=[
            jax.ShapeDtypeStruct((Bs * Ls, RET_V_DIM), BF16),
            jax.ShapeDtypeStruct((Bs, RET_HEADS, RET_HEAD_QK, RET_HEAD_V), F32),
        ],
        scratch_shapes=[pltpu.VMEM((rows, RET_V_DIM), F32)],
        compiler_params=_params(("parallel",)),
        name="ret_sample",
    )(P, P, P, P, state, *tabs)


def _split_dot(a, w_hi, w_lo):
    a_hi = a.astype(BF16)
    a_lo = (a - a_hi.astype(F32)).astype(BF16)
    return _dot(a_hi, w_hi) + (_dot(a_hi, w_lo) + _dot(a_lo, w_hi))


def _route(logits):
    lane = lax.broadcasted_iota(I32, logits.shape, 1)
    neg = jnp.float32(-jnp.inf)
    big = jnp.int32(LANES)
    gmask = lane < N_GROUPS
    gl = jnp.where(gmask, logits, neg)
    gmax = jnp.max(gl, axis=1, keepdims=True)
    gu = jnp.where(gmask, jnp.exp(gl - gmax), 0.0)
    gp = gu / jnp.sum(gu, axis=1, keepdims=True)
    g_w = jnp.max(gp, axis=1, keepdims=True)
    g_i = jnp.min(jnp.where(gmask & (gp == g_w), lane, big), axis=1, keepdims=True)
    lo = ROUTE_EXPERT_LANE + g_i * EXPERTS_PER_GROUP
    emask = (lane >= lo) & (lane < lo + EXPERTS_PER_GROUP)
    el = jnp.where(emask, logits, neg)
    e1 = jnp.max(el, axis=1, keepdims=True)
    i1 = jnp.min(jnp.where(emask & (el == e1), lane, big), axis=1, keepdims=True)
    el2 = jnp.where(lane == i1, neg, el)
    e2 = jnp.max(el2, axis=1, keepdims=True)
    i2 = jnp.min(jnp.where(emask & (lane != i1) & (el2 == e2), lane, big), axis=1, keepdims=True)
    u2 = jnp.exp(e2 - e1)
    den = 1.0 + u2
    w1 = (1.0 / den) * g_w
    w2 = (u2 / den) * g_w
    id1 = (i1 - ROUTE_EXPERT_LANE).astype(F32)
    id2 = (i2 - ROUTE_EXPERT_LANE).astype(F32)
    return (jnp.where(lane == LANE_E1, id1, 0.0) + jnp.where(lane == LANE_E2, id2, 0.0)
            + jnp.where(lane == LANE_W1, w1, 0.0) + jnp.where(lane == LANE_W2, w2, 0.0))


def _pick(i, bounds, refs):
    val = refs[-1][...]
    for n in range(len(refs) - 2, -1, -1):
        val = jnp.where(i < bounds[n + 1], refs[n][...], val)
    return val


def _mix_kernel(*refs, bounds):
    ns = len(bounds) - 1
    ohg_refs, or_refs, refs = refs[:ns], refs[ns:2 * ns], refs[2 * ns:]
    (ga0_ref, ga1_ref, gb0_ref, gb1_ref, xp_ref, xs_ref, wbh_ref, wbr_ref, wout_ref, nffn_ref,
     wr_hi_ref, wr_lo_ref, br_ref, h_ref, hn_ref, route_ref) = refs
    i = pl.program_id(0)
    tm = h_ref.shape[0]
    is_prompt = i < bounds[-2]
    p_hg = _dot(_pick(i, bounds, ohg_refs), wbh_ref[...])
    p_r = _dot(_pick(i, bounds, or_refs), wbr_ref[...])
    half = D_MODEL // 2
    m0 = ga0_ref[...] * p_hg[:, :half] + gb0_ref[...] * p_r[:, :half]
    m1 = ga1_ref[...] * p_hg[:, half:] + gb1_ref[...] * p_r[:, half:]
    mixed = jnp.concatenate([m0, m1], axis=1).astype(BF16)
    x = jnp.where(is_prompt, xp_ref[...], xs_ref[...])
    h = x + _dot(mixed, wout_ref[...])
    h_ref[...] = h
    hn = _rms(h) * nffn_ref[...]
    for s in range(ROW_GROUP):
        hn_ref[pl.ds(s, tm, stride=ROW_GROUP), :] = hn[:, s * LANES:(s + 1) * LANES]
    route_ref[...] = _route(_split_dot(hn, wr_hi_ref[...], wr_lo_ref[...]) + br_ref[...])


def _mix(o_hg, o_r, P, xp, xs, wbh, wbr, wout, norm_ffn, wr_hi, wr_lo, br, tm):
    npt, nst = xp.shape[0] // tm, xs.shape[0] // tm
    bounds = [0]
    for a in o_hg:
        bounds.append(bounds[-1] + a.shape[0] // tm)
    T = (npt + nst) * tm
    half = D_MODEL // 2
    ga = (IN_WIDTH - 2 * D_MODEL) // half

    def pcol(c):
        return pl.BlockSpec((tm, half), lambda i, c=c: (i, c))

    def const(a):
        return pl.BlockSpec(a.shape, lambda i: (0, 0), pipeline_mode=pl.Buffered(1))

    def prompt(w):
        return pl.BlockSpec((tm, w), lambda i: (jnp.minimum(i, npt - 1), 0))

    def sample(w):
        return pl.BlockSpec((tm, w), lambda i: (jnp.maximum(i - npt, 0), 0))

    def source(n):
        lo, hi = bounds[n], bounds[n + 1]
        return pl.BlockSpec((tm, half), lambda i: (jnp.clip(i - lo, 0, hi - lo - 1), 0))

    sources = [source(n) for n in range(len(o_hg))]
    return pl.pallas_call(
        functools.partial(_mix_kernel, bounds=tuple(bounds)),
        grid=(npt + nst,),
        in_specs=sources + sources + [
            pcol(ga), pcol(ga + 1), pcol(ga + 2), pcol(ga + 3),
            prompt(D_MODEL), sample(D_MODEL),
            const(wbh), const(wbr), const(wout), const(norm_ffn), const(wr_hi), const(wr_lo), const(br),
        ],
        out_specs=[
            pl.BlockSpec((tm, D_MODEL), lambda i: (i, 0)),
            pl.BlockSpec((tm * ROW_GROUP, LANES), lambda i: (i, 0)),
            pl.BlockSpec((tm, LANES), lambda i: (i, 0)),
        ],
        out_shape=[
            jax.ShapeDtypeStruct((T, D_MODEL), F32),
            jax.ShapeDtypeStruct((T * ROW_GROUP, LANES), F32),
            jax.ShapeDtypeStruct((T, LANES), F32),
        ],
        compiler_params=_params(("parallel",)),
        name="mix",
    )(*o_hg, *o_r, P, P, P, P, xp, xs, wbh, wbr, wout, norm_ffn, wr_hi, wr_lo, br)


def _plan_kernel(route_ref, pos_ref, te_ref, tail_ref, cum_scr, *, tb):
    T = route_ref.shape[0]
    lane_t = lax.broadcasted_iota(I32, (tb, LANES), 1).astype(F32)
    r = lax.broadcasted_iota(I32, (tb, tb), 0)
    c = lax.broadcasted_iota(I32, (tb, tb), 1)
    strict_lower = jnp.where(r > c, 1.0, 0.0).astype(BF16)

    def onehots(rows):
        rt = route_ref[rows, :]
        a1 = jnp.where(lane_t == rt[:, LANE_E1:LANE_E1 + 1], 1.0, 0.0)
        a2 = jnp.where(lane_t == rt[:, LANE_E2:LANE_E2 + 1], 1.0, 0.0)
        return a1, a2

    def rank_body(bi, carry):
        rows = pl.ds(pl.multiple_of(bi * tb, tb), tb)
        a1, a2 = onehots(rows)
        m = a1 + a2
        cum_scr[rows, :] = _dot(strict_lower, m.astype(BF16)) + carry
        return carry + jnp.sum(m, axis=0, keepdims=True)

    counts = lax.fori_loop(0, T // tb, rank_body, jnp.zeros((1, LANES), F32))
    ntiles = jnp.floor((counts + (MOE_TILE - 1)) * (1.0 / MOE_TILE))
    rr = lax.broadcasted_iota(I32, (LANES, LANES), 0)
    cc = lax.broadcasted_iota(I32, (LANES, LANES), 1)
    upper = jnp.where(rr < cc, 1.0, 0.0).astype(BF16)
    tile0 = _dot(jnp.broadcast_to(ntiles, (SUBLANES, LANES)).astype(BF16), upper)[0:1, :]
    slot0 = tile0 * MOE_TILE

    def pos_body(bi, carry):
        rows = pl.ds(pl.multiple_of(bi * tb, tb), tb)
        a1, a2 = onehots(rows)
        base = cum_scr[rows, :] + slot0
        p1 = jnp.sum(a1 * base, axis=1, keepdims=True)
        p2 = jnp.sum(a2 * base, axis=1, keepdims=True)
        pos_ref[rows, :] = (jnp.where(lane_t == 0.0, p1, 0.0) + jnp.where(lane_t == 1.0, p2, 0.0)).astype(I32)
        return carry

    lax.fori_loop(0, T // tb, pos_body, 0)
    tend = tile0 + ntiles
    lane = lax.broadcasted_iota(I32, (LANES, LANES), 1)
    tile_i = lax.broadcasted_iota(I32, (LANES, LANES), 0).astype(F32)
    is_expert = lane < N_EXPERTS
    te = jnp.sum(jnp.where(is_expert & (jnp.broadcast_to(tend, (LANES, LANES)) <= tile_i), 1.0, 0.0),
                 axis=1, keepdims=True)
    te_ref[...] = jnp.broadcast_to(jnp.minimum(te, N_EXPERTS - 1.0), (LANES, LANES)).astype(I32)
    lane1 = lax.broadcasted_iota(I32, (1, LANES), 1)
    total = jnp.sum(jnp.where(lane1 < N_EXPERTS, ntiles, 0.0), axis=1, keepdims=True)
    tail = jnp.where(ntiles > 0.0, (tend - 1.0) * MOE_TILE, -1.0)
    tail = jnp.where(lane1 < N_EXPERTS, tail, jnp.where(lane1 == N_EXPERTS, total, 0.0))
    tail_ref[...] = jnp.broadcast_to(tail, (SUBLANES, LANES)).astype(I32)


def _plan(route, tb):
    T = route.shape[0]
    return pl.pallas_call(
        functools.partial(_plan_kernel, tb=tb),
        grid=(1,),
        in_specs=[pl.BlockSpec((T, LANES), lambda i: (0, 0))],
        out_specs=[
            pl.BlockSpec((T, LANES), lambda i: (0, 0)),
            pl.BlockSpec((LANES, LANES), lambda i: (0, 0)),
            pl.BlockSpec((SUBLANES, LANES), lambda i: (0, 0)),
        ],
        out_shape=[
            jax.ShapeDtypeStruct((T, LANES), I32),
            jax.ShapeDtypeStruct((LANES, LANES), I32),
            jax.ShapeDtypeStruct((SUBLANES, LANES), I32),
        ],
        scratch_shapes=[pltpu.VMEM((T, LANES), F32)],
        compiler_params=_params(("arbitrary",)),
        name="moe_plan",
    )(route)


def _row_group(ref, idx):
    return ref.at[pl.ds(pl.multiple_of(idx * ROW_GROUP, ROW_GROUP), ROW_GROUP), :]


def _dispatch_kernel(pos_ref, tail_ref, hn_ref, xs_ref, zero_scr, zsem, sem):
    i = pl.program_id(0)
    tm = hn_ref.shape[0] // ROW_GROUP
    tile_rows = MOE_TILE * ROW_GROUP

    @pl.when(i == 0)
    def _():
        zero_scr[...] = jnp.zeros_like(zero_scr)
        n_used = tail_ref[N_EXPERTS]

        def zero_copy(first_slot):
            start = pl.multiple_of(first_slot * ROW_GROUP, tile_rows)
            return pltpu.make_async_copy(zero_scr, xs_ref.at[pl.ds(start, tile_rows), :], zsem)

        def tails(fn):
            def body(e, carry):
                pl.when(tail_ref[e] >= 0)(lambda: fn(zero_copy(jnp.maximum(tail_ref[e], 0))))
                return carry
            lax.fori_loop(0, N_EXPERTS, body, 0)

        def unused(fn):
            def body(t, carry):
                fn(zero_copy(t * MOE_TILE))
                return carry
            lax.fori_loop(n_used, xs_ref.shape[0] // tile_rows, body, 0)

        tails(lambda c: c.start())
        unused(lambda c: c.start())
        tails(lambda c: c.wait())
        unused(lambda c: c.wait())

    def copy(r, k):
        slot = pos_ref[(i * tm + r) * TOP_K + k]
        return pltpu.make_async_copy(_row_group(hn_ref, r), _row_group(xs_ref, slot), sem)

    def start(r, carry):
        copy(r, 0).start()
        copy(r, 1).start()
        return carry

    lax.fori_loop(0, tm, start, 0, unroll=8)
    for _ in range(TOP_K):
        pltpu.make_async_copy(hn_ref, xs_ref.at[pl.ds(0, tm * ROW_GROUP), :], sem).wait()


def _dispatch(pos, tails, hn_rows, n_slots, tm):
    T = hn_rows.shape[0] // ROW_GROUP
    return pl.pallas_call(
        _dispatch_kernel,
        grid_spec=pltpu.PrefetchScalarGridSpec(
            num_scalar_prefetch=2,
            grid=(T // tm,),
            in_specs=[pl.BlockSpec((tm * ROW_GROUP, LANES), lambda i, *_: (i, 0))],
            out_specs=pl.BlockSpec(memory_space=pl.ANY),
            scratch_shapes=[pltpu.VMEM((MOE_TILE * ROW_GROUP, LANES), F32),
                            pltpu.SemaphoreType.DMA, pltpu.SemaphoreType.DMA],
        ),
        out_shape=jax.ShapeDtypeStruct((n_slots * ROW_GROUP, LANES), F32),
        compiler_params=_params(("arbitrary",)),
        name="moe_dispatch",
    )(pos, tails, hn_rows)


def _experts_kernel(te_ref, nt_ref, xs_ref, wg_hbm, wu_hbm, wd_hbm, ys_ref,
                    wg_buf, wu_buf, wd_buf, wg_scr, wu_scr, wd_scr, slot_ref, sem):
    i = pl.program_id(0)
    nt = nt_ref[0]
    M = MOE_TILE
    e = te_ref[i]

    def fetch(expert, slot):
        return [pltpu.make_async_copy(hbm.at[expert], buf.at[slot], sem.at[slot, n])
                for n, (hbm, buf) in enumerate(((wg_hbm, wg_buf), (wu_hbm, wu_buf), (wd_hbm, wd_buf)))]

    @pl.when(i == 0)
    def _():
        slot_ref[0] = 0
        for c in fetch(e, 0):
            c.start()

    @pl.when((i < nt) & ((i == 0) | (te_ref[jnp.maximum(i - 1, 0)] != e)))
    def _():
        slot = slot_ref[0]
        for c in fetch(e, slot):
            c.wait()
        wg_scr[...] = wg_buf[slot].astype(BF16)
        wu_scr[...] = wu_buf[slot].astype(BF16)
        wd_scr[...] = wd_buf[slot].astype(BF16)
        nxt = lax.while_loop(lambda j: (j < nt) & (te_ref[jnp.minimum(j, nt - 1)] == e), lambda j: j + 1, i + 1)

        @pl.when(nxt < nt)
        def _():
            for c in fetch(te_ref[jnp.minimum(nxt, nt - 1)], 1 - slot):
                c.start()

        slot_ref[0] = 1 - slot

    @pl.when(i < nt)
    def _():
        x = jnp.concatenate([xs_ref[pl.ds(s, M, stride=ROW_GROUP), :] for s in range(ROW_GROUP)],
                            axis=1).astype(BF16)
        hmid = (_silu(_dot(x, wg_scr[...])) * _dot(x, wu_scr[...])).astype(BF16)
        y = _dot(hmid, wd_scr[...])
        for s in range(ROW_GROUP):
            ys_ref[pl.ds(s, M, stride=ROW_GROUP), :] = y[:, s * LANES:(s + 1) * LANES]


def _experts(te, nt, xs_rows, w_ge, w_ue, w_de, n_tiles):
    M = MOE_TILE
    rows = pl.BlockSpec((M * ROW_GROUP, LANES), lambda i, te_ref, nt_ref: (jnp.minimum(i, nt_ref[0] - 1), 0))
    hbm = pl.BlockSpec(memory_space=pl.ANY)
    up, down = (D_MODEL, EXPERT_FF), (EXPERT_FF, D_MODEL)
    return pl.pallas_call(
        _experts_kernel,
        grid_spec=pltpu.PrefetchScalarGridSpec(
            num_scalar_prefetch=2,
            grid=(n_tiles,),
            in_specs=[rows, hbm, hbm, hbm],
            out_specs=rows,
            scratch_shapes=[pltpu.VMEM((2,) + up, F32), pltpu.VMEM((2,) + up, F32), pltpu.VMEM((2,) + down, F32),
                            pltpu.VMEM(up, BF16), pltpu.VMEM(up, BF16), pltpu.VMEM(down, BF16),
                            pltpu.SMEM((1,), I32), pltpu.SemaphoreType.DMA((2, 3))],
        ),
        out_shape=jax.ShapeDtypeStruct(xs_rows.shape, F32),
        input_output_aliases={2: 0},
        compiler_params=_params(("arbitrary",)),
        name="moe_experts",
    )(te, nt, xs_rows, w_ge, w_ue, w_de)


def _final_kernel(pos_ref, h_ref, route_ref, gain_ref, ys_ref, y_ref, buf, sem, *, tile0):
    i = pl.program_id(0)
    n = pl.num_programs(0)
    tm = h_ref.shape[0]

    def copy(step, r, k, slot_buf):
        slot = pos_ref[((tile0 + step) * tm + r) * TOP_K + k]
        return pltpu.make_async_copy(_row_group(ys_ref, slot), _row_group(buf.at[slot_buf], r * TOP_K + k),
                                     sem.at[slot_buf])

    def start_all(step, slot_buf):
        def body(r, carry):
            copy(step, r, 0, slot_buf).start()
            copy(step, r, 1, slot_buf).start()
            return carry
        lax.fori_loop(0, tm, body, 0, unroll=8)

    cur = i % 2
    pl.when(i == 0)(lambda: start_all(0, 0))
    pl.when(i + 1 < n)(lambda: start_all(i + 1, 1 - cur))
    pltpu.make_async_copy(ys_ref.at[pl.ds(0, buf.shape[1]), :], buf.at[cur], sem.at[cur]).wait()

    rt = route_ref[...]
    w1 = rt[:, LANE_W1:LANE_W1 + 1]
    w2 = rt[:, LANE_W2:LANE_W2 + 1]
    ssq = jnp.zeros((tm, 1), F32)
    for s in range(ROW_GROUP):
        cols = slice(s * LANES, (s + 1) * LANES)
        o1 = buf[cur, pl.ds(s, tm, stride=TOP_K * ROW_GROUP), :]
        o2 = buf[cur, pl.ds(ROW_GROUP + s, tm, stride=TOP_K * ROW_GROUP), :]
        v = h_ref[:, cols] + (w1 * o1 + w2 * o2)
        y_ref[:, cols] = v
        ssq = ssq + jnp.sum(v * v, axis=1, keepdims=True)
    y_ref[...] = y_ref[...] * lax.rsqrt(ssq * (1.0 / D_MODEL) + EPS) * gain_ref[...]


def _final(pos, h, route, gain, ys_rows, tile0, n_tiles, tm):
    return pl.pallas_call(
        functools.partial(_final_kernel, tile0=tile0),
        grid_spec=pltpu.PrefetchScalarGridSpec(
            num_scalar_prefetch=1,
            grid=(n_tiles,),
            in_specs=[
                pl.BlockSpec((tm, D_MODEL), lambda i, *_: (tile0 + i, 0)),
                pl.BlockSpec((tm, LANES), lambda i, *_: (tile0 + i, 0)),
                pl.BlockSpec((1, D_MODEL), lambda i, *_: (0, 0)),
                pl.BlockSpec(memory_space=pl.ANY),
            ],
            out_specs=pl.BlockSpec((tm, D_MODEL), lambda i, *_: (i, 0)),
            scratch_shapes=[pltpu.VMEM((2, tm * TOP_K * ROW_GROUP, LANES), F32), pltpu.SemaphoreType.DMA((2,))],
        ),
        out_shape=jax.ShapeDtypeStruct((n_tiles * tm, D_MODEL), F32),
        compiler_params=_params(("arbitrary",)),
        name="final",
    )(pos, h, route, gain, ys_rows)


def _rope_tables(pos):
    half = RET_HEAD_QK // 2
    inv = 1.0 / (ROPE_BASE ** jnp.linspace(0.0, 1.0, half, dtype=F32))
    ang = jnp.repeat(pos[:, None] * inv[None, :], 2, axis=-1)
    cos, sin = jnp.cos(ang), jnp.sin(ang)
    even = (jnp.arange(RET_HEAD_QK) % 2 == 0)[None, :]
    sa = jnp.where(even, -sin, 0.0)
    sb = jnp.where(even, 0.0, sin)
    return cos, sa, sb


def _tile_sizes(Tp, Ts, Lp):
    g = math.gcd(Tp, Ts)
    return dict(
        xnorm=math.gcd(g, 512),
        proj=(Tp + Ts) // 4,
        mix=math.gcd(g, 256),
        plan=math.gcd(Tp + Ts, 256),
        dispatch=math.gcd(g, 256),
        final=math.gcd(g, 256),
        scan=math.gcd(Lp, 128),
    )


def kernel(x_prompt, x_sample, state_hgrn, state_ret, norm_mix, w_in, hg_lb_logits, hg_norm,
           w_branch_hg, w_branch_ret, b_gate, w_out, norm_ffn, w_router_group, b_router_group,
           w_router_expert, b_router_expert, w_expert_gate, w_expert_up, w_expert_down, norm_final):
    assert w_in.shape == (1, D_MODEL, IN_WIDTH), "single-layer trunk only"
    Bp, Lp, _ = x_prompt.shape
    Bs, Ls, _ = x_sample.shape
    Tp, Ts = Bp * Lp, Bs * Ls
    T = Tp + Ts
    ts = _tile_sizes(Tp, Ts, Lp)
    xp = x_prompt.reshape(Tp, D_MODEL)
    xs = x_sample.reshape(Ts, D_MODEL)

    lower = jnp.cumsum(jax.nn.softmax(hg_lb_logits.astype(F32), axis=0), axis=0)[0]
    act = _act_coefficients(lower, b_gate[0])
    pos_all = jnp.concatenate([jnp.tile(jnp.arange(Lp, dtype=F32), Bp),
                               jnp.tile(jnp.arange(Ls, dtype=F32) + jnp.float32(PAST_LEN), Bs)])
    xn = _xnorm(xp, xs, norm_mix, ts["xnorm"])
    P = _in_proj(xn, w_in[0], act, _rope_tables(pos_all), ts["proj"])

    bb = math.gcd(Bs, 8)
    o_hg_p, shp = _hgrn_prompt(P, hg_norm, Bp, Lp, ts["scan"])
    o_hg_s, shs = _hgrn_sample(P, hg_norm, state_hgrn[0].astype(F32), Bs, Ls, bb, Tp)
    o_r_p, srp = _ret_prompt(P, Bp, Lp, ts["scan"])
    o_r_s, srs = _ret_sample(P, state_ret[0].astype(F32), Bs, Ls, bb, Tp)

    wbh = w_branch_hg[0].astype(BF16)
    wbr = w_branch_ret[0].astype(BF16)
    wout = w_out[0].astype(BF16)
    pad = LANES - N_GROUPS - N_EXPERTS
    wr = jnp.concatenate([w_router_group[0], w_router_expert[0], jnp.zeros((D_MODEL, pad), F32)], axis=1)
    wr_hi = wr.astype(BF16)
    wr_lo = (wr - wr_hi.astype(F32)).astype(BF16)
    br = jnp.concatenate([b_router_group[0], b_router_expert[0], jnp.zeros((pad,), F32)])[None, :]
    h, hn_rows, route = _mix(list(o_hg_p) + [o_hg_s], list(o_r_p) + [o_r_s], P, xp, xs, wbh, wbr, wout,
                             norm_ffn, wr_hi, wr_lo, br, ts["mix"])

    n_tiles = -(-T * TOP_K // MOE_TILE) + N_EXPERTS
    assert n_tiles <= LANES
    pos_slab, te_slab, tail_slab = _plan(route, ts["plan"])
    pos = pos_slab[:, :TOP_K].reshape(-1)
    te = te_slab[:n_tiles, 0]
    tails = tail_slab[0, :N_EXPERTS + 1]
    nt = tail_slab[0, N_EXPERTS:N_EXPERTS + 1]
    xs_rows = _dispatch(pos, tails, hn_rows, n_tiles * MOE_TILE, ts["dispatch"])
    ys_rows = _experts(te, nt, xs_rows, w_expert_gate[0], w_expert_up[0], w_expert_down[0], n_tiles)

    tf = ts["final"]
    gain = norm_final[None, :]
    y_p = _final(pos, h, route, gain, ys_rows, 0, Tp // tf, tf)
    y_s = _final(pos, h, route, gain, ys_rows, Tp // tf, Ts // tf, tf)
    return (y_p.reshape(Bp, Lp, D_MODEL), y_s.reshape(Bs, Ls, D_MODEL),
            shp[None], srp[None], shs[None], srs[None])
```

```python
import functools
import math

import jax
import jax.numpy as jnp
from jax import lax
from jax.experimental import pallas as pl
from jax.experimental.pallas import tpu as pltpu

F32 = jnp.float32
BF16 = jnp.bfloat16
I32 = jnp.int32

D_MODEL = 2048
PAST_LEN = 16384
HG_HEAD_V = 128
HG_EXPAND = 128
HG_VAL_DIM = D_MODEL // 2
HG_HEADS = HG_VAL_DIM // HG_HEAD_V
HG_KEY_DIM = HG_HEADS * HG_EXPAND
RET_HEAD_V = 256
RET_HEAD_QK = 128
RET_V_DIM = D_MODEL // 2
RET_HEADS = RET_V_DIM // RET_HEAD_V
RET_QK_DIM = RET_HEADS * RET_HEAD_QK
IN_WIDTH = 2 * HG_KEY_DIM + 2 * HG_VAL_DIM + 2 * RET_QK_DIM + 2 * RET_V_DIM + 2 * D_MODEL
CHUNK = 64
SCAN_CHUNK = 128
N_GROUPS = 4
EXPERTS_PER_GROUP = 8
N_EXPERTS = N_GROUPS * EXPERTS_PER_GROUP
TOP_K = 2
EXPERT_FF = D_MODEL // 4
ROPE_BASE = 10000.0
EPS = 1e-6

LANES = 128
SUBLANES = 8
VMEM_LIMIT = 56 * 1024 * 1024
PROJ_MC = 256
PROJ_TN = 512
SEG_HQ, SEG_HF, SEG_HI, SEG_HG, SEG_RQ, SEG_RK, SEG_RV, SEG_RG, SEG_GA = 0, 2, 4, 6, 8, 9, 10, 12, 14
SAFE_EXP_SPAN = 80.0
ROW_GROUP = D_MODEL // LANES
MOE_TILE = 256
LANE_E1, LANE_E2, LANE_W1, LANE_W2 = 0, 1, 2, 3
ROUTE_EXPERT_LANE = N_GROUPS


def _params(sem):
    return pltpu.CompilerParams(dimension_semantics=sem, vmem_limit_bytes=VMEM_LIMIT)


def _silu(z):
    return z * jax.nn.sigmoid(z)


def _dot(a, b):
    return jnp.dot(a, b, preferred_element_type=F32)


def _dot_nt(a, b):
    return lax.dot_general(a, b, (((1,), (1,)), ((), ())), preferred_element_type=F32)


def _dot_tn(a, b):
    return lax.dot_general(a, b, (((0,), (0,)), ((), ())), preferred_element_type=F32)


def _rms(x):
    return x * lax.rsqrt(jnp.mean(x * x, axis=-1, keepdims=True) + EPS)


def _xnorm_kernel(xp_ref, xs_ref, gain_ref, o_ref, *, n_prompt_tiles):
    i = pl.program_id(0)

    def norm(x_ref):
        o_ref[...] = (_rms(x_ref[...]) * gain_ref[...]).astype(BF16)

    pl.when(i < n_prompt_tiles)(lambda: norm(xp_ref))
    pl.when(i >= n_prompt_tiles)(lambda: norm(xs_ref))


def _xnorm(xp, xs, gain, tm):
    npt, nst = xp.shape[0] // tm, xs.shape[0] // tm
    return pl.pallas_call(
        functools.partial(_xnorm_kernel, n_prompt_tiles=npt),
        grid=(npt + nst,),
        in_specs=[
            pl.BlockSpec((tm, D_MODEL), lambda i: (jnp.minimum(i, npt - 1), 0)),
            pl.BlockSpec((tm, D_MODEL), lambda i: (jnp.maximum(i - npt, 0), 0)),
            pl.BlockSpec((1, D_MODEL), lambda i: (0, 0)),
        ],
        out_specs=pl.BlockSpec((tm, D_MODEL), lambda i: (i, 0)),
        out_shape=jax.ShapeDtypeStruct(((npt + nst) * tm, D_MODEL), BF16),
        compiler_params=_params(("parallel",)),
        name="xnorm",
    )(xp, xs, gain)


ACT_ALPHA, ACT_BETA, ACT_GAMMA, ACT_DELTA, ACT_EPS = 0, 1, 2, 3, 4


def _act_coefficients(lower, b_gate):
    z = jnp.zeros((IN_WIDTH,), F32)
    seg = lambda a, b: slice(a * PROJ_TN, b * PROJ_TN)
    alpha = z.at[seg(SEG_HI, SEG_HG)].set(1.0).at[seg(SEG_RQ, SEG_RG)].set(1.0)
    beta = z.at[seg(SEG_GA, IN_WIDTH // PROJ_TN)].set(b_gate)
    gamma = (z.at[seg(SEG_HQ, SEG_HF)].set(HG_EXPAND ** -0.5)
             .at[seg(SEG_HG, SEG_RQ)].set(1.0).at[seg(SEG_RG, SEG_GA)].set(1.0))
    delta = z.at[seg(SEG_HF, SEG_HI)].set(1.0 - lower).at[seg(SEG_GA, IN_WIDTH // PROJ_TN)].set(1.0)
    eps = z.at[seg(SEG_HF, SEG_HI)].set(lower)
    return jnp.stack([alpha, beta, gamma, delta, eps, z, z, z])


def _in_proj_kernel(x_ref, w_ref, act_ref, cos_ref, sa_ref, sb_ref, o_ref):
    j = pl.program_id(1)
    w = w_ref[...].astype(BF16)
    row = lambda r: act_ref[r:r + 1, :]
    tm = x_ref.shape[0]
    mc = math.gcd(tm, PROJ_MC)
    for m in range(0, tm, mc):
        p = _dot(x_ref[m:m + mc, :], w)
        o_ref[m:m + mc, :] = (row(ACT_ALPHA) * p + row(ACT_EPS)
                              + jax.nn.sigmoid(p + row(ACT_BETA)) * (row(ACT_GAMMA) * p + row(ACT_DELTA)))

    @pl.when((j >= SEG_RQ) & (j < SEG_RV))
    def _():
        scale = jnp.where(j == SEG_RK, RET_HEAD_QK ** -0.5, 1.0).astype(F32)
        cos, sa, sb = cos_ref[...], sa_ref[...], sb_ref[...]
        for hh in range(PROJ_TN // LANES):
            cols = slice(hh * LANES, (hh + 1) * LANES)
            xs = o_ref[:, cols]
            r = xs * cos + pltpu.roll(xs, LANES - 1, 1) * sa + pltpu.roll(xs, 1, 1) * sb
            o_ref[:, cols] = r * scale


def _in_proj(xn, w_in, act, tabs, tm):
    T = xn.shape[0]
    once = dict(pipeline_mode=pl.Buffered(1))
    tab = pl.BlockSpec((tm, LANES), lambda i, j: (i, 0), **once)
    return pl.pallas_call(
        _in_proj_kernel,
        grid=(T // tm, IN_WIDTH // PROJ_TN),
        in_specs=[
            pl.BlockSpec((tm, D_MODEL), lambda i, j: (i, 0), **once),
            pl.BlockSpec((D_MODEL, PROJ_TN), lambda i, j: (0, j)),
            pl.BlockSpec((SUBLANES, PROJ_TN), lambda i, j: (0, j)),
            tab, tab, tab,
        ],
        out_specs=pl.BlockSpec((tm, PROJ_TN), lambda i, j: (i, j)),
        out_shape=jax.ShapeDtypeStruct((T, IN_WIDTH), F32),
        compiler_params=_params(("parallel", "arbitrary")),
        name="in_proj",
    )(xn, w_in, act, *tabs)


def _prefix_matrix(C, G):
    r = lax.broadcasted_iota(I32, (C, 3 * C), 0)
    c = lax.broadcasted_iota(I32, (C, 3 * C), 1) % C
    return jnp.where((c <= r) & (c // G == r // G), 1.0, 0.0).astype(BF16)


def _prefix_sum(g, pm):
    g1 = g.astype(BF16)
    r1 = g - g1.astype(F32)
    g2 = r1.astype(BF16)
    g3 = (r1 - g2.astype(F32)).astype(BF16)
    return _dot(pm, jnp.concatenate([g1, g2, g3], axis=0))


def _state_decay_column(d_row):
    r = lax.broadcasted_iota(I32, (LANES, LANES), 0)
    c = lax.broadcasted_iota(I32, (LANES, LANES), 1)
    return jnp.sum(jnp.where(r == c, jnp.broadcast_to(d_row, (LANES, LANES)), 0.0), axis=1, keepdims=True)


def _hgrn_chunk(q, f, g, v, states, pm, tmp_ref, NG, factorised):
    C, W = q.shape
    G, H = C // NG, W // LANES
    hs = [slice(h * LANES, (h + 1) * LANES) for h in range(H)]
    gs = [slice(n * G, (n + 1) * G) for n in range(NG)]
    k = 1.0 - f
    b = _prefix_sum(g, pm)
    b3 = b.reshape(NG, G, W)
    b_end = b3[:, G - 1:G, :]
    q3, k3 = q.reshape(NG, G, W), k.reshape(NG, G, W)
    vb = v.astype(BF16)
    if factorised:
        b_mid = b3[:, G // 2 - 1:G // 2, :]
        qm3 = q3 * jnp.exp(b3 - b_mid)
        km3 = k3 * jnp.exp(b_mid - b3)
        qb = (qm3 * jnp.exp(b_mid)).reshape(C, W)
        kd = (km3 * jnp.exp(b_end - b_mid)).reshape(C, W)
        qm = qm3.reshape(C, W).astype(BF16)
        km = km3.reshape(C, W).astype(BF16)
        row = lax.broadcasted_iota(I32, (C, C), 0)
        col = lax.broadcasted_iota(I32, (C, C), 1)
        amask = (row >= col) & (row // G == col // G)
        o_intra = []
        for h in range(H):
            att = jnp.where(amask, _dot_nt(qm[:, hs[h]], km[:, hs[h]]), 0.0)
            o_intra.append(_dot(att.astype(BF16), vb[:, hs[h]]))
    else:
        qb = (q3 * jnp.exp(b3)).reshape(C, W)
        kd = (k3 * jnp.exp(b_end - b3)).reshape(C, W)
        trow = lax.broadcasted_iota(I32, (G, LANES), 0)
        o_intra = []
        for h in range(H):
            parts = []
            for n in range(NG):
                bh, qh = b[gs[n], hs[h]], q[gs[n], hs[h]]
                tmp_ref[0] = bh
                tmp_ref[1] = k[gs[n], hs[h]]
                tmp_ref[2] = v[gs[n], hs[h]]

                def body(s, acc, bh=bh, qh=qh):
                    bs = tmp_ref[0, pl.ds(s, 1), :]
                    ks = tmp_ref[1, pl.ds(s, 1), :]
                    vs = tmp_ref[2, pl.ds(s, 1), :]
                    w = jnp.where(trow >= s, jnp.exp(jnp.minimum(bh - bs, 0.0)), 0.0)
                    return acc + jnp.sum(qh * ks * w, axis=1, keepdims=True) * vs

                parts.append(lax.fori_loop(0, G, body, jnp.zeros((G, LANES), F32)))
            o_intra.append(parts[0] if NG == 1 else jnp.concatenate(parts, axis=0))
    d = jnp.exp(b_end)
    outs, new_states = [], [[None] * H for _ in range(NG)]
    for h in range(H):
        inter = []
        for n in range(NG):
            S = states[n][h]
            inter.append(_dot(qb[gs[n], hs[h]].astype(BF16), S.astype(BF16)))
            upd = _dot_tn(kd[gs[n], hs[h]].astype(BF16), v[gs[n], hs[h]].astype(BF16))
            new_states[n][h] = _state_decay_column(d[n, :, hs[h]]) * S + upd
        inter = inter[0] if NG == 1 else jnp.concatenate(inter, axis=0)
        outs.append(inter + o_intra[h])
    return jnp.concatenate(outs, axis=1), new_states


def _decay_is_safe(g, G):
    R, W = g.shape
    return jnp.min(jnp.sum(g.reshape(R // G, G, W), axis=1)) >= -SAFE_EXP_SPAN


def _hgrn_epilogue(o_scr, gain_ref, gate_ref, o_ref):
    o_ref[...] = (_rms(o_scr[...]) * gain_ref[...] * gate_ref[...]).astype(o_ref.dtype)


def _hgrn_prompt_kernel(*refs, C, NB):
    ins, (gain_ref,), rest = refs[:4 * NB], refs[4 * NB:4 * NB + 1], refs[4 * NB + 1:]
    o_refs, (st_ref, s_scr, g_scr, o_scr, tmp_ref) = rest[:NB], rest[NB:]
    seq = [ins[4 * n:4 * n + 4] for n in range(NB)]
    l = pl.program_id(0)
    Lb = o_refs[0].shape[0]

    @pl.when(l == 0)
    def _():
        s_scr[...] = jnp.zeros_like(s_scr)

    for n in range(NB):
        g_scr[n] = jnp.log(seq[n][1][...])
    pm = _prefix_matrix(C, C)

    def run(factorised):
        def body(ci, carry):
            rows = pl.ds(pl.multiple_of(ci * C, C), C)
            args = [(seq[n][0][rows, :], seq[n][1][rows, :], g_scr[n, rows, :], seq[n][2][rows, :],
                     [[s_scr[n, h] for h in range(HG_HEADS)]]) for n in range(NB)]
            outs = [_hgrn_chunk(*a, pm, tmp_ref, 1, factorised) for a in args]
            for n, (o, new) in enumerate(outs):
                o_scr[n, rows, :] = o
                for h in range(HG_HEADS):
                    s_scr[n, h] = new[0][h]
            return carry
        lax.fori_loop(0, Lb // C, body, 0)

    safe = _decay_is_safe(g_scr[...].reshape(NB * Lb, -1), C // 2)
    pl.when(safe)(lambda: run(True))
    pl.when(jnp.logical_not(safe))(lambda: run(False))
    for n in range(NB):
        o_refs[n][...] = (_rms(o_scr[n]) * gain_ref[...] * seq[n][3][...]).astype(o_refs[n].dtype)

    @pl.when(l == pl.num_programs(0) - 1)
    def _():
        st_ref[...] = s_scr[...]


def _hgrn_prompt(P, gain, B, L, Lb):
    C = math.gcd(L, SCAN_CHUNK)
    nl = L // Lb
    W = HG_KEY_DIM
    blk = lambda b, seg: pl.BlockSpec((Lb, W), lambda l, b=b, seg=seg: (b * nl + l, seg))
    outs = pl.pallas_call(
        functools.partial(_hgrn_prompt_kernel, C=C, NB=B),
        grid=(nl,),
        in_specs=[blk(b, seg) for b in range(B) for seg in range(4)] + [pl.BlockSpec((1, W), lambda l: (0, 0))],
        out_specs=[pl.BlockSpec((Lb, W), lambda l: (l, 0)) for _ in range(B)]
        + [pl.BlockSpec((B, HG_HEADS, HG_EXPAND, HG_HEAD_V), lambda l: (0, 0, 0, 0))],
        out_shape=[jax.ShapeDtypeStruct((L, HG_VAL_DIM), BF16) for _ in range(B)]
        + [jax.ShapeDtypeStruct((B, HG_HEADS, HG_EXPAND, HG_HEAD_V), F32)],
        scratch_shapes=[pltpu.VMEM((B, HG_HEADS, HG_EXPAND, HG_HEAD_V), F32), pltpu.VMEM((B, Lb, W), F32),
                        pltpu.VMEM((B, Lb, W), F32), pltpu.VMEM((3, C, LANES), F32)],
        compiler_params=_params(("arbitrary",)),
        name="hgrn_prompt",
    )(*([P] * (4 * B)), gain)
    return outs[:B], outs[B]


def _hgrn_sample_kernel(q_ref, f_ref, v_ref, gate_ref, gain_ref, s_ref, o_ref, so_ref, o_scr, tmp_ref, *, G):
    NG = s_ref.shape[0]
    g = jnp.log(f_ref[...])
    pm = _prefix_matrix(NG * G, G)

    def run(factorised):
        states = [[s_ref[n, h] for h in range(HG_HEADS)] for n in range(NG)]
        o, new = _hgrn_chunk(q_ref[...], f_ref[...], g, v_ref[...], states, pm, tmp_ref, NG, factorised)
        o_scr[...] = o
        for n in range(NG):
            for h in range(HG_HEADS):
                so_ref[n, h] = new[n][h]

    safe = _decay_is_safe(g, G // 2)
    pl.when(safe)(lambda: run(True))
    pl.when(jnp.logical_not(safe))(lambda: run(False))
    _hgrn_epilogue(o_scr, gain_ref, gate_ref, o_ref)


def _hgrn_sample(P, gain, state, Bs, Ls, Bb, row0):
    W = HG_KEY_DIM
    rows = Bb * Ls
    blk0 = row0 // rows

    def col(seg):
        return pl.BlockSpec((rows, W), lambda i, seg=seg: (blk0 + i, seg))

    st = pl.BlockSpec((Bb, HG_HEADS, HG_EXPAND, HG_HEAD_V), lambda i: (i, 0, 0, 0))
    return pl.pallas_call(
        functools.partial(_hgrn_sample_kernel, G=Ls),
        grid=(Bs // Bb,),
        in_specs=[col(0), col(1), col(2), col(3), pl.BlockSpec((1, W), lambda i: (0, 0)), st],
        out_specs=[pl.BlockSpec((rows, W), lambda i: (i, 0)), st],
        out_shape=[
            jax.ShapeDtypeStruct((Bs * Ls, HG_VAL_DIM), BF16),
            jax.ShapeDtypeStruct((Bs, HG_HEADS, HG_EXPAND, HG_HEAD_V), F32),
        ],
        scratch_shapes=[pltpu.VMEM((rows, W), F32), pltpu.VMEM((3, Ls, LANES), F32)],
        compiler_params=_params(("parallel",)),
        name="hgrn_sample",
    )(P, P, P, P, gain, state)


def _ret_chunk(q, k, v, states, dm_ref, qw_ref, kw_ref, sd_ref):
    outs, new_states = [], []
    qb, kb, vb = q.astype(BF16), k.astype(BF16), v.astype(BF16)
    for h in range(RET_HEADS):
        qs = slice(h * RET_HEAD_QK, (h + 1) * RET_HEAD_QK)
        vs = slice(h * RET_HEAD_V, (h + 1) * RET_HEAD_V)
        att = _dot_nt(qb[:, qs], kb[:, qs]) * dm_ref[h]
        o = _dot(att.astype(BF16), vb[:, vs]) + _dot((q[:, qs] * qw_ref[h]).astype(BF16), states[h].astype(BF16))
        new_states.append(sd_ref[h, 0:1, 0:1] * states[h] + _dot_tn((k[:, qs] * kw_ref[h]).astype(BF16), vb[:, vs]))
        outs.append(o)
    return jnp.concatenate(outs, axis=1), new_states


def _ret_epilogue(o_scr, gate_ref, o_ref):
    for h in range(RET_HEADS):
        vs = slice(h * RET_HEAD_V, (h + 1) * RET_HEAD_V)
        o_ref[:, vs] = (_rms(o_scr[:, vs]) * gate_ref[:, vs]).astype(o_ref.dtype)


def _ret_prompt_kernel(*refs, NB):
    ins, (dm_ref, qw_ref, kw_ref, sd_ref), rest = refs[:4 * NB], refs[4 * NB:4 * NB + 4], refs[4 * NB + 4:]
    o_refs, (st_ref, s_scr, o_scr) = rest[:NB], rest[NB:]
    seq = [ins[4 * n:4 * n + 4] for n in range(NB)]
    l = pl.program_id(0)
    Lb = o_refs[0].shape[0]
    C = dm_ref.shape[1]

    @pl.when(l == 0)
    def _():
        s_scr[...] = jnp.zeros_like(s_scr)

    def body(ci, carry):
        rows = pl.ds(pl.multiple_of(ci * C, C), C)
        args = [(seq[n][0][rows, :], seq[n][1][rows, :], seq[n][2][rows, :],
                 [s_scr[n, h] for h in range(RET_HEADS)]) for n in range(NB)]
        outs = [_ret_chunk(*a, dm_ref, qw_ref, kw_ref, sd_ref) for a in args]
        for n, (o, new) in enumerate(outs):
            o_scr[n, rows, :] = o
            for h in range(RET_HEADS):
                s_scr[n, h] = new[h]
        return carry

    lax.fori_loop(0, Lb // C, body, 0)
    for n in range(NB):
        _ret_epilogue(o_scr.at[n], seq[n][3], o_refs[n])

    @pl.when(l == pl.num_programs(0) - 1)
    def _():
        st_ref[...] = s_scr[...]


def _ret_tables(C):
    log_gamma = jnp.log(1.0 - jnp.exp2(-5.0 - jnp.arange(RET_HEADS, dtype=F32)))
    idx = jnp.arange(C, dtype=F32)
    rel = idx[:, None] - idx[None, :]
    tri = jnp.tril(jnp.ones((C, C), dtype=bool))
    dmat = jnp.exp(jnp.where(tri[None], log_gamma[:, None, None] * rel[None], -jnp.inf))
    qw = jnp.exp(log_gamma[:, None] * (idx[None, :] + 1.0))[..., None]
    kw = jnp.exp(log_gamma[:, None] * (C - 1.0 - idx[None, :]))[..., None]
    sdec = jnp.exp(log_gamma * C)[:, None, None]
    bc = lambda a: jnp.broadcast_to(a, (RET_HEADS, a.shape[1], LANES))
    return dmat, bc(qw), bc(kw), jnp.broadcast_to(sdec, (RET_HEADS, SUBLANES, LANES))


_QK0 = (2 * HG_KEY_DIM + 2 * HG_VAL_DIM) // RET_QK_DIM
_V0 = (2 * HG_KEY_DIM + 2 * HG_VAL_DIM + 2 * RET_QK_DIM) // RET_V_DIM


def _full3(a):
    return pl.BlockSpec(a.shape, lambda *_: (0, 0, 0))


def _ret_prompt(P, B, L, Lb):
    C = math.gcd(L, SCAN_CHUNK)
    nl = L // Lb
    tabs = _ret_tables(C)
    st_shape = (B, RET_HEADS, RET_HEAD_QK, RET_HEAD_V)
    blk = lambda b, w, c: pl.BlockSpec((Lb, w), lambda l, b=b, c=c: (b * nl + l, c))
    per_seq = lambda b: [blk(b, RET_QK_DIM, _QK0), blk(b, RET_QK_DIM, _QK0 + 1),
                         blk(b, RET_V_DIM, _V0), blk(b, RET_V_DIM, _V0 + 1)]
    outs = pl.pallas_call(
        functools.partial(_ret_prompt_kernel, NB=B),
        grid=(nl,),
        in_specs=[spec for b in range(B) for spec in per_seq(b)] + [_full3(t) for t in tabs],
        out_specs=[pl.BlockSpec((Lb, RET_V_DIM), lambda l: (l, 0)) for _ in range(B)]
        + [pl.BlockSpec(st_shape, lambda l: (0, 0, 0, 0))],
        out_shape=[jax.ShapeDtypeStruct((L, RET_V_DIM), BF16) for _ in range(B)]
        + [jax.ShapeDtypeStruct(st_shape, F32)],
        scratch_shapes=[pltpu.VMEM(st_shape, F32), pltpu.VMEM((B, Lb, RET_V_DIM), F32)],
        compiler_params=_params(("arbitrary",)),
        name="ret_prompt",
    )(*([P] * (4 * B)), *tabs)
    return outs[:B], outs[B]


def _ret_sample_kernel(q_ref, k_ref, v_ref, gate_ref, s_ref, dm_ref, qw_ref, kw_ref, sd_ref,
                       o_ref, so_ref, o_scr, *, C):
    Bb = s_ref.shape[0]
    args = [(q_ref[bb * C:(bb + 1) * C, :], k_ref[bb * C:(bb + 1) * C, :], v_ref[bb * C:(bb + 1) * C, :],
             [s_ref[bb, h] for h in range(RET_HEADS)]) for bb in range(Bb)]
    outs = [_ret_chunk(*a, dm_ref, qw_ref, kw_ref, sd_ref) for a in args]
    for bb, (o, new) in enumerate(outs):
        o_scr[bb * C:(bb + 1) * C, :] = o
        for h in range(RET_HEADS):
            so_ref[bb, h] = new[h]
    _ret_epilogue(o_scr, gate_ref, o_ref)


def _ret_sample(P, state, Bs, Ls, Bb, row0):
    tabs = _ret_tables(Ls)
    rows = Bb * Ls
    blk0 = row0 // rows
    st = pl.BlockSpec((Bb, RET_HEADS, RET_HEAD_QK, RET_HEAD_V), lambda i: (i, 0, 0, 0))
    return pl.pallas_call(
        functools.partial(_ret_sample_kernel, C=Ls),
        grid=(Bs // Bb,),
        in_specs=[
            pl.BlockSpec((rows, RET_QK_DIM), lambda i: (blk0 + i, _QK0)),
            pl.BlockSpec((rows, RET_QK_DIM), lambda i: (blk0 + i, _QK0 + 1)),
            pl.BlockSpec((rows, RET_V_DIM), lambda i: (blk0 + i, _V0)),
            pl.BlockSpec((rows, RET_V_DIM), lambda i: (blk0 + i, _V0 + 1)),
            st,
        ] + [_full3(t) for t in tabs],
        out_specs=[pl.BlockSpec((rows, RET_V_DIM), lambda i: (i, 0)), st],
        out_shape=[
            jax.ShapeDtypeStruct((Bs * Ls, RET_V_DIM), BF16),
            jax.ShapeDtypeStruct((Bs, RET_HEADS, RET_HEAD_QK, RET_HEAD_V), F32),
        ],
        scratch_shapes=[pltpu.VMEM((rows, RET_V_DIM), F32)],
        compiler_params=_params(("parallel",)),
        name="ret_sample",
    )(P, P, P, P, state, *tabs)


def _split_dot(a, w_hi, w_lo):
    a_hi = a.astype(BF16)
    a_lo = (a - a_hi.astype(F32)).astype(BF16)
    return _dot(a_hi, w_hi) + (_dot(a_hi, w_lo) + _dot(a_lo, w_hi))


def _route(logits):
    lane = lax.broadcasted_iota(I32, logits.shape, 1)
    neg = jnp.float32(-jnp.inf)
    big = jnp.int32(LANES)
    gmask = lane < N_GROUPS
    gl = jnp.where(gmask, logits, neg)
    gmax = jnp.max(gl, axis=1, keepdims=True)
    gu = jnp.where(gmask, jnp.exp(gl - gmax), 0.0)
    gp = gu / jnp.sum(gu, axis=1, keepdims=True)
    g_w = jnp.max(gp, axis=1, keepdims=True)
    g_i = jnp.min(jnp.where(gmask & (gp == g_w), lane, big), axis=1, keepdims=True)
    lo = ROUTE_EXPERT_LANE + g_i * EXPERTS_PER_GROUP
    emask = (lane >= lo) & (lane < lo + EXPERTS_PER_GROUP)
    el = jnp.where(emask, logits, neg)
    e1 = jnp.max(el, axis=1, keepdims=True)
    i1 = jnp.min(jnp.where(emask & (el == e1), lane, big), axis=1, keepdims=True)
    el2 = jnp.where(lane == i1, neg, el)
    e2 = jnp.max(el2, axis=1, keepdims=True)
    i2 = jnp.min(jnp.where(emask & (lane != i1) & (el2 == e2), lane, big), axis=1, keepdims=True)
    u2 = jnp.exp(e2 - e1)
    den = 1.0 + u2
    w1 = (1.0 / den) * g_w
    w2 = (u2 / den) * g_w
    id1 = (i1 - ROUTE_EXPERT_LANE).astype(F32)
    id2 = (i2 - ROUTE_EXPERT_LANE).astype(F32)
    return (jnp.where(lane == LANE_E1, id1, 0.0) + jnp.where(lane == LANE_E2, id2, 0.0)
            + jnp.where(lane == LANE_W1, w1, 0.0) + jnp.where(lane == LANE_W2, w2, 0.0))


def _pick(i, bounds, refs):
    val = refs[-1][...]
    for n in range(len(refs) - 2, -1, -1):
        val = jnp.where(i < bounds[n + 1], refs[n][...], val)
    return val


def _mix_kernel(*refs, bounds):
    ns = len(bounds) - 1
    ohg_refs, or_refs, refs = refs[:ns], refs[ns:2 * ns], refs[2 * ns:]
    (ga0_ref, ga1_ref, gb0_ref, gb1_ref, xp_ref, xs_ref, wbh_ref, wbr_ref, wout_ref, nffn_ref,
     wr_hi_ref, wr_lo_ref, br_ref, h_ref, hn_ref, route_ref) = refs
    i = pl.program_id(0)
    tm = h_ref.shape[0]
    is_prompt = i < bounds[-2]
    p_hg = _dot(_pick(i, bounds, ohg_refs), wbh_ref[...])
    p_r = _dot(_pick(i, bounds, or_refs), wbr_ref[...])
    half = D_MODEL // 2
    m0 = ga0_ref[...] * p_hg[:, :half] + gb0_ref[...] * p_r[:, :half]
    m1 = ga1_ref[...] * p_hg[:, half:] + gb1_ref[...] * p_r[:, half:]
    mixed = jnp.concatenate([m0, m1], axis=1).astype(BF16)
    x = jnp.where(is_prompt, xp_ref[...], xs_ref[...])
    h = x + _dot(mixed, wout_ref[...])
    h_ref[...] = h
    hn = _rms(h) * nffn_ref[...]
    for s in range(ROW_GROUP):
        hn_ref[pl.ds(s, tm, stride=ROW_GROUP), :] = hn[:, s * LANES:(s + 1) * LANES]
    route_ref[...] = _route(_split_dot(hn, wr_hi_ref[...], wr_lo_ref[...]) + br_ref[...])


def _mix(o_hg, o_r, P, xp, xs, wbh, wbr, wout, norm_ffn, wr_hi, wr_lo, br, tm):
    npt, nst = xp.shape[0] // tm, xs.shape[0] // tm
    bounds = [0]
    for a in o_hg:
        bounds.append(bounds[-1] + a.shape[0] // tm)
    T = (npt + nst) * tm
    half = D_MODEL // 2
    ga = (IN_WIDTH - 2 * D_MODEL) // half

    def pcol(c):
        return pl.BlockSpec((tm, half), lambda i, c=c: (i, c))

    def const(a):
        return pl.BlockSpec(a.shape, lambda i: (0, 0), pipeline_mode=pl.Buffered(1))

    def prompt(w):
        return pl.BlockSpec((tm, w), lambda i: (jnp.minimum(i, npt - 1), 0))

    def sample(w):
        return pl.BlockSpec((tm, w), lambda i: (jnp.maximum(i - npt, 0), 0))

    def source(n):
        lo, hi = bounds[n], bounds[n + 1]
        return pl.BlockSpec((tm, half), lambda i: (jnp.clip(i - lo, 0, hi - lo - 1), 0))

    sources = [source(n) for n in range(len(o_hg))]
    return pl.pallas_call(
        functools.partial(_mix_kernel, bounds=tuple(bounds)),
        grid=(npt + nst,),
        in_specs=sources + sources + [
            pcol(ga), pcol(ga + 1), pcol(ga + 2), pcol(ga + 3),
            prompt(D_MODEL), sample(D_MODEL),
            const(wbh), const(wbr), const(wout), const(norm_ffn), const(wr_hi), const(wr_lo), const(br),
        ],
        out_specs=[
            pl.BlockSpec((tm, D_MODEL), lambda i: (i, 0)),
            pl.BlockSpec((tm * ROW_GROUP, LANES), lambda i: (i, 0)),
            pl.BlockSpec((tm, LANES), lambda i: (i, 0)),
        ],
        out_shape=[
            jax.ShapeDtypeStruct((T, D_MODEL), F32),
            jax.ShapeDtypeStruct((T * ROW_GROUP, LANES), F32),
            jax.ShapeDtypeStruct((T, LANES), F32),
        ],
        compiler_params=_params(("parallel",)),
        name="mix",
    )(*o_hg, *o_r, P, P, P, P, xp, xs, wbh, wbr, wout, norm_ffn, wr_hi, wr_lo, br)


def _plan_kernel(route_ref, pos_ref, te_ref, tail_ref, cum_scr, *, tb):
    T = route_ref.shape[0]
    lane_t = lax.broadcasted_iota(I32, (tb, LANES), 1).astype(F32)
    r = lax.broadcasted_iota(I32, (tb, tb), 0)
    c = lax.broadcasted_iota(I32, (tb, tb), 1)
    strict_lower = jnp.where(r > c, 1.0, 0.0).astype(BF16)

    def onehots(rows):
        rt = route_ref[rows, :]
        a1 = jnp.where(lane_t == rt[:, LANE_E1:LANE_E1 + 1], 1.0, 0.0)
        a2 = jnp.where(lane_t == rt[:, LANE_E2:LANE_E2 + 1], 1.0, 0.0)
        return a1, a2

    def rank_body(bi, carry):
        rows = pl.ds(pl.multiple_of(bi * tb, tb), tb)
        a1, a2 = onehots(rows)
        m = a1 + a2
        cum_scr[rows, :] = _dot(strict_lower, m.astype(BF16)) + carry
        return carry + jnp.sum(m, axis=0, keepdims=True)

    counts = lax.fori_loop(0, T // tb, rank_body, jnp.zeros((1, LANES), F32))
    ntiles = jnp.floor((counts + (MOE_TILE - 1)) * (1.0 / MOE_TILE))
    rr = lax.broadcasted_iota(I32, (LANES, LANES), 0)
    cc = lax.broadcasted_iota(I32, (LANES, LANES), 1)
    upper = jnp.where(rr < cc, 1.0, 0.0).astype(BF16)
    tile0 = _dot(jnp.broadcast_to(ntiles, (SUBLANES, LANES)).astype(BF16), upper)[0:1, :]
    slot0 = tile0 * MOE_TILE

    def pos_body(bi, carry):
        rows = pl.ds(pl.multiple_of(bi * tb, tb), tb)
        a1, a2 = onehots(rows)
        base = cum_scr[rows, :] + slot0
        p1 = jnp.sum(a1 * base, axis=1, keepdims=True)
        p2 = jnp.sum(a2 * base, axis=1, keepdims=True)
        pos_ref[rows, :] = (jnp.where(lane_t == 0.0, p1, 0.0) + jnp.where(lane_t == 1.0, p2, 0.0)).astype(I32)
        return carry

    lax.fori_loop(0, T // tb, pos_body, 0)
    tend = tile0 + ntiles
    lane = lax.broadcasted_iota(I32, (LANES, LANES), 1)
    tile_i = lax.broadcasted_iota(I32, (LANES, LANES), 0).astype(F32)
    is_expert = lane < N_EXPERTS
    te = jnp.sum(jnp.where(is_expert & (jnp.broadcast_to(tend, (LANES, LANES)) <= tile_i), 1.0, 0.0),
                 axis=1, keepdims=True)
    te_ref[...] = jnp.broadcast_to(jnp.minimum(te, N_EXPERTS - 1.0), (LANES, LANES)).astype(I32)
    lane1 = lax.broadcasted_iota(I32, (1, LANES), 1)
    total = jnp.sum(jnp.where(lane1 < N_EXPERTS, ntiles, 0.0), axis=1, keepdims=True)
    tail = jnp.where(ntiles > 0.0, (tend - 1.0) * MOE_TILE, -1.0)
    tail = jnp.where(lane1 < N_EXPERTS, tail, jnp.where(lane1 == N_EXPERTS, total, 0.0))
    tail_ref[...] = jnp.broadcast_to(tail, (SUBLANES, LANES)).astype(I32)


def _plan(route, tb):
    T = route.shape[0]
    return pl.pallas_call(
        functools.partial(_plan_kernel, tb=tb),
        grid=(1,),
        in_specs=[pl.BlockSpec((T, LANES), lambda i: (0, 0))],
        out_specs=[
            pl.BlockSpec((T, LANES), lambda i: (0, 0)),
            pl.BlockSpec((LANES, LANES), lambda i: (0, 0)),
            pl.BlockSpec((SUBLANES, LANES), lambda i: (0, 0)),
        ],
        out_shape=[
            jax.ShapeDtypeStruct((T, LANES), I32),
            jax.ShapeDtypeStruct((LANES, LANES), I32),
            jax.ShapeDtypeStruct((SUBLANES, LANES), I32),
        ],
        scratch_shapes=[pltpu.VMEM((T, LANES), F32)],
        compiler_params=_params(("arbitrary",)),
        name="moe_plan",
    )(route)


def _row_group(ref, idx):
    return ref.at[pl.ds(pl.multiple_of(idx * ROW_GROUP, ROW_GROUP), ROW_GROUP), :]


def _dispatch_kernel(pos_ref, tail_ref, hn_ref, xs_ref, zero_scr, zsem, sem):
    i = pl.program_id(0)
    tm = hn_ref.shape[0] // ROW_GROUP
    tile_rows = MOE_TILE * ROW_GROUP

    @pl.when(i == 0)
    def _():
        zero_scr[...] = jnp.zeros_like(zero_scr)
        n_used = tail_ref[N_EXPERTS]

        def zero_copy(first_slot):
            start = pl.multiple_of(first_slot * ROW_GROUP, tile_rows)
            return pltpu.make_async_copy(zero_scr, xs_ref.at[pl.ds(start, tile_rows), :], zsem)

        def tails(fn):
            def body(e, carry):
                pl.when(tail_ref[e] >= 0)(lambda: fn(zero_copy(jnp.maximum(tail_ref[e], 0))))
                return carry
            lax.fori_loop(0, N_EXPERTS, body, 0)

        def unused(fn):
            def body(t, carry):
                fn(zero_copy(t * MOE_TILE))
                return carry
            lax.fori_loop(n_used, xs_ref.shape[0] // tile_rows, body, 0)

        tails(lambda c: c.start())
        unused(lambda c: c.start())
        tails(lambda c: c.wait())
        unused(lambda c: c.wait())

    def copy(r, k):
        slot = pos_ref[(i * tm + r) * TOP_K + k]
        return pltpu.make_async_copy(_row_group(hn_ref, r), _row_group(xs_ref, slot), sem)

    def start(r, carry):
        for k in range(TOP_K):
            copy(r, k).start(priority=k)
        return carry

    lax.fori_loop(0, tm, start, 0, unroll=8)
    for _ in range(TOP_K):
        pltpu.make_async_copy(hn_ref, xs_ref.at[pl.ds(0, tm * ROW_GROUP), :], sem).wait()


def _dispatch(pos, tails, hn_rows, n_slots, tm):
    T = hn_rows.shape[0] // ROW_GROUP
    return pl.pallas_call(
        _dispatch_kernel,
        grid_spec=pltpu.PrefetchScalarGridSpec(
            num_scalar_prefetch=2,
            grid=(T // tm,),
            in_specs=[pl.BlockSpec((tm * ROW_GROUP, LANES), lambda i, *_: (i, 0))],
            out_specs=pl.BlockSpec(memory_space=pl.ANY),
            scratch_shapes=[pltpu.VMEM((MOE_TILE * ROW_GROUP, LANES), F32),
                            pltpu.SemaphoreType.DMA, pltpu.SemaphoreType.DMA],
        ),
        out_shape=jax.ShapeDtypeStruct((n_slots * ROW_GROUP, LANES), F32),
        compiler_params=_params(("arbitrary",)),
        name="moe_dispatch",
    )(pos, tails, hn_rows)


def _experts_kernel(te_ref, nt_ref, xs_ref, wg_hbm, wu_hbm, wd_hbm, ys_ref,
                    wg_buf, wu_buf, wd_buf, wg_scr, wu_scr, wd_scr, slot_ref, sem):
    i = pl.program_id(0)
    nt = nt_ref[0]
    M = MOE_TILE
    e = te_ref[i]

    def fetch(expert, slot):
        return [pltpu.make_async_copy(hbm.at[expert], buf.at[slot], sem.at[slot, n])
                for n, (hbm, buf) in enumerate(((wg_hbm, wg_buf), (wu_hbm, wu_buf), (wd_hbm, wd_buf)))]

    @pl.when(i == 0)
    def _():
        slot_ref[0] = 0
        for c in fetch(e, 0):
            c.start()

    @pl.when((i < nt) & ((i == 0) | (te_ref[jnp.maximum(i - 1, 0)] != e)))
    def _():
        slot = slot_ref[0]
        for c in fetch(e, slot):
            c.wait()
        wg_scr[...] = wg_buf[slot].astype(BF16)
        wu_scr[...] = wu_buf[slot].astype(BF16)
        wd_scr[...] = wd_buf[slot].astype(BF16)
        nxt = lax.while_loop(lambda j: (j < nt) & (te_ref[jnp.minimum(j, nt - 1)] == e), lambda j: j + 1, i + 1)

        @pl.when(nxt < nt)
        def _():
            for c in fetch(te_ref[jnp.minimum(nxt, nt - 1)], 1 - slot):
                c.start()

        slot_ref[0] = 1 - slot

    @pl.when(i < nt)
    def _():
        x = jnp.concatenate([xs_ref[pl.ds(s, M, stride=ROW_GROUP), :] for s in range(ROW_GROUP)],
                            axis=1).astype(BF16)
        hmid = (_silu(_dot(x, wg_scr[...])) * _dot(x, wu_scr[...])).astype(BF16)
        y = _dot(hmid, wd_scr[...])
        for s in range(ROW_GROUP):
            ys_ref[pl.ds(s, M, stride=ROW_GROUP), :] = y[:, s * LANES:(s + 1) * LANES]


def _experts(te, nt, xs_rows, w_ge, w_ue, w_de, n_tiles):
    M = MOE_TILE
    rows = pl.BlockSpec((M * ROW_GROUP, LANES), lambda i, te_ref, nt_ref: (jnp.minimum(i, nt_ref[0] - 1), 0))
    hbm = pl.BlockSpec(memory_space=pl.ANY)
    up, down = (D_MODEL, EXPERT_FF), (EXPERT_FF, D_MODEL)
    return pl.pallas_call(
        _experts_kernel,
        grid_spec=pltpu.PrefetchScalarGridSpec(
            num_scalar_prefetch=2,
            grid=(n_tiles,),
            in_specs=[rows, hbm, hbm, hbm],
            out_specs=rows,
            scratch_shapes=[pltpu.VMEM((2,) + up, F32), pltpu.VMEM((2,) + up, F32), pltpu.VMEM((2,) + down, F32),
                            pltpu.VMEM(up, BF16), pltpu.VMEM(up, BF16), pltpu.VMEM(down, BF16),
                            pltpu.SMEM((1,), I32), pltpu.SemaphoreType.DMA((2, 3))],
        ),
        out_shape=jax.ShapeDtypeStruct(xs_rows.shape, F32),
        input_output_aliases={2: 0},
        compiler_params=_params(("arbitrary",)),
        name="moe_experts",
    )(te, nt, xs_rows, w_ge, w_ue, w_de)


def _final_kernel(pos_ref, h_ref, route_ref, gain_ref, ys_ref, y_ref, buf, sem, *, tile0):
    i = pl.program_id(0)
    n = pl.num_programs(0)
    tm = h_ref.shape[0]

    def copy(step, r, k, slot_buf):
        slot = pos_ref[((tile0 + step) * tm + r) * TOP_K + k]
        return pltpu.make_async_copy(_row_group(ys_ref, slot), _row_group(buf.at[slot_buf], r * TOP_K + k),
                                     sem.at[slot_buf])

    def start_all(step, slot_buf):
        def body(r, carry):
            for k in range(TOP_K):
                copy(step, r, k, slot_buf).start(priority=k)
            return carry
        lax.fori_loop(0, tm, body, 0, unroll=8)

    cur = i % 2
    pl.when(i == 0)(lambda: start_all(0, 0))
    pl.when(i + 1 < n)(lambda: start_all(i + 1, 1 - cur))
    pltpu.make_async_copy(ys_ref.at[pl.ds(0, buf.shape[1]), :], buf.at[cur], sem.at[cur]).wait()

    rt = route_ref[...]
    w1 = rt[:, LANE_W1:LANE_W1 + 1]
    w2 = rt[:, LANE_W2:LANE_W2 + 1]
    ssq = jnp.zeros((tm, 1), F32)
    for s in range(ROW_GROUP):
        cols = slice(s * LANES, (s + 1) * LANES)
        o1 = buf[cur, pl.ds(s, tm, stride=TOP_K * ROW_GROUP), :]
        o2 = buf[cur, pl.ds(ROW_GROUP + s, tm, stride=TOP_K * ROW_GROUP), :]
        v = h_ref[:, cols] + (w1 * o1 + w2 * o2)
        y_ref[:, cols] = v
        ssq = ssq + jnp.sum(v * v, axis=1, keepdims=True)
    y_ref[...] = y_ref[...] * lax.rsqrt(ssq * (1.0 / D_MODEL) + EPS) * gain_ref[...]


def _final(pos, h, route, gain, ys_rows, tile0, n_tiles, tm):
    return pl.pallas_call(
        functools.partial(_final_kernel, tile0=tile0),
        grid_spec=pltpu.PrefetchScalarGridSpec(
            num_scalar_prefetch=1,
            grid=(n_tiles,),
            in_specs=[
                pl.BlockSpec((tm, D_MODEL), lambda i, *_: (tile0 + i, 0)),
                pl.BlockSpec((tm, LANES), lambda i, *_: (tile0 + i, 0)),
                pl.BlockSpec((1, D_MODEL), lambda i, *_: (0, 0)),
                pl.BlockSpec(memory_space=pl.ANY),
            ],
            out_specs=pl.BlockSpec((tm, D_MODEL), lambda i, *_: (i, 0)),
            scratch_shapes=[pltpu.VMEM((2, tm * TOP_K * ROW_GROUP, LANES), F32), pltpu.SemaphoreType.DMA((2,))],
        ),
        out_shape=jax.ShapeDtypeStruct((n_tiles * tm, D_MODEL), F32),
        compiler_params=_params(("arbitrary",)),
        name="final",
    )(pos, h, route, gain, ys_rows)


def _rope_tables(pos):
    half = RET_HEAD_QK // 2
    inv = 1.0 / (ROPE_BASE ** jnp.linspace(0.0, 1.0, half, dtype=F32))
    ang = jnp.repeat(pos[:, None] * inv[None, :], 2, axis=-1)
    cos, sin = jnp.cos(ang), jnp.sin(ang)
    even = (jnp.arange(RET_HEAD_QK) % 2 == 0)[None, :]
    sa = jnp.where(even, -sin, 0.0)
    sb = jnp.where(even, 0.0, sin)
    return cos, sa, sb


def _tile_sizes(Tp, Ts, Lp):
    g = math.gcd(Tp, Ts)
    return dict(
        xnorm=math.gcd(g, 512),
        proj=(Tp + Ts) // 4,
        mix=math.gcd(g, 256),
        plan=math.gcd(Tp + Ts, 256),
        dispatch=math.gcd(g, 256),
        final=math.gcd(g, 256),
        scan=math.gcd(Lp, 128),
    )


def kernel(x_prompt, x_sample, state_hgrn, state_ret, norm_mix, w_in, hg_lb_logits, hg_norm,
           w_branch_hg, w_branch_ret, b_gate, w_out, norm_ffn, w_router_group, b_router_group,
           w_router_expert, b_router_expert, w_expert_gate, w_expert_up, w_expert_down, norm_final):
    assert w_in.shape == (1, D_MODEL, IN_WIDTH), "single-layer trunk only"
    Bp, Lp, _ = x_prompt.shape
    Bs, Ls, _ = x_sample.shape
    Tp, Ts = Bp * Lp, Bs * Ls
    T = Tp + Ts
    ts = _tile_sizes(Tp, Ts, Lp)
    xp = x_prompt.reshape(Tp, D_MODEL)
    xs = x_sample.reshape(Ts, D_MODEL)

    lower = jnp.cumsum(jax.nn.softmax(hg_lb_logits.astype(F32), axis=0), axis=0)[0]
    act = _act_coefficients(lower, b_gate[0])
    tab_p = _rope_tables(jnp.arange(Lp, dtype=F32))
    tab_s = _rope_tables(jnp.arange(Ls, dtype=F32) + jnp.float32(PAST_LEN))
    tabs = tuple(jnp.concatenate([jnp.tile(a, (Bp, 1)), jnp.tile(b, (Bs, 1))]) for a, b in zip(tab_p, tab_s))
    xn = _xnorm(xp, xs, norm_mix, ts["xnorm"])
    P = _in_proj(xn, w_in[0], act, tabs, ts["proj"])

    bb = math.gcd(Bs, 8)
    o_hg_p, shp = _hgrn_prompt(P, hg_norm, Bp, Lp, ts["scan"])
    o_hg_s, shs = _hgrn_sample(P, hg_norm, state_hgrn[0].astype(F32), Bs, Ls, bb, Tp)
    o_r_p, srp = _ret_prompt(P, Bp, Lp, ts["scan"])
    o_r_s, srs = _ret_sample(P, state_ret[0].astype(F32), Bs, Ls, bb, Tp)

    wbh = w_branch_hg[0].astype(BF16)
    wbr = w_branch_ret[0].astype(BF16)
    wout = w_out[0].astype(BF16)
    pad = LANES - N_GROUPS - N_EXPERTS
    wr = jnp.concatenate([w_router_group[0], w_router_expert[0], jnp.zeros((D_MODEL, pad), F32)], axis=1)
    wr_hi = wr.astype(BF16)
    wr_lo = (wr - wr_hi.astype(F32)).astype(BF16)
    br = jnp.concatenate([b_router_group[0], b_router_expert[0], jnp.zeros((pad,), F32)])[None, :]
    h, hn_rows, route = _mix(list(o_hg_p) + [o_hg_s], list(o_r_p) + [o_r_s], P, xp, xs, wbh, wbr, wout,
                             norm_ffn, wr_hi, wr_lo, br, ts["mix"])

    n_tiles = -(-T * TOP_K // MOE_TILE) + N_EXPERTS
    assert n_tiles <= LANES
    pos_slab, te_slab, tail_slab = _plan(route, ts["plan"])
    pos = pos_slab[:, :TOP_K].reshape(-1)
    te = te_slab[:n_tiles, 0]
    tails = tail_slab[0, :N_EXPERTS + 1]
    nt = tail_slab[0, N_EXPERTS:N_EXPERTS + 1]
    xs_rows = _dispatch(pos, tails, hn_rows, n_tiles * MOE_TILE, ts["dispatch"])
    ys_rows = _experts(te, nt, xs_rows, w_expert_gate[0], w_expert_up[0], w_expert_down[0], n_tiles)

    tf = ts["final"]
    gain = norm_final[None, :]
    y_p = _final(pos, h, route, gain, ys_rows, 0, Tp // tf, tf)
    y_s = _final(pos, h, route, gain, ys_rows, Tp // tf, Ts // tf, tf)
    return (y_p.reshape(Bp, Lp, D_MODEL), y_s.reshape(Bs, Ls, D_MODEL),
            shp[None], srp[None], shs[None], srs[None])
```

```python
import functools
import math

import jax
import jax.numpy as jnp
from jax import lax
from jax.experimental import pallas as pl
from jax.experimental.pallas import tpu as pltpu

F32 = jnp.float32
BF16 = jnp.bfloat16
I32 = jnp.int32

D_MODEL = 2048
PAST_LEN = 16384
HG_HEAD_V = 128
HG_EXPAND = 128
HG_VAL_DIM = D_MODEL // 2
HG_HEADS = HG_VAL_DIM // HG_HEAD_V
HG_KEY_DIM = HG_HEADS * HG_EXPAND
RET_HEAD_V = 256
RET_HEAD_QK = 128
RET_V_DIM = D_MODEL // 2
RET_HEADS = RET_V_DIM // RET_HEAD_V
RET_QK_DIM = RET_HEADS * RET_HEAD_QK
IN_WIDTH = 2 * HG_KEY_DIM + 2 * HG_VAL_DIM + 2 * RET_QK_DIM + 2 * RET_V_DIM + 2 * D_MODEL
CHUNK = 64
SCAN_CHUNK = 128
N_GROUPS = 4
EXPERTS_PER_GROUP = 8
N_EXPERTS = N_GROUPS * EXPERTS_PER_GROUP
TOP_K = 2
EXPERT_FF = D_MODEL // 4
ROPE_BASE = 10000.0
EPS = 1e-6

LANES = 128
SUBLANES = 8
VMEM_LIMIT = 56 * 1024 * 1024
PROJ_MC = 256
PROJ_TN = 512
SEG_HQ, SEG_HF, SEG_HI, SEG_HG, SEG_RQ, SEG_RK, SEG_RV, SEG_RG, SEG_GA = 0, 2, 4, 6, 8, 9, 10, 12, 14
SAFE_EXP_SPAN = 80.0
ROW_GROUP = D_MODEL // LANES
MOE_TILE = 256
LANE_E1, LANE_E2, LANE_W1, LANE_W2 = 0, 1, 2, 3
ROUTE_EXPERT_LANE = N_GROUPS


def _params(sem):
    return pltpu.CompilerParams(dimension_semantics=sem, vmem_limit_bytes=VMEM_LIMIT)


def _silu(z):
    return z * jax.nn.sigmoid(z)


def _dot(a, b):
    return jnp.dot(a, b, preferred_element_type=F32)


def _dot_nt(a, b):
    return lax.dot_general(a, b, (((1,), (1,)), ((), ())), preferred_element_type=F32)


def _dot_tn(a, b):
    return lax.dot_general(a, b, (((0,), (0,)), ((), ())), preferred_element_type=F32)


def _rms(x):
    return x * lax.rsqrt(jnp.mean(x * x, axis=-1, keepdims=True) + EPS)


def _to_token_tiles(x):
    M = x.shape[0]
    pieces = jnp.stack([x[:, s * LANES:(s + 1) * LANES] for s in range(ROW_GROUP)], axis=0)
    return pltpu.einshape("nmd->mnd", pieces).reshape(M * ROW_GROUP, LANES)


def _from_token_tiles(rows, groups=1):
    M = rows.shape[0] // (groups * ROW_GROUP)
    t = pltpu.einshape("mnd->nmd", rows.reshape(M, groups * ROW_GROUP, LANES))
    return [jnp.concatenate([t[g * ROW_GROUP + s] for s in range(ROW_GROUP)], axis=1) for g in range(groups)]


def _xnorm_kernel(xp_ref, xs_ref, gain_ref, o_ref, *, n_prompt_tiles):
    i = pl.program_id(0)

    def norm(x_ref):
        o_ref[...] = (_rms(x_ref[...]) * gain_ref[...]).astype(BF16)

    pl.when(i < n_prompt_tiles)(lambda: norm(xp_ref))
    pl.when(i >= n_prompt_tiles)(lambda: norm(xs_ref))


def _xnorm(xp, xs, gain, tm):
    npt, nst = xp.shape[0] // tm, xs.shape[0] // tm
    return pl.pallas_call(
        functools.partial(_xnorm_kernel, n_prompt_tiles=npt),
        grid=(npt + nst,),
        in_specs=[
            pl.BlockSpec((tm, D_MODEL), lambda i: (jnp.minimum(i, npt - 1), 0)),
            pl.BlockSpec((tm, D_MODEL), lambda i: (jnp.maximum(i - npt, 0), 0)),
            pl.BlockSpec((1, D_MODEL), lambda i: (0, 0)),
        ],
        out_specs=pl.BlockSpec((tm, D_MODEL), lambda i: (i, 0)),
        out_shape=jax.ShapeDtypeStruct(((npt + nst) * tm, D_MODEL), BF16),
        compiler_params=_params(("parallel",)),
        name="xnorm",
    )(xp, xs, gain)


ACT_ALPHA, ACT_BETA, ACT_GAMMA, ACT_DELTA, ACT_EPS = 0, 1, 2, 3, 4


def _act_coefficients(lower, b_gate):
    z = jnp.zeros((IN_WIDTH,), F32)
    seg = lambda a, b: slice(a * PROJ_TN, b * PROJ_TN)
    alpha = z.at[seg(SEG_HI, SEG_HG)].set(1.0).at[seg(SEG_RQ, SEG_RG)].set(1.0)
    beta = z.at[seg(SEG_GA, IN_WIDTH // PROJ_TN)].set(b_gate)
    gamma = (z.at[seg(SEG_HQ, SEG_HF)].set(HG_EXPAND ** -0.5)
             .at[seg(SEG_HG, SEG_RQ)].set(1.0).at[seg(SEG_RG, SEG_GA)].set(1.0))
    delta = z.at[seg(SEG_HF, SEG_HI)].set(1.0 - lower).at[seg(SEG_GA, IN_WIDTH // PROJ_TN)].set(1.0)
    eps = z.at[seg(SEG_HF, SEG_HI)].set(lower)
    return jnp.stack([alpha, beta, gamma, delta, eps, z, z, z])


def _in_proj_kernel(x_ref, w_ref, act_ref, cos_ref, sa_ref, sb_ref, o_ref):
    j = pl.program_id(1)
    w = w_ref[...].astype(BF16)
    row = lambda r: act_ref[r:r + 1, :]
    tm = x_ref.shape[0]
    mc = math.gcd(tm, PROJ_MC)
    for m in range(0, tm, mc):
        p = _dot(x_ref[m:m + mc, :], w)
        o_ref[m:m + mc, :] = (row(ACT_ALPHA) * p + row(ACT_EPS)
                              + jax.nn.sigmoid(p + row(ACT_BETA)) * (row(ACT_GAMMA) * p + row(ACT_DELTA)))

    @pl.when((j >= SEG_RQ) & (j < SEG_RV))
    def _():
        scale = jnp.where(j == SEG_RK, RET_HEAD_QK ** -0.5, 1.0).astype(F32)
        cos, sa, sb = cos_ref[...], sa_ref[...], sb_ref[...]
        for hh in range(PROJ_TN // LANES):
            cols = slice(hh * LANES, (hh + 1) * LANES)
            xs = o_ref[:, cols]
            r = xs * cos + pltpu.roll(xs, LANES - 1, 1) * sa + pltpu.roll(xs, 1, 1) * sb
            o_ref[:, cols] = r * scale


def _in_proj(xn, w_in, act, tabs, tm):
    T = xn.shape[0]
    once = dict(pipeline_mode=pl.Buffered(1))
    tab = pl.BlockSpec((tm, LANES), lambda i, j: (i, 0), **once)
    return pl.pallas_call(
        _in_proj_kernel,
        grid=(T // tm, IN_WIDTH // PROJ_TN),
        in_specs=[
            pl.BlockSpec((tm, D_MODEL), lambda i, j: (i, 0), **once),
            pl.BlockSpec((D_MODEL, PROJ_TN), lambda i, j: (0, j)),
            pl.BlockSpec((SUBLANES, PROJ_TN), lambda i, j: (0, j)),
            tab, tab, tab,
        ],
        out_specs=pl.BlockSpec((tm, PROJ_TN), lambda i, j: (i, j)),
        out_shape=jax.ShapeDtypeStruct((T, IN_WIDTH), F32),
        compiler_params=_params(("parallel", "arbitrary")),
        name="in_proj",
    )(xn, w_in, act, *tabs)


def _prefix_matrix(C, G):
    r = lax.broadcasted_iota(I32, (C, 3 * C), 0)
    c = lax.broadcasted_iota(I32, (C, 3 * C), 1) % C
    return jnp.where((c <= r) & (c // G == r // G), 1.0, 0.0).astype(BF16)


def _prefix_sum(g, pm):
    g1 = g.astype(BF16)
    r1 = g - g1.astype(F32)
    g2 = r1.astype(BF16)
    g3 = (r1 - g2.astype(F32)).astype(BF16)
    return _dot(pm, jnp.concatenate([g1, g2, g3], axis=0))


def _state_decay_column(d_row):
    r = lax.broadcasted_iota(I32, (LANES, LANES), 0)
    c = lax.broadcasted_iota(I32, (LANES, LANES), 1)
    return jnp.sum(jnp.where(r == c, jnp.broadcast_to(d_row, (LANES, LANES)), 0.0), axis=1, keepdims=True)


def _hgrn_chunk(q, f, g, v, states, pm, tmp_ref, NG, factorised):
    C, W = q.shape
    G, H = C // NG, W // LANES
    hs = [slice(h * LANES, (h + 1) * LANES) for h in range(H)]
    gs = [slice(n * G, (n + 1) * G) for n in range(NG)]
    k = 1.0 - f
    b = _prefix_sum(g, pm)
    b3 = b.reshape(NG, G, W)
    b_end = b3[:, G - 1:G, :]
    q3, k3 = q.reshape(NG, G, W), k.reshape(NG, G, W)
    vb = v.astype(BF16)
    if factorised:
        b_mid = b3[:, G // 2 - 1:G // 2, :]
        qm3 = q3 * jnp.exp(b3 - b_mid)
        km3 = k3 * jnp.exp(b_mid - b3)
        qb = (qm3 * jnp.exp(b_mid)).reshape(C, W)
        kd = (km3 * jnp.exp(b_end - b_mid)).reshape(C, W)
        qm = qm3.reshape(C, W).astype(BF16)
        km = km3.reshape(C, W).astype(BF16)
        row = lax.broadcasted_iota(I32, (C, C), 0)
        col = lax.broadcasted_iota(I32, (C, C), 1)
        amask = (row >= col) & (row // G == col // G)
        o_intra = []
        for h in range(H):
            att = jnp.where(amask, _dot_nt(qm[:, hs[h]], km[:, hs[h]]), 0.0)
            o_intra.append(_dot(att.astype(BF16), vb[:, hs[h]]))
    else:
        qb = (q3 * jnp.exp(b3)).reshape(C, W)
        kd = (k3 * jnp.exp(b_end - b3)).reshape(C, W)
        trow = lax.broadcasted_iota(I32, (G, LANES), 0)
        o_intra = []
        for h in range(H):
            parts = []
            for n in range(NG):
                bh, qh = b[gs[n], hs[h]], q[gs[n], hs[h]]
                tmp_ref[0] = bh
                tmp_ref[1] = k[gs[n], hs[h]]
                tmp_ref[2] = v[gs[n], hs[h]]

                def body(s, acc, bh=bh, qh=qh):
                    bs = tmp_ref[0, pl.ds(s, 1), :]
                    ks = tmp_ref[1, pl.ds(s, 1), :]
                    vs = tmp_ref[2, pl.ds(s, 1), :]
                    w = jnp.where(trow >= s, jnp.exp(jnp.minimum(bh - bs, 0.0)), 0.0)
                    return acc + jnp.sum(qh * ks * w, axis=1, keepdims=True) * vs

                parts.append(lax.fori_loop(0, G, body, jnp.zeros((G, LANES), F32)))
            o_intra.append(parts[0] if NG == 1 else jnp.concatenate(parts, axis=0))
    d = jnp.exp(b_end)
    outs, new_states = [], [[None] * H for _ in range(NG)]
    for h in range(H):
        inter = []
        for n in range(NG):
            S = states[n][h]
            inter.append(_dot(qb[gs[n], hs[h]].astype(BF16), S.astype(BF16)))
            upd = _dot_tn(kd[gs[n], hs[h]].astype(BF16), v[gs[n], hs[h]].astype(BF16))
            new_states[n][h] = _state_decay_column(d[n, :, hs[h]]) * S + upd
        inter = inter[0] if NG == 1 else jnp.concatenate(inter, axis=0)
        outs.append(inter + o_intra[h])
    return jnp.concatenate(outs, axis=1), new_states


def _decay_is_safe(g, G):
    R, W = g.shape
    return jnp.min(jnp.sum(g.reshape(R // G, G, W), axis=1)) >= -SAFE_EXP_SPAN


def _hgrn_epilogue(o_scr, gain_ref, gate_ref, o_ref):
    o_ref[...] = (_rms(o_scr[...]) * gain_ref[...] * gate_ref[...]).astype(o_ref.dtype)


def _hgrn_prompt_kernel(*refs, C, NB):
    ins, (gain_ref,), rest = refs[:4 * NB], refs[4 * NB:4 * NB + 1], refs[4 * NB + 1:]
    o_refs, (st_ref, s_scr, g_scr, o_scr, tmp_ref) = rest[:NB], rest[NB:]
    seq = [ins[4 * n:4 * n + 4] for n in range(NB)]
    l = pl.program_id(0)
    Lb = o_refs[0].shape[0]

    @pl.when(l == 0)
    def _():
        s_scr[...] = jnp.zeros_like(s_scr)

    for n in range(NB):
        g_scr[n] = jnp.log(seq[n][1][...])
    pm = _prefix_matrix(C, C)

    def run(factorised):
        def body(ci, carry):
            rows = pl.ds(pl.multiple_of(ci * C, C), C)
            args = [(seq[n][0][rows, :], seq[n][1][rows, :], g_scr[n, rows, :], seq[n][2][rows, :],
                     [[s_scr[n, h] for h in range(HG_HEADS)]]) for n in range(NB)]
            outs = [_hgrn_chunk(*a, pm, tmp_ref, 1, factorised) for a in args]
            for n, (o, new) in enumerate(outs):
                o_scr[n, rows, :] = o
                for h in range(HG_HEADS):
                    s_scr[n, h] = new[0][h]
            return carry
        lax.fori_loop(0, Lb // C, body, 0)

    safe = _decay_is_safe(g_scr[...].reshape(NB * Lb, -1), C // 2)
    pl.when(safe)(lambda: run(True))
    pl.when(jnp.logical_not(safe))(lambda: run(False))
    for n in range(NB):
        o_refs[n][...] = (_rms(o_scr[n]) * gain_ref[...] * seq[n][3][...]).astype(o_refs[n].dtype)

    @pl.when(l == pl.num_programs(0) - 1)
    def _():
        st_ref[...] = s_scr[...]


def _hgrn_prompt(P, gain, B, L, Lb):
    C = math.gcd(L, SCAN_CHUNK)
    nl = L // Lb
    W = HG_KEY_DIM
    blk = lambda b, seg: pl.BlockSpec((Lb, W), lambda l, b=b, seg=seg: (b * nl + l, seg))
    outs = pl.pallas_call(
        functools.partial(_hgrn_prompt_kernel, C=C, NB=B),
        grid=(nl,),
        in_specs=[blk(b, seg) for b in range(B) for seg in range(4)] + [pl.BlockSpec((1, W), lambda l: (0, 0))],
        out_specs=[pl.BlockSpec((Lb, W), lambda l: (l, 0)) for _ in range(B)]
        + [pl.BlockSpec((B, HG_HEADS, HG_EXPAND, HG_HEAD_V), lambda l: (0, 0, 0, 0))],
        out_shape=[jax.ShapeDtypeStruct((L, HG_VAL_DIM), BF16) for _ in range(B)]
        + [jax.ShapeDtypeStruct((B, HG_HEADS, HG_EXPAND, HG_HEAD_V), F32)],
        scratch_shapes=[pltpu.VMEM((B, HG_HEADS, HG_EXPAND, HG_HEAD_V), F32), pltpu.VMEM((B, Lb, W), F32),
                        pltpu.VMEM((B, Lb, W), F32), pltpu.VMEM((3, C, LANES), F32)],
        compiler_params=_params(("arbitrary",)),
        name="hgrn_prompt",
    )(*([P] * (4 * B)), gain)
    return outs[:B], outs[B]


def _hgrn_sample_kernel(q_ref, f_ref, v_ref, gate_ref, gain_ref, s_ref, o_ref, so_ref, o_scr, tmp_ref, *, G):
    NG = s_ref.shape[0]
    g = jnp.log(f_ref[...])
    pm = _prefix_matrix(NG * G, G)

    def run(factorised):
        states = [[s_ref[n, h] for h in range(HG_HEADS)] for n in range(NG)]
        o, new = _hgrn_chunk(q_ref[...], f_ref[...], g, v_ref[...], states, pm, tmp_ref, NG, factorised)
        o_scr[...] = o
        for n in range(NG):
            for h in range(HG_HEADS):
                so_ref[n, h] = new[n][h]

    safe = _decay_is_safe(g, G // 2)
    pl.when(safe)(lambda: run(True))
    pl.when(jnp.logical_not(safe))(lambda: run(False))
    _hgrn_epilogue(o_scr, gain_ref, gate_ref, o_ref)


def _hgrn_sample(P, gain, state, Bs, Ls, Bb, row0):
    W = HG_KEY_DIM
    rows = Bb * Ls
    blk0 = row0 // rows

    def col(seg):
        return pl.BlockSpec((rows, W), lambda i, seg=seg: (blk0 + i, seg))

    st = pl.BlockSpec((Bb, HG_HEADS, HG_EXPAND, HG_HEAD_V), lambda i: (i, 0, 0, 0))
    return pl.pallas_call(
        functools.partial(_hgrn_sample_kernel, G=Ls),
        grid=(Bs // Bb,),
        in_specs=[col(0), col(1), col(2), col(3), pl.BlockSpec((1, W), lambda i: (0, 0)), st],
        out_specs=[pl.BlockSpec((rows, W), lambda i: (i, 0)), st],
        out_shape=[
            jax.ShapeDtypeStruct((Bs * Ls, HG_VAL_DIM), BF16),
            jax.ShapeDtypeStruct((Bs, HG_HEADS, HG_EXPAND, HG_HEAD_V), F32),
        ],
        scratch_shapes=[pltpu.VMEM((rows, W), F32), pltpu.VMEM((3, Ls, LANES), F32)],
        compiler_params=_params(("parallel",)),
        name="hgrn_sample",
    )(P, P, P, P, gain, state)


def _ret_chunk(q, k, v, states, dm_ref, qw_ref, kw_ref, sd_ref):
    outs, new_states = [], []
    qb, kb, vb = q.astype(BF16), k.astype(BF16), v.astype(BF16)
    for h in range(RET_HEADS):
        qs = slice(h * RET_HEAD_QK, (h + 1) * RET_HEAD_QK)
        vs = slice(h * RET_HEAD_V, (h + 1) * RET_HEAD_V)
        att = _dot_nt(qb[:, qs], kb[:, qs]) * dm_ref[h]
        o = _dot(att.astype(BF16), vb[:, vs]) + _dot((q[:, qs] * qw_ref[h]).astype(BF16), states[h].astype(BF16))
        new_states.append(sd_ref[h, 0:1, 0:1] * states[h] + _dot_tn((k[:, qs] * kw_ref[h]).astype(BF16), vb[:, vs]))
        outs.append(o)
    return jnp.concatenate(outs, axis=1), new_states


def _ret_epilogue(o_scr, gate_ref, o_ref):
    for h in range(RET_HEADS):
        vs = slice(h * RET_HEAD_V, (h + 1) * RET_HEAD_V)
        o_ref[:, vs] = (_rms(o_scr[:, vs]) * gate_ref[:, vs]).astype(o_ref.dtype)


def _ret_prompt_kernel(*refs, NB):
    ins, (dm_ref, qw_ref, kw_ref, sd_ref), rest = refs[:4 * NB], refs[4 * NB:4 * NB + 4], refs[4 * NB + 4:]
    o_refs, (st_ref, s_scr, o_scr) = rest[:NB], rest[NB:]
    seq = [ins[4 * n:4 * n + 4] for n in range(NB)]
    l = pl.program_id(0)
    Lb = o_refs[0].shape[0]
    C = dm_ref.shape[1]

    @pl.when(l == 0)
    def _():
        s_scr[...] = jnp.zeros_like(s_scr)

    def body(ci, carry):
        rows = pl.ds(pl.multiple_of(ci * C, C), C)
        args = [(seq[n][0][rows, :], seq[n][1][rows, :], seq[n][2][rows, :],
                 [s_scr[n, h] for h in range(RET_HEADS)]) for n in range(NB)]
        outs = [_ret_chunk(*a, dm_ref, qw_ref, kw_ref, sd_ref) for a in args]
        for n, (o, new) in enumerate(outs):
            o_scr[n, rows, :] = o
            for h in range(RET_HEADS):
                s_scr[n, h] = new[h]
        return carry

    lax.fori_loop(0, Lb // C, body, 0)
    for n in range(NB):
        _ret_epilogue(o_scr.at[n], seq[n][3], o_refs[n])

    @pl.when(l == pl.num_programs(0) - 1)
    def _():
        st_ref[...] = s_scr[...]


def _ret_tables(C):
    log_gamma = jnp.log(1.0 - jnp.exp2(-5.0 - jnp.arange(RET_HEADS, dtype=F32)))
    idx = jnp.arange(C, dtype=F32)
    rel = idx[:, None] - idx[None, :]
    tri = jnp.tril(jnp.ones((C, C), dtype=bool))
    dmat = jnp.exp(jnp.where(tri[None], log_gamma[:, None, None] * rel[None], -jnp.inf))
    qw = jnp.exp(log_gamma[:, None] * (idx[None, :] + 1.0))[..., None]
    kw = jnp.exp(log_gamma[:, None] * (C - 1.0 - idx[None, :]))[..., None]
    sdec = jnp.exp(log_gamma * C)[:, None, None]
    bc = lambda a: jnp.broadcast_to(a, (RET_HEADS, a.shape[1], LANES))
    return dmat, bc(qw), bc(kw), jnp.broadcast_to(sdec, (RET_HEADS, SUBLANES, LANES))


_QK0 = (2 * HG_KEY_DIM + 2 * HG_VAL_DIM) // RET_QK_DIM
_V0 = (2 * HG_KEY_DIM + 2 * HG_VAL_DIM + 2 * RET_QK_DIM) // RET_V_DIM


def _full3(a):
    return pl.BlockSpec(a.shape, lambda *_: (0, 0, 0))


def _ret_prompt(P, B, L, Lb):
    C = math.gcd(L, SCAN_CHUNK)
    nl = L // Lb
    tabs = _ret_tables(C)
    st_shape = (B, RET_HEADS, RET_HEAD_QK, RET_HEAD_V)
    blk = lambda b, w, c: pl.BlockSpec((Lb, w), lambda l, b=b, c=c: (b * nl + l, c))
    per_seq = lambda b: [blk(b, RET_QK_DIM, _QK0), blk(b, RET_QK_DIM, _QK0 + 1),
                         blk(b, RET_V_DIM, _V0), blk(b, RET_V_DIM, _V0 + 1)]
    outs = pl.pallas_call(
        functools.partial(_ret_prompt_kernel, NB=B),
        grid=(nl,),
        in_specs=[spec for b in range(B) for spec in per_seq(b)] + [_full3(t) for t in tabs],
        out_specs=[pl.BlockSpec((Lb, RET_V_DIM), lambda l: (l, 0)) for _ in range(B)]
        + [pl.BlockSpec(st_shape, lambda l: (0, 0, 0, 0))],
        out_shape=[jax.ShapeDtypeStruct((L, RET_V_DIM), BF16) for _ in range(B)]
        + [jax.ShapeDtypeStruct(st_shape, F32)],
        scratch_shapes=[pltpu.VMEM(st_shape, F32), pltpu.VMEM((B, Lb, RET_V_DIM), F32)],
        compiler_params=_params(("arbitrary",)),
        name="ret_prompt",
    )(*([P] * (4 * B)), *tabs)
    return outs[:B], outs[B]


def _ret_sample_kernel(q_ref, k_ref, v_ref, gate_ref, s_ref, dm_ref, qw_ref, kw_ref, sd_ref,
                       o_ref, so_ref, o_scr, *, C):
    Bb = s_ref.shape[0]
    args = [(q_ref[bb * C:(bb + 1) * C, :], k_ref[bb * C:(bb + 1) * C, :], v_ref[bb * C:(bb + 1) * C, :],
             [s_ref[bb, h] for h in range(RET_HEADS)]) for bb in range(Bb)]
    outs = [_ret_chunk(*a, dm_ref, qw_ref, kw_ref, sd_ref) for a in args]
    for bb, (o, new) in enumerate(outs):
        o_scr[bb * C:(bb + 1) * C, :] = o
        for h in range(RET_HEADS):
            so_ref[bb, h] = new[h]
    _ret_epilogue(o_scr, gate_ref, o_ref)


def _ret_sample(P, state, Bs, Ls, Bb, row0):
    tabs = _ret_tables(Ls)
    rows = Bb * Ls
    blk0 = row0 // rows
    st = pl.BlockSpec((Bb, RET_HEADS, RET_HEAD_QK, RET_HEAD_V), lambda i: (i, 0, 0, 0))
    return pl.pallas_call(
        functools.partial(_ret_sample_kernel, C=Ls),
        grid=(Bs // Bb,),
        in_specs=[
            pl.BlockSpec((rows, RET_QK_DIM), lambda i: (blk0 + i, _QK0)),
            pl.BlockSpec((rows, RET_QK_DIM), lambda i: (blk0 + i, _QK0 + 1)),
            pl.BlockSpec((rows, RET_V_DIM), lambda i: (blk0 + i, _V0)),
            pl.BlockSpec((rows, RET_V_DIM), lambda i: (blk0 + i, _V0 + 1)),
            st,
        ] + [_full3(t) for t in tabs],
        out_specs=[pl.BlockSpec((rows, RET_V_DIM), lambda i: (i, 0)), st],
        out_shape=[
            jax.ShapeDtypeStruct((Bs * Ls, RET_V_DIM), BF16),
            jax.ShapeDtypeStruct((Bs, RET_HEADS, RET_HEAD_QK, RET_HEAD_V), F32),
        ],
        scratch_shapes=[pltpu.VMEM((rows, RET_V_DIM), F32)],
        compiler_params=_params(("parallel",)),
        name="ret_sample",
    )(P, P, P, P, state, *tabs)


def _split_dot(a, w_hi, w_lo):
    a_hi = a.astype(BF16)
    a_lo = (a - a_hi.astype(F32)).astype(BF16)
    return _dot(a_hi, w_hi) + (_dot(a_hi, w_lo) + _dot(a_lo, w_hi))


def _route(logits):
    lane = lax.broadcasted_iota(I32, logits.shape, 1)
    neg = jnp.float32(-jnp.inf)
    big = jnp.int32(LANES)
    gmask = lane < N_GROUPS
    gl = jnp.where(gmask, logits, neg)
    gmax = jnp.max(gl, axis=1, keepdims=True)
    gu = jnp.where(gmask, jnp.exp(gl - gmax), 0.0)
    gp = gu / jnp.sum(gu, axis=1, keepdims=True)
    g_w = jnp.max(gp, axis=1, keepdims=True)
    g_i = jnp.min(jnp.where(gmask & (gp == g_w), lane, big), axis=1, keepdims=True)
    lo = ROUTE_EXPERT_LANE + g_i * EXPERTS_PER_GROUP
    emask = (lane >= lo) & (lane < lo + EXPERTS_PER_GROUP)
    el = jnp.where(emask, logits, neg)
    e1 = jnp.max(el, axis=1, keepdims=True)
    i1 = jnp.min(jnp.where(emask & (el == e1), lane, big), axis=1, keepdims=True)
    el2 = jnp.where(lane == i1, neg, el)
    e2 = jnp.max(el2, axis=1, keepdims=True)
    i2 = jnp.min(jnp.where(emask & (lane != i1) & (el2 == e2), lane, big), axis=1, keepdims=True)
    u2 = jnp.exp(e2 - e1)
    den = 1.0 + u2
    w1 = (1.0 / den) * g_w
    w2 = (u2 / den) * g_w
    id1 = (i1 - ROUTE_EXPERT_LANE).astype(F32)
    id2 = (i2 - ROUTE_EXPERT_LANE).astype(F32)
    return (jnp.where(lane == LANE_E1, id1, 0.0) + jnp.where(lane == LANE_E2, id2, 0.0)
            + jnp.where(lane == LANE_W1, w1, 0.0) + jnp.where(lane == LANE_W2, w2, 0.0))


def _pick(i, bounds, refs):
    val = refs[-1][...]
    for n in range(len(refs) - 2, -1, -1):
        val = jnp.where(i < bounds[n + 1], refs[n][...], val)
    return val


def _mix_kernel(*refs, bounds):
    ns = len(bounds) - 1
    ohg_refs, or_refs, refs = refs[:ns], refs[ns:2 * ns], refs[2 * ns:]
    (ga0_ref, ga1_ref, gb0_ref, gb1_ref, xp_ref, xs_ref, wbh_ref, wbr_ref, wout_ref, nffn_ref,
     wr_hi_ref, wr_lo_ref, br_ref, h_ref, hn_ref, route_ref) = refs
    i = pl.program_id(0)
    tm = h_ref.shape[0]
    is_prompt = i < bounds[-2]
    p_hg = _dot(_pick(i, bounds, ohg_refs), wbh_ref[...])
    p_r = _dot(_pick(i, bounds, or_refs), wbr_ref[...])
    half = D_MODEL // 2
    m0 = ga0_ref[...] * p_hg[:, :half] + gb0_ref[...] * p_r[:, :half]
    m1 = ga1_ref[...] * p_hg[:, half:] + gb1_ref[...] * p_r[:, half:]
    mixed = jnp.concatenate([m0, m1], axis=1).astype(BF16)
    x = jnp.where(is_prompt, xp_ref[...], xs_ref[...])
    h = x + _dot(mixed, wout_ref[...])
    h_ref[...] = h
    hn = _rms(h) * nffn_ref[...]
    for s in range(ROW_GROUP):
        hn_ref[pl.ds(s, tm, stride=ROW_GROUP), :] = hn[:, s * LANES:(s + 1) * LANES]
    route_ref[...] = _route(_split_dot(hn, wr_hi_ref[...], wr_lo_ref[...]) + br_ref[...])


def _mix(o_hg, o_r, P, xp, xs, wbh, wbr, wout, norm_ffn, wr_hi, wr_lo, br, tm):
    npt, nst = xp.shape[0] // tm, xs.shape[0] // tm
    bounds = [0]
    for a in o_hg:
        bounds.append(bounds[-1] + a.shape[0] // tm)
    T = (npt + nst) * tm
    half = D_MODEL // 2
    ga = (IN_WIDTH - 2 * D_MODEL) // half

    def pcol(c):
        return pl.BlockSpec((tm, half), lambda i, c=c: (i, c))

    def const(a):
        return pl.BlockSpec(a.shape, lambda i: (0, 0), pipeline_mode=pl.Buffered(1))

    def prompt(w):
        return pl.BlockSpec((tm, w), lambda i: (jnp.minimum(i, npt - 1), 0))

    def sample(w):
        return pl.BlockSpec((tm, w), lambda i: (jnp.maximum(i - npt, 0), 0))

    def source(n):
        lo, hi = bounds[n], bounds[n + 1]
        return pl.BlockSpec((tm, half), lambda i: (jnp.clip(i - lo, 0, hi - lo - 1), 0))

    sources = [source(n) for n in range(len(o_hg))]
    return pl.pallas_call(
        functools.partial(_mix_kernel, bounds=tuple(bounds)),
        grid=(npt + nst,),
        in_specs=sources + sources + [
            pcol(ga), pcol(ga + 1), pcol(ga + 2), pcol(ga + 3),
            prompt(D_MODEL), sample(D_MODEL),
            const(wbh), const(wbr), const(wout), const(norm_ffn), const(wr_hi), const(wr_lo), const(br),
        ],
        out_specs=[
            pl.BlockSpec((tm, D_MODEL), lambda i: (i, 0)),
            pl.BlockSpec((tm * ROW_GROUP, LANES), lambda i: (i, 0)),
            pl.BlockSpec((tm, LANES), lambda i: (i, 0)),
        ],
        out_shape=[
            jax.ShapeDtypeStruct((T, D_MODEL), F32),
            jax.ShapeDtypeStruct((T * ROW_GROUP, LANES), F32),
            jax.ShapeDtypeStruct((T, LANES), F32),
        ],
        compiler_params=_params(("parallel",)),
        name="mix",
    )(*o_hg, *o_r, P, P, P, P, xp, xs, wbh, wbr, wout, norm_ffn, wr_hi, wr_lo, br)


def _plan_kernel(route_ref, pos_ref, te_ref, tail_ref, cum_scr, *, tb):
    T = route_ref.shape[0]
    lane_t = lax.broadcasted_iota(I32, (tb, LANES), 1).astype(F32)
    r = lax.broadcasted_iota(I32, (tb, tb), 0)
    c = lax.broadcasted_iota(I32, (tb, tb), 1)
    strict_lower = jnp.where(r > c, 1.0, 0.0).astype(BF16)

    def onehots(rows):
        rt = route_ref[rows, :]
        a1 = jnp.where(lane_t == rt[:, LANE_E1:LANE_E1 + 1], 1.0, 0.0)
        a2 = jnp.where(lane_t == rt[:, LANE_E2:LANE_E2 + 1], 1.0, 0.0)
        return a1, a2

    def rank_body(bi, carry):
        rows = pl.ds(pl.multiple_of(bi * tb, tb), tb)
        a1, a2 = onehots(rows)
        m = a1 + a2
        cum_scr[rows, :] = _dot(strict_lower, m.astype(BF16)) + carry
        return carry + jnp.sum(m, axis=0, keepdims=True)

    counts = lax.fori_loop(0, T // tb, rank_body, jnp.zeros((1, LANES), F32))
    ntiles = jnp.floor((counts + (MOE_TILE - 1)) * (1.0 / MOE_TILE))
    rr = lax.broadcasted_iota(I32, (LANES, LANES), 0)
    cc = lax.broadcasted_iota(I32, (LANES, LANES), 1)
    upper = jnp.where(rr < cc, 1.0, 0.0).astype(BF16)
    tile0 = _dot(jnp.broadcast_to(ntiles, (SUBLANES, LANES)).astype(BF16), upper)[0:1, :]
    slot0 = tile0 * MOE_TILE

    def pos_body(bi, carry):
        rows = pl.ds(pl.multiple_of(bi * tb, tb), tb)
        a1, a2 = onehots(rows)
        base = cum_scr[rows, :] + slot0
        p1 = jnp.sum(a1 * base, axis=1, keepdims=True)
        p2 = jnp.sum(a2 * base, axis=1, keepdims=True)
        pos_ref[rows, :] = (jnp.where(lane_t == 0.0, p1, 0.0) + jnp.where(lane_t == 1.0, p2, 0.0)).astype(I32)
        return carry

    lax.fori_loop(0, T // tb, pos_body, 0)
    tend = tile0 + ntiles
    lane = lax.broadcasted_iota(I32, (LANES, LANES), 1)
    tile_i = lax.broadcasted_iota(I32, (LANES, LANES), 0).astype(F32)
    is_expert = lane < N_EXPERTS
    te = jnp.sum(jnp.where(is_expert & (jnp.broadcast_to(tend, (LANES, LANES)) <= tile_i), 1.0, 0.0),
                 axis=1, keepdims=True)
    te_ref[...] = jnp.broadcast_to(jnp.minimum(te, N_EXPERTS - 1.0), (LANES, LANES)).astype(I32)
    lane1 = lax.broadcasted_iota(I32, (1, LANES), 1)
    total = jnp.sum(jnp.where(lane1 < N_EXPERTS, ntiles, 0.0), axis=1, keepdims=True)
    tail = jnp.where(ntiles > 0.0, (tend - 1.0) * MOE_TILE, -1.0)
    tail = jnp.where(lane1 < N_EXPERTS, tail, jnp.where(lane1 == N_EXPERTS, total, 0.0))
    tail_ref[...] = jnp.broadcast_to(tail, (SUBLANES, LANES)).astype(I32)


def _plan(route, tb):
    T = route.shape[0]
    return pl.pallas_call(
        functools.partial(_plan_kernel, tb=tb),
        grid=(1,),
        in_specs=[pl.BlockSpec((T, LANES), lambda i: (0, 0))],
        out_specs=[
            pl.BlockSpec((T, LANES), lambda i: (0, 0)),
            pl.BlockSpec((LANES, LANES), lambda i: (0, 0)),
            pl.BlockSpec((SUBLANES, LANES), lambda i: (0, 0)),
        ],
        out_shape=[
            jax.ShapeDtypeStruct((T, LANES), I32),
            jax.ShapeDtypeStruct((LANES, LANES), I32),
            jax.ShapeDtypeStruct((SUBLANES, LANES), I32),
        ],
        scratch_shapes=[pltpu.VMEM((T, LANES), F32)],
        compiler_params=_params(("arbitrary",)),
        name="moe_plan",
    )(route)


def _row_group(ref, idx):
    return ref.at[pl.ds(pl.multiple_of(idx * ROW_GROUP, ROW_GROUP), ROW_GROUP), :]


def _dispatch_kernel(pos_ref, tail_ref, hn_ref, xs_ref, zero_scr, zsem, sem):
    i = pl.program_id(0)
    tm = hn_ref.shape[0] // ROW_GROUP
    tile_rows = MOE_TILE * ROW_GROUP

    @pl.when(i == 0)
    def _():
        zero_scr[...] = jnp.zeros_like(zero_scr)
        n_used = tail_ref[N_EXPERTS]

        def zero_copy(first_slot):
            start = pl.multiple_of(first_slot * ROW_GROUP, tile_rows)
            return pltpu.make_async_copy(zero_scr, xs_ref.at[pl.ds(start, tile_rows), :], zsem)

        def tails(fn):
            def body(e, carry):
                pl.when(tail_ref[e] >= 0)(lambda: fn(zero_copy(jnp.maximum(tail_ref[e], 0))))
                return carry
            lax.fori_loop(0, N_EXPERTS, body, 0)

        def unused(fn):
            def body(t, carry):
                fn(zero_copy(t * MOE_TILE))
                return carry
            lax.fori_loop(n_used, xs_ref.shape[0] // tile_rows, body, 0)

        tails(lambda c: c.start())
        unused(lambda c: c.start())
        tails(lambda c: c.wait())
        unused(lambda c: c.wait())

    def copy(r, k):
        slot = pos_ref[(i * tm + r) * TOP_K + k]
        return pltpu.make_async_copy(_row_group(hn_ref, r), _row_group(xs_ref, slot), sem)

    def start(r, carry):
        for k in range(TOP_K):
            copy(r, k).start(priority=k)
        return carry

    lax.fori_loop(0, tm, start, 0, unroll=8)
    for _ in range(TOP_K):
        pltpu.make_async_copy(hn_ref, xs_ref.at[pl.ds(0, tm * ROW_GROUP), :], sem).wait()


def _dispatch(pos, tails, hn_rows, n_slots, tm):
    T = hn_rows.shape[0] // ROW_GROUP
    return pl.pallas_call(
        _dispatch_kernel,
        grid_spec=pltpu.PrefetchScalarGridSpec(
            num_scalar_prefetch=2,
            grid=(T // tm,),
            in_specs=[pl.BlockSpec((tm * ROW_GROUP, LANES), lambda i, *_: (i, 0))],
            out_specs=pl.BlockSpec(memory_space=pl.ANY),
            scratch_shapes=[pltpu.VMEM((MOE_TILE * ROW_GROUP, LANES), F32),
                            pltpu.SemaphoreType.DMA, pltpu.SemaphoreType.DMA],
        ),
        out_shape=jax.ShapeDtypeStruct((n_slots * ROW_GROUP, LANES), F32),
        compiler_params=_params(("arbitrary",)),
        name="moe_dispatch",
    )(pos, tails, hn_rows)


def _experts_kernel(te_ref, nt_ref, xs_ref, wg_hbm, wu_hbm, wd_hbm, ys_ref,
                    wg_buf, wu_buf, wd_buf, wg_scr, wu_scr, wd_scr, slot_ref, sem):
    i = pl.program_id(0)
    nt = nt_ref[0]
    e = te_ref[i]

    def fetch(expert, slot):
        return [pltpu.make_async_copy(hbm.at[expert], buf.at[slot], sem.at[slot, n])
                for n, (hbm, buf) in enumerate(((wg_hbm, wg_buf), (wu_hbm, wu_buf), (wd_hbm, wd_buf)))]

    @pl.when(i == 0)
    def _():
        slot_ref[0] = 0
        for c in fetch(e, 0):
            c.start()

    @pl.when((i < nt) & ((i == 0) | (te_ref[jnp.maximum(i - 1, 0)] != e)))
    def _():
        slot = slot_ref[0]
        for c in fetch(e, slot):
            c.wait()
        wg_scr[...] = wg_buf[slot].astype(BF16)
        wu_scr[...] = wu_buf[slot].astype(BF16)
        wd_scr[...] = wd_buf[slot].astype(BF16)
        nxt = lax.while_loop(lambda j: (j < nt) & (te_ref[jnp.minimum(j, nt - 1)] == e), lambda j: j + 1, i + 1)

        @pl.when(nxt < nt)
        def _():
            for c in fetch(te_ref[jnp.minimum(nxt, nt - 1)], 1 - slot):
                c.start()

        slot_ref[0] = 1 - slot

    @pl.when(i < nt)
    def _():
        x = _from_token_tiles(xs_ref[...])[0].astype(BF16)
        hmid = (_silu(_dot(x, wg_scr[...])) * _dot(x, wu_scr[...])).astype(BF16)
        ys_ref[...] = _to_token_tiles(_dot(hmid, wd_scr[...]))


def _experts(te, nt, xs_rows, w_ge, w_ue, w_de, n_tiles):
    M = MOE_TILE
    rows = pl.BlockSpec((M * ROW_GROUP, LANES), lambda i, te_ref, nt_ref: (jnp.minimum(i, nt_ref[0] - 1), 0))
    hbm = pl.BlockSpec(memory_space=pl.ANY)
    up, down = (D_MODEL, EXPERT_FF), (EXPERT_FF, D_MODEL)
    return pl.pallas_call(
        _experts_kernel,
        grid_spec=pltpu.PrefetchScalarGridSpec(
            num_scalar_prefetch=2,
            grid=(n_tiles,),
            in_specs=[rows, hbm, hbm, hbm],
            out_specs=rows,
            scratch_shapes=[pltpu.VMEM((2,) + up, F32), pltpu.VMEM((2,) + up, F32), pltpu.VMEM((2,) + down, F32),
                            pltpu.VMEM(up, BF16), pltpu.VMEM(up, BF16), pltpu.VMEM(down, BF16),
                            pltpu.SMEM((1,), I32), pltpu.SemaphoreType.DMA((2, 3))],
        ),
        out_shape=jax.ShapeDtypeStruct(xs_rows.shape, F32),
        input_output_aliases={2: 0},
        compiler_params=_params(("arbitrary",)),
        name="moe_experts",
    )(te, nt, xs_rows, w_ge, w_ue, w_de)


def _final_kernel(pos_ref, h_ref, route_ref, gain_ref, ys_ref, y_ref, buf, sem, *, tile0):
    i = pl.program_id(0)
    n = pl.num_programs(0)
    tm = h_ref.shape[0]

    def copy(step, r, k, slot_buf):
        slot = pos_ref[((tile0 + step) * tm + r) * TOP_K + k]
        return pltpu.make_async_copy(_row_group(ys_ref, slot), _row_group(buf.at[slot_buf], r * TOP_K + k),
                                     sem.at[slot_buf])

    def start_all(step, slot_buf):
        def body(r, carry):
            for k in range(TOP_K):
                copy(step, r, k, slot_buf).start(priority=k)
            return carry
        lax.fori_loop(0, tm, body, 0, unroll=8)

    cur = i % 2
    pl.when(i == 0)(lambda: start_all(0, 0))
    pl.when(i + 1 < n)(lambda: start_all(i + 1, 1 - cur))
    pltpu.make_async_copy(ys_ref.at[pl.ds(0, buf.shape[1]), :], buf.at[cur], sem.at[cur]).wait()

    rt = route_ref[...]
    w1 = rt[:, LANE_W1:LANE_W1 + 1]
    w2 = rt[:, LANE_W2:LANE_W2 + 1]
    o1, o2 = _from_token_tiles(buf[cur], groups=TOP_K)
    v = h_ref[...] + (w1 * o1 + w2 * o2)
    y_ref[...] = _rms(v) * gain_ref[...]


def _final(pos, h, route, gain, ys_rows, tile0, n_tiles, tm):
    return pl.pallas_call(
        functools.partial(_final_kernel, tile0=tile0),
        grid_spec=pltpu.PrefetchScalarGridSpec(
            num_scalar_prefetch=1,
            grid=(n_tiles,),
            in_specs=[
                pl.BlockSpec((tm, D_MODEL), lambda i, *_: (tile0 + i, 0)),
                pl.BlockSpec((tm, LANES), lambda i, *_: (tile0 + i, 0)),
                pl.BlockSpec((1, D_MODEL), lambda i, *_: (0, 0)),
                pl.BlockSpec(memory_space=pl.ANY),
            ],
            out_specs=pl.BlockSpec((tm, D_MODEL), lambda i, *_: (i, 0)),
            scratch_shapes=[pltpu.VMEM((2, tm * TOP_K * ROW_GROUP, LANES), F32), pltpu.SemaphoreType.DMA((2,))],
        ),
        out_shape=jax.ShapeDtypeStruct((n_tiles * tm, D_MODEL), F32),
        compiler_params=_params(("arbitrary",)),
        name="final",
    )(pos, h, route, gain, ys_rows)


def _rope_tables(pos):
    half = RET_HEAD_QK // 2
    inv = 1.0 / (ROPE_BASE ** jnp.linspace(0.0, 1.0, half, dtype=F32))
    ang = jnp.repeat(pos[:, None] * inv[None, :], 2, axis=-1)
    cos, sin = jnp.cos(ang), jnp.sin(ang)
    even = (jnp.arange(RET_HEAD_QK) % 2 == 0)[None, :]
    sa = jnp.where(even, -sin, 0.0)
    sb = jnp.where(even, 0.0, sin)
    return cos, sa, sb


def _tile_sizes(Tp, Ts, Lp):
    g = math.gcd(Tp, Ts)
    return dict(
        xnorm=math.gcd(g, 512),
        proj=(Tp + Ts) // 4,
        mix=math.gcd(g, 256),
        plan=math.gcd(Tp + Ts, 256),
        dispatch=math.gcd(g, 256),
        final=math.gcd(g, 256),
        scan=math.gcd(Lp, 128),
    )


def kernel(x_prompt, x_sample, state_hgrn, state_ret, norm_mix, w_in, hg_lb_logits, hg_norm,
           w_branch_hg, w_branch_ret, b_gate, w_out, norm_ffn, w_router_group, b_router_group,
           w_router_expert, b_router_expert, w_expert_gate, w_expert_up, w_expert_down, norm_final):
    assert w_in.shape == (1, D_MODEL, IN_WIDTH), "single-layer trunk only"
    Bp, Lp, _ = x_prompt.shape
    Bs, Ls, _ = x_sample.shape
    Tp, Ts = Bp * Lp, Bs * Ls
    T = Tp + Ts
    ts = _tile_sizes(Tp, Ts, Lp)
    xp = x_prompt.reshape(Tp, D_MODEL)
    xs = x_sample.reshape(Ts, D_MODEL)

    lower = jnp.cumsum(jax.nn.softmax(hg_lb_logits.astype(F32), axis=0), axis=0)[0]
    act = _act_coefficients(lower, b_gate[0])
    tab_p = _rope_tables(jnp.arange(Lp, dtype=F32))
    tab_s = _rope_tables(jnp.arange(Ls, dtype=F32) + jnp.float32(PAST_LEN))
    tabs = tuple(jnp.concatenate([jnp.tile(a, (Bp, 1)), jnp.tile(b, (Bs, 1))]) for a, b in zip(tab_p, tab_s))
    xn = _xnorm(xp, xs, norm_mix, ts["xnorm"])
    P = _in_proj(xn, w_in[0], act, tabs, ts["proj"])

    bb = math.gcd(Bs, 8)
    o_hg_p, shp = _hgrn_prompt(P, hg_norm, Bp, Lp, ts["scan"])
    o_hg_s, shs = _hgrn_sample(P, hg_norm, state_hgrn[0].astype(F32), Bs, Ls, bb, Tp)
    o_r_p, srp = _ret_prompt(P, Bp, Lp, ts["scan"])
    o_r_s, srs = _ret_sample(P, state_ret[0].astype(F32), Bs, Ls, bb, Tp)

    wbh = w_branch_hg[0].astype(BF16)
    wbr = w_branch_ret[0].astype(BF16)
    wout = w_out[0].astype(BF16)
    pad = LANES - N_GROUPS - N_EXPERTS
    wr = jnp.concatenate([w_router_group[0], w_router_expert[0], jnp.zeros((D_MODEL, pad), F32)], axis=1)
    wr_hi = wr.astype(BF16)
    wr_lo = (wr - wr_hi.astype(F32)).astype(BF16)
    br = jnp.concatenate([b_router_group[0], b_router_expert[0], jnp.zeros((pad,), F32)])[None, :]
    h, hn_rows, route = _mix(list(o_hg_p) + [o_hg_s], list(o_r_p) + [o_r_s], P, xp, xs, wbh, wbr, wout,
                             norm_ffn, wr_hi, wr_lo, br, ts["mix"])

    n_tiles = -(-T * TOP_K // MOE_TILE) + N_EXPERTS
    assert n_tiles <= LANES
    pos_slab, te_slab, tail_slab = _plan(route, ts["plan"])
    pos = pos_slab[:, :TOP_K].reshape(-1)
    te = te_slab[:n_tiles, 0]
    tails = tail_slab[0, :N_EXPERTS + 1]
    nt = tail_slab[0, N_EXPERTS:N_EXPERTS + 1]
    xs_rows = _dispatch(pos, tails, hn_rows, n_tiles * MOE_TILE, ts["dispatch"])
    ys_rows = _experts(te, nt, xs_rows, w_expert_gate[0], w_expert_up[0], w_expert_down[0], n_tiles)

    tf = ts["final"]
    gain = norm_final[None, :]
    y_p = _final(pos, h, route, gain, ys_rows, 0, Tp // tf, tf)
    y_s = _final(pos, h, route, gain, ys_rows, Tp // tf, Ts // tf, tf)
    return (y_p.reshape(Bp, Lp, D_MODEL), y_s.reshape(Bs, Ls, D_MODEL),
            shp[None], srp[None], shs[None], srs[None])
```

```python
import functools
import math

import jax
import jax.numpy as jnp
from jax import lax
from jax.experimental import pallas as pl
from jax.experimental.pallas import tpu as pltpu

F32 = jnp.float32
BF16 = jnp.bfloat16
I32 = jnp.int32

D_MODEL = 2048
PAST_LEN = 16384
HG_HEAD_V = 128
HG_EXPAND = 128
HG_VAL_DIM = D_MODEL // 2
HG_HEADS = HG_VAL_DIM // HG_HEAD_V
HG_KEY_DIM = HG_HEADS * HG_EXPAND
RET_HEAD_V = 256
RET_HEAD_QK = 128
RET_V_DIM = D_MODEL // 2
RET_HEADS = RET_V_DIM // RET_HEAD_V
RET_QK_DIM = RET_HEADS * RET_HEAD_QK
IN_WIDTH = 2 * HG_KEY_DIM + 2 * HG_VAL_DIM + 2 * RET_QK_DIM + 2 * RET_V_DIM + 2 * D_MODEL
CHUNK = 64
SCAN_CHUNK = 128
N_GROUPS = 4
EXPERTS_PER_GROUP = 8
N_EXPERTS = N_GROUPS * EXPERTS_PER_GROUP
TOP_K = 2
EXPERT_FF = D_MODEL // 4
ROPE_BASE = 10000.0
EPS = 1e-6

LANES = 128
SUBLANES = 8
VMEM_LIMIT = 56 * 1024 * 1024
PROJ_MC = 256
PROJ_TN = 512
SEG_HQ, SEG_HF, SEG_HI, SEG_HG, SEG_RQ, SEG_RK, SEG_RV, SEG_RG, SEG_GA = 0, 2, 4, 6, 8, 9, 10, 12, 14
SAFE_EXP_SPAN = 80.0
ROW_GROUP = D_MODEL // LANES
MOE_TILE = 256
LANE_E1, LANE_E2, LANE_W1, LANE_W2 = 0, 1, 2, 3
ROUTE_EXPERT_LANE = N_GROUPS


def _params(sem):
    return pltpu.CompilerParams(dimension_semantics=sem, vmem_limit_bytes=VMEM_LIMIT)


def _silu(z):
    return z * jax.nn.sigmoid(z)


def _dot(a, b):
    return jnp.dot(a, b, preferred_element_type=F32)


def _dot_nt(a, b):
    return lax.dot_general(a, b, (((1,), (1,)), ((), ())), preferred_element_type=F32)


def _dot_tn(a, b):
    return lax.dot_general(a, b, (((0,), (0,)), ((), ())), preferred_element_type=F32)


def _rms(x):
    return x * lax.rsqrt(jnp.mean(x * x, axis=-1, keepdims=True) + EPS)


def _to_token_tiles(x):
    M = x.shape[0]
    pieces = jnp.stack([x[:, s * LANES:(s + 1) * LANES] for s in range(ROW_GROUP)], axis=0)
    return jnp.swapaxes(pieces, 0, 1).reshape(M * ROW_GROUP, LANES)


def _from_token_tiles(rows, groups=1):
    M = rows.shape[0] // (groups * ROW_GROUP)
    t = jnp.swapaxes(rows.reshape(M, groups * ROW_GROUP, LANES), 0, 1)
    return [jnp.concatenate([t[g * ROW_GROUP + s] for s in range(ROW_GROUP)], axis=1) for g in range(groups)]


def _xnorm_kernel(xp_ref, xs_ref, gain_ref, o_ref, *, n_prompt_tiles):
    i = pl.program_id(0)

    def norm(x_ref):
        o_ref[...] = (_rms(x_ref[...]) * gain_ref[...]).astype(BF16)

    pl.when(i < n_prompt_tiles)(lambda: norm(xp_ref))
    pl.when(i >= n_prompt_tiles)(lambda: norm(xs_ref))


def _xnorm(xp, xs, gain, tm):
    npt, nst = xp.shape[0] // tm, xs.shape[0] // tm
    return pl.pallas_call(
        functools.partial(_xnorm_kernel, n_prompt_tiles=npt),
        grid=(npt + nst,),
        in_specs=[
            pl.BlockSpec((tm, D_MODEL), lambda i: (jnp.minimum(i, npt - 1), 0)),
            pl.BlockSpec((tm, D_MODEL), lambda i: (jnp.maximum(i - npt, 0), 0)),
            pl.BlockSpec((1, D_MODEL), lambda i: (0, 0)),
        ],
        out_specs=pl.BlockSpec((tm, D_MODEL), lambda i: (i, 0)),
        out_shape=jax.ShapeDtypeStruct(((npt + nst) * tm, D_MODEL), BF16),
        compiler_params=_params(("parallel",)),
        name="xnorm",
    )(xp, xs, gain)


ACT_ALPHA, ACT_BETA, ACT_GAMMA, ACT_DELTA, ACT_EPS = 0, 1, 2, 3, 4


def _act_coefficients(lower, b_gate):
    z = jnp.zeros((IN_WIDTH,), F32)
    seg = lambda a, b: slice(a * PROJ_TN, b * PROJ_TN)
    alpha = z.at[seg(SEG_HI, SEG_HG)].set(1.0).at[seg(SEG_RQ, SEG_RG)].set(1.0)
    beta = z.at[seg(SEG_GA, IN_WIDTH // PROJ_TN)].set(b_gate)
    gamma = (z.at[seg(SEG_HQ, SEG_HF)].set(HG_EXPAND ** -0.5)
             .at[seg(SEG_HG, SEG_RQ)].set(1.0).at[seg(SEG_RG, SEG_GA)].set(1.0))
    delta = z.at[seg(SEG_HF, SEG_HI)].set(1.0 - lower).at[seg(SEG_GA, IN_WIDTH // PROJ_TN)].set(1.0)
    eps = z.at[seg(SEG_HF, SEG_HI)].set(lower)
    return jnp.stack([alpha, beta, gamma, delta, eps, z, z, z])


def _in_proj_kernel(x_ref, w_ref, act_ref, cos_ref, sa_ref, sb_ref, o_ref):
    j = pl.program_id(1)
    w = w_ref[...].astype(BF16)
    row = lambda r: act_ref[r:r + 1, :]
    tm = x_ref.shape[0]
    mc = math.gcd(tm, PROJ_MC)
    for m in range(0, tm, mc):
        p = _dot(x_ref[m:m + mc, :], w)
        o_ref[m:m + mc, :] = (row(ACT_ALPHA) * p + row(ACT_EPS)
                              + jax.nn.sigmoid(p + row(ACT_BETA)) * (row(ACT_GAMMA) * p + row(ACT_DELTA)))

    @pl.when((j >= SEG_RQ) & (j < SEG_RV))
    def _():
        scale = jnp.where(j == SEG_RK, RET_HEAD_QK ** -0.5, 1.0).astype(F32)
        cos, sa, sb = cos_ref[...], sa_ref[...], sb_ref[...]
        for hh in range(PROJ_TN // LANES):
            cols = slice(hh * LANES, (hh + 1) * LANES)
            xs = o_ref[:, cols]
            r = xs * cos + pltpu.roll(xs, LANES - 1, 1) * sa + pltpu.roll(xs, 1, 1) * sb
            o_ref[:, cols] = r * scale


def _in_proj(xn, w_in, act, tabs, tm):
    T = xn.shape[0]
    once = dict(pipeline_mode=pl.Buffered(1))
    tab = pl.BlockSpec((tm, LANES), lambda i, j: (i, 0), **once)
    return pl.pallas_call(
        _in_proj_kernel,
        grid=(T // tm, IN_WIDTH // PROJ_TN),
        in_specs=[
            pl.BlockSpec((tm, D_MODEL), lambda i, j: (i, 0), **once),
            pl.BlockSpec((D_MODEL, PROJ_TN), lambda i, j: (0, j)),
            pl.BlockSpec((SUBLANES, PROJ_TN), lambda i, j: (0, j)),
            tab, tab, tab,
        ],
        out_specs=pl.BlockSpec((tm, PROJ_TN), lambda i, j: (i, j)),
        out_shape=jax.ShapeDtypeStruct((T, IN_WIDTH), F32),
        compiler_params=_params(("parallel", "arbitrary")),
        name="in_proj",
    )(xn, w_in, act, *tabs)


def _prefix_matrix(C, G):
    r = lax.broadcasted_iota(I32, (C, 3 * C), 0)
    c = lax.broadcasted_iota(I32, (C, 3 * C), 1) % C
    return jnp.where((c <= r) & (c // G == r // G), 1.0, 0.0).astype(BF16)


def _prefix_sum(g, pm):
    g1 = g.astype(BF16)
    r1 = g - g1.astype(F32)
    g2 = r1.astype(BF16)
    g3 = (r1 - g2.astype(F32)).astype(BF16)
    return _dot(pm, jnp.concatenate([g1, g2, g3], axis=0))


def _state_decay_column(d_row):
    r = lax.broadcasted_iota(I32, (LANES, LANES), 0)
    c = lax.broadcasted_iota(I32, (LANES, LANES), 1)
    return jnp.sum(jnp.where(r == c, jnp.broadcast_to(d_row, (LANES, LANES)), 0.0), axis=1, keepdims=True)


def _hgrn_chunk(q, f, g, v, states, pm, tmp_ref, NG, factorised):
    C, W = q.shape
    G, H = C // NG, W // LANES
    hs = [slice(h * LANES, (h + 1) * LANES) for h in range(H)]
    gs = [slice(n * G, (n + 1) * G) for n in range(NG)]
    k = 1.0 - f
    b = _prefix_sum(g, pm)
    b3 = b.reshape(NG, G, W)
    b_end = b3[:, G - 1:G, :]
    q3, k3 = q.reshape(NG, G, W), k.reshape(NG, G, W)
    vb = v.astype(BF16)
    if factorised:
        b_mid = b3[:, G // 2 - 1:G // 2, :]
        qm3 = q3 * jnp.exp(b3 - b_mid)
        km3 = k3 * jnp.exp(b_mid - b3)
        qb = (qm3 * jnp.exp(b_mid)).reshape(C, W)
        kd = (km3 * jnp.exp(b_end - b_mid)).reshape(C, W)
        qm = qm3.reshape(C, W).astype(BF16)
        km = km3.reshape(C, W).astype(BF16)
        row = lax.broadcasted_iota(I32, (C, C), 0)
        col = lax.broadcasted_iota(I32, (C, C), 1)
        amask = (row >= col) & (row // G == col // G)
        o_intra = []
        for h in range(H):
            att = jnp.where(amask, _dot_nt(qm[:, hs[h]], km[:, hs[h]]), 0.0)
            o_intra.append(_dot(att.astype(BF16), vb[:, hs[h]]))
    else:
        qb = (q3 * jnp.exp(b3)).reshape(C, W)
        kd = (k3 * jnp.exp(b_end - b3)).reshape(C, W)
        trow = lax.broadcasted_iota(I32, (G, LANES), 0)
        o_intra = []
        for h in range(H):
            parts = []
            for n in range(NG):
                bh, qh = b[gs[n], hs[h]], q[gs[n], hs[h]]
                tmp_ref[0] = bh
                tmp_ref[1] = k[gs[n], hs[h]]
                tmp_ref[2] = v[gs[n], hs[h]]

                def body(s, acc, bh=bh, qh=qh):
                    bs = tmp_ref[0, pl.ds(s, 1), :]
                    ks = tmp_ref[1, pl.ds(s, 1), :]
                    vs = tmp_ref[2, pl.ds(s, 1), :]
                    w = jnp.where(trow >= s, jnp.exp(jnp.minimum(bh - bs, 0.0)), 0.0)
                    return acc + jnp.sum(qh * ks * w, axis=1, keepdims=True) * vs

                parts.append(lax.fori_loop(0, G, body, jnp.zeros((G, LANES), F32)))
            o_intra.append(parts[0] if NG == 1 else jnp.concatenate(parts, axis=0))
    d = jnp.exp(b_end)
    outs, new_states = [], [[None] * H for _ in range(NG)]
    for h in range(H):
        inter = []
        for n in range(NG):
            S = states[n][h]
            inter.append(_dot(qb[gs[n], hs[h]].astype(BF16), S.astype(BF16)))
            upd = _dot_tn(kd[gs[n], hs[h]].astype(BF16), v[gs[n], hs[h]].astype(BF16))
            new_states[n][h] = _state_decay_column(d[n, :, hs[h]]) * S + upd
        inter = inter[0] if NG == 1 else jnp.concatenate(inter, axis=0)
        outs.append(inter + o_intra[h])
    return jnp.concatenate(outs, axis=1), new_states


def _decay_is_safe(g, G):
    R, W = g.shape
    return jnp.min(jnp.sum(g.reshape(R // G, G, W), axis=1)) >= -SAFE_EXP_SPAN


def _hgrn_epilogue(o_scr, gain_ref, gate_ref, o_ref):
    o_ref[...] = (_rms(o_scr[...]) * gain_ref[...] * gate_ref[...]).astype(o_ref.dtype)


def _hgrn_prompt_kernel(*refs, C, NB):
    ins, (gain_ref,), rest = refs[:4 * NB], refs[4 * NB:4 * NB + 1], refs[4 * NB + 1:]
    o_refs, (st_ref, s_scr, g_scr, o_scr, tmp_ref) = rest[:NB], rest[NB:]
    seq = [ins[4 * n:4 * n + 4] for n in range(NB)]
    l = pl.program_id(0)
    Lb = o_refs[0].shape[0]

    @pl.when(l == 0)
    def _():
        s_scr[...] = jnp.zeros_like(s_scr)

    for n in range(NB):
        g_scr[n] = jnp.log(seq[n][1][...])
    pm = _prefix_matrix(C, C)

    def run(factorised):
        def body(ci, carry):
            rows = pl.ds(pl.multiple_of(ci * C, C), C)
            args = [(seq[n][0][rows, :], seq[n][1][rows, :], g_scr[n, rows, :], seq[n][2][rows, :],
                     [[s_scr[n, h] for h in range(HG_HEADS)]]) for n in range(NB)]
            outs = [_hgrn_chunk(*a, pm, tmp_ref, 1, factorised) for a in args]
            for n, (o, new) in enumerate(outs):
                o_scr[n, rows, :] = o
                for h in range(HG_HEADS):
                    s_scr[n, h] = new[0][h]
            return carry
        lax.fori_loop(0, Lb // C, body, 0)

    safe = _decay_is_safe(g_scr[...].reshape(NB * Lb, -1), C // 2)
    pl.when(safe)(lambda: run(True))
    pl.when(jnp.logical_not(safe))(lambda: run(False))
    for n in range(NB):
        o_refs[n][...] = (_rms(o_scr[n]) * gain_ref[...] * seq[n][3][...]).astype(o_refs[n].dtype)

    @pl.when(l == pl.num_programs(0) - 1)
    def _():
        st_ref[...] = s_scr[...]


def _hgrn_prompt(P, gain, B, L, Lb):
    C = math.gcd(L, SCAN_CHUNK)
    nl = L // Lb
    W = HG_KEY_DIM
    blk = lambda b, seg: pl.BlockSpec((Lb, W), lambda l, b=b, seg=seg: (b * nl + l, seg))
    outs = pl.pallas_call(
        functools.partial(_hgrn_prompt_kernel, C=C, NB=B),
        grid=(nl,),
        in_specs=[blk(b, seg) for b in range(B) for seg in range(4)] + [pl.BlockSpec((1, W), lambda l: (0, 0))],
        out_specs=[pl.BlockSpec((Lb, W), lambda l: (l, 0)) for _ in range(B)]
        + [pl.BlockSpec((B, HG_HEADS, HG_EXPAND, HG_HEAD_V), lambda l: (0, 0, 0, 0))],
        out_shape=[jax.ShapeDtypeStruct((L, HG_VAL_DIM), BF16) for _ in range(B)]
        + [jax.ShapeDtypeStruct((B, HG_HEADS, HG_EXPAND, HG_HEAD_V), F32)],
        scratch_shapes=[pltpu.VMEM((B, HG_HEADS, HG_EXPAND, HG_HEAD_V), F32), pltpu.VMEM((B, Lb, W), F32),
                        pltpu.VMEM((B, Lb, W), F32), pltpu.VMEM((3, C, LANES), F32)],
        compiler_params=_params(("arbitrary",)),
        name="hgrn_prompt",
    )(*([P] * (4 * B)), gain)
    return outs[:B], outs[B]


def _hgrn_sample_kernel(q_ref, f_ref, v_ref, gate_ref, gain_ref, s_ref, o_ref, so_ref, o_scr, tmp_ref, *, G):
    NG = s_ref.shape[0]
    g = jnp.log(f_ref[...])
    pm = _prefix_matrix(NG * G, G)

    def run(factorised):
        states = [[s_ref[n, h] for h in range(HG_HEADS)] for n in range(NG)]
        o, new = _hgrn_chunk(q_ref[...], f_ref[...], g, v_ref[...], states, pm, tmp_ref, NG, factorised)
        o_scr[...] = o
        for n in range(NG):
            for h in range(HG_HEADS):
                so_ref[n, h] = new[n][h]

    safe = _decay_is_safe(g, G // 2)
    pl.when(safe)(lambda: run(True))
    pl.when(jnp.logical_not(safe))(lambda: run(False))
    _hgrn_epilogue(o_scr, gain_ref, gate_ref, o_ref)


def _hgrn_sample(P, gain, state, Bs, Ls, Bb, row0):
    W = HG_KEY_DIM
    rows = Bb * Ls
    blk0 = row0 // rows

    def col(seg):
        return pl.BlockSpec((rows, W), lambda i, seg=seg: (blk0 + i, seg))

    st = pl.BlockSpec((Bb, HG_HEADS, HG_EXPAND, HG_HEAD_V), lambda i: (i, 0, 0, 0))
    return pl.pallas_call(
        functools.partial(_hgrn_sample_kernel, G=Ls),
        grid=(Bs // Bb,),
        in_specs=[col(0), col(1), col(2), col(3), pl.BlockSpec((1, W), lambda i: (0, 0)), st],
        out_specs=[pl.BlockSpec((rows, W), lambda i: (i, 0)), st],
        out_shape=[
            jax.ShapeDtypeStruct((Bs * Ls, HG_VAL_DIM), BF16),
            jax.ShapeDtypeStruct((Bs, HG_HEADS, HG_EXPAND, HG_HEAD_V), F32),
        ],
        scratch_shapes=[pltpu.VMEM((rows, W), F32), pltpu.VMEM((3, Ls, LANES), F32)],
        compiler_params=_params(("parallel",)),
        name="hgrn_sample",
    )(P, P, P, P, gain, state)


def _ret_chunk(q, k, v, states, dm_ref, qw_ref, kw_ref, sd_ref):
    outs, new_states = [], []
    C = q.shape[0]
    qb, kb, vb = q.astype(BF16), k.astype(BF16), v.astype(BF16)
    for h in range(RET_HEADS):
        qs = slice(h * RET_HEAD_QK, (h + 1) * RET_HEAD_QK)
        vs = slice(h * RET_HEAD_V, (h + 1) * RET_HEAD_V)
        att = (_dot_nt(qb[:, qs], kb[:, qs]) * dm_ref[h]).astype(BF16)
        kw = (k[:, qs] * kw_ref[h]).astype(BF16)
        if C <= 2 * SUBLANES:
            both = _dot(jnp.concatenate([att, kw.T], axis=0), vb[:, vs])
            intra, upd = both[:C], both[C:]
        else:
            intra, upd = _dot(att, vb[:, vs]), _dot_tn(kw, vb[:, vs])
        o = intra + _dot((q[:, qs] * qw_ref[h]).astype(BF16), states[h].astype(BF16))
        new_states.append(sd_ref[h, 0:1, 0:1] * states[h] + upd)
        outs.append(o)
    return jnp.concatenate(outs, axis=1), new_states


def _ret_epilogue(o_scr, gate_ref, o_ref):
    for h in range(RET_HEADS):
        vs = slice(h * RET_HEAD_V, (h + 1) * RET_HEAD_V)
        o_ref[:, vs] = (_rms(o_scr[:, vs]) * gate_ref[:, vs]).astype(o_ref.dtype)


def _ret_prompt_kernel(*refs, NB):
    ins, (dm_ref, qw_ref, kw_ref, sd_ref), rest = refs[:4 * NB], refs[4 * NB:4 * NB + 4], refs[4 * NB + 4:]
    o_refs, (st_ref, s_scr, o_scr) = rest[:NB], rest[NB:]
    seq = [ins[4 * n:4 * n + 4] for n in range(NB)]
    l = pl.program_id(0)
    Lb = o_refs[0].shape[0]
    C = dm_ref.shape[1]

    @pl.when(l == 0)
    def _():
        s_scr[...] = jnp.zeros_like(s_scr)

    def body(ci, carry):
        rows = pl.ds(pl.multiple_of(ci * C, C), C)
        args = [(seq[n][0][rows, :], seq[n][1][rows, :], seq[n][2][rows, :],
                 [s_scr[n, h] for h in range(RET_HEADS)]) for n in range(NB)]
        outs = [_ret_chunk(*a, dm_ref, qw_ref, kw_ref, sd_ref) for a in args]
        for n, (o, new) in enumerate(outs):
            o_scr[n, rows, :] = o
            for h in range(RET_HEADS):
                s_scr[n, h] = new[h]
        return carry

    lax.fori_loop(0, Lb // C, body, 0)
    for n in range(NB):
        _ret_epilogue(o_scr.at[n], seq[n][3], o_refs[n])

    @pl.when(l == pl.num_programs(0) - 1)
    def _():
        st_ref[...] = s_scr[...]


def _ret_tables(C):
    log_gamma = jnp.log(1.0 - jnp.exp2(-5.0 - jnp.arange(RET_HEADS, dtype=F32)))
    idx = jnp.arange(C, dtype=F32)
    rel = idx[:, None] - idx[None, :]
    tri = jnp.tril(jnp.ones((C, C), dtype=bool))
    dmat = jnp.exp(jnp.where(tri[None], log_gamma[:, None, None] * rel[None], -jnp.inf))
    qw = jnp.exp(log_gamma[:, None] * (idx[None, :] + 1.0))[..., None]
    kw = jnp.exp(log_gamma[:, None] * (C - 1.0 - idx[None, :]))[..., None]
    sdec = jnp.exp(log_gamma * C)[:, None, None]
    bc = lambda a: jnp.broadcast_to(a, (RET_HEADS, a.shape[1], LANES))
    return dmat, bc(qw), bc(kw), jnp.broadcast_to(sdec, (RET_HEADS, SUBLANES, LANES))


_QK0 = (2 * HG_KEY_DIM + 2 * HG_VAL_DIM) // RET_QK_DIM
_V0 = (2 * HG_KEY_DIM + 2 * HG_VAL_DIM + 2 * RET_QK_DIM) // RET_V_DIM


def _full3(a):
    return pl.BlockSpec(a.shape, lambda *_: (0, 0, 0))


def _ret_prompt(P, B, L, Lb):
    C = math.gcd(L, SCAN_CHUNK)
    nl = L // Lb
    tabs = _ret_tables(C)
    st_shape = (B, RET_HEADS, RET_HEAD_QK, RET_HEAD_V)
    blk = lambda b, w, c: pl.BlockSpec((Lb, w), lambda l, b=b, c=c: (b * nl + l, c))
    per_seq = lambda b: [blk(b, RET_QK_DIM, _QK0), blk(b, RET_QK_DIM, _QK0 + 1),
                         blk(b, RET_V_DIM, _V0), blk(b, RET_V_DIM, _V0 + 1)]
    outs = pl.pallas_call(
        functools.partial(_ret_prompt_kernel, NB=B),
        grid=(nl,),
        in_specs=[spec for b in range(B) for spec in per_seq(b)] + [_full3(t) for t in tabs],
        out_specs=[pl.BlockSpec((Lb, RET_V_DIM), lambda l: (l, 0)) for _ in range(B)]
        + [pl.BlockSpec(st_shape, lambda l: (0, 0, 0, 0))],
        out_shape=[jax.ShapeDtypeStruct((L, RET_V_DIM), BF16) for _ in range(B)]
        + [jax.ShapeDtypeStruct(st_shape, F32)],
        scratch_shapes=[pltpu.VMEM(st_shape, F32), pltpu.VMEM((B, Lb, RET_V_DIM), F32)],
        compiler_params=_params(("arbitrary",)),
        name="ret_prompt",
    )(*([P] * (4 * B)), *tabs)
    return outs[:B], outs[B]


def _ret_sample_kernel(q_ref, k_ref, v_ref, gate_ref, s_ref, dm_ref, qw_ref, kw_ref, sd_ref,
                       o_ref, so_ref, o_scr, *, C):
    Bb = s_ref.shape[0]
    args = [(q_ref[bb * C:(bb + 1) * C, :], k_ref[bb * C:(bb + 1) * C, :], v_ref[bb * C:(bb + 1) * C, :],
             [s_ref[bb, h] for h in range(RET_HEADS)]) for bb in range(Bb)]
    outs = [_ret_chunk(*a, dm_ref, qw_ref, kw_ref, sd_ref) for a in args]
    for bb, (o, new) in enumerate(outs):
        o_scr[bb * C:(bb + 1) * C, :] = o
        for h in range(RET_HEADS):
            so_ref[bb, h] = new[h]
    _ret_epilogue(o_scr, gate_ref, o_ref)


def _ret_sample(P, state, Bs, Ls, Bb, row0):
    tabs = _ret_tables(Ls)
    rows = Bb * Ls
    blk0 = row0 // rows
    st = pl.BlockSpec((Bb, RET_HEADS, RET_HEAD_QK, RET_HEAD_V), lambda i: (i, 0, 0, 0))
    return pl.pallas_call(
        functools.partial(_ret_sample_kernel, C=Ls),
        grid=(Bs // Bb,),
        in_specs=[
            pl.BlockSpec((rows, RET_QK_DIM), lambda i: (blk0 + i, _QK0)),
            pl.BlockSpec((rows, RET_QK_DIM), lambda i: (blk0 + i, _QK0 + 1)),
            pl.BlockSpec((rows, RET_V_DIM), lambda i: (blk0 + i, _V0)),
            pl.BlockSpec((rows, RET_V_DIM), lambda i: (blk0 + i, _V0 + 1)),
            st,
        ] + [_full3(t) for t in tabs],
        out_specs=[pl.BlockSpec((rows, RET_V_DIM), lambda i: (i, 0)), st],
        out_shape=[
            jax.ShapeDtypeStruct((Bs * Ls, RET_V_DIM), BF16),
            jax.ShapeDtypeStruct((Bs, RET_HEADS, RET_HEAD_QK, RET_HEAD_V), F32),
        ],
        scratch_shapes=[pltpu.VMEM((rows, RET_V_DIM), F32)],
        compiler_params=_params(("parallel",)),
        name="ret_sample",
    )(P, P, P, P, state, *tabs)


def _split_dot(a, w_hilo):
    a_hi = a.astype(BF16)
    a_lo = (a - a_hi.astype(F32)).astype(BF16)
    hi = _dot(a_hi, w_hilo)
    return hi[:, :LANES] + (hi[:, LANES:] + _dot(a_lo, w_hilo[:, :LANES]))


def _route(logits):
    lane = lax.broadcasted_iota(I32, logits.shape, 1)
    neg = jnp.float32(-jnp.inf)
    big = jnp.int32(LANES)
    gmask = lane < N_GROUPS
    gl = jnp.where(gmask, logits, neg)
    gmax = jnp.max(gl, axis=1, keepdims=True)
    gu = jnp.where(gmask, jnp.exp(gl - gmax), 0.0)
    gp = gu / jnp.sum(gu, axis=1, keepdims=True)
    g_w = jnp.max(gp, axis=1, keepdims=True)
    g_i = jnp.min(jnp.where(gmask & (gp == g_w), lane, big), axis=1, keepdims=True)
    lo = ROUTE_EXPERT_LANE + g_i * EXPERTS_PER_GROUP
    emask = (lane >= lo) & (lane < lo + EXPERTS_PER_GROUP)
    el = jnp.where(emask, logits, neg)
    e1 = jnp.max(el, axis=1, keepdims=True)
    i1 = jnp.min(jnp.where(emask & (el == e1), lane, big), axis=1, keepdims=True)
    el2 = jnp.where(lane == i1, neg, el)
    e2 = jnp.max(el2, axis=1, keepdims=True)
    i2 = jnp.min(jnp.where(emask & (lane != i1) & (el2 == e2), lane, big), axis=1, keepdims=True)
    u2 = jnp.exp(e2 - e1)
    den = 1.0 + u2
    w1 = (1.0 / den) * g_w
    w2 = (u2 / den) * g_w
    id1 = (i1 - ROUTE_EXPERT_LANE).astype(F32)
    id2 = (i2 - ROUTE_EXPERT_LANE).astype(F32)
    return (jnp.where(lane == LANE_E1, id1, 0.0) + jnp.where(lane == LANE_E2, id2, 0.0)
            + jnp.where(lane == LANE_W1, w1, 0.0) + jnp.where(lane == LANE_W2, w2, 0.0))


def _pick(i, bounds, refs):
    val = refs[-1][...]
    for n in range(len(refs) - 2, -1, -1):
        val = jnp.where(i < bounds[n + 1], refs[n][...], val)
    return val


def _mix_kernel(*refs, bounds):
    ns = len(bounds) - 1
    ohg_refs, or_refs, refs = refs[:ns], refs[ns:2 * ns], refs[2 * ns:]
    (ga0_ref, ga1_ref, gb0_ref, gb1_ref, xp_ref, xs_ref, wbh_ref, wbr_ref, wout_ref, nffn_ref,
     wr_ref, br_ref, h_ref, hn_ref, route_ref) = refs
    i = pl.program_id(0)
    tm = h_ref.shape[0]
    is_prompt = i < bounds[-2]
    p_hg = _dot(_pick(i, bounds, ohg_refs), wbh_ref[...])
    p_r = _dot(_pick(i, bounds, or_refs), wbr_ref[...])
    half = D_MODEL // 2
    m0 = ga0_ref[...] * p_hg[:, :half] + gb0_ref[...] * p_r[:, :half]
    m1 = ga1_ref[...] * p_hg[:, half:] + gb1_ref[...] * p_r[:, half:]
    mixed = jnp.concatenate([m0, m1], axis=1).astype(BF16)
    x = jnp.where(is_prompt, xp_ref[...], xs_ref[...])
    h = x + _dot(mixed, wout_ref[...])
    h_ref[...] = h
    hn = _rms(h) * nffn_ref[...]
    for s in range(ROW_GROUP):
        hn_ref[pl.ds(s, tm, stride=ROW_GROUP), :] = hn[:, s * LANES:(s + 1) * LANES]
    route_ref[...] = _route(_split_dot(hn, wr_ref[...]) + br_ref[...])


def _mix(o_hg, o_r, P, xp, xs, wbh, wbr, wout, norm_ffn, wr, br, tm):
    npt, nst = xp.shape[0] // tm, xs.shape[0] // tm
    bounds = [0]
    for a in o_hg:
        bounds.append(bounds[-1] + a.shape[0] // tm)
    T = (npt + nst) * tm
    half = D_MODEL // 2
    ga = (IN_WIDTH - 2 * D_MODEL) // half

    def pcol(c):
        return pl.BlockSpec((tm, half), lambda i, c=c: (i, c))

    def const(a):
        return pl.BlockSpec(a.shape, lambda i: (0, 0), pipeline_mode=pl.Buffered(1))

    def prompt(w):
        return pl.BlockSpec((tm, w), lambda i: (jnp.minimum(i, npt - 1), 0))

    def sample(w):
        return pl.BlockSpec((tm, w), lambda i: (jnp.maximum(i - npt, 0), 0))

    def source(n):
        lo, hi = bounds[n], bounds[n + 1]
        return pl.BlockSpec((tm, half), lambda i: (jnp.clip(i - lo, 0, hi - lo - 1), 0))

    sources = [source(n) for n in range(len(o_hg))]
    return pl.pallas_call(
        functools.partial(_mix_kernel, bounds=tuple(bounds)),
        grid=(npt + nst,),
        in_specs=sources + sources + [
            pcol(ga), pcol(ga + 1), pcol(ga + 2), pcol(ga + 3),
            prompt(D_MODEL), sample(D_MODEL),
            const(wbh), const(wbr), const(wout), const(norm_ffn), const(wr), const(br),
        ],
        out_specs=[
            pl.BlockSpec((tm, D_MODEL), lambda i: (i, 0)),
            pl.BlockSpec((tm * ROW_GROUP, LANES), lambda i: (i, 0)),
            pl.BlockSpec((tm, LANES), lambda i: (i, 0)),
        ],
        out_shape=[
            jax.ShapeDtypeStruct((T, D_MODEL), F32),
            jax.ShapeDtypeStruct((T * ROW_GROUP, LANES), F32),
            jax.ShapeDtypeStruct((T, LANES), F32),
        ],
        compiler_params=_params(("parallel",)),
        name="mix",
    )(*o_hg, *o_r, P, P, P, P, xp, xs, wbh, wbr, wout, norm_ffn, wr, br)


def _plan_kernel(route_ref, pos_ref, te_ref, tail_ref, cum_scr, *, tb):
    T = route_ref.shape[0]
    lane_t = lax.broadcasted_iota(I32, (tb, LANES), 1).astype(F32)
    r = lax.broadcasted_iota(I32, (tb, tb), 0)
    c = lax.broadcasted_iota(I32, (tb, tb), 1)
    strict_lower = jnp.where(r > c, 1.0, 0.0).astype(BF16)

    def onehots(rows):
        rt = route_ref[rows, :]
        a1 = jnp.where(lane_t == rt[:, LANE_E1:LANE_E1 + 1], 1.0, 0.0)
        a2 = jnp.where(lane_t == rt[:, LANE_E2:LANE_E2 + 1], 1.0, 0.0)
        return a1, a2

    def rank_body(bi, carry):
        rows = pl.ds(pl.multiple_of(bi * tb, tb), tb)
        a1, a2 = onehots(rows)
        m = a1 + a2
        cum_scr[rows, :] = _dot(strict_lower, m.astype(BF16)) + carry
        return carry + jnp.sum(m, axis=0, keepdims=True)

    counts = lax.fori_loop(0, T // tb, rank_body, jnp.zeros((1, LANES), F32))
    ntiles = jnp.floor((counts + (MOE_TILE - 1)) * (1.0 / MOE_TILE))
    rr = lax.broadcasted_iota(I32, (LANES, LANES), 0)
    cc = lax.broadcasted_iota(I32, (LANES, LANES), 1)
    upper = jnp.where(rr < cc, 1.0, 0.0).astype(BF16)
    tile0 = _dot(jnp.broadcast_to(ntiles, (SUBLANES, LANES)).astype(BF16), upper)[0:1, :]
    slot0 = tile0 * MOE_TILE

    def pos_body(bi, carry):
        rows = pl.ds(pl.multiple_of(bi * tb, tb), tb)
        a1, a2 = onehots(rows)
        base = cum_scr[rows, :] + slot0
        p1 = jnp.sum(a1 * base, axis=1, keepdims=True)
        p2 = jnp.sum(a2 * base, axis=1, keepdims=True)
        pos_ref[rows, :] = (jnp.where(lane_t == 0.0, p1, 0.0) + jnp.where(lane_t == 1.0, p2, 0.0)).astype(I32)
        return carry

    lax.fori_loop(0, T // tb, pos_body, 0)
    tend = tile0 + ntiles
    lane = lax.broadcasted_iota(I32, (LANES, LANES), 1)
    tile_i = lax.broadcasted_iota(I32, (LANES, LANES), 0).astype(F32)
    is_expert = lane < N_EXPERTS
    te = jnp.sum(jnp.where(is_expert & (jnp.broadcast_to(tend, (LANES, LANES)) <= tile_i), 1.0, 0.0),
                 axis=1, keepdims=True)
    te_ref[...] = jnp.broadcast_to(jnp.minimum(te, N_EXPERTS - 1.0), (LANES, LANES)).astype(I32)
    lane1 = lax.broadcasted_iota(I32, (1, LANES), 1)
    total = jnp.sum(jnp.where(lane1 < N_EXPERTS, ntiles, 0.0), axis=1, keepdims=True)
    tail = jnp.where(ntiles > 0.0, (tend - 1.0) * MOE_TILE, -1.0)
    tail = jnp.where(lane1 < N_EXPERTS, tail, jnp.where(lane1 == N_EXPERTS, total, 0.0))
    tail_ref[...] = jnp.broadcast_to(tail, (SUBLANES, LANES)).astype(I32)


def _plan(route, tb):
    T = route.shape[0]
    return pl.pallas_call(
        functools.partial(_plan_kernel, tb=tb),
        grid=(1,),
        in_specs=[pl.BlockSpec((T, LANES), lambda i: (0, 0))],
        out_specs=[
            pl.BlockSpec((T, LANES), lambda i: (0, 0)),
            pl.BlockSpec((LANES, LANES), lambda i: (0, 0)),
            pl.BlockSpec((SUBLANES, LANES), lambda i: (0, 0)),
        ],
        out_shape=[
            jax.ShapeDtypeStruct((T, LANES), I32),
            jax.ShapeDtypeStruct((LANES, LANES), I32),
            jax.ShapeDtypeStruct((SUBLANES, LANES), I32),
        ],
        scratch_shapes=[pltpu.VMEM((T, LANES), F32)],
        compiler_params=_params(("arbitrary",)),
        name="moe_plan",
    )(route)


def _row_group(ref, idx):
    return ref.at[pl.ds(pl.multiple_of(idx * ROW_GROUP, ROW_GROUP), ROW_GROUP), :]


def _dispatch_kernel(pos_ref, tail_ref, hn_ref, xs_ref, zero_scr, zsem, sem):
    i = pl.program_id(0)
    tm = hn_ref.shape[0] // ROW_GROUP
    tile_rows = MOE_TILE * ROW_GROUP

    @pl.when(i == 0)
    def _():
        zero_scr[...] = jnp.zeros_like(zero_scr)
        n_used = tail_ref[N_EXPERTS]

        def zero_copy(first_slot):
            start = pl.multiple_of(first_slot * ROW_GROUP, tile_rows)
            return pltpu.make_async_copy(zero_scr, xs_ref.at[pl.ds(start, tile_rows), :], zsem)

        def tails(fn):
            def body(e, carry):
                pl.when(tail_ref[e] >= 0)(lambda: fn(zero_copy(jnp.maximum(tail_ref[e], 0))))
                return carry
            lax.fori_loop(0, N_EXPERTS, body, 0)

        def unused(fn):
            def body(t, carry):
                fn(zero_copy(t * MOE_TILE))
                return carry
            lax.fori_loop(n_used, xs_ref.shape[0] // tile_rows, body, 0)

        tails(lambda c: c.start())
        unused(lambda c: c.start())
        tails(lambda c: c.wait())
        unused(lambda c: c.wait())

    def copy(r, k):
        slot = pos_ref[(i * tm + r) * TOP_K + k]
        return pltpu.make_async_copy(_row_group(hn_ref, r), _row_group(xs_ref, slot), sem)

    def start(r, carry):
        for k in range(TOP_K):
            copy(r, k).start(priority=k)
        return carry

    lax.fori_loop(0, tm, start, 0, unroll=8)
    for _ in range(TOP_K):
        pltpu.make_async_copy(hn_ref, xs_ref.at[pl.ds(0, tm * ROW_GROUP), :], sem).wait()


def _dispatch(pos, tails, hn_rows, n_slots, tm):
    T = hn_rows.shape[0] // ROW_GROUP
    return pl.pallas_call(
        _dispatch_kernel,
        grid_spec=pltpu.PrefetchScalarGridSpec(
            num_scalar_prefetch=2,
            grid=(T // tm,),
            in_specs=[pl.BlockSpec((tm * ROW_GROUP, LANES), lambda i, *_: (i, 0))],
            out_specs=pl.BlockSpec(memory_space=pl.ANY),
            scratch_shapes=[pltpu.VMEM((MOE_TILE * ROW_GROUP, LANES), F32),
                            pltpu.SemaphoreType.DMA, pltpu.SemaphoreType.DMA],
        ),
        out_shape=jax.ShapeDtypeStruct((n_slots * ROW_GROUP, LANES), F32),
        compiler_params=_params(("arbitrary",)),
        name="moe_dispatch",
    )(pos, tails, hn_rows)


def _experts_kernel(te_ref, nt_ref, xs_ref, wg_hbm, wu_hbm, wd_hbm, ys_ref,
                    wg_buf, wu_buf, wd_buf, wg_scr, wu_scr, wd_scr, slot_ref, sem):
    i = pl.program_id(0)
    nt = nt_ref[0]
    e = te_ref[i]

    def fetch(expert, slot):
        return [pltpu.make_async_copy(hbm.at[expert], buf.at[slot], sem.at[slot, n])
                for n, (hbm, buf) in enumerate(((wg_hbm, wg_buf), (wu_hbm, wu_buf), (wd_hbm, wd_buf)))]

    @pl.when(i == 0)
    def _():
        slot_ref[0] = 0
        for c in fetch(e, 0):
            c.start(priority=1)

    @pl.when((i < nt) & ((i == 0) | (te_ref[jnp.maximum(i - 1, 0)] != e)))
    def _():
        slot = slot_ref[0]
        for c in fetch(e, slot):
            c.wait()
        wg_scr[...] = wg_buf[slot].astype(BF16)
        wu_scr[...] = wu_buf[slot].astype(BF16)
        wd_scr[...] = wd_buf[slot].astype(BF16)
        nxt = lax.while_loop(lambda j: (j < nt) & (te_ref[jnp.minimum(j, nt - 1)] == e), lambda j: j + 1, i + 1)

        @pl.when(nxt < nt)
        def _():
            for c in fetch(te_ref[jnp.minimum(nxt, nt - 1)], 1 - slot):
                c.start(priority=1)

        slot_ref[0] = 1 - slot

    @pl.when(i < nt)
    def _():
        x = _from_token_tiles(xs_ref[...])[0].astype(BF16)
        hmid = (_silu(_dot(x, wg_scr[...])) * _dot(x, wu_scr[...])).astype(BF16)
        ys_ref[...] = _to_token_tiles(_dot(hmid, wd_scr[...]))


def _experts(te, nt, xs_rows, w_ge, w_ue, w_de, n_tiles):
    M = MOE_TILE
    rows = pl.BlockSpec((M * ROW_GROUP, LANES), lambda i, te_ref, nt_ref: (jnp.minimum(i, nt_ref[0] - 1), 0))
    hbm = pl.BlockSpec(memory_space=pl.ANY)
    up, down = (D_MODEL, EXPERT_FF), (EXPERT_FF, D_MODEL)
    return pl.pallas_call(
        _experts_kernel,
        grid_spec=pltpu.PrefetchScalarGridSpec(
            num_scalar_prefetch=2,
            grid=(n_tiles,),
            in_specs=[rows, hbm, hbm, hbm],
            out_specs=rows,
            scratch_shapes=[pltpu.VMEM((2,) + up, F32), pltpu.VMEM((2,) + up, F32), pltpu.VMEM((2,) + down, F32),
                            pltpu.VMEM(up, BF16), pltpu.VMEM(up, BF16), pltpu.VMEM(down, BF16),
                            pltpu.SMEM((1,), I32), pltpu.SemaphoreType.DMA((2, 3))],
        ),
        out_shape=jax.ShapeDtypeStruct(xs_rows.shape, F32),
        input_output_aliases={2: 0},
        compiler_params=_params(("arbitrary",)),
        name="moe_experts",
    )(te, nt, xs_rows, w_ge, w_ue, w_de)


def _final_kernel(pos_ref, h_ref, route_ref, gain_ref, ys_ref, y_ref, buf, sem, *, tile0):
    i = pl.program_id(0)
    n = pl.num_programs(0)
    tm = h_ref.shape[0]

    def copy(step, r, k, slot_buf):
        slot = pos_ref[((tile0 + step) * tm + r) * TOP_K + k]
        return pltpu.make_async_copy(_row_group(ys_ref, slot), _row_group(buf.at[slot_buf], r * TOP_K + k),
                                     sem.at[slot_buf])

    def start_all(step, slot_buf):
        def body(r, carry):
            for k in range(TOP_K):
                copy(step, r, k, slot_buf).start(priority=k)
            return carry
        lax.fori_loop(0, tm, body, 0, unroll=8)

    cur = i % 2
    pl.when(i == 0)(lambda: start_all(0, 0))
    pl.when(i + 1 < n)(lambda: start_all(i + 1, 1 - cur))
    pltpu.make_async_copy(ys_ref.at[pl.ds(0, buf.shape[1]), :], buf.at[cur], sem.at[cur]).wait()

    rt = route_ref[...]
    w1 = rt[:, LANE_W1:LANE_W1 + 1]
    w2 = rt[:, LANE_W2:LANE_W2 + 1]
    o1, o2 = _from_token_tiles(buf[cur], groups=TOP_K)
    v = h_ref[...] + (w1 * o1 + w2 * o2)
    y_ref[...] = _rms(v) * gain_ref[...]


def _final(pos, h, route, gain, ys_rows, tile0, n_tiles, tm):
    return pl.pallas_call(
        functools.partial(_final_kernel, tile0=tile0),
        grid_spec=pltpu.PrefetchScalarGridSpec(
            num_scalar_prefetch=1,
            grid=(n_tiles,),
            in_specs=[
                pl.BlockSpec((tm, D_MODEL), lambda i, *_: (tile0 + i, 0)),
                pl.BlockSpec((tm, LANES), lambda i, *_: (tile0 + i, 0)),
                pl.BlockSpec((1, D_MODEL), lambda i, *_: (0, 0)),
                pl.BlockSpec(memory_space=pl.ANY),
            ],
            out_specs=pl.BlockSpec((tm, D_MODEL), lambda i, *_: (i, 0)),
            scratch_shapes=[pltpu.VMEM((2, tm * TOP_K * ROW_GROUP, LANES), F32), pltpu.SemaphoreType.DMA((2,))],
        ),
        out_shape=jax.ShapeDtypeStruct((n_tiles * tm, D_MODEL), F32),
        compiler_params=_params(("arbitrary",)),
        name="final",
    )(pos, h, route, gain, ys_rows)


def _rope_tables(pos):
    half = RET_HEAD_QK // 2
    inv = 1.0 / (ROPE_BASE ** jnp.linspace(0.0, 1.0, half, dtype=F32))
    ang = jnp.repeat(pos[:, None] * inv[None, :], 2, axis=-1)
    cos, sin = jnp.cos(ang), jnp.sin(ang)
    even = (jnp.arange(RET_HEAD_QK) % 2 == 0)[None, :]
    sa = jnp.where(even, -sin, 0.0)
    sb = jnp.where(even, 0.0, sin)
    return cos, sa, sb


def _tile_sizes(Tp, Ts, Lp):
    g = math.gcd(Tp, Ts)
    return dict(
        xnorm=math.gcd(g, 512),
        proj=(Tp + Ts) // 4,
        mix=math.gcd(g, 256),
        plan=math.gcd(Tp + Ts, 256),
        dispatch=math.gcd(g, 256),
        final=math.gcd(g, 256),
        scan=math.gcd(Lp, 128),
    )


def kernel(x_prompt, x_sample, state_hgrn, state_ret, norm_mix, w_in, hg_lb_logits, hg_norm,
           w_branch_hg, w_branch_ret, b_gate, w_out, norm_ffn, w_router_group, b_router_group,
           w_router_expert, b_router_expert, w_expert_gate, w_expert_up, w_expert_down, norm_final):
    assert w_in.shape == (1, D_MODEL, IN_WIDTH), "single-layer trunk only"
    Bp, Lp, _ = x_prompt.shape
    Bs, Ls, _ = x_sample.shape
    Tp, Ts = Bp * Lp, Bs * Ls
    T = Tp + Ts
    ts = _tile_sizes(Tp, Ts, Lp)
    xp = x_prompt.reshape(Tp, D_MODEL)
    xs = x_sample.reshape(Ts, D_MODEL)

    lower = jnp.cumsum(jax.nn.softmax(hg_lb_logits.astype(F32), axis=0), axis=0)[0]
    act = _act_coefficients(lower, b_gate[0])
    tab_p = _rope_tables(jnp.arange(Lp, dtype=F32))
    tab_s = _rope_tables(jnp.arange(Ls, dtype=F32) + jnp.float32(PAST_LEN))
    tabs = tuple(jnp.concatenate([jnp.tile(a, (Bp, 1)), jnp.tile(b, (Bs, 1))]) for a, b in zip(tab_p, tab_s))
    xn = _xnorm(xp, xs, norm_mix, ts["xnorm"])
    P = _in_proj(xn, w_in[0], act, tabs, ts["proj"])

    bb = math.gcd(Bs, 8)
    o_hg_p, shp = _hgrn_prompt(P, hg_norm, Bp, Lp, ts["scan"])
    o_hg_s, shs = _hgrn_sample(P, hg_norm, state_hgrn[0].astype(F32), Bs, Ls, bb, Tp)
    o_r_p, srp = _ret_prompt(P, Bp, Lp, ts["scan"])
    o_r_s, srs = _ret_sample(P, state_ret[0].astype(F32), Bs, Ls, bb, Tp)

    wbh = w_branch_hg[0].astype(BF16)
    wbr = w_branch_ret[0].astype(BF16)
    wout = w_out[0].astype(BF16)
    pad = LANES - N_GROUPS - N_EXPERTS
    wr = jnp.concatenate([w_router_group[0], w_router_expert[0], jnp.zeros((D_MODEL, pad), F32)], axis=1)
    wr_hi = wr.astype(BF16)
    wr_hilo = jnp.concatenate([wr_hi, (wr - wr_hi.astype(F32)).astype(BF16)], axis=1)
    br = jnp.concatenate([b_router_group[0], b_router_expert[0], jnp.zeros((pad,), F32)])[None, :]
    h, hn_rows, route = _mix(list(o_hg_p) + [o_hg_s], list(o_r_p) + [o_r_s], P, xp, xs, wbh, wbr, wout,
                             norm_ffn, wr_hilo, br, ts["mix"])

    n_tiles = -(-T * TOP_K // MOE_TILE) + N_EXPERTS
    assert n_tiles <= LANES
    pos_slab, te_slab, tail_slab = _plan(route, ts["plan"])
    pos = pos_slab[:, :TOP_K].reshape(-1)
    te = te_slab[:n_tiles, 0]
    tails = tail_slab[0, :N_EXPERTS + 1]
    nt = tail_slab[0, N_EXPERTS:N_EXPERTS + 1]
    xs_rows = _dispatch(pos, tails, hn_rows, n_tiles * MOE_TILE, ts["dispatch"])
    ys_rows = _experts(te, nt, xs_rows, w_expert_gate[0], w_expert_up[0], w_expert_down[0], n_tiles)

    tf = ts["final"]
    gain = norm_final[None, :]
    y_p = _final(pos, h, route, gain, ys_rows, 0, Tp // tf, tf)
    y_s = _final(pos, h, route, gain, ys_rows, Tp // tf, Ts // tf, tf)
    return (y_p.reshape(Bp, Lp, D_MODEL), y_s.reshape(Bs, Ls, D_MODEL),
            shp[None], srp[None], shs[None], srs[None])
```

```python
import functools
import math

import jax
import jax.numpy as jnp
from jax import lax
from jax.experimental import pallas as pl
from jax.experimental.pallas import tpu as pltpu

F32 = jnp.float32
BF16 = jnp.bfloat16
I32 = jnp.int32

D_MODEL = 2048
PAST_LEN = 16384
HG_HEAD_V = 128
HG_EXPAND = 128
HG_VAL_DIM = D_MODEL // 2
HG_HEADS = HG_VAL_DIM // HG_HEAD_V
HG_KEY_DIM = HG_HEADS * HG_EXPAND
RET_HEAD_V = 256
RET_HEAD_QK = 128
RET_V_DIM = D_MODEL // 2
RET_HEADS = RET_V_DIM // RET_HEAD_V
RET_QK_DIM = RET_HEADS * RET_HEAD_QK
IN_WIDTH = 2 * HG_KEY_DIM + 2 * HG_VAL_DIM + 2 * RET_QK_DIM + 2 * RET_V_DIM + 2 * D_MODEL
CHUNK = 64
SCAN_CHUNK = 128
N_GROUPS = 4
EXPERTS_PER_GROUP = 8
N_EXPERTS = N_GROUPS * EXPERTS_PER_GROUP
TOP_K = 2
EXPERT_FF = D_MODEL // 4
ROPE_BASE = 10000.0
EPS = 1e-6

LANES = 128
SUBLANES = 8
VMEM_LIMIT = 56 * 1024 * 1024
PROJ_MC = 256
PROJ_TN = 512
SEG_HQ, SEG_HF, SEG_HI, SEG_HG, SEG_RQ, SEG_RK, SEG_RV, SEG_RG, SEG_GA = 0, 2, 4, 6, 8, 9, 10, 12, 14
SAFE_EXP_SPAN = 80.0
ROW_GROUP = D_MODEL // LANES
MOE_TILE = 256
LANE_E1, LANE_E2, LANE_W1, LANE_W2 = 0, 1, 2, 3
ROUTE_EXPERT_LANE = N_GROUPS


def _params(sem):
    return pltpu.CompilerParams(dimension_semantics=sem, vmem_limit_bytes=VMEM_LIMIT)


def _silu(z):
    return z * jax.nn.sigmoid(z)


def _dot(a, b):
    return jnp.dot(a, b, preferred_element_type=F32)


def _dot_nt(a, b):
    return lax.dot_general(a, b, (((1,), (1,)), ((), ())), preferred_element_type=F32)


def _dot_tn(a, b):
    return lax.dot_general(a, b, (((0,), (0,)), ((), ())), preferred_element_type=F32)


def _rms(x):
    return x * lax.rsqrt(jnp.mean(x * x, axis=-1, keepdims=True) + EPS)


def _to_token_tiles(x):
    M = x.shape[0]
    pieces = jnp.stack([x[:, s * LANES:(s + 1) * LANES] for s in range(ROW_GROUP)], axis=0)
    return jnp.swapaxes(pieces, 0, 1).reshape(M * ROW_GROUP, LANES)


def _from_token_tiles(rows, groups=1):
    M = rows.shape[0] // (groups * ROW_GROUP)
    t = jnp.swapaxes(rows.reshape(M, groups * ROW_GROUP, LANES), 0, 1)
    return [jnp.concatenate([t[g * ROW_GROUP + s] for s in range(ROW_GROUP)], axis=1) for g in range(groups)]


def _xnorm_kernel(xp_ref, xs_ref, gain_ref, o_ref, *, n_prompt_tiles):
    i = pl.program_id(0)

    def norm(x_ref):
        o_ref[...] = (_rms(x_ref[...]) * gain_ref[...]).astype(BF16)

    pl.when(i < n_prompt_tiles)(lambda: norm(xp_ref))
    pl.when(i >= n_prompt_tiles)(lambda: norm(xs_ref))


def _xnorm(xp, xs, gain, tm):
    npt, nst = xp.shape[0] // tm, xs.shape[0] // tm
    return pl.pallas_call(
        functools.partial(_xnorm_kernel, n_prompt_tiles=npt),
        grid=(npt + nst,),
        in_specs=[
            pl.BlockSpec((tm, D_MODEL), lambda i: (jnp.minimum(i, npt - 1), 0)),
            pl.BlockSpec((tm, D_MODEL), lambda i: (jnp.maximum(i - npt, 0), 0)),
            pl.BlockSpec((1, D_MODEL), lambda i: (0, 0)),
        ],
        out_specs=pl.BlockSpec((tm, D_MODEL), lambda i: (i, 0)),
        out_shape=jax.ShapeDtypeStruct(((npt + nst) * tm, D_MODEL), BF16),
        compiler_params=_params(("parallel",)),
        name="xnorm",
    )(xp, xs, gain)


ACT_ALPHA, ACT_BETA, ACT_GAMMA, ACT_DELTA, ACT_EPS = 0, 1, 2, 3, 4


def _act_coefficients(lower, b_gate):
    z = jnp.zeros((IN_WIDTH,), F32)
    seg = lambda a, b: slice(a * PROJ_TN, b * PROJ_TN)
    alpha = z.at[seg(SEG_HI, SEG_HG)].set(1.0).at[seg(SEG_RQ, SEG_RG)].set(1.0)
    beta = z.at[seg(SEG_GA, IN_WIDTH // PROJ_TN)].set(b_gate)
    gamma = (z.at[seg(SEG_HQ, SEG_HF)].set(HG_EXPAND ** -0.5)
             .at[seg(SEG_HG, SEG_RQ)].set(1.0).at[seg(SEG_RG, SEG_GA)].set(1.0))
    delta = z.at[seg(SEG_HF, SEG_HI)].set(1.0 - lower).at[seg(SEG_GA, IN_WIDTH // PROJ_TN)].set(1.0)
    eps = z.at[seg(SEG_HF, SEG_HI)].set(lower)
    return jnp.stack([alpha, beta, gamma, delta, eps, z, z, z])


def _in_proj_kernel(x_ref, w_ref, act_ref, cos_ref, sa_ref, sb_ref, o_ref):
    j = pl.program_id(1)
    w = w_ref[...].astype(BF16)
    row = lambda r: act_ref[r:r + 1, :]
    tm = x_ref.shape[0]
    mc = math.gcd(tm, PROJ_MC)
    for m in range(0, tm, mc):
        p = _dot(x_ref[m:m + mc, :], w)
        o_ref[m:m + mc, :] = (row(ACT_ALPHA) * p + row(ACT_EPS)
                              + jax.nn.sigmoid(p + row(ACT_BETA)) * (row(ACT_GAMMA) * p + row(ACT_DELTA)))

    @pl.when((j >= SEG_RQ) & (j < SEG_RV))
    def _():
        scale = jnp.where(j == SEG_RK, RET_HEAD_QK ** -0.5, 1.0).astype(F32)
        cos, sa, sb = cos_ref[...], sa_ref[...], sb_ref[...]
        for hh in range(PROJ_TN // LANES):
            cols = slice(hh * LANES, (hh + 1) * LANES)
            xs = o_ref[:, cols]
            r = xs * cos + pltpu.roll(xs, LANES - 1, 1) * sa + pltpu.roll(xs, 1, 1) * sb
            o_ref[:, cols] = r * scale


def _in_proj(xn, w_in, act, tabs, tm):
    T = xn.shape[0]
    once = dict(pipeline_mode=pl.Buffered(1))
    tab = pl.BlockSpec((tm, LANES), lambda i, j: (i, 0), **once)
    return pl.pallas_call(
        _in_proj_kernel,
        grid=(T // tm, IN_WIDTH // PROJ_TN),
        in_specs=[
            pl.BlockSpec((tm, D_MODEL), lambda i, j: (i, 0), **once),
            pl.BlockSpec((D_MODEL, PROJ_TN), lambda i, j: (0, j)),
            pl.BlockSpec((SUBLANES, PROJ_TN), lambda i, j: (0, j)),
            tab, tab, tab,
        ],
        out_specs=pl.BlockSpec((tm, PROJ_TN), lambda i, j: (i, j)),
        out_shape=jax.ShapeDtypeStruct((T, IN_WIDTH), F32),
        compiler_params=_params(("parallel", "arbitrary")),
        name="in_proj",
    )(xn, w_in, act, *tabs)


def _prefix_matrix(C, G):
    r = lax.broadcasted_iota(I32, (C, 3 * C), 0)
    c = lax.broadcasted_iota(I32, (C, 3 * C), 1) % C
    return jnp.where((c <= r) & (c // G == r // G), 1.0, 0.0).astype(BF16)


def _prefix_sum(g, pm):
    g1 = g.astype(BF16)
    r1 = g - g1.astype(F32)
    g2 = r1.astype(BF16)
    g3 = (r1 - g2.astype(F32)).astype(BF16)
    return _dot(pm, jnp.concatenate([g1, g2, g3], axis=0))


def _state_decay_column(d_row):
    r = lax.broadcasted_iota(I32, (LANES, LANES), 0)
    c = lax.broadcasted_iota(I32, (LANES, LANES), 1)
    return jnp.sum(jnp.where(r == c, jnp.broadcast_to(d_row, (LANES, LANES)), 0.0), axis=1, keepdims=True)


def _hgrn_chunk(q, f, g, v, states, pm, tmp_ref, NG, factorised):
    C, W = q.shape
    G, H = C // NG, W // LANES
    hs = [slice(h * LANES, (h + 1) * LANES) for h in range(H)]
    gs = [slice(n * G, (n + 1) * G) for n in range(NG)]
    k = 1.0 - f
    b = _prefix_sum(g, pm)
    b3 = b.reshape(NG, G, W)
    b_end = b3[:, G - 1:G, :]
    q3, k3 = q.reshape(NG, G, W), k.reshape(NG, G, W)
    vb = v.astype(BF16)
    if factorised:
        b_mid = b3[:, G // 2 - 1:G // 2, :]
        qm3 = q3 * jnp.exp(b3 - b_mid)
        km3 = k3 * jnp.exp(b_mid - b3)
        qb = (qm3 * jnp.exp(b_mid)).reshape(C, W)
        kd = (km3 * jnp.exp(b_end - b_mid)).reshape(C, W)
        qm = qm3.reshape(C, W).astype(BF16)
        km = km3.reshape(C, W).astype(BF16)
        row = lax.broadcasted_iota(I32, (C, C), 0)
        col = lax.broadcasted_iota(I32, (C, C), 1)
        amask = (row >= col) & (row // G == col // G)
        o_intra = []
        for h in range(H):
            att = jnp.where(amask, _dot_nt(qm[:, hs[h]], km[:, hs[h]]), 0.0)
            o_intra.append(_dot(att.astype(BF16), vb[:, hs[h]]))
    else:
        qb = (q3 * jnp.exp(b3)).reshape(C, W)
        kd = (k3 * jnp.exp(b_end - b3)).reshape(C, W)
        trow = lax.broadcasted_iota(I32, (G, LANES), 0)
        o_intra = []
        for h in range(H):
            parts = []
            for n in range(NG):
                bh, qh = b[gs[n], hs[h]], q[gs[n], hs[h]]
                tmp_ref[0] = bh
                tmp_ref[1] = k[gs[n], hs[h]]
                tmp_ref[2] = v[gs[n], hs[h]]

                def body(s, acc, bh=bh, qh=qh):
                    bs = tmp_ref[0, pl.ds(s, 1), :]
                    ks = tmp_ref[1, pl.ds(s, 1), :]
                    vs = tmp_ref[2, pl.ds(s, 1), :]
                    w = jnp.where(trow >= s, jnp.exp(jnp.minimum(bh - bs, 0.0)), 0.0)
                    return acc + jnp.sum(qh * ks * w, axis=1, keepdims=True) * vs

                parts.append(lax.fori_loop(0, G, body, jnp.zeros((G, LANES), F32)))
            o_intra.append(parts[0] if NG == 1 else jnp.concatenate(parts, axis=0))
    d = jnp.exp(b_end)
    outs, new_states = [], [[None] * H for _ in range(NG)]
    for h in range(H):
        inter = []
        for n in range(NG):
            S = states[n][h]
            inter.append(_dot(qb[gs[n], hs[h]].astype(BF16), S.astype(BF16)))
            upd = _dot_tn(kd[gs[n], hs[h]].astype(BF16), v[gs[n], hs[h]].astype(BF16))
            new_states[n][h] = _state_decay_column(d[n, :, hs[h]]) * S + upd
        inter = inter[0] if NG == 1 else jnp.concatenate(inter, axis=0)
        outs.append(inter + o_intra[h])
    return jnp.concatenate(outs, axis=1), new_states


def _decay_is_safe(g, G):
    R, W = g.shape
    return jnp.min(jnp.sum(g.reshape(R // G, G, W), axis=1)) >= -SAFE_EXP_SPAN


def _hgrn_epilogue(o_scr, gain_ref, gate_ref, o_ref):
    o_ref[...] = (_rms(o_scr[...]) * gain_ref[...] * gate_ref[...]).astype(o_ref.dtype)


def _hgrn_prompt_kernel(*refs, C, NB):
    ins, (gain_ref,), rest = refs[:4 * NB], refs[4 * NB:4 * NB + 1], refs[4 * NB + 1:]
    o_refs, (st_ref, s_scr, g_scr, o_scr, tmp_ref) = rest[:NB], rest[NB:]
    seq = [ins[4 * n:4 * n + 4] for n in range(NB)]
    l = pl.program_id(0)
    Lb = o_refs[0].shape[0]

    @pl.when(l == 0)
    def _():
        s_scr[...] = jnp.zeros_like(s_scr)

    for n in range(NB):
        g_scr[n] = jnp.log(seq[n][1][...])
    pm = _prefix_matrix(C, C)

    def run(factorised):
        def body(ci, carry):
            rows = pl.ds(pl.multiple_of(ci * C, C), C)
            args = [(seq[n][0][rows, :], seq[n][1][rows, :], g_scr[n, rows, :], seq[n][2][rows, :],
                     [[s_scr[n, h] for h in range(HG_HEADS)]]) for n in range(NB)]
            outs = [_hgrn_chunk(*a, pm, tmp_ref, 1, factorised) for a in args]
            for n, (o, new) in enumerate(outs):
                o_scr[n, rows, :] = o
                for h in range(HG_HEADS):
                    s_scr[n, h] = new[0][h]
            return carry
        lax.fori_loop(0, Lb // C, body, 0)

    safe = _decay_is_safe(g_scr[...].reshape(NB * Lb, -1), C // 2)
    pl.when(safe)(lambda: run(True))
    pl.when(jnp.logical_not(safe))(lambda: run(False))
    for n in range(NB):
        o_refs[n][...] = (_rms(o_scr[n]) * gain_ref[...] * seq[n][3][...]).astype(o_refs[n].dtype)

    @pl.when(l == pl.num_programs(0) - 1)
    def _():
        st_ref[...] = s_scr[...]


def _hgrn_prompt(P, gain, B, L, Lb):
    C = math.gcd(L, SCAN_CHUNK)
    nl = L // Lb
    W = HG_KEY_DIM
    blk = lambda b, seg: pl.BlockSpec((Lb, W), lambda l, b=b, seg=seg: (b * nl + l, seg))
    outs = pl.pallas_call(
        functools.partial(_hgrn_prompt_kernel, C=C, NB=B),
        grid=(nl,),
        in_specs=[blk(b, seg) for b in range(B) for seg in range(4)] + [pl.BlockSpec((1, W), lambda l: (0, 0))],
        out_specs=[pl.BlockSpec((Lb, W), lambda l: (l, 0)) for _ in range(B)]
        + [pl.BlockSpec((B, HG_HEADS, HG_EXPAND, HG_HEAD_V), lambda l: (0, 0, 0, 0))],
        out_shape=[jax.ShapeDtypeStruct((L, HG_VAL_DIM), BF16) for _ in range(B)]
        + [jax.ShapeDtypeStruct((B, HG_HEADS, HG_EXPAND, HG_HEAD_V), F32)],
        scratch_shapes=[pltpu.VMEM((B, HG_HEADS, HG_EXPAND, HG_HEAD_V), F32), pltpu.VMEM((B, Lb, W), F32),
                        pltpu.VMEM((B, Lb, W), F32), pltpu.VMEM((3, C, LANES), F32)],
        compiler_params=_params(("arbitrary",)),
        name="hgrn_prompt",
    )(*([P] * (4 * B)), gain)
    return outs[:B], outs[B]


def _hgrn_sample_kernel(q_ref, f_ref, v_ref, gate_ref, gain_ref, s_ref, o_ref, so_ref, o_scr, tmp_ref, *, G):
    NG = s_ref.shape[0]
    g = jnp.log(f_ref[...])
    pm = _prefix_matrix(NG * G, G)

    def run(factorised):
        states = [[s_ref[n, h] for h in range(HG_HEADS)] for n in range(NG)]
        o, new = _hgrn_chunk(q_ref[...], f_ref[...], g, v_ref[...], states, pm, tmp_ref, NG, factorised)
        o_scr[...] = o
        for n in range(NG):
            for h in range(HG_HEADS):
                so_ref[n, h] = new[n][h]

    safe = _decay_is_safe(g, G // 2)
    pl.when(safe)(lambda: run(True))
    pl.when(jnp.logical_not(safe))(lambda: run(False))
    _hgrn_epilogue(o_scr, gain_ref, gate_ref, o_ref)


def _hgrn_sample(P, gain, state, Bs, Ls, Bb, row0):
    W = HG_KEY_DIM
    rows = Bb * Ls
    blk0 = row0 // rows

    def col(seg):
        return pl.BlockSpec((rows, W), lambda i, seg=seg: (blk0 + i, seg))

    st = pl.BlockSpec((Bb, HG_HEADS, HG_EXPAND, HG_HEAD_V), lambda i: (i, 0, 0, 0))
    return pl.pallas_call(
        functools.partial(_hgrn_sample_kernel, G=Ls),
        grid=(Bs // Bb,),
        in_specs=[col(0), col(1), col(2), col(3), pl.BlockSpec((1, W), lambda i: (0, 0)), st],
        out_specs=[pl.BlockSpec((rows, W), lambda i: (i, 0)), st],
        out_shape=[
            jax.ShapeDtypeStruct((Bs * Ls, HG_VAL_DIM), BF16),
            jax.ShapeDtypeStruct((Bs, HG_HEADS, HG_EXPAND, HG_HEAD_V), F32),
        ],
        scratch_shapes=[pltpu.VMEM((rows, W), F32), pltpu.VMEM((3, Ls, LANES), F32)],
        compiler_params=_params(("parallel",)),
        name="hgrn_sample",
    )(P, P, P, P, gain, state)


def _ret_chunk(q, k, v, states, dm_ref, qw_ref, kw_ref, sd_ref):
    outs, new_states = [], []
    C = q.shape[0]
    qb, kb, vb = q.astype(BF16), k.astype(BF16), v.astype(BF16)
    for h in range(RET_HEADS):
        qs = slice(h * RET_HEAD_QK, (h + 1) * RET_HEAD_QK)
        vs = slice(h * RET_HEAD_V, (h + 1) * RET_HEAD_V)
        att = (_dot_nt(qb[:, qs], kb[:, qs]) * dm_ref[h]).astype(BF16)
        kw = (k[:, qs] * kw_ref[h]).astype(BF16)
        if C <= 2 * SUBLANES:
            both = _dot(jnp.concatenate([att, kw.T], axis=0), vb[:, vs])
            intra, upd = both[:C], both[C:]
        else:
            intra, upd = _dot(att, vb[:, vs]), _dot_tn(kw, vb[:, vs])
        o = intra + _dot((q[:, qs] * qw_ref[h]).astype(BF16), states[h].astype(BF16))
        new_states.append(sd_ref[h, 0:1, 0:1] * states[h] + upd)
        outs.append(o)
    return jnp.concatenate(outs, axis=1), new_states


def _ret_epilogue(o_scr, gate_ref, o_ref):
    for h in range(RET_HEADS):
        vs = slice(h * RET_HEAD_V, (h + 1) * RET_HEAD_V)
        o_ref[:, vs] = (_rms(o_scr[:, vs]) * gate_ref[:, vs]).astype(o_ref.dtype)


def _ret_prompt_kernel(*refs, NB):
    ins, (dm_ref, qw_ref, kw_ref, sd_ref), rest = refs[:4 * NB], refs[4 * NB:4 * NB + 4], refs[4 * NB + 4:]
    o_refs, (st_ref, s_scr, o_scr) = rest[:NB], rest[NB:]
    seq = [ins[4 * n:4 * n + 4] for n in range(NB)]
    l = pl.program_id(0)
    Lb = o_refs[0].shape[0]
    C = dm_ref.shape[1]

    @pl.when(l == 0)
    def _():
        s_scr[...] = jnp.zeros_like(s_scr)

    def body(ci, carry):
        rows = pl.ds(pl.multiple_of(ci * C, C), C)
        args = [(seq[n][0][rows, :], seq[n][1][rows, :], seq[n][2][rows, :],
                 [s_scr[n, h] for h in range(RET_HEADS)]) for n in range(NB)]
        outs = [_ret_chunk(*a, dm_ref, qw_ref, kw_ref, sd_ref) for a in args]
        for n, (o, new) in enumerate(outs):
            o_scr[n, rows, :] = o
            for h in range(RET_HEADS):
                s_scr[n, h] = new[h]
        return carry

    lax.fori_loop(0, Lb // C, body, 0)
    for n in range(NB):
        _ret_epilogue(o_scr.at[n], seq[n][3], o_refs[n])

    @pl.when(l == pl.num_programs(0) - 1)
    def _():
        st_ref[...] = s_scr[...]


def _ret_tables(C):
    log_gamma = jnp.log(1.0 - jnp.exp2(-5.0 - jnp.arange(RET_HEADS, dtype=F32)))
    idx = jnp.arange(C, dtype=F32)
    rel = idx[:, None] - idx[None, :]
    tri = jnp.tril(jnp.ones((C, C), dtype=bool))
    dmat = jnp.exp(jnp.where(tri[None], log_gamma[:, None, None] * rel[None], -jnp.inf))
    qw = jnp.exp(log_gamma[:, None] * (idx[None, :] + 1.0))[..., None]
    kw = jnp.exp(log_gamma[:, None] * (C - 1.0 - idx[None, :]))[..., None]
    sdec = jnp.exp(log_gamma * C)[:, None, None]
    bc = lambda a: jnp.broadcast_to(a, (RET_HEADS, a.shape[1], LANES))
    return dmat, bc(qw), bc(kw), jnp.broadcast_to(sdec, (RET_HEADS, SUBLANES, LANES))


_QK0 = (2 * HG_KEY_DIM + 2 * HG_VAL_DIM) // RET_QK_DIM
_V0 = (2 * HG_KEY_DIM + 2 * HG_VAL_DIM + 2 * RET_QK_DIM) // RET_V_DIM


def _full3(a):
    return pl.BlockSpec(a.shape, lambda *_: (0, 0, 0))


def _ret_prompt(P, B, L, Lb):
    C = math.gcd(L, SCAN_CHUNK)
    nl = L // Lb
    tabs = _ret_tables(C)
    st_shape = (B, RET_HEADS, RET_HEAD_QK, RET_HEAD_V)
    blk = lambda b, w, c: pl.BlockSpec((Lb, w), lambda l, b=b, c=c: (b * nl + l, c))
    per_seq = lambda b: [blk(b, RET_QK_DIM, _QK0), blk(b, RET_QK_DIM, _QK0 + 1),
                         blk(b, RET_V_DIM, _V0), blk(b, RET_V_DIM, _V0 + 1)]
    outs = pl.pallas_call(
        functools.partial(_ret_prompt_kernel, NB=B),
        grid=(nl,),
        in_specs=[spec for b in range(B) for spec in per_seq(b)] + [_full3(t) for t in tabs],
        out_specs=[pl.BlockSpec((Lb, RET_V_DIM), lambda l: (l, 0)) for _ in range(B)]
        + [pl.BlockSpec(st_shape, lambda l: (0, 0, 0, 0))],
        out_shape=[jax.ShapeDtypeStruct((L, RET_V_DIM), BF16) for _ in range(B)]
        + [jax.ShapeDtypeStruct(st_shape, F32)],
        scratch_shapes=[pltpu.VMEM(st_shape, F32), pltpu.VMEM((B, Lb, RET_V_DIM), F32)],
        compiler_params=_params(("arbitrary",)),
        name="ret_prompt",
    )(*([P] * (4 * B)), *tabs)
    return outs[:B], outs[B]


def _ret_sample_kernel(q_ref, k_ref, v_ref, gate_ref, s_ref, dm_ref, qw_ref, kw_ref, sd_ref,
                       o_ref, so_ref, o_scr, *, C):
    Bb = s_ref.shape[0]
    args = [(q_ref[bb * C:(bb + 1) * C, :], k_ref[bb * C:(bb + 1) * C, :], v_ref[bb * C:(bb + 1) * C, :],
             [s_ref[bb, h] for h in range(RET_HEADS)]) for bb in range(Bb)]
    outs = [_ret_chunk(*a, dm_ref, qw_ref, kw_ref, sd_ref) for a in args]
    for bb, (o, new) in enumerate(outs):
        o_scr[bb * C:(bb + 1) * C, :] = o
        for h in range(RET_HEADS):
            so_ref[bb, h] = new[h]
    _ret_epilogue(o_scr, gate_ref, o_ref)


def _ret_sample(P, state, Bs, Ls, Bb, row0):
    tabs = _ret_tables(Ls)
    rows = Bb * Ls
    blk0 = row0 // rows
    st = pl.BlockSpec((Bb, RET_HEADS, RET_HEAD_QK, RET_HEAD_V), lambda i: (i, 0, 0, 0))
    return pl.pallas_call(
        functools.partial(_ret_sample_kernel, C=Ls),
        grid=(Bs // Bb,),
        in_specs=[
            pl.BlockSpec((rows, RET_QK_DIM), lambda i: (blk0 + i, _QK0)),
            pl.BlockSpec((rows, RET_QK_DIM), lambda i: (blk0 + i, _QK0 + 1)),
            pl.BlockSpec((rows, RET_V_DIM), lambda i: (blk0 + i, _V0)),
            pl.BlockSpec((rows, RET_V_DIM), lambda i: (blk0 + i, _V0 + 1)),
            st,
        ] + [_full3(t) for t in tabs],
        out_specs=[pl.BlockSpec((rows, RET_V_DIM), lambda i: (i, 0)), st],
        out_shape=[
            jax.ShapeDtypeStruct((Bs * Ls, RET_V_DIM), BF16),
            jax.ShapeDtypeStruct((Bs, RET_HEADS, RET_HEAD_QK, RET_HEAD_V), F32),
        ],
        scratch_shapes=[pltpu.VMEM((rows, RET_V_DIM), F32)],
        compiler_params=_params(("parallel",)),
        name="ret_sample",
    )(P, P, P, P, state, *tabs)


def _split_dot(a, w_hilo):
    a_hi = a.astype(BF16)
    a_lo = (a - a_hi.astype(F32)).astype(BF16)
    hi = _dot(a_hi, w_hilo)
    return hi[:, :LANES] + (hi[:, LANES:] + _dot(a_lo, w_hilo[:, :LANES]))


def _route(logits):
    lane = lax.broadcasted_iota(I32, logits.shape, 1)
    neg = jnp.float32(-jnp.inf)
    big = jnp.int32(LANES)
    gmask = lane < N_GROUPS
    gl = jnp.where(gmask, logits, neg)
    gmax = jnp.max(gl, axis=1, keepdims=True)
    gu = jnp.where(gmask, jnp.exp(gl - gmax), 0.0)
    gp = gu / jnp.sum(gu, axis=1, keepdims=True)
    g_w = jnp.max(gp, axis=1, keepdims=True)
    g_i = jnp.min(jnp.where(gmask & (gp == g_w), lane, big), axis=1, keepdims=True)
    lo = ROUTE_EXPERT_LANE + g_i * EXPERTS_PER_GROUP
    emask = (lane >= lo) & (lane < lo + EXPERTS_PER_GROUP)
    el = jnp.where(emask, logits, neg)
    e1 = jnp.max(el, axis=1, keepdims=True)
    i1 = jnp.min(jnp.where(emask & (el == e1), lane, big), axis=1, keepdims=True)
    el2 = jnp.where(lane == i1, neg, el)
    e2 = jnp.max(el2, axis=1, keepdims=True)
    i2 = jnp.min(jnp.where(emask & (lane != i1) & (el2 == e2), lane, big), axis=1, keepdims=True)
    u2 = jnp.exp(e2 - e1)
    den = 1.0 + u2
    w1 = (1.0 / den) * g_w
    w2 = (u2 / den) * g_w
    id1 = (i1 - ROUTE_EXPERT_LANE).astype(F32)
    id2 = (i2 - ROUTE_EXPERT_LANE).astype(F32)
    return (jnp.where(lane == LANE_E1, id1, 0.0) + jnp.where(lane == LANE_E2, id2, 0.0)
            + jnp.where(lane == LANE_W1, w1, 0.0) + jnp.where(lane == LANE_W2, w2, 0.0))


def _pick(i, bounds, refs):
    val = refs[-1][...]
    for n in range(len(refs) - 2, -1, -1):
        val = jnp.where(i < bounds[n + 1], refs[n][...], val)
    return val


def _mix_kernel(*refs, bounds):
    ns = len(bounds) - 1
    ohg_refs, or_refs, refs = refs[:ns], refs[ns:2 * ns], refs[2 * ns:]
    (ga0_ref, ga1_ref, gb0_ref, gb1_ref, xp_ref, xs_ref, wbh_ref, wbr_ref, wout_ref, nffn_ref,
     wr_ref, br_ref, h_ref, hn_ref, route_ref) = refs
    i = pl.program_id(0)
    tm = h_ref.shape[0]
    is_prompt = i < bounds[-2]
    p_hg = _dot(_pick(i, bounds, ohg_refs), wbh_ref[...])
    p_r = _dot(_pick(i, bounds, or_refs), wbr_ref[...])
    half = D_MODEL // 2
    m0 = ga0_ref[...] * p_hg[:, :half] + gb0_ref[...] * p_r[:, :half]
    m1 = ga1_ref[...] * p_hg[:, half:] + gb1_ref[...] * p_r[:, half:]
    mixed = jnp.concatenate([m0, m1], axis=1).astype(BF16)
    x = jnp.where(is_prompt, xp_ref[...], xs_ref[...])
    h = x + _dot(mixed, wout_ref[...])
    h_ref[...] = h
    hn = _rms(h) * nffn_ref[...]
    hn_ref[...] = _to_token_tiles(hn.astype(BF16))
    route_ref[...] = _route(_split_dot(hn, wr_ref[...]) + br_ref[...])


def _mix(o_hg, o_r, P, xp, xs, wbh, wbr, wout, norm_ffn, wr, br, tm):
    npt, nst = xp.shape[0] // tm, xs.shape[0] // tm
    bounds = [0]
    for a in o_hg:
        bounds.append(bounds[-1] + a.shape[0] // tm)
    T = (npt + nst) * tm
    half = D_MODEL // 2
    ga = (IN_WIDTH - 2 * D_MODEL) // half

    def pcol(c):
        return pl.BlockSpec((tm, half), lambda i, c=c: (i, c))

    def const(a):
        return pl.BlockSpec(a.shape, lambda i: (0, 0), pipeline_mode=pl.Buffered(1))

    def prompt(w):
        return pl.BlockSpec((tm, w), lambda i: (jnp.minimum(i, npt - 1), 0))

    def sample(w):
        return pl.BlockSpec((tm, w), lambda i: (jnp.maximum(i - npt, 0), 0))

    def source(n):
        lo, hi = bounds[n], bounds[n + 1]
        return pl.BlockSpec((tm, half), lambda i: (jnp.clip(i - lo, 0, hi - lo - 1), 0))

    sources = [source(n) for n in range(len(o_hg))]
    return pl.pallas_call(
        functools.partial(_mix_kernel, bounds=tuple(bounds)),
        grid=(npt + nst,),
        in_specs=sources + sources + [
            pcol(ga), pcol(ga + 1), pcol(ga + 2), pcol(ga + 3),
            prompt(D_MODEL), sample(D_MODEL),
            const(wbh), const(wbr), const(wout), const(norm_ffn), const(wr), const(br),
        ],
        out_specs=[
            pl.BlockSpec((tm, D_MODEL), lambda i: (i, 0)),
            pl.BlockSpec((tm * ROW_GROUP, LANES), lambda i: (i, 0)),
            pl.BlockSpec((tm, LANES), lambda i: (i, 0)),
        ],
        out_shape=[
            jax.ShapeDtypeStruct((T, D_MODEL), F32),
            jax.ShapeDtypeStruct((T * ROW_GROUP, LANES), BF16),
            jax.ShapeDtypeStruct((T, LANES), F32),
        ],
        compiler_params=_params(("parallel",)),
        name="mix",
    )(*o_hg, *o_r, P, P, P, P, xp, xs, wbh, wbr, wout, norm_ffn, wr, br)


def _plan_kernel(route_ref, pos_ref, te_ref, tail_ref, cum_scr, *, tb):
    T = route_ref.shape[0]
    lane_t = lax.broadcasted_iota(I32, (tb, LANES), 1).astype(F32)
    r = lax.broadcasted_iota(I32, (tb, tb), 0)
    c = lax.broadcasted_iota(I32, (tb, tb), 1)
    strict_lower = jnp.where(r > c, 1.0, 0.0).astype(BF16)

    def onehots(rows):
        rt = route_ref[rows, :]
        a1 = jnp.where(lane_t == rt[:, LANE_E1:LANE_E1 + 1], 1.0, 0.0)
        a2 = jnp.where(lane_t == rt[:, LANE_E2:LANE_E2 + 1], 1.0, 0.0)
        return a1, a2

    def rank_body(bi, carry):
        rows = pl.ds(pl.multiple_of(bi * tb, tb), tb)
        a1, a2 = onehots(rows)
        m = a1 + a2
        cum_scr[rows, :] = _dot(strict_lower, m.astype(BF16)) + carry
        return carry + jnp.sum(m, axis=0, keepdims=True)

    counts = lax.fori_loop(0, T // tb, rank_body, jnp.zeros((1, LANES), F32))
    ntiles = jnp.floor((counts + (MOE_TILE - 1)) * (1.0 / MOE_TILE))
    rr = lax.broadcasted_iota(I32, (LANES, LANES), 0)
    cc = lax.broadcasted_iota(I32, (LANES, LANES), 1)
    upper = jnp.where(rr < cc, 1.0, 0.0).astype(BF16)
    tile0 = _dot(jnp.broadcast_to(ntiles, (SUBLANES, LANES)).astype(BF16), upper)[0:1, :]
    slot0 = tile0 * MOE_TILE

    def pos_body(bi, carry):
        rows = pl.ds(pl.multiple_of(bi * tb, tb), tb)
        a1, a2 = onehots(rows)
        base = cum_scr[rows, :] + slot0
        p1 = jnp.sum(a1 * base, axis=1, keepdims=True)
        p2 = jnp.sum(a2 * base, axis=1, keepdims=True)
        pos_ref[rows, :] = (jnp.where(lane_t == 0.0, p1, 0.0) + jnp.where(lane_t == 1.0, p2, 0.0)).astype(I32)
        return carry

    lax.fori_loop(0, T // tb, pos_body, 0)
    tend = tile0 + ntiles
    lane = lax.broadcasted_iota(I32, (LANES, LANES), 1)
    tile_i = lax.broadcasted_iota(I32, (LANES, LANES), 0).astype(F32)
    is_expert = lane < N_EXPERTS
    te = jnp.sum(jnp.where(is_expert & (jnp.broadcast_to(tend, (LANES, LANES)) <= tile_i), 1.0, 0.0),
                 axis=1, keepdims=True)
    te_ref[...] = jnp.broadcast_to(jnp.minimum(te, N_EXPERTS - 1.0), (LANES, LANES)).astype(I32)
    lane1 = lax.broadcasted_iota(I32, (1, LANES), 1)
    total = jnp.sum(jnp.where(lane1 < N_EXPERTS, ntiles, 0.0), axis=1, keepdims=True)
    tail = jnp.where(ntiles > 0.0, (tend - 1.0) * MOE_TILE, -1.0)
    tail = jnp.where(lane1 < N_EXPERTS, tail, jnp.where(lane1 == N_EXPERTS, total, 0.0))
    tail_ref[...] = jnp.broadcast_to(tail, (SUBLANES, LANES)).astype(I32)


def _plan(route, tb):
    T = route.shape[0]
    return pl.pallas_call(
        functools.partial(_plan_kernel, tb=tb),
        grid=(1,),
        in_specs=[pl.BlockSpec((T, LANES), lambda i: (0, 0))],
        out_specs=[
            pl.BlockSpec((T, LANES), lambda i: (0, 0)),
            pl.BlockSpec((LANES, LANES), lambda i: (0, 0)),
            pl.BlockSpec((SUBLANES, LANES), lambda i: (0, 0)),
        ],
        out_shape=[
            jax.ShapeDtypeStruct((T, LANES), I32),
            jax.ShapeDtypeStruct((LANES, LANES), I32),
            jax.ShapeDtypeStruct((SUBLANES, LANES), I32),
        ],
        scratch_shapes=[pltpu.VMEM((T, LANES), F32)],
        compiler_params=_params(("arbitrary",)),
        name="moe_plan",
    )(route)


def _row_group(ref, idx):
    return ref.at[pl.ds(pl.multiple_of(idx * ROW_GROUP, ROW_GROUP), ROW_GROUP), :]


def _dispatch_kernel(pos_ref, tail_ref, hn_ref, xs_ref, zero_scr, zsem, sem):
    i = pl.program_id(0)
    tm = hn_ref.shape[0] // ROW_GROUP
    tile_rows = MOE_TILE * ROW_GROUP

    @pl.when(i == 0)
    def _():
        zero_scr[...] = jnp.zeros_like(zero_scr)
        n_used = tail_ref[N_EXPERTS]

        def zero_copy(first_slot):
            start = pl.multiple_of(first_slot * ROW_GROUP, tile_rows)
            return pltpu.make_async_copy(zero_scr, xs_ref.at[pl.ds(start, tile_rows), :], zsem)

        def tails(fn):
            def body(e, carry):
                pl.when(tail_ref[e] >= 0)(lambda: fn(zero_copy(jnp.maximum(tail_ref[e], 0))))
                return carry
            lax.fori_loop(0, N_EXPERTS, body, 0)

        def unused(fn):
            def body(t, carry):
                fn(zero_copy(t * MOE_TILE))
                return carry
            lax.fori_loop(n_used, xs_ref.shape[0] // tile_rows, body, 0)

        tails(lambda c: c.start())
        unused(lambda c: c.start())
        tails(lambda c: c.wait())
        unused(lambda c: c.wait())

    def copy(r, k):
        slot = pos_ref[(i * tm + r) * TOP_K + k]
        return pltpu.make_async_copy(_row_group(hn_ref, r), _row_group(xs_ref, slot), sem)

    def start(r, carry):
        for k in range(TOP_K):
            copy(r, k).start(priority=k)
        return carry

    lax.fori_loop(0, tm, start, 0, unroll=8)
    for _ in range(TOP_K):
        pltpu.make_async_copy(hn_ref, xs_ref.at[pl.ds(0, tm * ROW_GROUP), :], sem).wait()


def _dispatch(pos, tails, hn_rows, n_slots, tm):
    T = hn_rows.shape[0] // ROW_GROUP
    return pl.pallas_call(
        _dispatch_kernel,
        grid_spec=pltpu.PrefetchScalarGridSpec(
            num_scalar_prefetch=2,
            grid=(T // tm,),
            in_specs=[pl.BlockSpec((tm * ROW_GROUP, LANES), lambda i, *_: (i, 0))],
            out_specs=pl.BlockSpec(memory_space=pl.ANY),
            scratch_shapes=[pltpu.VMEM((MOE_TILE * ROW_GROUP, LANES), BF16),
                            pltpu.SemaphoreType.DMA, pltpu.SemaphoreType.DMA],
        ),
        out_shape=jax.ShapeDtypeStruct((n_slots * ROW_GROUP, LANES), BF16),
        compiler_params=_params(("arbitrary",)),
        name="moe_dispatch",
    )(pos, tails, hn_rows)


def _experts_kernel(te_ref, nt_ref, xs_ref, wg_hbm, wu_hbm, wd_hbm, ys_ref,
                    wg_buf, wu_buf, wd_buf, wg_scr, wu_scr, wd_scr, slot_ref, sem):
    i = pl.program_id(0)
    nt = nt_ref[0]
    e = te_ref[i]

    def fetch(expert, slot):
        return [pltpu.make_async_copy(hbm.at[expert], buf.at[slot], sem.at[slot, n])
                for n, (hbm, buf) in enumerate(((wg_hbm, wg_buf), (wu_hbm, wu_buf), (wd_hbm, wd_buf)))]

    @pl.when(i == 0)
    def _():
        slot_ref[0] = 0
        for c in fetch(e, 0):
            c.start(priority=1)

    @pl.when((i < nt) & ((i == 0) | (te_ref[jnp.maximum(i - 1, 0)] != e)))
    def _():
        slot = slot_ref[0]
        for c in fetch(e, slot):
            c.wait()
        wg_scr[...] = wg_buf[slot].astype(BF16)
        wu_scr[...] = wu_buf[slot].astype(BF16)
        wd_scr[...] = wd_buf[slot].astype(BF16)
        nxt = lax.while_loop(lambda j: (j < nt) & (te_ref[jnp.minimum(j, nt - 1)] == e), lambda j: j + 1, i + 1)

        @pl.when(nxt < nt)
        def _():
            for c in fetch(te_ref[jnp.minimum(nxt, nt - 1)], 1 - slot):
                c.start(priority=1)

        slot_ref[0] = 1 - slot

    @pl.when(i < nt)
    def _():
        x = _from_token_tiles(xs_ref[...])[0]
        hmid = (_silu(_dot(x, wg_scr[...])) * _dot(x, wu_scr[...])).astype(BF16)
        ys_ref[...] = _to_token_tiles(_dot(hmid, wd_scr[...]))

    @pl.when(i >= nt)
    def _():
        ys_ref[...] = jnp.zeros_like(ys_ref)


def _experts(te, nt, xs_rows, w_ge, w_ue, w_de, n_tiles):
    M = MOE_TILE
    rows = pl.BlockSpec((M * ROW_GROUP, LANES), lambda i, te_ref, nt_ref: (jnp.minimum(i, nt_ref[0] - 1), 0))
    hbm = pl.BlockSpec(memory_space=pl.ANY)
    up, down = (D_MODEL, EXPERT_FF), (EXPERT_FF, D_MODEL)
    return pl.pallas_call(
        _experts_kernel,
        grid_spec=pltpu.PrefetchScalarGridSpec(
            num_scalar_prefetch=2,
            grid=(n_tiles,),
            in_specs=[rows, hbm, hbm, hbm],
            out_specs=pl.BlockSpec((M * ROW_GROUP, LANES), lambda i, te_ref, nt_ref: (i, 0)),
            scratch_shapes=[pltpu.VMEM((2,) + up, F32), pltpu.VMEM((2,) + up, F32), pltpu.VMEM((2,) + down, F32),
                            pltpu.VMEM(up, BF16), pltpu.VMEM(up, BF16), pltpu.VMEM(down, BF16),
                            pltpu.SMEM((1,), I32), pltpu.SemaphoreType.DMA((2, 3))],
        ),
        out_shape=jax.ShapeDtypeStruct(xs_rows.shape, F32),
        compiler_params=_params(("arbitrary",)),
        name="moe_experts",
    )(te, nt, xs_rows, w_ge, w_ue, w_de)


def _final_kernel(pos_ref, h_ref, route_ref, gain_ref, ys_ref, y_ref, buf, sem, *, tile0):
    i = pl.program_id(0)
    n = pl.num_programs(0)
    tm = h_ref.shape[0]

    def copy(step, r, k, slot_buf):
        slot = pos_ref[((tile0 + step) * tm + r) * TOP_K + k]
        return pltpu.make_async_copy(_row_group(ys_ref, slot), _row_group(buf.at[slot_buf], r * TOP_K + k),
                                     sem.at[slot_buf])

    def start_all(step, slot_buf):
        def body(r, carry):
            for k in range(TOP_K):
                copy(step, r, k, slot_buf).start(priority=k)
            return carry
        lax.fori_loop(0, tm, body, 0, unroll=8)

    cur = i % 2
    pl.when(i == 0)(lambda: start_all(0, 0))
    pl.when(i + 1 < n)(lambda: start_all(i + 1, 1 - cur))
    pltpu.make_async_copy(ys_ref.at[pl.ds(0, buf.shape[1]), :], buf.at[cur], sem.at[cur]).wait()

    rt = route_ref[...]
    w1 = rt[:, LANE_W1:LANE_W1 + 1]
    w2 = rt[:, LANE_W2:LANE_W2 + 1]
    o1, o2 = _from_token_tiles(buf[cur], groups=TOP_K)
    v = h_ref[...] + (w1 * o1 + w2 * o2)
    y_ref[...] = _rms(v) * gain_ref[...]


def _final(pos, h, route, gain, ys_rows, tile0, n_tiles, tm):
    return pl.pallas_call(
        functools.partial(_final_kernel, tile0=tile0),
        grid_spec=pltpu.PrefetchScalarGridSpec(
            num_scalar_prefetch=1,
            grid=(n_tiles,),
            in_specs=[
                pl.BlockSpec((tm, D_MODEL), lambda i, *_: (tile0 + i, 0)),
                pl.BlockSpec((tm, LANES), lambda i, *_: (tile0 + i, 0)),
                pl.BlockSpec((1, D_MODEL), lambda i, *_: (0, 0)),
                pl.BlockSpec(memory_space=pl.ANY),
            ],
            out_specs=pl.BlockSpec((tm, D_MODEL), lambda i, *_: (i, 0)),
            scratch_shapes=[pltpu.VMEM((2, tm * TOP_K * ROW_GROUP, LANES), F32), pltpu.SemaphoreType.DMA((2,))],
        ),
        out_shape=jax.ShapeDtypeStruct((n_tiles * tm, D_MODEL), F32),
        compiler_params=_params(("arbitrary",)),
        name="final",
    )(pos, h, route, gain, ys_rows)


def _rope_tables(pos):
    half = RET_HEAD_QK // 2
    inv = 1.0 / (ROPE_BASE ** jnp.linspace(0.0, 1.0, half, dtype=F32))
    ang = jnp.repeat(pos[:, None] * inv[None, :], 2, axis=-1)
    cos, sin = jnp.cos(ang), jnp.sin(ang)
    even = (jnp.arange(RET_HEAD_QK) % 2 == 0)[None, :]
    sa = jnp.where(even, -sin, 0.0)
    sb = jnp.where(even, 0.0, sin)
    return cos, sa, sb


def _tile_sizes(Tp, Ts, Lp):
    g = math.gcd(Tp, Ts)
    return dict(
        xnorm=math.gcd(g, 512),
        proj=(Tp + Ts) // 4,
        mix=math.gcd(g, 256),
        plan=math.gcd(Tp + Ts, 256),
        dispatch=math.gcd(g, 256),
        final=math.gcd(g, 256),
        scan=math.gcd(Lp, 128),
    )


def kernel(x_prompt, x_sample, state_hgrn, state_ret, norm_mix, w_in, hg_lb_logits, hg_norm,
           w_branch_hg, w_branch_ret, b_gate, w_out, norm_ffn, w_router_group, b_router_group,
           w_router_expert, b_router_expert, w_expert_gate, w_expert_up, w_expert_down, norm_final):
    assert w_in.shape == (1, D_MODEL, IN_WIDTH), "single-layer trunk only"
    Bp, Lp, _ = x_prompt.shape
    Bs, Ls, _ = x_sample.shape
    Tp, Ts = Bp * Lp, Bs * Ls
    T = Tp + Ts
    ts = _tile_sizes(Tp, Ts, Lp)
    xp = x_prompt.reshape(Tp, D_MODEL)
    xs = x_sample.reshape(Ts, D_MODEL)

    lower = jnp.cumsum(jax.nn.softmax(hg_lb_logits.astype(F32), axis=0), axis=0)[0]
    act = _act_coefficients(lower, b_gate[0])
    tab_p = _rope_tables(jnp.arange(Lp, dtype=F32))
    tab_s = _rope_tables(jnp.arange(Ls, dtype=F32) + jnp.float32(PAST_LEN))
    tabs = tuple(jnp.concatenate([jnp.tile(a, (Bp, 1)), jnp.tile(b, (Bs, 1))]) for a, b in zip(tab_p, tab_s))
    xn = _xnorm(xp, xs, norm_mix, ts["xnorm"])
    P = _in_proj(xn, w_in[0], act, tabs, ts["proj"])

    bb = math.gcd(Bs, 8)
    o_hg_p, shp = _hgrn_prompt(P, hg_norm, Bp, Lp, ts["scan"])
    o_hg_s, shs = _hgrn_sample(P, hg_norm, state_hgrn[0].astype(F32), Bs, Ls, bb, Tp)
    o_r_p, srp = _ret_prompt(P, Bp, Lp, ts["scan"])
    o_r_s, srs = _ret_sample(P, state_ret[0].astype(F32), Bs, Ls, bb, Tp)

    wbh = w_branch_hg[0].astype(BF16)
    wbr = w_branch_ret[0].astype(BF16)
    wout = w_out[0].astype(BF16)
    pad = LANES - N_GROUPS - N_EXPERTS
    wr = jnp.concatenate([w_router_group[0], w_router_expert[0], jnp.zeros((D_MODEL, pad), F32)], axis=1)
    wr_hi = wr.astype(BF16)
    wr_hilo = jnp.concatenate([wr_hi, (wr - wr_hi.astype(F32)).astype(BF16)], axis=1)
    br = jnp.concatenate([b_router_group[0], b_router_expert[0], jnp.zeros((pad,), F32)])[None, :]
    h, hn_rows, route = _mix(list(o_hg_p) + [o_hg_s], list(o_r_p) + [o_r_s], P, xp, xs, wbh, wbr, wout,
                             norm_ffn, wr_hilo, br, ts["mix"])

    n_tiles = -(-T * TOP_K // MOE_TILE) + N_EXPERTS
    assert n_tiles <= LANES
    pos_slab, te_slab, tail_slab = _plan(route, ts["plan"])
    pos = pos_slab[:, :TOP_K].reshape(-1)
    te = te_slab[:n_tiles, 0]
    tails = tail_slab[0, :N_EXPERTS + 1]
    nt = tail_slab[0, N_EXPERTS:N_EXPERTS + 1]
    xs_rows = _dispatch(pos, tails, hn_rows, n_tiles * MOE_TILE, ts["dispatch"])
    ys_rows = _experts(te, nt, xs_rows, w_expert_gate[0], w_expert_up[0], w_expert_down[0], n_tiles)

    tf = ts["final"]
    gain = norm_final[None, :]
    y_p = _final(pos, h, route, gain, ys_rows, 0, Tp // tf, tf)
    y_s = _final(pos, h, route, gain, ys_rows, Tp // tf, Ts // tf, tf)
    return (y_p.reshape(Bp, Lp, D_MODEL), y_s.reshape(Bs, Ls, D_MODEL),
            shp[None], srp[None], shs[None], srs[None])
```

```python
import functools
import math

import jax
import jax.numpy as jnp
from jax import lax
from jax.experimental import pallas as pl
from jax.experimental.pallas import tpu as pltpu

F32 = jnp.float32
BF16 = jnp.bfloat16
I32 = jnp.int32

D_MODEL = 2048
PAST_LEN = 16384
HG_HEAD_V = 128
HG_EXPAND = 128
HG_VAL_DIM = D_MODEL // 2
HG_HEADS = HG_VAL_DIM // HG_HEAD_V
HG_KEY_DIM = HG_HEADS * HG_EXPAND
RET_HEAD_V = 256
RET_HEAD_QK = 128
RET_V_DIM = D_MODEL // 2
RET_HEADS = RET_V_DIM // RET_HEAD_V
RET_QK_DIM = RET_HEADS * RET_HEAD_QK
IN_WIDTH = 2 * HG_KEY_DIM + 2 * HG_VAL_DIM + 2 * RET_QK_DIM + 2 * RET_V_DIM + 2 * D_MODEL
CHUNK = 64
SCAN_CHUNK = 128
N_GROUPS = 4
EXPERTS_PER_GROUP = 8
N_EXPERTS = N_GROUPS * EXPERTS_PER_GROUP
TOP_K = 2
EXPERT_FF = D_MODEL // 4
ROPE_BASE = 10000.0
EPS = 1e-6

LANES = 128
SUBLANES = 8
VMEM_LIMIT = 56 * 1024 * 1024
PROJ_MC = 256
PROJ_TN = 512
SEG_HQ, SEG_HF, SEG_HI, SEG_HG, SEG_RQ, SEG_RK, SEG_RV, SEG_RG, SEG_GA = 0, 2, 4, 6, 8, 9, 10, 12, 14
SAFE_EXP_SPAN = 80.0
ROW_GROUP = D_MODEL // LANES
MOE_TILE = 256
LANE_E1, LANE_E2, LANE_W1, LANE_W2 = 0, 1, 2, 3
ROUTE_EXPERT_LANE = N_GROUPS


def _params(sem):
    return pltpu.CompilerParams(dimension_semantics=sem, vmem_limit_bytes=VMEM_LIMIT)


def _silu(z):
    return z * jax.nn.sigmoid(z)


def _dot(a, b):
    return jnp.dot(a, b, preferred_element_type=F32)


def _dot_nt(a, b):
    return lax.dot_general(a, b, (((1,), (1,)), ((), ())), preferred_element_type=F32)


def _dot_tn(a, b):
    return lax.dot_general(a, b, (((0,), (0,)), ((), ())), preferred_element_type=F32)


def _rms(x):
    return x * lax.rsqrt(jnp.mean(x * x, axis=-1, keepdims=True) + EPS)


def _to_token_tiles(x):
    M = x.shape[0]
    pieces = jnp.stack([x[:, s * LANES:(s + 1) * LANES] for s in range(ROW_GROUP)], axis=0)
    return jnp.swapaxes(pieces, 0, 1).reshape(M * ROW_GROUP, LANES)


def _from_token_tiles(rows, groups=1):
    M = rows.shape[0] // (groups * ROW_GROUP)
    t = jnp.swapaxes(rows.reshape(M, groups * ROW_GROUP, LANES), 0, 1)
    return [jnp.concatenate([t[g * ROW_GROUP + s] for s in range(ROW_GROUP)], axis=1) for g in range(groups)]


def _xnorm_kernel(xp_ref, xs_ref, gain_ref, o_ref, *, n_prompt_tiles):
    i = pl.program_id(0)

    def norm(x_ref):
        o_ref[...] = (_rms(x_ref[...]) * gain_ref[...]).astype(BF16)

    pl.when(i < n_prompt_tiles)(lambda: norm(xp_ref))
    pl.when(i >= n_prompt_tiles)(lambda: norm(xs_ref))


def _xnorm(xp, xs, gain, tm):
    npt, nst = xp.shape[0] // tm, xs.shape[0] // tm
    return pl.pallas_call(
        functools.partial(_xnorm_kernel, n_prompt_tiles=npt),
        grid=(npt + nst,),
        in_specs=[
            pl.BlockSpec((tm, D_MODEL), lambda i: (jnp.minimum(i, npt - 1), 0)),
            pl.BlockSpec((tm, D_MODEL), lambda i: (jnp.maximum(i - npt, 0), 0)),
            pl.BlockSpec((1, D_MODEL), lambda i: (0, 0)),
        ],
        out_specs=pl.BlockSpec((tm, D_MODEL), lambda i: (i, 0)),
        out_shape=jax.ShapeDtypeStruct(((npt + nst) * tm, D_MODEL), BF16),
        compiler_params=_params(("parallel",)),
        name="xnorm",
    )(xp, xs, gain)


ACT_ALPHA, ACT_BETA, ACT_GAMMA, ACT_DELTA, ACT_EPS = 0, 1, 2, 3, 4


def _act_coefficients(lower, b_gate):
    z = jnp.zeros((IN_WIDTH,), F32)
    seg = lambda a, b: slice(a * PROJ_TN, b * PROJ_TN)
    alpha = z.at[seg(SEG_HI, SEG_HG)].set(1.0).at[seg(SEG_RQ, SEG_RG)].set(1.0)
    beta = z.at[seg(SEG_GA, IN_WIDTH // PROJ_TN)].set(b_gate)
    gamma = (z.at[seg(SEG_HQ, SEG_HF)].set(HG_EXPAND ** -0.5)
             .at[seg(SEG_HG, SEG_RQ)].set(1.0).at[seg(SEG_RG, SEG_GA)].set(1.0))
    delta = z.at[seg(SEG_HF, SEG_HI)].set(1.0 - lower).at[seg(SEG_GA, IN_WIDTH // PROJ_TN)].set(1.0)
    eps = z.at[seg(SEG_HF, SEG_HI)].set(lower)
    return jnp.stack([alpha, beta, gamma, delta, eps, z, z, z])


def _in_proj_kernel(x_ref, w_ref, act_ref, cos_ref, sa_ref, sb_ref, o_ref):
    j = pl.program_id(1)
    w = w_ref[...].astype(BF16)
    row = lambda r: act_ref[r:r + 1, :]
    tm = x_ref.shape[0]
    mc = math.gcd(tm, PROJ_MC)
    for m in range(0, tm, mc):
        p = _dot(x_ref[m:m + mc, :], w)
        o_ref[m:m + mc, :] = (row(ACT_ALPHA) * p + row(ACT_EPS)
                              + jax.nn.sigmoid(p + row(ACT_BETA)) * (row(ACT_GAMMA) * p + row(ACT_DELTA)))

    @pl.when((j >= SEG_RQ) & (j < SEG_RV))
    def _():
        scale = jnp.where(j == SEG_RK, RET_HEAD_QK ** -0.5, 1.0).astype(F32)
        cos, sa, sb = cos_ref[...], sa_ref[...], sb_ref[...]
        for hh in range(PROJ_TN // LANES):
            cols = slice(hh * LANES, (hh + 1) * LANES)
            xs = o_ref[:, cols]
            r = xs * cos + pltpu.roll(xs, LANES - 1, 1) * sa + pltpu.roll(xs, 1, 1) * sb
            o_ref[:, cols] = r * scale


def _in_proj(xn, w_in, act, tabs, tm):
    T = xn.shape[0]
    once = dict(pipeline_mode=pl.Buffered(1))
    tab = pl.BlockSpec((tm, LANES), lambda i, j: (i, 0), **once)
    return pl.pallas_call(
        _in_proj_kernel,
        grid=(T // tm, IN_WIDTH // PROJ_TN),
        in_specs=[
            pl.BlockSpec((tm, D_MODEL), lambda i, j: (i, 0), **once),
            pl.BlockSpec((D_MODEL, PROJ_TN), lambda i, j: (0, j)),
            pl.BlockSpec((SUBLANES, PROJ_TN), lambda i, j: (0, j)),
            tab, tab, tab,
        ],
        out_specs=pl.BlockSpec((tm, PROJ_TN), lambda i, j: (i, j)),
        out_shape=jax.ShapeDtypeStruct((T, IN_WIDTH), F32),
        compiler_params=_params(("parallel", "arbitrary")),
        name="in_proj",
    )(xn, w_in, act, *tabs)


def _prefix_matrix(C, G):
    r = lax.broadcasted_iota(I32, (C, 3 * C), 0)
    c = lax.broadcasted_iota(I32, (C, 3 * C), 1) % C
    return jnp.where((c <= r) & (c // G == r // G), 1.0, 0.0).astype(BF16)


def _prefix_sum(g, pm):
    g1 = g.astype(BF16)
    r1 = g - g1.astype(F32)
    g2 = r1.astype(BF16)
    g3 = (r1 - g2.astype(F32)).astype(BF16)
    return _dot(pm, jnp.concatenate([g1, g2, g3], axis=0))


def _state_decay_column(d_row):
    r = lax.broadcasted_iota(I32, (LANES, LANES), 0)
    c = lax.broadcasted_iota(I32, (LANES, LANES), 1)
    return jnp.sum(jnp.where(r == c, jnp.broadcast_to(d_row, (LANES, LANES)), 0.0), axis=1, keepdims=True)


def _hgrn_chunk(q, f, g, v, states, pm, tmp_ref, NG, factorised):
    C, W = q.shape
    G, H = C // NG, W // LANES
    hs = [slice(h * LANES, (h + 1) * LANES) for h in range(H)]
    gs = [slice(n * G, (n + 1) * G) for n in range(NG)]
    k = 1.0 - f
    b = _prefix_sum(g, pm)
    b3 = b.reshape(NG, G, W)
    b_end = b3[:, G - 1:G, :]
    q3, k3 = q.reshape(NG, G, W), k.reshape(NG, G, W)
    vb = v.astype(BF16)
    if factorised:
        b_mid = b3[:, G // 2 - 1:G // 2, :]
        qm3 = q3 * jnp.exp(b3 - b_mid)
        km3 = k3 * jnp.exp(b_mid - b3)
        qb = (qm3 * jnp.exp(b_mid)).reshape(C, W)
        kd = (km3 * jnp.exp(b_end - b_mid)).reshape(C, W)
        qm = qm3.reshape(C, W).astype(BF16)
        km = km3.reshape(C, W).astype(BF16)
        row = lax.broadcasted_iota(I32, (C, C), 0)
        col = lax.broadcasted_iota(I32, (C, C), 1)
        amask = (row >= col) & (row // G == col // G)
        o_intra = []
        for h in range(H):
            att = jnp.where(amask, _dot_nt(qm[:, hs[h]], km[:, hs[h]]), 0.0)
            o_intra.append(_dot(att.astype(BF16), vb[:, hs[h]]))
    else:
        qb = (q3 * jnp.exp(b3)).reshape(C, W)
        kd = (k3 * jnp.exp(b_end - b3)).reshape(C, W)
        trow = lax.broadcasted_iota(I32, (G, LANES), 0)
        o_intra = []
        for h in range(H):
            parts = []
            for n in range(NG):
                bh, qh = b[gs[n], hs[h]], q[gs[n], hs[h]]
                tmp_ref[0] = bh
                tmp_ref[1] = k[gs[n], hs[h]]
                tmp_ref[2] = v[gs[n], hs[h]]

                def body(s, acc, bh=bh, qh=qh):
                    bs = tmp_ref[0, pl.ds(s, 1), :]
                    ks = tmp_ref[1, pl.ds(s, 1), :]
                    vs = tmp_ref[2, pl.ds(s, 1), :]
                    w = jnp.where(trow >= s, jnp.exp(jnp.minimum(bh - bs, 0.0)), 0.0)
                    return acc + jnp.sum(qh * ks * w, axis=1, keepdims=True) * vs

                parts.append(lax.fori_loop(0, G, body, jnp.zeros((G, LANES), F32)))
            o_intra.append(parts[0] if NG == 1 else jnp.concatenate(parts, axis=0))
    d = jnp.exp(b_end)
    outs, new_states = [], [[None] * H for _ in range(NG)]
    for h in range(H):
        inter = []
        for n in range(NG):
            S = states[n][h]
            inter.append(_dot(qb[gs[n], hs[h]].astype(BF16), S.astype(BF16)))
            upd = _dot_tn(kd[gs[n], hs[h]].astype(BF16), v[gs[n], hs[h]].astype(BF16))
            new_states[n][h] = _state_decay_column(d[n, :, hs[h]]) * S + upd
        inter = inter[0] if NG == 1 else jnp.concatenate(inter, axis=0)
        outs.append(inter + o_intra[h])
    return jnp.concatenate(outs, axis=1), new_states


def _decay_is_safe(g, G):
    R, W = g.shape
    return jnp.min(jnp.sum(g.reshape(R // G, G, W), axis=1)) >= -SAFE_EXP_SPAN


def _hgrn_epilogue(o_scr, gain_ref, gate_ref, o_ref):
    o_ref[...] = (_rms(o_scr[...]) * gain_ref[...] * gate_ref[...]).astype(o_ref.dtype)


def _hgrn_prompt_kernel(*refs, C, NB):
    ins, (gain_ref,), rest = refs[:4 * NB], refs[4 * NB:4 * NB + 1], refs[4 * NB + 1:]
    o_refs, (st_ref, s_scr, g_scr, o_scr, tmp_ref) = rest[:NB], rest[NB:]
    seq = [ins[4 * n:4 * n + 4] for n in range(NB)]
    l = pl.program_id(0)
    Lb = o_refs[0].shape[0]

    @pl.when(l == 0)
    def _():
        s_scr[...] = jnp.zeros_like(s_scr)

    for n in range(NB):
        g_scr[n] = jnp.log(seq[n][1][...])
    pm = _prefix_matrix(C, C)

    def run(factorised):
        def body(ci, carry):
            rows = pl.ds(pl.multiple_of(ci * C, C), C)
            args = [(seq[n][0][rows, :], seq[n][1][rows, :], g_scr[n, rows, :], seq[n][2][rows, :],
                     [[s_scr[n, h] for h in range(HG_HEADS)]]) for n in range(NB)]
            outs = [_hgrn_chunk(*a, pm, tmp_ref, 1, factorised) for a in args]
            for n, (o, new) in enumerate(outs):
                o_scr[n, rows, :] = o
                for h in range(HG_HEADS):
                    s_scr[n, h] = new[0][h]
            return carry
        lax.fori_loop(0, Lb // C, body, 0)

    safe = _decay_is_safe(g_scr[...].reshape(NB * Lb, -1), C // 2)
    pl.when(safe)(lambda: run(True))
    pl.when(jnp.logical_not(safe))(lambda: run(False))
    for n in range(NB):
        o_refs[n][...] = (_rms(o_scr[n]) * gain_ref[...] * seq[n][3][...]).astype(o_refs[n].dtype)

    @pl.when(l == pl.num_programs(0) - 1)
    def _():
        st_ref[...] = s_scr[...]


def _hgrn_prompt(P, gain, B, L, Lb):
    C = math.gcd(L, SCAN_CHUNK)
    nl = L // Lb
    W = HG_KEY_DIM
    blk = lambda b, seg: pl.BlockSpec((Lb, W), lambda l, b=b, seg=seg: (b * nl + l, seg))
    outs = pl.pallas_call(
        functools.partial(_hgrn_prompt_kernel, C=C, NB=B),
        grid=(nl,),
        in_specs=[blk(b, seg) for b in range(B) for seg in range(4)] + [pl.BlockSpec((1, W), lambda l: (0, 0))],
        out_specs=[pl.BlockSpec((Lb, W), lambda l: (l, 0)) for _ in range(B)]
        + [pl.BlockSpec((B, HG_HEADS, HG_EXPAND, HG_HEAD_V), lambda l: (0, 0, 0, 0))],
        out_shape=[jax.ShapeDtypeStruct((L, HG_VAL_DIM), BF16) for _ in range(B)]
        + [jax.ShapeDtypeStruct((B, HG_HEADS, HG_EXPAND, HG_HEAD_V), F32)],
        scratch_shapes=[pltpu.VMEM((B, HG_HEADS, HG_EXPAND, HG_HEAD_V), F32), pltpu.VMEM((B, Lb, W), F32),
                        pltpu.VMEM((B, Lb, W), F32), pltpu.VMEM((3, C, LANES), F32)],
        compiler_params=_params(("arbitrary",)),
        name="hgrn_prompt",
    )(*([P] * (4 * B)), gain)
    return outs[:B], outs[B]


def _hgrn_sample_kernel(q_ref, f_ref, v_ref, gate_ref, gain_ref, s_ref, o_ref, so_ref, o_scr, tmp_ref, *, G):
    NG = s_ref.shape[0]
    g = jnp.log(f_ref[...])
    pm = _prefix_matrix(NG * G, G)

    def run(factorised):
        states = [[s_ref[n, h] for h in range(HG_HEADS)] for n in range(NG)]
        o, new = _hgrn_chunk(q_ref[...], f_ref[...], g, v_ref[...], states, pm, tmp_ref, NG, factorised)
        o_scr[...] = o
        for n in range(NG):
            for h in range(HG_HEADS):
                so_ref[n, h] = new[n][h]

    safe = _decay_is_safe(g, G // 2)
    pl.when(safe)(lambda: run(True))
    pl.when(jnp.logical_not(safe))(lambda: run(False))
    _hgrn_epilogue(o_scr, gain_ref, gate_ref, o_ref)


def _hgrn_sample(P, gain, state, Bs, Ls, Bb, row0):
    W = HG_KEY_DIM
    rows = Bb * Ls
    blk0 = row0 // rows

    def col(seg):
        return pl.BlockSpec((rows, W), lambda i, seg=seg: (blk0 + i, seg))

    st = pl.BlockSpec((Bb, HG_HEADS, HG_EXPAND, HG_HEAD_V), lambda i: (i, 0, 0, 0))
    return pl.pallas_call(
        functools.partial(_hgrn_sample_kernel, G=Ls),
        grid=(Bs // Bb,),
        in_specs=[col(0), col(1), col(2), col(3), pl.BlockSpec((1, W), lambda i: (0, 0)), st],
        out_specs=[pl.BlockSpec((rows, W), lambda i: (i, 0)), st],
        out_shape=[
            jax.ShapeDtypeStruct((Bs * Ls, HG_VAL_DIM), BF16),
            jax.ShapeDtypeStruct((Bs, HG_HEADS, HG_EXPAND, HG_HEAD_V), F32),
        ],
        scratch_shapes=[pltpu.VMEM((rows, W), F32), pltpu.VMEM((3, Ls, LANES), F32)],
        compiler_params=_params(("parallel",)),
        name="hgrn_sample",
    )(P, P, P, P, gain, state)


def _ret_chunk(q, k, v, states, dm_ref, qw_ref, kw_ref, sd_ref):
    outs, new_states = [], []
    C = q.shape[0]
    qb, kb, vb = q.astype(BF16), k.astype(BF16), v.astype(BF16)
    for h in range(RET_HEADS):
        qs = slice(h * RET_HEAD_QK, (h + 1) * RET_HEAD_QK)
        vs = slice(h * RET_HEAD_V, (h + 1) * RET_HEAD_V)
        att = (_dot_nt(qb[:, qs], kb[:, qs]) * dm_ref[h]).astype(BF16)
        kw = (k[:, qs] * kw_ref[h]).astype(BF16)
        if C <= 2 * SUBLANES:
            both = _dot(jnp.concatenate([att, kw.T], axis=0), vb[:, vs])
            intra, upd = both[:C], both[C:]
        else:
            intra, upd = _dot(att, vb[:, vs]), _dot_tn(kw, vb[:, vs])
        o = intra + _dot((q[:, qs] * qw_ref[h]).astype(BF16), states[h].astype(BF16))
        new_states.append(sd_ref[h, 0:1, 0:1] * states[h] + upd)
        outs.append(o)
    return jnp.concatenate(outs, axis=1), new_states


def _ret_epilogue(o_scr, gate_ref, o_ref):
    for h in range(RET_HEADS):
        vs = slice(h * RET_HEAD_V, (h + 1) * RET_HEAD_V)
        o_ref[:, vs] = (_rms(o_scr[:, vs]) * gate_ref[:, vs]).astype(o_ref.dtype)


def _ret_prompt_kernel(*refs, NB):
    ins, (dm_ref, qw_ref, kw_ref, sd_ref), rest = refs[:4 * NB], refs[4 * NB:4 * NB + 4], refs[4 * NB + 4:]
    o_refs, (st_ref, s_scr, o_scr) = rest[:NB], rest[NB:]
    seq = [ins[4 * n:4 * n + 4] for n in range(NB)]
    l = pl.program_id(0)
    Lb = o_refs[0].shape[0]
    C = dm_ref.shape[1]

    @pl.when(l == 0)
    def _():
        s_scr[...] = jnp.zeros_like(s_scr)

    def body(ci, carry):
        rows = pl.ds(pl.multiple_of(ci * C, C), C)
        args = [(seq[n][0][rows, :], seq[n][1][rows, :], seq[n][2][rows, :],
                 [s_scr[n, h] for h in range(RET_HEADS)]) for n in range(NB)]
        outs = [_ret_chunk(*a, dm_ref, qw_ref, kw_ref, sd_ref) for a in args]
        for n, (o, new) in enumerate(outs):
            o_scr[n, rows, :] = o
            for h in range(RET_HEADS):
                s_scr[n, h] = new[h]
        return carry

    lax.fori_loop(0, Lb // C, body, 0)
    for n in range(NB):
        _ret_epilogue(o_scr.at[n], seq[n][3], o_refs[n])

    @pl.when(l == pl.num_programs(0) - 1)
    def _():
        st_ref[...] = s_scr[...]


def _ret_tables(C):
    log_gamma = jnp.log(1.0 - jnp.exp2(-5.0 - jnp.arange(RET_HEADS, dtype=F32)))
    idx = jnp.arange(C, dtype=F32)
    rel = idx[:, None] - idx[None, :]
    tri = jnp.tril(jnp.ones((C, C), dtype=bool))
    dmat = jnp.exp(jnp.where(tri[None], log_gamma[:, None, None] * rel[None], -jnp.inf))
    qw = jnp.exp(log_gamma[:, None] * (idx[None, :] + 1.0))[..., None]
    kw = jnp.exp(log_gamma[:, None] * (C - 1.0 - idx[None, :]))[..., None]
    sdec = jnp.exp(log_gamma * C)[:, None, None]
    bc = lambda a: jnp.broadcast_to(a, (RET_HEADS, a.shape[1], LANES))
    return dmat, bc(qw), bc(kw), jnp.broadcast_to(sdec, (RET_HEADS, SUBLANES, LANES))


_QK0 = (2 * HG_KEY_DIM + 2 * HG_VAL_DIM) // RET_QK_DIM
_V0 = (2 * HG_KEY_DIM + 2 * HG_VAL_DIM + 2 * RET_QK_DIM) // RET_V_DIM


def _full3(a):
    return pl.BlockSpec(a.shape, lambda *_: (0, 0, 0))


def _ret_prompt(P, B, L, Lb):
    C = math.gcd(L, SCAN_CHUNK)
    nl = L // Lb
    tabs = _ret_tables(C)
    st_shape = (B, RET_HEADS, RET_HEAD_QK, RET_HEAD_V)
    blk = lambda b, w, c: pl.BlockSpec((Lb, w), lambda l, b=b, c=c: (b * nl + l, c))
    per_seq = lambda b: [blk(b, RET_QK_DIM, _QK0), blk(b, RET_QK_DIM, _QK0 + 1),
                         blk(b, RET_V_DIM, _V0), blk(b, RET_V_DIM, _V0 + 1)]
    outs = pl.pallas_call(
        functools.partial(_ret_prompt_kernel, NB=B),
        grid=(nl,),
        in_specs=[spec for b in range(B) for spec in per_seq(b)] + [_full3(t) for t in tabs],
        out_specs=[pl.BlockSpec((Lb, RET_V_DIM), lambda l: (l, 0)) for _ in range(B)]
        + [pl.BlockSpec(st_shape, lambda l: (0, 0, 0, 0))],
        out_shape=[jax.ShapeDtypeStruct((L, RET_V_DIM), BF16) for _ in range(B)]
        + [jax.ShapeDtypeStruct(st_shape, F32)],
        scratch_shapes=[pltpu.VMEM(st_shape, F32), pltpu.VMEM((B, Lb, RET_V_DIM), F32)],
        compiler_params=_params(("arbitrary",)),
        name="ret_prompt",
    )(*([P] * (4 * B)), *tabs)
    return outs[:B], outs[B]


def _ret_sample_kernel(q_ref, k_ref, v_ref, gate_ref, s_ref, dm_ref, qw_ref, kw_ref, sd_ref,
                       o_ref, so_ref, o_scr, *, C):
    Bb = s_ref.shape[0]
    args = [(q_ref[bb * C:(bb + 1) * C, :], k_ref[bb * C:(bb + 1) * C, :], v_ref[bb * C:(bb + 1) * C, :],
             [s_ref[bb, h] for h in range(RET_HEADS)]) for bb in range(Bb)]
    outs = [_ret_chunk(*a, dm_ref, qw_ref, kw_ref, sd_ref) for a in args]
    for bb, (o, new) in enumerate(outs):
        o_scr[bb * C:(bb + 1) * C, :] = o
        for h in range(RET_HEADS):
            so_ref[bb, h] = new[h]
    _ret_epilogue(o_scr, gate_ref, o_ref)


def _ret_sample(P, state, Bs, Ls, Bb, row0):
    tabs = _ret_tables(Ls)
    rows = Bb * Ls
    blk0 = row0 // rows
    st = pl.BlockSpec((Bb, RET_HEADS, RET_HEAD_QK, RET_HEAD_V), lambda i: (i, 0, 0, 0))
    return pl.pallas_call(
        functools.partial(_ret_sample_kernel, C=Ls),
        grid=(Bs // Bb,),
        in_specs=[
            pl.BlockSpec((rows, RET_QK_DIM), lambda i: (blk0 + i, _QK0)),
            pl.BlockSpec((rows, RET_QK_DIM), lambda i: (blk0 + i, _QK0 + 1)),
            pl.BlockSpec((rows, RET_V_DIM), lambda i: (blk0 + i, _V0)),
            pl.BlockSpec((rows, RET_V_DIM), lambda i: (blk0 + i, _V0 + 1)),
            st,
        ] + [_full3(t) for t in tabs],
        out_specs=[pl.BlockSpec((rows, RET_V_DIM), lambda i: (i, 0)), st],
        out_shape=[
            jax.ShapeDtypeStruct((Bs * Ls, RET_V_DIM), BF16),
            jax.ShapeDtypeStruct((Bs, RET_HEADS, RET_HEAD_QK, RET_HEAD_V), F32),
        ],
        scratch_shapes=[pltpu.VMEM((rows, RET_V_DIM), F32)],
        compiler_params=_params(("parallel",)),
        name="ret_sample",
    )(P, P, P, P, state, *tabs)


def _split_dot(a, w_hilo):
    a_hi = a.astype(BF16)
    a_lo = (a - a_hi.astype(F32)).astype(BF16)
    hi = _dot(a_hi, w_hilo)
    return hi[:, :LANES] + (hi[:, LANES:] + _dot(a_lo, w_hilo[:, :LANES]))


def _route(logits):
    lane = lax.broadcasted_iota(I32, logits.shape, 1)
    neg = jnp.float32(-jnp.inf)
    big = jnp.int32(LANES)
    gmask = lane < N_GROUPS
    gl = jnp.where(gmask, logits, neg)
    gmax = jnp.max(gl, axis=1, keepdims=True)
    gu = jnp.where(gmask, jnp.exp(gl - gmax), 0.0)
    gp = gu / jnp.sum(gu, axis=1, keepdims=True)
    g_w = jnp.max(gp, axis=1, keepdims=True)
    g_i = jnp.min(jnp.where(gmask & (gp == g_w), lane, big), axis=1, keepdims=True)
    lo = ROUTE_EXPERT_LANE + g_i * EXPERTS_PER_GROUP
    emask = (lane >= lo) & (lane < lo + EXPERTS_PER_GROUP)
    el = jnp.where(emask, logits, neg)
    e1 = jnp.max(el, axis=1, keepdims=True)
    i1 = jnp.min(jnp.where(emask & (el == e1), lane, big), axis=1, keepdims=True)
    el2 = jnp.where(lane == i1, neg, el)
    e2 = jnp.max(el2, axis=1, keepdims=True)
    i2 = jnp.min(jnp.where(emask & (lane != i1) & (el2 == e2), lane, big), axis=1, keepdims=True)
    u2 = jnp.exp(e2 - e1)
    den = 1.0 + u2
    w1 = (1.0 / den) * g_w
    w2 = (u2 / den) * g_w
    id1 = (i1 - ROUTE_EXPERT_LANE).astype(F32)
    id2 = (i2 - ROUTE_EXPERT_LANE).astype(F32)
    return (jnp.where(lane == LANE_E1, id1, 0.0) + jnp.where(lane == LANE_E2, id2, 0.0)
            + jnp.where(lane == LANE_W1, w1, 0.0) + jnp.where(lane == LANE_W2, w2, 0.0))


def _pick(i, bounds, refs):
    val = refs[-1][...]
    for n in range(len(refs) - 2, -1, -1):
        val = jnp.where(i < bounds[n + 1], refs[n][...], val)
    return val


def _mix_kernel(*refs, bounds):
    ns = len(bounds) - 1
    ohg_refs, or_refs, refs = refs[:ns], refs[ns:2 * ns], refs[2 * ns:]
    (ga0_ref, ga1_ref, gb0_ref, gb1_ref, xp_ref, xs_ref, wbh_ref, wbr_ref, wout_ref, nffn_ref,
     wr_ref, br_ref, h_ref, hn_ref, route_ref) = refs
    i = pl.program_id(0)
    tm = h_ref.shape[0]
    is_prompt = i < bounds[-2]
    p_hg = _dot(_pick(i, bounds, ohg_refs), wbh_ref[...])
    p_r = _dot(_pick(i, bounds, or_refs), wbr_ref[...])
    half = D_MODEL // 2
    m0 = ga0_ref[...] * p_hg[:, :half] + gb0_ref[...] * p_r[:, :half]
    m1 = ga1_ref[...] * p_hg[:, half:] + gb1_ref[...] * p_r[:, half:]
    mixed = jnp.concatenate([m0, m1], axis=1).astype(BF16)
    x = jnp.where(is_prompt, xp_ref[...], xs_ref[...])
    h = x + _dot(mixed, wout_ref[...])
    h_ref[...] = h
    hn = _rms(h) * nffn_ref[...]
    hn_ref[...] = _to_token_tiles(hn.astype(BF16))
    route_ref[...] = _route(_split_dot(hn, wr_ref[...]) + br_ref[...])


def _mix(o_hg, o_r, P, xp, xs, wbh, wbr, wout, norm_ffn, wr, br, tm):
    npt, nst = xp.shape[0] // tm, xs.shape[0] // tm
    bounds = [0]
    for a in o_hg:
        bounds.append(bounds[-1] + a.shape[0] // tm)
    T = (npt + nst) * tm
    half = D_MODEL // 2
    ga = (IN_WIDTH - 2 * D_MODEL) // half

    def pcol(c):
        return pl.BlockSpec((tm, half), lambda i, c=c: (i, c))

    def const(a):
        return pl.BlockSpec(a.shape, lambda i: (0, 0), pipeline_mode=pl.Buffered(1))

    def prompt(w):
        return pl.BlockSpec((tm, w), lambda i: (jnp.minimum(i, npt - 1), 0))

    def sample(w):
        return pl.BlockSpec((tm, w), lambda i: (jnp.maximum(i - npt, 0), 0))

    def source(n):
        lo, hi = bounds[n], bounds[n + 1]
        return pl.BlockSpec((tm, half), lambda i: (jnp.clip(i - lo, 0, hi - lo - 1), 0))

    sources = [source(n) for n in range(len(o_hg))]
    return pl.pallas_call(
        functools.partial(_mix_kernel, bounds=tuple(bounds)),
        grid=(npt + nst,),
        in_specs=sources + sources + [
            pcol(ga), pcol(ga + 1), pcol(ga + 2), pcol(ga + 3),
            prompt(D_MODEL), sample(D_MODEL),
            const(wbh), const(wbr), const(wout), const(norm_ffn), const(wr), const(br),
        ],
        out_specs=[
            pl.BlockSpec((tm, D_MODEL), lambda i: (i, 0)),
            pl.BlockSpec((tm * ROW_GROUP, LANES), lambda i: (i, 0)),
            pl.BlockSpec((tm, LANES), lambda i: (i, 0)),
        ],
        out_shape=[
            jax.ShapeDtypeStruct((T, D_MODEL), F32),
            jax.ShapeDtypeStruct((T * ROW_GROUP, LANES), BF16),
            jax.ShapeDtypeStruct((T, LANES), F32),
        ],
        compiler_params=_params(("parallel",)),
        name="mix",
    )(*o_hg, *o_r, P, P, P, P, xp, xs, wbh, wbr, wout, norm_ffn, wr, br)


def _plan_kernel(route_ref, pos_ref, te_ref, tail_ref, cum_scr, *, tb):
    T = route_ref.shape[0]
    lane_t = lax.broadcasted_iota(I32, (tb, LANES), 1).astype(F32)
    r = lax.broadcasted_iota(I32, (tb, tb), 0)
    c = lax.broadcasted_iota(I32, (tb, tb), 1)
    strict_lower = jnp.where(r > c, 1.0, 0.0).astype(BF16)

    def onehots(rows):
        rt = route_ref[rows, :]
        a1 = jnp.where(lane_t == rt[:, LANE_E1:LANE_E1 + 1], 1.0, 0.0)
        a2 = jnp.where(lane_t == rt[:, LANE_E2:LANE_E2 + 1], 1.0, 0.0)
        return a1, a2

    def rank_body(bi, carry):
        rows = pl.ds(pl.multiple_of(bi * tb, tb), tb)
        a1, a2 = onehots(rows)
        m = a1 + a2
        cum_scr[rows, :] = _dot(strict_lower, m.astype(BF16)) + carry
        return carry + jnp.sum(m, axis=0, keepdims=True)

    counts = lax.fori_loop(0, T // tb, rank_body, jnp.zeros((1, LANES), F32))
    ntiles = jnp.floor((counts + (MOE_TILE - 1)) * (1.0 / MOE_TILE))
    rr = lax.broadcasted_iota(I32, (LANES, LANES), 0)
    cc = lax.broadcasted_iota(I32, (LANES, LANES), 1)
    upper = jnp.where(rr < cc, 1.0, 0.0).astype(BF16)
    tile0 = _dot(jnp.broadcast_to(ntiles, (SUBLANES, LANES)).astype(BF16), upper)[0:1, :]
    slot0 = tile0 * MOE_TILE

    def pos_body(bi, carry):
        rows = pl.ds(pl.multiple_of(bi * tb, tb), tb)
        a1, a2 = onehots(rows)
        base = cum_scr[rows, :] + slot0
        p1 = jnp.sum(a1 * base, axis=1, keepdims=True)
        p2 = jnp.sum(a2 * base, axis=1, keepdims=True)
        pos_ref[rows, :] = (jnp.where(lane_t == 0.0, p1, 0.0) + jnp.where(lane_t == 1.0, p2, 0.0)).astype(I32)
        return carry

    lax.fori_loop(0, T // tb, pos_body, 0)
    tend = tile0 + ntiles
    lane = lax.broadcasted_iota(I32, (LANES, LANES), 1)
    tile_i = lax.broadcasted_iota(I32, (LANES, LANES), 0).astype(F32)
    is_expert = lane < N_EXPERTS
    te = jnp.sum(jnp.where(is_expert & (jnp.broadcast_to(tend, (LANES, LANES)) <= tile_i), 1.0, 0.0),
                 axis=1, keepdims=True)
    te_ref[...] = jnp.broadcast_to(jnp.minimum(te, N_EXPERTS - 1.0), (LANES, LANES)).astype(I32)
    lane1 = lax.broadcasted_iota(I32, (1, LANES), 1)
    total = jnp.sum(jnp.where(lane1 < N_EXPERTS, ntiles, 0.0), axis=1, keepdims=True)
    tail = jnp.where(ntiles > 0.0, (tend - 1.0) * MOE_TILE, -1.0)
    tail = jnp.where(lane1 < N_EXPERTS, tail, jnp.where(lane1 == N_EXPERTS, total, 0.0))
    tail_ref[...] = jnp.broadcast_to(tail, (SUBLANES, LANES)).astype(I32)


def _plan(route, tb):
    T = route.shape[0]
    return pl.pallas_call(
        functools.partial(_plan_kernel, tb=tb),
        grid=(1,),
        in_specs=[pl.BlockSpec((T, LANES), lambda i: (0, 0))],
        out_specs=[
            pl.BlockSpec((T, LANES), lambda i: (0, 0)),
            pl.BlockSpec((LANES, LANES), lambda i: (0, 0)),
            pl.BlockSpec((SUBLANES, LANES), lambda i: (0, 0)),
        ],
        out_shape=[
            jax.ShapeDtypeStruct((T, LANES), I32),
            jax.ShapeDtypeStruct((LANES, LANES), I32),
            jax.ShapeDtypeStruct((SUBLANES, LANES), I32),
        ],
        scratch_shapes=[pltpu.VMEM((T, LANES), F32)],
        compiler_params=_params(("arbitrary",)),
        name="moe_plan",
    )(route)


def _row_group(ref, idx):
    return ref.at[pl.ds(pl.multiple_of(idx * ROW_GROUP, ROW_GROUP), ROW_GROUP), :]


def _dispatch_kernel(pos_ref, tail_ref, hn_ref, xs_ref, zero_scr, zsem, sem):
    i = pl.program_id(0)
    tm = hn_ref.shape[0] // ROW_GROUP
    tile_rows = MOE_TILE * ROW_GROUP

    @pl.when(i == 0)
    def _():
        zero_scr[...] = jnp.zeros_like(zero_scr)
        n_used = tail_ref[N_EXPERTS]

        def zero_copy(first_slot):
            start = pl.multiple_of(first_slot * ROW_GROUP, tile_rows)
            return pltpu.make_async_copy(zero_scr, xs_ref.at[pl.ds(start, tile_rows), :], zsem)

        def tails(fn):
            def body(e, carry):
                pl.when(tail_ref[e] >= 0)(lambda: fn(zero_copy(jnp.maximum(tail_ref[e], 0))))
                return carry
            lax.fori_loop(0, N_EXPERTS, body, 0)

        def unused(fn):
            def body(t, carry):
                fn(zero_copy(t * MOE_TILE))
                return carry
            lax.fori_loop(n_used, xs_ref.shape[0] // tile_rows, body, 0)

        tails(lambda c: c.start())
        unused(lambda c: c.start())
        tails(lambda c: c.wait())
        unused(lambda c: c.wait())

    def copy(r, k):
        slot = pos_ref[(i * tm + r) * TOP_K + k]
        return pltpu.make_async_copy(_row_group(hn_ref, r), _row_group(xs_ref, slot), sem)

    def start(r, carry):
        for k in range(TOP_K):
            copy(r, k).start(priority=k)
        return carry

    lax.fori_loop(0, tm, start, 0, unroll=8)
    for _ in range(TOP_K):
        pltpu.make_async_copy(hn_ref, xs_ref.at[pl.ds(0, tm * ROW_GROUP), :], sem).wait()


def _dispatch(pos, tails, hn_rows, n_slots, tm):
    T = hn_rows.shape[0] // ROW_GROUP
    return pl.pallas_call(
        _dispatch_kernel,
        grid_spec=pltpu.PrefetchScalarGridSpec(
            num_scalar_prefetch=2,
            grid=(T // tm,),
            in_specs=[pl.BlockSpec((tm * ROW_GROUP, LANES), lambda i, *_: (i, 0))],
            out_specs=pl.BlockSpec(memory_space=pl.ANY),
            scratch_shapes=[pltpu.VMEM((MOE_TILE * ROW_GROUP, LANES), BF16),
                            pltpu.SemaphoreType.DMA, pltpu.SemaphoreType.DMA],
        ),
        out_shape=jax.ShapeDtypeStruct((n_slots * ROW_GROUP, LANES), BF16),
        compiler_params=_params(("arbitrary",)),
        name="moe_dispatch",
    )(pos, tails, hn_rows)


def _experts_kernel(te_ref, nt_ref, xs_ref, wg_hbm, wu_hbm, wd_hbm, ys_ref,
                    wg_buf, wu_buf, wd_buf, slot_ref, sem):
    i = pl.program_id(0)
    nt = nt_ref[0]
    e = te_ref[i]

    def fetch(expert, slot):
        return [pltpu.make_async_copy(hbm.at[expert], buf.at[slot], sem.at[slot, n])
                for n, (hbm, buf) in enumerate(((wg_hbm, wg_buf), (wu_hbm, wu_buf), (wd_hbm, wd_buf)))]

    @pl.when(i == 0)
    def _():
        slot_ref[1] = 0
        for c in fetch(e, 0):
            c.start(priority=1)

    @pl.when((i < nt) & ((i == 0) | (te_ref[jnp.maximum(i - 1, 0)] != e)))
    def _():
        slot = slot_ref[1]
        for c in fetch(e, slot):
            c.wait()
        nxt = lax.while_loop(lambda j: (j < nt) & (te_ref[jnp.minimum(j, nt - 1)] == e), lambda j: j + 1, i + 1)

        @pl.when(nxt < nt)
        def _():
            for c in fetch(te_ref[jnp.minimum(nxt, nt - 1)], 1 - slot):
                c.start(priority=1)

        slot_ref[0] = slot
        slot_ref[1] = 1 - slot

    @pl.when(i < nt)
    def _():
        slot = slot_ref[0]
        x = _from_token_tiles(xs_ref[...])[0]
        hmid = (_silu(_dot(x, wg_buf[slot].astype(BF16))) * _dot(x, wu_buf[slot].astype(BF16))).astype(BF16)
        ys_ref[...] = _to_token_tiles(_dot(hmid, wd_buf[slot].astype(BF16)))

    @pl.when(i >= nt)
    def _():
        ys_ref[...] = jnp.zeros_like(ys_ref)


def _experts(te, nt, xs_rows, w_ge, w_ue, w_de, n_tiles):
    M = MOE_TILE
    rows = pl.BlockSpec((M * ROW_GROUP, LANES), lambda i, te_ref, nt_ref: (jnp.minimum(i, nt_ref[0] - 1), 0))
    hbm = pl.BlockSpec(memory_space=pl.ANY)
    up, down = (D_MODEL, EXPERT_FF), (EXPERT_FF, D_MODEL)
    return pl.pallas_call(
        _experts_kernel,
        grid_spec=pltpu.PrefetchScalarGridSpec(
            num_scalar_prefetch=2,
            grid=(n_tiles,),
            in_specs=[rows, hbm, hbm, hbm],
            out_specs=pl.BlockSpec((M * ROW_GROUP, LANES), lambda i, te_ref, nt_ref: (i, 0)),
            scratch_shapes=[pltpu.VMEM((2,) + up, F32), pltpu.VMEM((2,) + up, F32), pltpu.VMEM((2,) + down, F32),
                            pltpu.SMEM((2,), I32), pltpu.SemaphoreType.DMA((2, 3))],
        ),
        out_shape=jax.ShapeDtypeStruct(xs_rows.shape, F32),
        compiler_params=_params(("arbitrary",)),
        name="moe_experts",
    )(te, nt, xs_rows, w_ge, w_ue, w_de)


def _final_kernel(pos_ref, h_ref, route_ref, gain_ref, ys_ref, y_ref, buf, sem, *, tile0):
    i = pl.program_id(0)
    n = pl.num_programs(0)
    tm = h_ref.shape[0]

    def copy(step, r, k, slot_buf):
        slot = pos_ref[((tile0 + step) * tm + r) * TOP_K + k]
        return pltpu.make_async_copy(_row_group(ys_ref, slot), _row_group(buf.at[slot_buf], r * TOP_K + k),
                                     sem.at[slot_buf])

    def start_all(step, slot_buf):
        def body(r, carry):
            for k in range(TOP_K):
                copy(step, r, k, slot_buf).start(priority=k)
            return carry
        lax.fori_loop(0, tm, body, 0, unroll=8)

    cur = i % 2
    pl.when(i == 0)(lambda: start_all(0, 0))
    pl.when(i + 1 < n)(lambda: start_all(i + 1, 1 - cur))
    pltpu.make_async_copy(ys_ref.at[pl.ds(0, buf.shape[1]), :], buf.at[cur], sem.at[cur]).wait()

    rt = route_ref[...]
    w1 = rt[:, LANE_W1:LANE_W1 + 1]
    w2 = rt[:, LANE_W2:LANE_W2 + 1]
    o1, o2 = _from_token_tiles(buf[cur], groups=TOP_K)
    v = h_ref[...] + (w1 * o1 + w2 * o2)
    y_ref[...] = _rms(v) * gain_ref[...]


def _final(pos, h, route, gain, ys_rows, tile0, n_tiles, tm):
    return pl.pallas_call(
        functools.partial(_final_kernel, tile0=tile0),
        grid_spec=pltpu.PrefetchScalarGridSpec(
            num_scalar_prefetch=1,
            grid=(n_tiles,),
            in_specs=[
                pl.BlockSpec((tm, D_MODEL), lambda i, *_: (tile0 + i, 0)),
                pl.BlockSpec((tm, LANES), lambda i, *_: (tile0 + i, 0)),
                pl.BlockSpec((1, D_MODEL), lambda i, *_: (0, 0)),
                pl.BlockSpec(memory_space=pl.ANY),
            ],
            out_specs=pl.BlockSpec((tm, D_MODEL), lambda i, *_: (i, 0)),
            scratch_shapes=[pltpu.VMEM((2, tm * TOP_K * ROW_GROUP, LANES), F32), pltpu.SemaphoreType.DMA((2,))],
        ),
        out_shape=jax.ShapeDtypeStruct((n_tiles * tm, D_MODEL), F32),
        compiler_params=_params(("arbitrary",)),
        name="final",
    )(pos, h, route, gain, ys_rows)


def _rope_tables(pos):
    half = RET_HEAD_QK // 2
    inv = 1.0 / (ROPE_BASE ** jnp.linspace(0.0, 1.0, half, dtype=F32))
    ang = jnp.repeat(pos[:, None] * inv[None, :], 2, axis=-1)
    cos, sin = jnp.cos(ang), jnp.sin(ang)
    even = (jnp.arange(RET_HEAD_QK) % 2 == 0)[None, :]
    sa = jnp.where(even, -sin, 0.0)
    sb = jnp.where(even, 0.0, sin)
    return cos, sa, sb


def _tile_sizes(Tp, Ts, Lp):
    g = math.gcd(Tp, Ts)
    return dict(
        xnorm=math.gcd(g, 512),
        proj=(Tp + Ts) // 4,
        mix=math.gcd(g, 256),
        plan=math.gcd(Tp + Ts, 512),
        dispatch=math.gcd(g, 256),
        final=math.gcd(g, 256),
        scan=math.gcd(Lp, 128),
    )


def kernel(x_prompt, x_sample, state_hgrn, state_ret, norm_mix, w_in, hg_lb_logits, hg_norm,
           w_branch_hg, w_branch_ret, b_gate, w_out, norm_ffn, w_router_group, b_router_group,
           w_router_expert, b_router_expert, w_expert_gate, w_expert_up, w_expert_down, norm_final):
    assert w_in.shape == (1, D_MODEL, IN_WIDTH), "single-layer trunk only"
    Bp, Lp, _ = x_prompt.shape
    Bs, Ls, _ = x_sample.shape
    Tp, Ts = Bp * Lp, Bs * Ls
    T = Tp + Ts
    ts = _tile_sizes(Tp, Ts, Lp)
    xp = x_prompt.reshape(Tp, D_MODEL)
    xs = x_sample.reshape(Ts, D_MODEL)

    lower = jnp.cumsum(jax.nn.softmax(hg_lb_logits.astype(F32), axis=0), axis=0)[0]
    act = _act_coefficients(lower, b_gate[0])
    tab_p = _rope_tables(jnp.arange(Lp, dtype=F32))
    tab_s = _rope_tables(jnp.arange(Ls, dtype=F32) + jnp.float32(PAST_LEN))
    tabs = tuple(jnp.concatenate([jnp.tile(a, (Bp, 1)), jnp.tile(b, (Bs, 1))]) for a, b in zip(tab_p, tab_s))
    xn = _xnorm(xp, xs, norm_mix, ts["xnorm"])
    P = _in_proj(xn, w_in[0], act, tabs, ts["proj"])

    bb = math.gcd(Bs, 16)
    o_hg_p, shp = _hgrn_prompt(P, hg_norm, Bp, Lp, ts["scan"])
    o_hg_s, shs = _hgrn_sample(P, hg_norm, state_hgrn[0].astype(F32), Bs, Ls, bb, Tp)
    o_r_p, srp = _ret_prompt(P, Bp, Lp, ts["scan"])
    o_r_s, srs = _ret_sample(P, state_ret[0].astype(F32), Bs, Ls, bb, Tp)

    wbh = w_branch_hg[0].astype(BF16)
    wbr = w_branch_ret[0].astype(BF16)
    wout = w_out[0].astype(BF16)
    pad = LANES - N_GROUPS - N_EXPERTS
    wr = jnp.concatenate([w_router_group[0], w_router_expert[0], jnp.zeros((D_MODEL, pad), F32)], axis=1)
    wr_hi = wr.astype(BF16)
    wr_hilo = jnp.concatenate([wr_hi, (wr - wr_hi.astype(F32)).astype(BF16)], axis=1)
    br = jnp.concatenate([b_router_group[0], b_router_expert[0], jnp.zeros((pad,), F32)])[None, :]
    h, hn_rows, route = _mix(list(o_hg_p) + [o_hg_s], list(o_r_p) + [o_r_s], P, xp, xs, wbh, wbr, wout,
                             norm_ffn, wr_hilo, br, ts["mix"])

    n_tiles = -(-T * TOP_K // MOE_TILE) + N_EXPERTS
    assert n_tiles <= LANES
    pos_slab, te_slab, tail_slab = _plan(route, ts["plan"])
    pos = pos_slab[:, :TOP_K].reshape(-1)
    te = te_slab[:n_tiles, 0]
    tails = tail_slab[0, :N_EXPERTS + 1]
    nt = tail_slab[0, N_EXPERTS:N_EXPERTS + 1]
    xs_rows = _dispatch(pos, tails, hn_rows, n_tiles * MOE_TILE, ts["dispatch"])
    ys_rows = _experts(te, nt, xs_rows, w_expert_gate[0], w_expert_up[0], w_expert_down[0], n_tiles)

    tf = ts["final"]
    gain = norm_final[None, :]
    y_p = _final(pos, h, route, gain, ys_rows, 0, Tp // tf, tf)
    y_s = _final(pos, h, route, gain, ys_rows, Tp // tf, Ts // tf, tf)
    return (y_p.reshape(Bp, Lp, D_MODEL), y_s.reshape(Bs, Ls, D_MODEL),
            shp[None], srp[None], shs[None], srs[None])
```

```python
import functools
import math

import jax
import jax.numpy as jnp
from jax import lax
from jax.experimental import pallas as pl
from jax.experimental.pallas import tpu as pltpu

F32 = jnp.float32
BF16 = jnp.bfloat16
I32 = jnp.int32

D_MODEL = 2048
PAST_LEN = 16384
HG_HEAD_V = 128
HG_EXPAND = 128
HG_VAL_DIM = D_MODEL // 2
HG_HEADS = HG_VAL_DIM // HG_HEAD_V
HG_KEY_DIM = HG_HEADS * HG_EXPAND
RET_HEAD_V = 256
RET_HEAD_QK = 128
RET_V_DIM = D_MODEL // 2
RET_HEADS = RET_V_DIM // RET_HEAD_V
RET_QK_DIM = RET_HEADS * RET_HEAD_QK
IN_WIDTH = 2 * HG_KEY_DIM + 2 * HG_VAL_DIM + 2 * RET_QK_DIM + 2 * RET_V_DIM + 2 * D_MODEL
CHUNK = 64
SCAN_CHUNK = 128
N_GROUPS = 4
EXPERTS_PER_GROUP = 8
N_EXPERTS = N_GROUPS * EXPERTS_PER_GROUP
TOP_K = 2
EXPERT_FF = D_MODEL // 4
ROPE_BASE = 10000.0
EPS = 1e-6

LANES = 128
SUBLANES = 8
VMEM_LIMIT = 56 * 1024 * 1024
PROJ_MC = 256
PROJ_TN = 512
SEG_HQ, SEG_HF, SEG_HI, SEG_HG, SEG_RQ, SEG_RK, SEG_RV, SEG_RG, SEG_GA = 0, 2, 4, 6, 8, 9, 10, 12, 14
SAFE_EXP_SPAN = 80.0
ROW_GROUP = D_MODEL // LANES
MOE_TILE = 256
LANE_E1, LANE_E2, LANE_W1, LANE_W2 = 0, 1, 2, 3
ROUTE_EXPERT_LANE = N_GROUPS


def _params(sem):
    return pltpu.CompilerParams(dimension_semantics=sem, vmem_limit_bytes=VMEM_LIMIT)


def _silu(z):
    return z * jax.nn.sigmoid(z)


def _dot(a, b):
    return jnp.dot(a, b, preferred_element_type=F32)


def _dot_nt(a, b):
    return lax.dot_general(a, b, (((1,), (1,)), ((), ())), preferred_element_type=F32)


def _dot_tn(a, b):
    return lax.dot_general(a, b, (((0,), (0,)), ((), ())), preferred_element_type=F32)


def _rms(x):
    return x * lax.rsqrt(jnp.mean(x * x, axis=-1, keepdims=True) + EPS)


def _to_token_tiles(x):
    M = x.shape[0]
    pieces = jnp.stack([x[:, s * LANES:(s + 1) * LANES] for s in range(ROW_GROUP)], axis=0)
    return jnp.swapaxes(pieces, 0, 1).reshape(M * ROW_GROUP, LANES)


def _from_token_tiles(rows, groups=1):
    M = rows.shape[0] // (groups * ROW_GROUP)
    t = jnp.swapaxes(rows.reshape(M, groups * ROW_GROUP, LANES), 0, 1)
    return [jnp.concatenate([t[g * ROW_GROUP + s] for s in range(ROW_GROUP)], axis=1) for g in range(groups)]


def _xnorm_kernel(xp_ref, xs_ref, gain_ref, o_ref, *, n_prompt_tiles):
    i = pl.program_id(0)

    def norm(x_ref):
        o_ref[...] = (_rms(x_ref[...]) * gain_ref[...]).astype(BF16)

    pl.when(i < n_prompt_tiles)(lambda: norm(xp_ref))
    pl.when(i >= n_prompt_tiles)(lambda: norm(xs_ref))


def _xnorm(xp, xs, gain, tm):
    npt, nst = xp.shape[0] // tm, xs.shape[0] // tm
    return pl.pallas_call(
        functools.partial(_xnorm_kernel, n_prompt_tiles=npt),
        grid=(npt + nst,),
        in_specs=[
            pl.BlockSpec((tm, D_MODEL), lambda i: (jnp.minimum(i, npt - 1), 0)),
            pl.BlockSpec((tm, D_MODEL), lambda i: (jnp.maximum(i - npt, 0), 0)),
            pl.BlockSpec((1, D_MODEL), lambda i: (0, 0)),
        ],
        out_specs=pl.BlockSpec((tm, D_MODEL), lambda i: (i, 0)),
        out_shape=jax.ShapeDtypeStruct(((npt + nst) * tm, D_MODEL), BF16),
        compiler_params=_params(("parallel",)),
        name="xnorm",
    )(xp, xs, gain)


ACT_ALPHA, ACT_BETA, ACT_GAMMA, ACT_DELTA, ACT_EPS = 0, 1, 2, 3, 4


def _act_coefficients(lower, b_gate):
    z = jnp.zeros((IN_WIDTH,), F32)
    seg = lambda a, b: slice(a * PROJ_TN, b * PROJ_TN)
    alpha = z.at[seg(SEG_HI, SEG_HG)].set(1.0).at[seg(SEG_RQ, SEG_RG)].set(1.0)
    beta = z.at[seg(SEG_GA, IN_WIDTH // PROJ_TN)].set(b_gate)
    gamma = (z.at[seg(SEG_HQ, SEG_HF)].set(HG_EXPAND ** -0.5)
             .at[seg(SEG_HG, SEG_RQ)].set(1.0).at[seg(SEG_RG, SEG_GA)].set(1.0))
    delta = z.at[seg(SEG_HF, SEG_HI)].set(1.0 - lower).at[seg(SEG_GA, IN_WIDTH // PROJ_TN)].set(1.0)
    eps = z.at[seg(SEG_HF, SEG_HI)].set(lower)
    return jnp.stack([alpha, beta, gamma, delta, eps, z, z, z])


def _in_proj_kernel(x_ref, w_ref, act_ref, cos_ref, sa_ref, sb_ref, o_ref):
    j = pl.program_id(1)
    w = w_ref[...].astype(BF16)
    row = lambda r: act_ref[r:r + 1, :]
    tm = x_ref.shape[0]
    mc = math.gcd(tm, PROJ_MC)
    for m in range(0, tm, mc):
        p = _dot(x_ref[m:m + mc, :], w)
        o_ref[m:m + mc, :] = (row(ACT_ALPHA) * p + row(ACT_EPS)
                              + jax.nn.sigmoid(p + row(ACT_BETA)) * (row(ACT_GAMMA) * p + row(ACT_DELTA)))

    @pl.when((j >= SEG_RQ) & (j < SEG_RV))
    def _():
        scale = jnp.where(j == SEG_RK, RET_HEAD_QK ** -0.5, 1.0).astype(F32)
        cos, sa, sb = cos_ref[...], sa_ref[...], sb_ref[...]
        for hh in range(PROJ_TN // LANES):
            cols = slice(hh * LANES, (hh + 1) * LANES)
            xs = o_ref[:, cols]
            r = xs * cos + pltpu.roll(xs, LANES - 1, 1) * sa + pltpu.roll(xs, 1, 1) * sb
            o_ref[:, cols] = r * scale


def _in_proj(xn, w_in, act, tabs, tm):
    T = xn.shape[0]
    tab = pl.BlockSpec((tm, LANES), lambda i, j: (i, 0))
    return pl.pallas_call(
        _in_proj_kernel,
        grid=(T // tm, IN_WIDTH // PROJ_TN),
        in_specs=[
            pl.BlockSpec((tm, D_MODEL), lambda i, j: (i, 0)),
            pl.BlockSpec((D_MODEL, PROJ_TN), lambda i, j: (0, j)),
            pl.BlockSpec((SUBLANES, PROJ_TN), lambda i, j: (0, j)),
            tab, tab, tab,
        ],
        out_specs=pl.BlockSpec((tm, PROJ_TN), lambda i, j: (i, j)),
        out_shape=jax.ShapeDtypeStruct((T, IN_WIDTH), F32),
        compiler_params=_params(("parallel", "arbitrary")),
        name="in_proj",
    )(xn, w_in, act, *tabs)


def _prefix_matrix(C, G):
    r = lax.broadcasted_iota(I32, (C, 3 * C), 0)
    c = lax.broadcasted_iota(I32, (C, 3 * C), 1) % C
    return jnp.where((c <= r) & (c // G == r // G), 1.0, 0.0).astype(BF16)


def _prefix_sum(g, pm):
    g1 = g.astype(BF16)
    r1 = g - g1.astype(F32)
    g2 = r1.astype(BF16)
    g3 = (r1 - g2.astype(F32)).astype(BF16)
    return _dot(pm, jnp.concatenate([g1, g2, g3], axis=0))


def _state_decay_column(d_row):
    r = lax.broadcasted_iota(I32, (LANES, LANES), 0)
    c = lax.broadcasted_iota(I32, (LANES, LANES), 1)
    return jnp.sum(jnp.where(r == c, jnp.broadcast_to(d_row, (LANES, LANES)), 0.0), axis=1, keepdims=True)


def _hgrn_chunk(q, f, g, v, states, pm, tmp_ref, NG, factorised):
    C, W = q.shape
    G, H = C // NG, W // LANES
    hs = [slice(h * LANES, (h + 1) * LANES) for h in range(H)]
    gs = [slice(n * G, (n + 1) * G) for n in range(NG)]
    k = 1.0 - f
    b = _prefix_sum(g, pm)
    b3 = b.reshape(NG, G, W)
    b_end = b3[:, G - 1:G, :]
    q3, k3 = q.reshape(NG, G, W), k.reshape(NG, G, W)
    vb = v.astype(BF16)
    if factorised:
        b_mid = b3[:, G // 2 - 1:G // 2, :]
        qm3 = q3 * jnp.exp(b3 - b_mid)
        km3 = k3 * jnp.exp(b_mid - b3)
        qb = (qm3 * jnp.exp(b_mid)).reshape(C, W)
        kd = (km3 * jnp.exp(b_end - b_mid)).reshape(C, W)
        qm = qm3.reshape(C, W).astype(BF16)
        km = km3.reshape(C, W).astype(BF16)
        row = lax.broadcasted_iota(I32, (C, C), 0)
        col = lax.broadcasted_iota(I32, (C, C), 1)
        amask = (row >= col) & (row // G == col // G)
        o_intra = []
        for h in range(H):
            att = jnp.where(amask, _dot_nt(qm[:, hs[h]], km[:, hs[h]]), 0.0)
            o_intra.append(_dot(att.astype(BF16), vb[:, hs[h]]))
    else:
        qb = (q3 * jnp.exp(b3)).reshape(C, W)
        kd = (k3 * jnp.exp(b_end - b3)).reshape(C, W)
        trow = lax.broadcasted_iota(I32, (G, LANES), 0)
        o_intra = []
        for h in range(H):
            parts = []
            for n in range(NG):
                bh, qh = b[gs[n], hs[h]], q[gs[n], hs[h]]
                tmp_ref[0] = bh
                tmp_ref[1] = k[gs[n], hs[h]]
                tmp_ref[2] = v[gs[n], hs[h]]

                def body(s, acc, bh=bh, qh=qh):
                    bs = tmp_ref[0, pl.ds(s, 1), :]
                    ks = tmp_ref[1, pl.ds(s, 1), :]
                    vs = tmp_ref[2, pl.ds(s, 1), :]
                    w = jnp.where(trow >= s, jnp.exp(jnp.minimum(bh - bs, 0.0)), 0.0)
                    return acc + jnp.sum(qh * ks * w, axis=1, keepdims=True) * vs

                parts.append(lax.fori_loop(0, G, body, jnp.zeros((G, LANES), F32)))
            o_intra.append(parts[0] if NG == 1 else jnp.concatenate(parts, axis=0))
    d = jnp.exp(b_end)
    outs, new_states = [], [[None] * H for _ in range(NG)]
    for h in range(H):
        inter = []
        for n in range(NG):
            S = states[n][h]
            inter.append(_dot(qb[gs[n], hs[h]].astype(BF16), S.astype(BF16)))
            upd = _dot_tn(kd[gs[n], hs[h]].astype(BF16), v[gs[n], hs[h]].astype(BF16))
            new_states[n][h] = _state_decay_column(d[n, :, hs[h]]) * S + upd
        inter = inter[0] if NG == 1 else jnp.concatenate(inter, axis=0)
        outs.append(inter + o_intra[h])
    return jnp.concatenate(outs, axis=1), new_states


def _decay_is_safe(g, G):
    R, W = g.shape
    return jnp.min(jnp.sum(g.reshape(R // G, G, W), axis=1)) >= -SAFE_EXP_SPAN


def _hgrn_epilogue(o_scr, gain_ref, gate_ref, o_ref):
    o_ref[...] = (_rms(o_scr[...]) * gain_ref[...] * gate_ref[...]).astype(o_ref.dtype)


def _hgrn_prompt_kernel(*refs, C, NB):
    ins, (gain_ref,), rest = refs[:4 * NB], refs[4 * NB:4 * NB + 1], refs[4 * NB + 1:]
    o_refs, (st_ref, s_scr, g_scr, o_scr, tmp_ref) = rest[:NB], rest[NB:]
    seq = [ins[4 * n:4 * n + 4] for n in range(NB)]
    l = pl.program_id(0)
    Lb = o_refs[0].shape[0]

    @pl.when(l == 0)
    def _():
        s_scr[...] = jnp.zeros_like(s_scr)

    for n in range(NB):
        g_scr[n] = jnp.log(seq[n][1][...])
    pm = _prefix_matrix(C, C)

    def run(factorised):
        def body(ci, carry):
            rows = pl.ds(pl.multiple_of(ci * C, C), C)
            args = [(seq[n][0][rows, :], seq[n][1][rows, :], g_scr[n, rows, :], seq[n][2][rows, :],
                     [[s_scr[n, h] for h in range(HG_HEADS)]]) for n in range(NB)]
            outs = [_hgrn_chunk(*a, pm, tmp_ref, 1, factorised) for a in args]
            for n, (o, new) in enumerate(outs):
                o_scr[n, rows, :] = o
                for h in range(HG_HEADS):
                    s_scr[n, h] = new[0][h]
            return carry
        lax.fori_loop(0, Lb // C, body, 0)

    safe = _decay_is_safe(g_scr[...].reshape(NB * Lb, -1), C // 2)
    pl.when(safe)(lambda: run(True))
    pl.when(jnp.logical_not(safe))(lambda: run(False))
    for n in range(NB):
        o_refs[n][...] = (_rms(o_scr[n]) * gain_ref[...] * seq[n][3][...]).astype(o_refs[n].dtype)

    @pl.when(l == pl.num_programs(0) - 1)
    def _():
        st_ref[...] = s_scr[...]


def _hgrn_prompt(P, gain, B, L, Lb):
    C = math.gcd(L, SCAN_CHUNK)
    nl = L // Lb
    W = HG_KEY_DIM
    blk = lambda b, seg: pl.BlockSpec((Lb, W), lambda l, b=b, seg=seg: (b * nl + l, seg))
    outs = pl.pallas_call(
        functools.partial(_hgrn_prompt_kernel, C=C, NB=B),
        grid=(nl,),
        in_specs=[blk(b, seg) for b in range(B) for seg in range(4)] + [pl.BlockSpec((1, W), lambda l: (0, 0))],
        out_specs=[pl.BlockSpec((Lb, W), lambda l: (l, 0)) for _ in range(B)]
        + [pl.BlockSpec((B, HG_HEADS, HG_EXPAND, HG_HEAD_V), lambda l: (0, 0, 0, 0))],
        out_shape=[jax.ShapeDtypeStruct((L, HG_VAL_DIM), BF16) for _ in range(B)]
        + [jax.ShapeDtypeStruct((B, HG_HEADS, HG_EXPAND, HG_HEAD_V), F32)],
        scratch_shapes=[pltpu.VMEM((B, HG_HEADS, HG_EXPAND, HG_HEAD_V), F32), pltpu.VMEM((B, Lb, W), F32),
                        pltpu.VMEM((B, Lb, W), F32), pltpu.VMEM((3, C, LANES), F32)],
        compiler_params=_params(("arbitrary",)),
        name="hgrn_prompt",
    )(*([P] * (4 * B)), gain)
    return outs[:B], outs[B]


def _hgrn_sample_kernel(q_ref, f_ref, v_ref, gate_ref, gain_ref, s_ref, o_ref, so_ref, o_scr, tmp_ref, *, G):
    NG = s_ref.shape[0]
    g = jnp.log(f_ref[...])
    pm = _prefix_matrix(NG * G, G)

    def run(factorised):
        states = [[s_ref[n, h] for h in range(HG_HEADS)] for n in range(NG)]
        o, new = _hgrn_chunk(q_ref[...], f_ref[...], g, v_ref[...], states, pm, tmp_ref, NG, factorised)
        o_scr[...] = o
        for n in range(NG):
            for h in range(HG_HEADS):
                so_ref[n, h] = new[n][h]

    safe = _decay_is_safe(g, G // 2)
    pl.when(safe)(lambda: run(True))
    pl.when(jnp.logical_not(safe))(lambda: run(False))
    _hgrn_epilogue(o_scr, gain_ref, gate_ref, o_ref)


def _hgrn_sample(P, gain, state, Bs, Ls, Bb, row0):
    W = HG_KEY_DIM
    rows = Bb * Ls
    blk0 = row0 // rows

    def col(seg):
        return pl.BlockSpec((rows, W), lambda i, seg=seg: (blk0 + i, seg))

    st = pl.BlockSpec((Bb, HG_HEADS, HG_EXPAND, HG_HEAD_V), lambda i: (i, 0, 0, 0))
    return pl.pallas_call(
        functools.partial(_hgrn_sample_kernel, G=Ls),
        grid=(Bs // Bb,),
        in_specs=[col(0), col(1), col(2), col(3), pl.BlockSpec((1, W), lambda i: (0, 0)), st],
        out_specs=[pl.BlockSpec((rows, W), lambda i: (i, 0)), st],
        out_shape=[
            jax.ShapeDtypeStruct((Bs * Ls, HG_VAL_DIM), BF16),
            jax.ShapeDtypeStruct((Bs, HG_HEADS, HG_EXPAND, HG_HEAD_V), F32),
        ],
        scratch_shapes=[pltpu.VMEM((rows, W), F32), pltpu.VMEM((3, Ls, LANES), F32)],
        compiler_params=_params(("parallel",)),
        name="hgrn_sample",
    )(P, P, P, P, gain, state)


def _ret_chunk(q, k, v, states, dm_ref, qw_ref, kw_ref, sd_ref):
    outs, new_states = [], []
    C = q.shape[0]
    qb, kb, vb = q.astype(BF16), k.astype(BF16), v.astype(BF16)
    for h in range(RET_HEADS):
        qs = slice(h * RET_HEAD_QK, (h + 1) * RET_HEAD_QK)
        vs = slice(h * RET_HEAD_V, (h + 1) * RET_HEAD_V)
        att = (_dot_nt(qb[:, qs], kb[:, qs]) * dm_ref[h]).astype(BF16)
        kw = (k[:, qs] * kw_ref[h]).astype(BF16)
        if C <= 2 * SUBLANES:
            both = _dot(jnp.concatenate([att, kw.T], axis=0), vb[:, vs])
            intra, upd = both[:C], both[C:]
        else:
            intra, upd = _dot(att, vb[:, vs]), _dot_tn(kw, vb[:, vs])
        o = intra + _dot((q[:, qs] * qw_ref[h]).astype(BF16), states[h].astype(BF16))
        new_states.append(sd_ref[h, 0:1, 0:1] * states[h] + upd)
        outs.append(o)
    return jnp.concatenate(outs, axis=1), new_states


def _ret_epilogue(o_scr, gate_ref, o_ref):
    for h in range(RET_HEADS):
        vs = slice(h * RET_HEAD_V, (h + 1) * RET_HEAD_V)
        o_ref[:, vs] = (_rms(o_scr[:, vs]) * gate_ref[:, vs]).astype(o_ref.dtype)


def _ret_prompt_kernel(*refs, NB):
    ins, (dm_ref, qw_ref, kw_ref, sd_ref), rest = refs[:4 * NB], refs[4 * NB:4 * NB + 4], refs[4 * NB + 4:]
    o_refs, (st_ref, s_scr, o_scr) = rest[:NB], rest[NB:]
    seq = [ins[4 * n:4 * n + 4] for n in range(NB)]
    l = pl.program_id(0)
    Lb = o_refs[0].shape[0]
    C = dm_ref.shape[1]

    @pl.when(l == 0)
    def _():
        s_scr[...] = jnp.zeros_like(s_scr)

    def body(ci, carry):
        rows = pl.ds(pl.multiple_of(ci * C, C), C)
        args = [(seq[n][0][rows, :], seq[n][1][rows, :], seq[n][2][rows, :],
                 [s_scr[n, h] for h in range(RET_HEADS)]) for n in range(NB)]
        outs = [_ret_chunk(*a, dm_ref, qw_ref, kw_ref, sd_ref) for a in args]
        for n, (o, new) in enumerate(outs):
            o_scr[n, rows, :] = o
            for h in range(RET_HEADS):
                s_scr[n, h] = new[h]
        return carry

    lax.fori_loop(0, Lb // C, body, 0)
    for n in range(NB):
        _ret_epilogue(o_scr.at[n], seq[n][3], o_refs[n])

    @pl.when(l == pl.num_programs(0) - 1)
    def _():
        st_ref[...] = s_scr[...]


def _ret_tables(C):
    log_gamma = jnp.log(1.0 - jnp.exp2(-5.0 - jnp.arange(RET_HEADS, dtype=F32)))
    idx = jnp.arange(C, dtype=F32)
    rel = idx[:, None] - idx[None, :]
    tri = jnp.tril(jnp.ones((C, C), dtype=bool))
    dmat = jnp.exp(jnp.where(tri[None], log_gamma[:, None, None] * rel[None], -jnp.inf))
    qw = jnp.exp(log_gamma[:, None] * (idx[None, :] + 1.0))[..., None]
    kw = jnp.exp(log_gamma[:, None] * (C - 1.0 - idx[None, :]))[..., None]
    sdec = jnp.exp(log_gamma * C)[:, None, None]
    bc = lambda a: jnp.broadcast_to(a, (RET_HEADS, a.shape[1], LANES))
    return dmat, bc(qw), bc(kw), jnp.broadcast_to(sdec, (RET_HEADS, SUBLANES, LANES))


_QK0 = (2 * HG_KEY_DIM + 2 * HG_VAL_DIM) // RET_QK_DIM
_V0 = (2 * HG_KEY_DIM + 2 * HG_VAL_DIM + 2 * RET_QK_DIM) // RET_V_DIM


def _full3(a):
    return pl.BlockSpec(a.shape, lambda *_: (0, 0, 0))


def _ret_prompt(P, B, L, Lb):
    C = math.gcd(L, 2 * SCAN_CHUNK)
    Lb = max(Lb, C)
    nl = L // Lb
    tabs = _ret_tables(C)
    st_shape = (B, RET_HEADS, RET_HEAD_QK, RET_HEAD_V)
    blk = lambda b, w, c: pl.BlockSpec((Lb, w), lambda l, b=b, c=c: (b * nl + l, c))
    per_seq = lambda b: [blk(b, RET_QK_DIM, _QK0), blk(b, RET_QK_DIM, _QK0 + 1),
                         blk(b, RET_V_DIM, _V0), blk(b, RET_V_DIM, _V0 + 1)]
    outs = pl.pallas_call(
        functools.partial(_ret_prompt_kernel, NB=B),
        grid=(nl,),
        in_specs=[spec for b in range(B) for spec in per_seq(b)] + [_full3(t) for t in tabs],
        out_specs=[pl.BlockSpec((Lb, RET_V_DIM), lambda l: (l, 0)) for _ in range(B)]
        + [pl.BlockSpec(st_shape, lambda l: (0, 0, 0, 0))],
        out_shape=[jax.ShapeDtypeStruct((L, RET_V_DIM), BF16) for _ in range(B)]
        + [jax.ShapeDtypeStruct(st_shape, F32)],
        scratch_shapes=[pltpu.VMEM(st_shape, F32), pltpu.VMEM((B, Lb, RET_V_DIM), F32)],
        compiler_params=_params(("arbitrary",)),
        name="ret_prompt",
    )(*([P] * (4 * B)), *tabs)
    return outs[:B], outs[B]


def _ret_sample_kernel(q_ref, k_ref, v_ref, gate_ref, s_ref, dm_ref, qw_ref, kw_ref, sd_ref,
                       o_ref, so_ref, o_scr, *, C):
    Bb = s_ref.shape[0]
    args = [(q_ref[bb * C:(bb + 1) * C, :], k_ref[bb * C:(bb + 1) * C, :], v_ref[bb * C:(bb + 1) * C, :],
             [s_ref[bb, h] for h in range(RET_HEADS)]) for bb in range(Bb)]
    outs = [_ret_chunk(*a, dm_ref, qw_ref, kw_ref, sd_ref) for a in args]
    for bb, (o, new) in enumerate(outs):
        o_scr[bb * C:(bb + 1) * C, :] = o
        for h in range(RET_HEADS):
            so_ref[bb, h] = new[h]
    _ret_epilogue(o_scr, gate_ref, o_ref)


def _ret_sample(P, state, Bs, Ls, Bb, row0):
    tabs = _ret_tables(Ls)
    rows = Bb * Ls
    blk0 = row0 // rows
    st = pl.BlockSpec((Bb, RET_HEADS, RET_HEAD_QK, RET_HEAD_V), lambda i: (i, 0, 0, 0))
    return pl.pallas_call(
        functools.partial(_ret_sample_kernel, C=Ls),
        grid=(Bs // Bb,),
        in_specs=[
            pl.BlockSpec((rows, RET_QK_DIM), lambda i: (blk0 + i, _QK0)),
            pl.BlockSpec((rows, RET_QK_DIM), lambda i: (blk0 + i, _QK0 + 1)),
            pl.BlockSpec((rows, RET_V_DIM), lambda i: (blk0 + i, _V0)),
            pl.BlockSpec((rows, RET_V_DIM), lambda i: (blk0 + i, _V0 + 1)),
            st,
        ] + [_full3(t) for t in tabs],
        out_specs=[pl.BlockSpec((rows, RET_V_DIM), lambda i: (i, 0)), st],
        out_shape=[
            jax.ShapeDtypeStruct((Bs * Ls, RET_V_DIM), BF16),
            jax.ShapeDtypeStruct((Bs, RET_HEADS, RET_HEAD_QK, RET_HEAD_V), F32),
        ],
        scratch_shapes=[pltpu.VMEM((rows, RET_V_DIM), F32)],
        compiler_params=_params(("parallel",)),
        name="ret_sample",
    )(P, P, P, P, state, *tabs)


def _split_dot(a, w_hilo):
    a_hi = a.astype(BF16)
    a_lo = (a - a_hi.astype(F32)).astype(BF16)
    hi = _dot(a_hi, w_hilo)
    return hi[:, :LANES] + (hi[:, LANES:] + _dot(a_lo, w_hilo[:, :LANES]))


def _route(logits):
    lane = lax.broadcasted_iota(I32, logits.shape, 1)
    neg = jnp.float32(-jnp.inf)
    big = jnp.int32(LANES)
    gmask = lane < N_GROUPS
    gl = jnp.where(gmask, logits, neg)
    gmax = jnp.max(gl, axis=1, keepdims=True)
    gu = jnp.where(gmask, jnp.exp(gl - gmax), 0.0)
    gp = gu / jnp.sum(gu, axis=1, keepdims=True)
    g_w = jnp.max(gp, axis=1, keepdims=True)
    g_i = jnp.min(jnp.where(gmask & (gp == g_w), lane, big), axis=1, keepdims=True)
    lo = ROUTE_EXPERT_LANE + g_i * EXPERTS_PER_GROUP
    emask = (lane >= lo) & (lane < lo + EXPERTS_PER_GROUP)
    el = jnp.where(emask, logits, neg)
    e1 = jnp.max(el, axis=1, keepdims=True)
    i1 = jnp.min(jnp.where(emask & (el == e1), lane, big), axis=1, keepdims=True)
    el2 = jnp.where(lane == i1, neg, el)
    e2 = jnp.max(el2, axis=1, keepdims=True)
    i2 = jnp.min(jnp.where(emask & (lane != i1) & (el2 == e2), lane, big), axis=1, keepdims=True)
    u2 = jnp.exp(e2 - e1)
    den = 1.0 + u2
    w1 = (1.0 / den) * g_w
    w2 = (u2 / den) * g_w
    id1 = (i1 - ROUTE_EXPERT_LANE).astype(F32)
    id2 = (i2 - ROUTE_EXPERT_LANE).astype(F32)
    return (jnp.where(lane == LANE_E1, id1, 0.0) + jnp.where(lane == LANE_E2, id2, 0.0)
            + jnp.where(lane == LANE_W1, w1, 0.0) + jnp.where(lane == LANE_W2, w2, 0.0))


def _pick(i, bounds, refs):
    val = refs[-1][...]
    for n in range(len(refs) - 2, -1, -1):
        val = jnp.where(i < bounds[n + 1], refs[n][...], val)
    return val


def _mix_kernel(*refs, bounds):
    ns = len(bounds) - 1
    ohg_refs, or_refs, refs = refs[:ns], refs[ns:2 * ns], refs[2 * ns:]
    (ga0_ref, ga1_ref, gb0_ref, gb1_ref, xp_ref, xs_ref, wbh_ref, wbr_ref, wout_ref, nffn_ref,
     wr_ref, br_ref, h_ref, hn_ref, route_ref) = refs
    i = pl.program_id(0)
    tm = h_ref.shape[0]
    is_prompt = i < bounds[-2]
    p_hg = _dot(_pick(i, bounds, ohg_refs), wbh_ref[...])
    p_r = _dot(_pick(i, bounds, or_refs), wbr_ref[...])
    half = D_MODEL // 2
    m0 = ga0_ref[...] * p_hg[:, :half] + gb0_ref[...] * p_r[:, :half]
    m1 = ga1_ref[...] * p_hg[:, half:] + gb1_ref[...] * p_r[:, half:]
    mixed = jnp.concatenate([m0, m1], axis=1).astype(BF16)
    x = jnp.where(is_prompt, xp_ref[...], xs_ref[...])
    h = x + _dot(mixed, wout_ref[...])
    h_ref[...] = h
    hn = _rms(h) * nffn_ref[...]
    hn_ref[...] = _to_token_tiles(hn.astype(BF16))
    route_ref[...] = _route(_split_dot(hn, wr_ref[...]) + br_ref[...])


def _mix(o_hg, o_r, P, xp, xs, wbh, wbr, wout, norm_ffn, wr, br, tm):
    npt, nst = xp.shape[0] // tm, xs.shape[0] // tm
    bounds = [0]
    for a in o_hg:
        bounds.append(bounds[-1] + a.shape[0] // tm)
    T = (npt + nst) * tm
    half = D_MODEL // 2
    ga = (IN_WIDTH - 2 * D_MODEL) // half

    def pcol(c):
        return pl.BlockSpec((tm, half), lambda i, c=c: (i, c))

    def const(a):
        return pl.BlockSpec(a.shape, lambda i: (0, 0), pipeline_mode=pl.Buffered(1))

    def prompt(w):
        return pl.BlockSpec((tm, w), lambda i: (jnp.minimum(i, npt - 1), 0))

    def sample(w):
        return pl.BlockSpec((tm, w), lambda i: (jnp.maximum(i - npt, 0), 0))

    def source(n):
        lo, hi = bounds[n], bounds[n + 1]
        return pl.BlockSpec((tm, half), lambda i: (jnp.clip(i - lo, 0, hi - lo - 1), 0))

    sources = [source(n) for n in range(len(o_hg))]
    return pl.pallas_call(
        functools.partial(_mix_kernel, bounds=tuple(bounds)),
        grid=(npt + nst,),
        in_specs=sources + sources + [
            pcol(ga), pcol(ga + 1), pcol(ga + 2), pcol(ga + 3),
            prompt(D_MODEL), sample(D_MODEL),
            const(wbh), const(wbr), const(wout), const(norm_ffn), const(wr), const(br),
        ],
        out_specs=[
            pl.BlockSpec((tm, D_MODEL), lambda i: (i, 0)),
            pl.BlockSpec((tm * ROW_GROUP, LANES), lambda i: (i, 0)),
            pl.BlockSpec((tm, LANES), lambda i: (i, 0)),
        ],
        out_shape=[
            jax.ShapeDtypeStruct((T, D_MODEL), F32),
            jax.ShapeDtypeStruct((T * ROW_GROUP, LANES), BF16),
            jax.ShapeDtypeStruct((T, LANES), F32),
        ],
        compiler_params=_params(("parallel",)),
        name="mix",
    )(*o_hg, *o_r, P, P, P, P, xp, xs, wbh, wbr, wout, norm_ffn, wr, br)


def _plan_kernel(route_ref, pos_ref, te_ref, tail_ref, cum_scr, *, tb):
    T = route_ref.shape[0]
    lane_t = lax.broadcasted_iota(I32, (tb, LANES), 1).astype(F32)
    r = lax.broadcasted_iota(I32, (tb, tb), 0)
    c = lax.broadcasted_iota(I32, (tb, tb), 1)
    strict_lower = jnp.where(r > c, 1.0, 0.0).astype(BF16)

    def onehots(rows):
        rt = route_ref[rows, :]
        a1 = jnp.where(lane_t == rt[:, LANE_E1:LANE_E1 + 1], 1.0, 0.0)
        a2 = jnp.where(lane_t == rt[:, LANE_E2:LANE_E2 + 1], 1.0, 0.0)
        return a1, a2

    def rank_body(bi, carry):
        rows = pl.ds(pl.multiple_of(bi * tb, tb), tb)
        a1, a2 = onehots(rows)
        m = a1 + a2
        cum_scr[rows, :] = _dot(strict_lower, m.astype(BF16)) + carry
        return carry + jnp.sum(m, axis=0, keepdims=True)

    counts = lax.fori_loop(0, T // tb, rank_body, jnp.zeros((1, LANES), F32))
    ntiles = jnp.floor((counts + (MOE_TILE - 1)) * (1.0 / MOE_TILE))
    rr = lax.broadcasted_iota(I32, (LANES, LANES), 0)
    cc = lax.broadcasted_iota(I32, (LANES, LANES), 1)
    upper = jnp.where(rr < cc, 1.0, 0.0).astype(BF16)
    tile0 = _dot(jnp.broadcast_to(ntiles, (SUBLANES, LANES)).astype(BF16), upper)[0:1, :]
    slot0 = tile0 * MOE_TILE

    def pos_body(bi, carry):
        rows = pl.ds(pl.multiple_of(bi * tb, tb), tb)
        a1, a2 = onehots(rows)
        base = cum_scr[rows, :] + slot0
        p1 = jnp.sum(a1 * base, axis=1, keepdims=True)
        p2 = jnp.sum(a2 * base, axis=1, keepdims=True)
        pos_ref[rows, :] = (jnp.where(lane_t == 0.0, p1, 0.0) + jnp.where(lane_t == 1.0, p2, 0.0)).astype(I32)
        return carry

    lax.fori_loop(0, T // tb, pos_body, 0)
    tend = tile0 + ntiles
    lane = lax.broadcasted_iota(I32, (LANES, LANES), 1)
    tile_i = lax.broadcasted_iota(I32, (LANES, LANES), 0).astype(F32)
    is_expert = lane < N_EXPERTS
    te = jnp.sum(jnp.where(is_expert & (jnp.broadcast_to(tend, (LANES, LANES)) <= tile_i), 1.0, 0.0),
                 axis=1, keepdims=True)
    te_ref[...] = jnp.broadcast_to(jnp.minimum(te, N_EXPERTS - 1.0), (LANES, LANES)).astype(I32)
    lane1 = lax.broadcasted_iota(I32, (1, LANES), 1)
    total = jnp.sum(jnp.where(lane1 < N_EXPERTS, ntiles, 0.0), axis=1, keepdims=True)
    tail = jnp.where(ntiles > 0.0, (tend - 1.0) * MOE_TILE, -1.0)
    tail = jnp.where(lane1 < N_EXPERTS, tail, jnp.where(lane1 == N_EXPERTS, total, 0.0))
    tail_ref[...] = jnp.broadcast_to(tail, (SUBLANES, LANES)).astype(I32)


def _plan(route, tb):
    T = route.shape[0]
    return pl.pallas_call(
        functools.partial(_plan_kernel, tb=tb),
        grid=(1,),
        in_specs=[pl.BlockSpec((T, LANES), lambda i: (0, 0))],
        out_specs=[
            pl.BlockSpec((T, LANES), lambda i: (0, 0)),
            pl.BlockSpec((LANES, LANES), lambda i: (0, 0)),
            pl.BlockSpec((SUBLANES, LANES), lambda i: (0, 0)),
        ],
        out_shape=[
            jax.ShapeDtypeStruct((T, LANES), I32),
            jax.ShapeDtypeStruct((LANES, LANES), I32),
            jax.ShapeDtypeStruct((SUBLANES, LANES), I32),
        ],
        scratch_shapes=[pltpu.VMEM((T, LANES), F32)],
        compiler_params=_params(("arbitrary",)),
        name="moe_plan",
    )(route)


def _row_group(ref, idx):
    return ref.at[pl.ds(pl.multiple_of(idx * ROW_GROUP, ROW_GROUP), ROW_GROUP), :]


def _dispatch_kernel(pos_ref, tail_ref, hn_ref, xs_ref, zero_scr, zsem, sem):
    i = pl.program_id(0)
    tm = hn_ref.shape[0] // ROW_GROUP
    tile_rows = MOE_TILE * ROW_GROUP

    @pl.when(i == 0)
    def _():
        zero_scr[...] = jnp.zeros_like(zero_scr)
        n_used = tail_ref[N_EXPERTS]

        def zero_copy(first_slot):
            start = pl.multiple_of(first_slot * ROW_GROUP, tile_rows)
            return pltpu.make_async_copy(zero_scr, xs_ref.at[pl.ds(start, tile_rows), :], zsem)

        def tails(fn):
            def body(e, carry):
                pl.when(tail_ref[e] >= 0)(lambda: fn(zero_copy(jnp.maximum(tail_ref[e], 0))))
                return carry
            lax.fori_loop(0, N_EXPERTS, body, 0)

        def unused(fn):
            def body(t, carry):
                fn(zero_copy(t * MOE_TILE))
                return carry
            lax.fori_loop(n_used, xs_ref.shape[0] // tile_rows, body, 0)

        tails(lambda c: c.start())
        unused(lambda c: c.start())
        tails(lambda c: c.wait())
        unused(lambda c: c.wait())

    def copy(r, k):
        slot = pos_ref[(i * tm + r) * TOP_K + k]
        return pltpu.make_async_copy(_row_group(hn_ref, r), _row_group(xs_ref, slot), sem)

    def start(r, carry):
        for k in range(TOP_K):
            copy(r, k).start(priority=k)
        return carry

    lax.fori_loop(0, tm, start, 0, unroll=8)
    for _ in range(TOP_K):
        pltpu.make_async_copy(hn_ref, xs_ref.at[pl.ds(0, tm * ROW_GROUP), :], sem).wait()


def _dispatch(pos, tails, hn_rows, n_slots, tm):
    T = hn_rows.shape[0] // ROW_GROUP
    return pl.pallas_call(
        _dispatch_kernel,
        grid_spec=pltpu.PrefetchScalarGridSpec(
            num_scalar_prefetch=2,
            grid=(T // tm,),
            in_specs=[pl.BlockSpec((tm * ROW_GROUP, LANES), lambda i, *_: (i, 0))],
            out_specs=pl.BlockSpec(memory_space=pl.ANY),
            scratch_shapes=[pltpu.VMEM((MOE_TILE * ROW_GROUP, LANES), BF16),
                            pltpu.SemaphoreType.DMA, pltpu.SemaphoreType.DMA],
        ),
        out_shape=jax.ShapeDtypeStruct((n_slots * ROW_GROUP, LANES), BF16),
        compiler_params=_params(("arbitrary",)),
        name="moe_dispatch",
    )(pos, tails, hn_rows)


def _experts_kernel(te_ref, nt_ref, xs_ref, wg_hbm, wu_hbm, wd_hbm, ys_ref,
                    wg_buf, wu_buf, wd_buf, slot_ref, sem):
    i = pl.program_id(0)
    nt = nt_ref[0]
    e = te_ref[i]

    def fetch(expert, slot):
        return [pltpu.make_async_copy(hbm.at[expert], buf.at[slot], sem.at[slot, n])
                for n, (hbm, buf) in enumerate(((wg_hbm, wg_buf), (wu_hbm, wu_buf), (wd_hbm, wd_buf)))]

    @pl.when(i == 0)
    def _():
        slot_ref[1] = 0
        for c in fetch(e, 0):
            c.start(priority=1)

    @pl.when((i < nt) & ((i == 0) | (te_ref[jnp.maximum(i - 1, 0)] != e)))
    def _():
        slot = slot_ref[1]
        for c in fetch(e, slot):
            c.wait()
        nxt = lax.while_loop(lambda j: (j < nt) & (te_ref[jnp.minimum(j, nt - 1)] == e), lambda j: j + 1, i + 1)

        @pl.when(nxt < nt)
        def _():
            for c in fetch(te_ref[jnp.minimum(nxt, nt - 1)], 1 - slot):
                c.start(priority=1)

        slot_ref[0] = slot
        slot_ref[1] = 1 - slot

    @pl.when(i < nt)
    def _():
        slot = slot_ref[0]
        x = _from_token_tiles(xs_ref[...])[0]
        hmid = (_silu(_dot(x, wg_buf[slot].astype(BF16))) * _dot(x, wu_buf[slot].astype(BF16))).astype(BF16)
        ys_ref[...] = _to_token_tiles(_dot(hmid, wd_buf[slot].astype(BF16)))

    @pl.when(i >= nt)
    def _():
        ys_ref[...] = jnp.zeros_like(ys_ref)


def _experts(te, nt, xs_rows, w_ge, w_ue, w_de, n_tiles):
    M = MOE_TILE
    rows = pl.BlockSpec((M * ROW_GROUP, LANES), lambda i, te_ref, nt_ref: (jnp.minimum(i, nt_ref[0] - 1), 0))
    hbm = pl.BlockSpec(memory_space=pl.ANY)
    up, down = (D_MODEL, EXPERT_FF), (EXPERT_FF, D_MODEL)
    return pl.pallas_call(
        _experts_kernel,
        grid_spec=pltpu.PrefetchScalarGridSpec(
            num_scalar_prefetch=2,
            grid=(n_tiles,),
            in_specs=[rows, hbm, hbm, hbm],
            out_specs=pl.BlockSpec((M * ROW_GROUP, LANES), lambda i, te_ref, nt_ref: (i, 0)),
            scratch_shapes=[pltpu.VMEM((2,) + up, F32), pltpu.VMEM((2,) + up, F32), pltpu.VMEM((2,) + down, F32),
                            pltpu.SMEM((2,), I32), pltpu.SemaphoreType.DMA((2, 3))],
        ),
        out_shape=jax.ShapeDtypeStruct(xs_rows.shape, F32),
        compiler_params=_params(("arbitrary",)),
        name="moe_experts",
    )(te, nt, xs_rows, w_ge, w_ue, w_de)


def _final_kernel(pos_ref, h_ref, route_ref, gain_ref, ys_ref, y_ref, buf, sem, *, tile0):
    i = pl.program_id(0)
    n = pl.num_programs(0)
    tm = h_ref.shape[0]

    def copy(step, r, k, slot_buf):
        slot = pos_ref[((tile0 + step) * tm + r) * TOP_K + k]
        return pltpu.make_async_copy(_row_group(ys_ref, slot), _row_group(buf.at[slot_buf], r * TOP_K + k),
                                     sem.at[slot_buf])

    def start_all(step, slot_buf):
        def body(r, carry):
            for k in range(TOP_K):
                copy(step, r, k, slot_buf).start(priority=k)
            return carry
        lax.fori_loop(0, tm, body, 0, unroll=8)

    cur = i % 2
    pl.when(i == 0)(lambda: start_all(0, 0))
    pl.when(i + 1 < n)(lambda: start_all(i + 1, 1 - cur))
    pltpu.make_async_copy(ys_ref.at[pl.ds(0, buf.shape[1]), :], buf.at[cur], sem.at[cur]).wait()

    rt = route_ref[...]
    w1 = rt[:, LANE_W1:LANE_W1 + 1]
    w2 = rt[:, LANE_W2:LANE_W2 + 1]
    o1, o2 = _from_token_tiles(buf[cur], groups=TOP_K)
    v = h_ref[...] + (w1 * o1 + w2 * o2)
    y_ref[...] = _rms(v) * gain_ref[...]


def _final(pos, h, route, gain, ys_rows, tile0, n_tiles, tm):
    return pl.pallas_call(
        functools.partial(_final_kernel, tile0=tile0),
        grid_spec=pltpu.PrefetchScalarGridSpec(
            num_scalar_prefetch=1,
            grid=(n_tiles,),
            in_specs=[
                pl.BlockSpec((tm, D_MODEL), lambda i, *_: (tile0 + i, 0)),
                pl.BlockSpec((tm, LANES), lambda i, *_: (tile0 + i, 0)),
                pl.BlockSpec((1, D_MODEL), lambda i, *_: (0, 0)),
                pl.BlockSpec(memory_space=pl.ANY),
            ],
            out_specs=pl.BlockSpec((tm, D_MODEL), lambda i, *_: (i, 0)),
            scratch_shapes=[pltpu.VMEM((2, tm * TOP_K * ROW_GROUP, LANES), F32), pltpu.SemaphoreType.DMA((2,))],
        ),
        out_shape=jax.ShapeDtypeStruct((n_tiles * tm, D_MODEL), F32),
        compiler_params=_params(("arbitrary",)),
        name="final",
    )(pos, h, route, gain, ys_rows)


def _rope_tables(pos):
    half = RET_HEAD_QK // 2
    inv = 1.0 / (ROPE_BASE ** jnp.linspace(0.0, 1.0, half, dtype=F32))
    ang = jnp.repeat(pos[:, None] * inv[None, :], 2, axis=-1)
    cos, sin = jnp.cos(ang), jnp.sin(ang)
    even = (jnp.arange(RET_HEAD_QK) % 2 == 0)[None, :]
    sa = jnp.where(even, -sin, 0.0)
    sb = jnp.where(even, 0.0, sin)
    return cos, sa, sb


def _tile_sizes(Tp, Ts, Lp):
    g = math.gcd(Tp, Ts)
    return dict(
        xnorm=math.gcd(g, 512),
        proj=(Tp + Ts) // 4,
        mix=math.gcd(g, 256),
        plan=math.gcd(Tp + Ts, 512),
        dispatch=math.gcd(g, 512),
        final=math.gcd(g, 512),
        scan=math.gcd(Lp, 128),
    )


def kernel(x_prompt, x_sample, state_hgrn, state_ret, norm_mix, w_in, hg_lb_logits, hg_norm,
           w_branch_hg, w_branch_ret, b_gate, w_out, norm_ffn, w_router_group, b_router_group,
           w_router_expert, b_router_expert, w_expert_gate, w_expert_up, w_expert_down, norm_final):
    assert w_in.shape == (1, D_MODEL, IN_WIDTH), "single-layer trunk only"
    Bp, Lp, _ = x_prompt.shape
    Bs, Ls, _ = x_sample.shape
    Tp, Ts = Bp * Lp, Bs * Ls
    T = Tp + Ts
    ts = _tile_sizes(Tp, Ts, Lp)
    xp = x_prompt.reshape(Tp, D_MODEL)
    xs = x_sample.reshape(Ts, D_MODEL)

    lower = jnp.cumsum(jax.nn.softmax(hg_lb_logits.astype(F32), axis=0), axis=0)[0]
    act = _act_coefficients(lower, b_gate[0])
    tab_p = _rope_tables(jnp.arange(Lp, dtype=F32))
    tab_s = _rope_tables(jnp.arange(Ls, dtype=F32) + jnp.float32(PAST_LEN))
    tabs = tuple(jnp.concatenate([jnp.tile(a, (Bp, 1)), jnp.tile(b, (Bs, 1))]) for a, b in zip(tab_p, tab_s))
    xn = _xnorm(xp, xs, norm_mix, ts["xnorm"])
    P = _in_proj(xn, w_in[0], act, tabs, ts["proj"])

    bb = math.gcd(Bs, 16)
    o_hg_p, shp = _hgrn_prompt(P, hg_norm, Bp, Lp, ts["scan"])
    o_hg_s, shs = _hgrn_sample(P, hg_norm, state_hgrn[0].astype(F32), Bs, Ls, bb, Tp)
    o_r_p, srp = _ret_prompt(P, Bp, Lp, ts["scan"])
    o_r_s, srs = _ret_sample(P, state_ret[0].astype(F32), Bs, Ls, bb, Tp)

    wbh = w_branch_hg[0].astype(BF16)
    wbr = w_branch_ret[0].astype(BF16)
    wout = w_out[0].astype(BF16)
    pad = LANES - N_GROUPS - N_EXPERTS
    wr = jnp.concatenate([w_router_group[0], w_router_expert[0], jnp.zeros((D_MODEL, pad), F32)], axis=1)
    wr_hi = wr.astype(BF16)
    wr_hilo = jnp.concatenate([wr_hi, (wr - wr_hi.astype(F32)).astype(BF16)], axis=1)
    br = jnp.concatenate([b_router_group[0], b_router_expert[0], jnp.zeros((pad,), F32)])[None, :]
    h, hn_rows, route = _mix(list(o_hg_p) + [o_hg_s], list(o_r_p) + [o_r_s], P, xp, xs, wbh, wbr, wout,
                             norm_ffn, wr_hilo, br, ts["mix"])

    n_tiles = -(-T * TOP_K // MOE_TILE) + N_EXPERTS
    assert n_tiles <= LANES
    pos_slab, te_slab, tail_slab = _plan(route, ts["plan"])
    pos = pos_slab[:, :TOP_K].reshape(-1)
    te = te_slab[:n_tiles, 0]
    tails = tail_slab[0, :N_EXPERTS + 1]
    nt = tail_slab[0, N_EXPERTS:N_EXPERTS + 1]
    xs_rows = _dispatch(pos, tails, hn_rows, n_tiles * MOE_TILE, ts["dispatch"])
    ys_rows = _experts(te, nt, xs_rows, w_expert_gate[0], w_expert_up[0], w_expert_down[0], n_tiles)

    tf = ts["final"]
    gain = norm_final[None, :]
    y_p = _final(pos, h, route, gain, ys_rows, 0, Tp // tf, tf)
    y_s = _final(pos, h, route, gain, ys_rows, Tp // tf, Ts // tf, tf)
    return (y_p.reshape(Bp, Lp, D_MODEL), y_s.reshape(Bs, Ls, D_MODEL),
            shp[None], srp[None], shs[None], srs[None])
```

```python
import functools
import math

import jax
import jax.numpy as jnp
from jax import lax
from jax.experimental import pallas as pl
from jax.experimental.pallas import tpu as pltpu

F32 = jnp.float32
BF16 = jnp.bfloat16
I32 = jnp.int32

D_MODEL = 2048
PAST_LEN = 16384
HG_HEAD_V = 128
HG_EXPAND = 128
HG_VAL_DIM = D_MODEL // 2
HG_HEADS = HG_VAL_DIM // HG_HEAD_V
HG_KEY_DIM = HG_HEADS * HG_EXPAND
RET_HEAD_V = 256
RET_HEAD_QK = 128
RET_V_DIM = D_MODEL // 2
RET_HEADS = RET_V_DIM // RET_HEAD_V
RET_QK_DIM = RET_HEADS * RET_HEAD_QK
IN_WIDTH = 2 * HG_KEY_DIM + 2 * HG_VAL_DIM + 2 * RET_QK_DIM + 2 * RET_V_DIM + 2 * D_MODEL
CHUNK = 64
SCAN_CHUNK = 128
N_GROUPS = 4
EXPERTS_PER_GROUP = 8
N_EXPERTS = N_GROUPS * EXPERTS_PER_GROUP
TOP_K = 2
EXPERT_FF = D_MODEL // 4
ROPE_BASE = 10000.0
EPS = 1e-6

LANES = 128
SUBLANES = 8
VMEM_LIMIT = 56 * 1024 * 1024
PROJ_MC = 256
PROJ_TN = 512
SEG_HQ, SEG_HF, SEG_HI, SEG_HG, SEG_RQ, SEG_RK, SEG_RV, SEG_RG, SEG_GA = 0, 2, 4, 6, 8, 9, 10, 12, 14
SAFE_EXP_SPAN = 80.0
ROW_GROUP = D_MODEL // LANES
MOE_TILE = 256
LANE_E1, LANE_E2, LANE_W1, LANE_W2 = 0, 1, 2, 3
ROUTE_EXPERT_LANE = N_GROUPS


def _params(sem):
    return pltpu.CompilerParams(dimension_semantics=sem, vmem_limit_bytes=VMEM_LIMIT)


def _silu(z):
    return z * jax.nn.sigmoid(z)


def _dot(a, b):
    return jnp.dot(a, b, preferred_element_type=F32)


def _dot_nt(a, b):
    return lax.dot_general(a, b, (((1,), (1,)), ((), ())), preferred_element_type=F32)


def _dot_tn(a, b):
    return lax.dot_general(a, b, (((0,), (0,)), ((), ())), preferred_element_type=F32)


def _rms(x):
    return x * lax.rsqrt(jnp.mean(x * x, axis=-1, keepdims=True) + EPS)


def _to_token_tiles(x):
    M = x.shape[0]
    pieces = jnp.stack([x[:, s * LANES:(s + 1) * LANES] for s in range(ROW_GROUP)], axis=0)
    return jnp.swapaxes(pieces, 0, 1).reshape(M * ROW_GROUP, LANES)


def _from_token_tiles(rows, groups=1):
    M = rows.shape[0] // (groups * ROW_GROUP)
    t = jnp.swapaxes(rows.reshape(M, groups * ROW_GROUP, LANES), 0, 1)
    return [jnp.concatenate([t[g * ROW_GROUP + s] for s in range(ROW_GROUP)], axis=1) for g in range(groups)]


def _xnorm_kernel(xp_ref, xs_ref, gain_ref, o_ref, *, n_prompt_tiles):
    i = pl.program_id(0)

    def norm(x_ref):
        o_ref[...] = (_rms(x_ref[...]) * gain_ref[...]).astype(BF16)

    pl.when(i < n_prompt_tiles)(lambda: norm(xp_ref))
    pl.when(i >= n_prompt_tiles)(lambda: norm(xs_ref))


def _xnorm(xp, xs, gain, tm):
    npt, nst = xp.shape[0] // tm, xs.shape[0] // tm
    return pl.pallas_call(
        functools.partial(_xnorm_kernel, n_prompt_tiles=npt),
        grid=(npt + nst,),
        in_specs=[
            pl.BlockSpec((tm, D_MODEL), lambda i: (jnp.minimum(i, npt - 1), 0)),
            pl.BlockSpec((tm, D_MODEL), lambda i: (jnp.maximum(i - npt, 0), 0)),
            pl.BlockSpec((1, D_MODEL), lambda i: (0, 0)),
        ],
        out_specs=pl.BlockSpec((tm, D_MODEL), lambda i: (i, 0)),
        out_shape=jax.ShapeDtypeStruct(((npt + nst) * tm, D_MODEL), BF16),
        compiler_params=_params(("parallel",)),
        name="xnorm",
    )(xp, xs, gain)


ACT_ALPHA, ACT_BETA, ACT_GAMMA, ACT_DELTA, ACT_EPS = 0, 1, 2, 3, 4


def _act_coefficients(lower, b_gate):
    z = jnp.zeros((IN_WIDTH,), F32)
    seg = lambda a, b: slice(a * PROJ_TN, b * PROJ_TN)
    alpha = z.at[seg(SEG_HI, SEG_HG)].set(1.0).at[seg(SEG_RQ, SEG_RG)].set(1.0)
    beta = z.at[seg(SEG_GA, IN_WIDTH // PROJ_TN)].set(b_gate)
    gamma = (z.at[seg(SEG_HQ, SEG_HF)].set(HG_EXPAND ** -0.5)
             .at[seg(SEG_HG, SEG_RQ)].set(1.0).at[seg(SEG_RG, SEG_GA)].set(1.0))
    delta = z.at[seg(SEG_HF, SEG_HI)].set(1.0 - lower).at[seg(SEG_GA, IN_WIDTH // PROJ_TN)].set(1.0)
    eps = z.at[seg(SEG_HF, SEG_HI)].set(lower)
    return jnp.stack([alpha, beta, gamma, delta, eps, z, z, z])


def _in_proj_kernel(x_ref, w_ref, act_ref, cos_ref, sa_ref, sb_ref, o_ref):
    j = pl.program_id(1)
    w = w_ref[...].astype(BF16)
    row = lambda r: act_ref[r:r + 1, :]
    tm = x_ref.shape[0]
    mc = math.gcd(tm, PROJ_MC)
    for m in range(0, tm, mc):
        p = _dot(x_ref[m:m + mc, :], w)
        o_ref[m:m + mc, :] = (row(ACT_ALPHA) * p + row(ACT_EPS)
                              + jax.nn.sigmoid(p + row(ACT_BETA)) * (row(ACT_GAMMA) * p + row(ACT_DELTA)))

    @pl.when((j >= SEG_RQ) & (j < SEG_RV))
    def _():
        scale = jnp.where(j == SEG_RK, RET_HEAD_QK ** -0.5, 1.0).astype(F32)
        cos, sa, sb = cos_ref[...], sa_ref[...], sb_ref[...]
        for hh in range(PROJ_TN // LANES):
            cols = slice(hh * LANES, (hh + 1) * LANES)
            xs = o_ref[:, cols]
            r = xs * cos + pltpu.roll(xs, LANES - 1, 1) * sa + pltpu.roll(xs, 1, 1) * sb
            o_ref[:, cols] = r * scale


def _in_proj(xn, w_in, act, tabs, tm):
    T = xn.shape[0]
    tab = pl.BlockSpec((tm, LANES), lambda i, j: (i, 0))
    return pl.pallas_call(
        _in_proj_kernel,
        grid=(T // tm, IN_WIDTH // PROJ_TN),
        in_specs=[
            pl.BlockSpec((tm, D_MODEL), lambda i, j: (i, 0)),
            pl.BlockSpec((D_MODEL, PROJ_TN), lambda i, j: (0, j)),
            pl.BlockSpec((SUBLANES, PROJ_TN), lambda i, j: (0, j)),
            tab, tab, tab,
        ],
        out_specs=pl.BlockSpec((tm, PROJ_TN), lambda i, j: (i, j)),
        out_shape=jax.ShapeDtypeStruct((T, IN_WIDTH), F32),
        compiler_params=_params(("parallel", "arbitrary")),
        name="in_proj",
    )(xn, w_in, act, *tabs)


def _prefix_matrix(C, G):
    r = lax.broadcasted_iota(I32, (C, 3 * C), 0)
    c = lax.broadcasted_iota(I32, (C, 3 * C), 1) % C
    return jnp.where((c <= r) & (c // G == r // G), 1.0, 0.0).astype(BF16)


def _prefix_sum(g, pm):
    g1 = g.astype(BF16)
    r1 = g - g1.astype(F32)
    g2 = r1.astype(BF16)
    g3 = (r1 - g2.astype(F32)).astype(BF16)
    return _dot(pm, jnp.concatenate([g1, g2, g3], axis=0))


def _state_decay_column(d_row):
    r = lax.broadcasted_iota(I32, (LANES, LANES), 0)
    c = lax.broadcasted_iota(I32, (LANES, LANES), 1)
    return jnp.sum(jnp.where(r == c, jnp.broadcast_to(d_row, (LANES, LANES)), 0.0), axis=1, keepdims=True)


def _hgrn_chunk(q, f, g, v, states, pm, tmp_ref, NG, factorised):
    C, W = q.shape
    G, H = C // NG, W // LANES
    hs = [slice(h * LANES, (h + 1) * LANES) for h in range(H)]
    gs = [slice(n * G, (n + 1) * G) for n in range(NG)]
    k = 1.0 - f
    b = _prefix_sum(g, pm)
    b3 = b.reshape(NG, G, W)
    b_end = b3[:, G - 1:G, :]
    q3, k3 = q.reshape(NG, G, W), k.reshape(NG, G, W)
    vb = v.astype(BF16)
    if factorised:
        b_mid = b3[:, G // 2 - 1:G // 2, :]
        qm3 = q3 * jnp.exp(b3 - b_mid)
        km3 = k3 * jnp.exp(b_mid - b3)
        qb = (qm3 * jnp.exp(b_mid)).reshape(C, W)
        kd = (km3 * jnp.exp(b_end - b_mid)).reshape(C, W)
        qm = qm3.reshape(C, W).astype(BF16)
        km = km3.reshape(C, W).astype(BF16)
        row = lax.broadcasted_iota(I32, (C, C), 0)
        col = lax.broadcasted_iota(I32, (C, C), 1)
        amask = (row >= col) & (row // G == col // G)
        o_intra = []
        for h in range(H):
            att = jnp.where(amask, _dot_nt(qm[:, hs[h]], km[:, hs[h]]), 0.0)
            o_intra.append(_dot(att.astype(BF16), vb[:, hs[h]]))
    else:
        qb = (q3 * jnp.exp(b3)).reshape(C, W)
        kd = (k3 * jnp.exp(b_end - b3)).reshape(C, W)
        trow = lax.broadcasted_iota(I32, (G, LANES), 0)
        o_intra = []
        for h in range(H):
            parts = []
            for n in range(NG):
                bh, qh = b[gs[n], hs[h]], q[gs[n], hs[h]]
                tmp_ref[0] = bh
                tmp_ref[1] = k[gs[n], hs[h]]
                tmp_ref[2] = v[gs[n], hs[h]]

                def body(s, acc, bh=bh, qh=qh):
                    bs = tmp_ref[0, pl.ds(s, 1), :]
                    ks = tmp_ref[1, pl.ds(s, 1), :]
                    vs = tmp_ref[2, pl.ds(s, 1), :]
                    w = jnp.where(trow >= s, jnp.exp(jnp.minimum(bh - bs, 0.0)), 0.0)
                    return acc + jnp.sum(qh * ks * w, axis=1, keepdims=True) * vs

                parts.append(lax.fori_loop(0, G, body, jnp.zeros((G, LANES), F32)))
            o_intra.append(parts[0] if NG == 1 else jnp.concatenate(parts, axis=0))
    d = jnp.exp(b_end)
    outs, new_states = [], [[None] * H for _ in range(NG)]
    for h in range(H):
        inter = []
        for n in range(NG):
            S = states[n][h]
            inter.append(_dot(qb[gs[n], hs[h]].astype(BF16), S.astype(BF16)))
            upd = _dot_tn(kd[gs[n], hs[h]].astype(BF16), v[gs[n], hs[h]].astype(BF16))
            new_states[n][h] = _state_decay_column(d[n, :, hs[h]]) * S + upd
        inter = inter[0] if NG == 1 else jnp.concatenate(inter, axis=0)
        outs.append(inter + o_intra[h])
    return jnp.concatenate(outs, axis=1), new_states


def _decay_is_safe(g, G):
    R, W = g.shape
    return jnp.min(jnp.sum(g.reshape(R // G, G, W), axis=1)) >= -SAFE_EXP_SPAN


def _hgrn_epilogue(o_scr, gain_ref, gate_ref, o_ref):
    o_ref[...] = (_rms(o_scr[...]) * gain_ref[...] * gate_ref[...]).astype(o_ref.dtype)


def _hgrn_prompt_kernel(*refs, C, NB):
    ins, (gain_ref,), rest = refs[:4 * NB], refs[4 * NB:4 * NB + 1], refs[4 * NB + 1:]
    o_refs, (st_ref, s_scr, g_scr, o_scr, tmp_ref) = rest[:NB], rest[NB:]
    seq = [ins[4 * n:4 * n + 4] for n in range(NB)]
    l = pl.program_id(0)
    Lb = o_refs[0].shape[0]

    @pl.when(l == 0)
    def _():
        s_scr[...] = jnp.zeros_like(s_scr)

    for n in range(NB):
        g_scr[n] = jnp.log(seq[n][1][...])
    pm = _prefix_matrix(C, C)

    def run(factorised):
        def body(ci, carry):
            rows = pl.ds(pl.multiple_of(ci * C, C), C)
            args = [(seq[n][0][rows, :], seq[n][1][rows, :], g_scr[n, rows, :], seq[n][2][rows, :],
                     [[s_scr[n, h] for h in range(HG_HEADS)]]) for n in range(NB)]
            outs = [_hgrn_chunk(*a, pm, tmp_ref, 1, factorised) for a in args]
            for n, (o, new) in enumerate(outs):
                o_scr[n, rows, :] = o
                for h in range(HG_HEADS):
                    s_scr[n, h] = new[0][h]
            return carry
        lax.fori_loop(0, Lb // C, body, 0)

    safe = _decay_is_safe(g_scr[...].reshape(NB * Lb, -1), C // 2)
    pl.when(safe)(lambda: run(True))
    pl.when(jnp.logical_not(safe))(lambda: run(False))
    for n in range(NB):
        o_refs[n][...] = (_rms(o_scr[n]) * gain_ref[...] * seq[n][3][...]).astype(o_refs[n].dtype)

    @pl.when(l == pl.num_programs(0) - 1)
    def _():
        st_ref[...] = s_scr[...]


def _hgrn_prompt(P, gain, B, L, Lb):
    C = math.gcd(L, SCAN_CHUNK)
    nl = L // Lb
    W = HG_KEY_DIM
    blk = lambda b, seg: pl.BlockSpec((Lb, W), lambda l, b=b, seg=seg: (b * nl + l, seg))
    outs = pl.pallas_call(
        functools.partial(_hgrn_prompt_kernel, C=C, NB=B),
        grid=(nl,),
        in_specs=[blk(b, seg) for b in range(B) for seg in range(4)] + [pl.BlockSpec((1, W), lambda l: (0, 0))],
        out_specs=[pl.BlockSpec((Lb, W), lambda l: (l, 0)) for _ in range(B)]
        + [pl.BlockSpec((B, HG_HEADS, HG_EXPAND, HG_HEAD_V), lambda l: (0, 0, 0, 0))],
        out_shape=[jax.ShapeDtypeStruct((L, HG_VAL_DIM), BF16) for _ in range(B)]
        + [jax.ShapeDtypeStruct((B, HG_HEADS, HG_EXPAND, HG_HEAD_V), F32)],
        scratch_shapes=[pltpu.VMEM((B, HG_HEADS, HG_EXPAND, HG_HEAD_V), F32), pltpu.VMEM((B, Lb, W), F32),
                        pltpu.VMEM((B, Lb, W), F32), pltpu.VMEM((3, C, LANES), F32)],
        compiler_params=_params(("arbitrary",)),
        name="hgrn_prompt",
    )(*([P] * (4 * B)), gain)
    return outs[:B], outs[B]


def _hgrn_sample_kernel(q_ref, f_ref, v_ref, gate_ref, gain_ref, s_ref, o_ref, so_ref, o_scr, tmp_ref, *, G):
    NG = s_ref.shape[0]
    g = jnp.log(f_ref[...])
    pm = _prefix_matrix(NG * G, G)

    def run(factorised):
        states = [[s_ref[n, h] for h in range(HG_HEADS)] for n in range(NG)]
        o, new = _hgrn_chunk(q_ref[...], f_ref[...], g, v_ref[...], states, pm, tmp_ref, NG, factorised)
        o_scr[...] = o
        for n in range(NG):
            for h in range(HG_HEADS):
                so_ref[n, h] = new[n][h]

    safe = _decay_is_safe(g, G // 2)
    pl.when(safe)(lambda: run(True))
    pl.when(jnp.logical_not(safe))(lambda: run(False))
    _hgrn_epilogue(o_scr, gain_ref, gate_ref, o_ref)


def _hgrn_sample(P, gain, state, Bs, Ls, Bb, row0):
    W = HG_KEY_DIM
    rows = Bb * Ls
    blk0 = row0 // rows

    def col(seg):
        return pl.BlockSpec((rows, W), lambda i, seg=seg: (blk0 + i, seg))

    st = pl.BlockSpec((Bb, HG_HEADS, HG_EXPAND, HG_HEAD_V), lambda i: (i, 0, 0, 0))
    return pl.pallas_call(
        functools.partial(_hgrn_sample_kernel, G=Ls),
        grid=(Bs // Bb,),
        in_specs=[col(0), col(1), col(2), col(3), pl.BlockSpec((1, W), lambda i: (0, 0)), st],
        out_specs=[pl.BlockSpec((rows, W), lambda i: (i, 0)), st],
        out_shape=[
            jax.ShapeDtypeStruct((Bs * Ls, HG_VAL_DIM), BF16),
            jax.ShapeDtypeStruct((Bs, HG_HEADS, HG_EXPAND, HG_HEAD_V), F32),
        ],
        scratch_shapes=[pltpu.VMEM((rows, W), F32), pltpu.VMEM((3, Ls, LANES), F32)],
        compiler_params=_params(("parallel",)),
        name="hgrn_sample",
    )(P, P, P, P, gain, state)


def _ret_chunk(q, k, v, states, dm_ref, qw_ref, kw_ref, sd_ref):
    outs, new_states = [], []
    C = q.shape[0]
    qb, kb, vb = q.astype(BF16), k.astype(BF16), v.astype(BF16)
    for h in range(RET_HEADS):
        qs = slice(h * RET_HEAD_QK, (h + 1) * RET_HEAD_QK)
        vs = slice(h * RET_HEAD_V, (h + 1) * RET_HEAD_V)
        att = (_dot_nt(qb[:, qs], kb[:, qs]) * dm_ref[h]).astype(BF16)
        kw = (k[:, qs] * kw_ref[h]).astype(BF16)
        if C <= 2 * SUBLANES:
            both = _dot(jnp.concatenate([att, kw.T], axis=0), vb[:, vs])
            intra, upd = both[:C], both[C:]
        else:
            intra, upd = _dot(att, vb[:, vs]), _dot_tn(kw, vb[:, vs])
        o = intra + _dot((q[:, qs] * qw_ref[h]).astype(BF16), states[h].astype(BF16))
        new_states.append(sd_ref[h, 0:1, 0:1] * states[h] + upd)
        outs.append(o)
    return jnp.concatenate(outs, axis=1), new_states


def _ret_epilogue(o_scr, gate_ref, o_ref):
    for h in range(RET_HEADS):
        vs = slice(h * RET_HEAD_V, (h + 1) * RET_HEAD_V)
        o_ref[:, vs] = (_rms(o_scr[:, vs]) * gate_ref[:, vs]).astype(o_ref.dtype)


def _ret_prompt_kernel(*refs, NB):
    ins, (dm_ref, qw_ref, kw_ref, sd_ref), rest = refs[:4 * NB], refs[4 * NB:4 * NB + 4], refs[4 * NB + 4:]
    o_refs, (st_ref, s_scr, o_scr) = rest[:NB], rest[NB:]
    seq = [ins[4 * n:4 * n + 4] for n in range(NB)]
    l = pl.program_id(0)
    Lb = o_refs[0].shape[0]
    C = dm_ref.shape[1]

    @pl.when(l == 0)
    def _():
        s_scr[...] = jnp.zeros_like(s_scr)

    def body(ci, carry):
        rows = pl.ds(pl.multiple_of(ci * C, C), C)
        args = [(seq[n][0][rows, :], seq[n][1][rows, :], seq[n][2][rows, :],
                 [s_scr[n, h] for h in range(RET_HEADS)]) for n in range(NB)]
        outs = [_ret_chunk(*a, dm_ref, qw_ref, kw_ref, sd_ref) for a in args]
        for n, (o, new) in enumerate(outs):
            o_scr[n, rows, :] = o
            for h in range(RET_HEADS):
                s_scr[n, h] = new[h]
        return carry

    lax.fori_loop(0, Lb // C, body, 0)
    for n in range(NB):
        _ret_epilogue(o_scr.at[n], seq[n][3], o_refs[n])

    @pl.when(l == pl.num_programs(0) - 1)
    def _():
        st_ref[...] = s_scr[...]


def _ret_tables(C):
    log_gamma = jnp.log(1.0 - jnp.exp2(-5.0 - jnp.arange(RET_HEADS, dtype=F32)))
    idx = jnp.arange(C, dtype=F32)
    rel = idx[:, None] - idx[None, :]
    tri = jnp.tril(jnp.ones((C, C), dtype=bool))
    dmat = jnp.exp(jnp.where(tri[None], log_gamma[:, None, None] * rel[None], -jnp.inf))
    qw = jnp.exp(log_gamma[:, None] * (idx[None, :] + 1.0))[..., None]
    kw = jnp.exp(log_gamma[:, None] * (C - 1.0 - idx[None, :]))[..., None]
    sdec = jnp.exp(log_gamma * C)[:, None, None]
    bc = lambda a: jnp.broadcast_to(a, (RET_HEADS, a.shape[1], LANES))
    return dmat, bc(qw), bc(kw), jnp.broadcast_to(sdec, (RET_HEADS, SUBLANES, LANES))


_QK0 = (2 * HG_KEY_DIM + 2 * HG_VAL_DIM) // RET_QK_DIM
_V0 = (2 * HG_KEY_DIM + 2 * HG_VAL_DIM + 2 * RET_QK_DIM) // RET_V_DIM


def _full3(a):
    return pl.BlockSpec(a.shape, lambda *_: (0, 0, 0))


def _ret_prompt(P, B, L, Lb):
    C = math.gcd(L, 2 * SCAN_CHUNK)
    Lb = max(Lb, C)
    nl = L // Lb
    tabs = _ret_tables(C)
    st_shape = (B, RET_HEADS, RET_HEAD_QK, RET_HEAD_V)
    blk = lambda b, w, c: pl.BlockSpec((Lb, w), lambda l, b=b, c=c: (b * nl + l, c))
    per_seq = lambda b: [blk(b, RET_QK_DIM, _QK0), blk(b, RET_QK_DIM, _QK0 + 1),
                         blk(b, RET_V_DIM, _V0), blk(b, RET_V_DIM, _V0 + 1)]
    outs = pl.pallas_call(
        functools.partial(_ret_prompt_kernel, NB=B),
        grid=(nl,),
        in_specs=[spec for b in range(B) for spec in per_seq(b)] + [_full3(t) for t in tabs],
        out_specs=[pl.BlockSpec((Lb, RET_V_DIM), lambda l: (l, 0)) for _ in range(B)]
        + [pl.BlockSpec(st_shape, lambda l: (0, 0, 0, 0))],
        out_shape=[jax.ShapeDtypeStruct((L, RET_V_DIM), BF16) for _ in range(B)]
        + [jax.ShapeDtypeStruct(st_shape, F32)],
        scratch_shapes=[pltpu.VMEM(st_shape, F32), pltpu.VMEM((B, Lb, RET_V_DIM), F32)],
        compiler_params=_params(("arbitrary",)),
        name="ret_prompt",
    )(*([P] * (4 * B)), *tabs)
    return outs[:B], outs[B]


def _ret_sample_kernel(q_ref, k_ref, v_ref, gate_ref, s_ref, dm_ref, qw_ref, kw_ref, sd_ref,
                       o_ref, so_ref, o_scr, *, C):
    Bb = s_ref.shape[0]
    args = [(q_ref[bb * C:(bb + 1) * C, :], k_ref[bb * C:(bb + 1) * C, :], v_ref[bb * C:(bb + 1) * C, :],
             [s_ref[bb, h] for h in range(RET_HEADS)]) for bb in range(Bb)]
    outs = [_ret_chunk(*a, dm_ref, qw_ref, kw_ref, sd_ref) for a in args]
    for bb, (o, new) in enumerate(outs):
        o_scr[bb * C:(bb + 1) * C, :] = o
        for h in range(RET_HEADS):
            so_ref[bb, h] = new[h]
    _ret_epilogue(o_scr, gate_ref, o_ref)


def _ret_sample(P, state, Bs, Ls, Bb, row0):
    tabs = _ret_tables(Ls)
    rows = Bb * Ls
    blk0 = row0 // rows
    st = pl.BlockSpec((Bb, RET_HEADS, RET_HEAD_QK, RET_HEAD_V), lambda i: (i, 0, 0, 0))
    return pl.pallas_call(
        functools.partial(_ret_sample_kernel, C=Ls),
        grid=(Bs // Bb,),
        in_specs=[
            pl.BlockSpec((rows, RET_QK_DIM), lambda i: (blk0 + i, _QK0)),
            pl.BlockSpec((rows, RET_QK_DIM), lambda i: (blk0 + i, _QK0 + 1)),
            pl.BlockSpec((rows, RET_V_DIM), lambda i: (blk0 + i, _V0)),
            pl.BlockSpec((rows, RET_V_DIM), lambda i: (blk0 + i, _V0 + 1)),
            st,
        ] + [_full3(t) for t in tabs],
        out_specs=[pl.BlockSpec((rows, RET_V_DIM), lambda i: (i, 0)), st],
        out_shape=[
            jax.ShapeDtypeStruct((Bs * Ls, RET_V_DIM), BF16),
            jax.ShapeDtypeStruct((Bs, RET_HEADS, RET_HEAD_QK, RET_HEAD_V), F32),
        ],
        scratch_shapes=[pltpu.VMEM((rows, RET_V_DIM), F32)],
        compiler_params=_params(("parallel",)),
        name="ret_sample",
    )(P, P, P, P, state, *tabs)


def _split_dot(a, w_hilo):
    a_hi = a.astype(BF16)
    a_lo = (a - a_hi.astype(F32)).astype(BF16)
    hi = _dot(a_hi, w_hilo)
    return hi[:, :LANES] + (hi[:, LANES:] + _dot(a_lo, w_hilo[:, :LANES]))


def _route(logits):
    lane = lax.broadcasted_iota(I32, logits.shape, 1)
    neg = jnp.float32(-jnp.inf)
    big = jnp.int32(LANES)
    gmask = lane < N_GROUPS
    gl = jnp.where(gmask, logits, neg)
    gmax = jnp.max(gl, axis=1, keepdims=True)
    gu = jnp.where(gmask, jnp.exp(gl - gmax), 0.0)
    gp = gu / jnp.sum(gu, axis=1, keepdims=True)
    g_w = jnp.max(gp, axis=1, keepdims=True)
    g_i = jnp.min(jnp.where(gmask & (gp == g_w), lane, big), axis=1, keepdims=True)
    lo = ROUTE_EXPERT_LANE + g_i * EXPERTS_PER_GROUP
    emask = (lane >= lo) & (lane < lo + EXPERTS_PER_GROUP)
    el = jnp.where(emask, logits, neg)
    e1 = jnp.max(el, axis=1, keepdims=True)
    i1 = jnp.min(jnp.where(emask & (el == e1), lane, big), axis=1, keepdims=True)
    el2 = jnp.where(lane == i1, neg, el)
    e2 = jnp.max(el2, axis=1, keepdims=True)
    i2 = jnp.min(jnp.where(emask & (lane != i1) & (el2 == e2), lane, big), axis=1, keepdims=True)
    u2 = jnp.exp(e2 - e1)
    den = 1.0 + u2
    w1 = (1.0 / den) * g_w
    w2 = (u2 / den) * g_w
    id1 = (i1 - ROUTE_EXPERT_LANE).astype(F32)
    id2 = (i2 - ROUTE_EXPERT_LANE).astype(F32)
    return (jnp.where(lane == LANE_E1, id1, 0.0) + jnp.where(lane == LANE_E2, id2, 0.0)
            + jnp.where(lane == LANE_W1, w1, 0.0) + jnp.where(lane == LANE_W2, w2, 0.0))


def _pick(i, bounds, refs):
    val = refs[-1][...]
    for n in range(len(refs) - 2, -1, -1):
        val = jnp.where(i < bounds[n + 1], refs[n][...], val)
    return val


def _mix_kernel(*refs, bounds):
    ns = len(bounds) - 1
    ohg_refs, or_refs, refs = refs[:ns], refs[ns:2 * ns], refs[2 * ns:]
    (ga0_ref, ga1_ref, gb0_ref, gb1_ref, xp_ref, xs_ref, wbh_ref, wbr_ref, wout_ref, nffn_ref,
     wr_ref, br_ref, h_ref, hn_ref, route_ref) = refs
    i = pl.program_id(0)
    tm = h_ref.shape[0]
    is_prompt = i < bounds[-2]
    p_hg = _dot(_pick(i, bounds, ohg_refs), wbh_ref[...])
    p_r = _dot(_pick(i, bounds, or_refs), wbr_ref[...])
    half = D_MODEL // 2
    m0 = ga0_ref[...] * p_hg[:, :half] + gb0_ref[...] * p_r[:, :half]
    m1 = ga1_ref[...] * p_hg[:, half:] + gb1_ref[...] * p_r[:, half:]
    mixed = jnp.concatenate([m0, m1], axis=1).astype(BF16)
    x = jnp.where(is_prompt, xp_ref[...], xs_ref[...])
    h = x + _dot(mixed, wout_ref[...])
    h_ref[...] = h
    hn = _rms(h) * nffn_ref[...]
    hn_ref[...] = _to_token_tiles(hn.astype(BF16))
    route_ref[...] = _route(_split_dot(hn, wr_ref[...]) + br_ref[...])


def _mix(o_hg, o_r, P, xp, xs, wbh, wbr, wout, norm_ffn, wr, br, tm):
    npt, nst = xp.shape[0] // tm, xs.shape[0] // tm
    bounds = [0]
    for a in o_hg:
        bounds.append(bounds[-1] + a.shape[0] // tm)
    T = (npt + nst) * tm
    half = D_MODEL // 2
    ga = (IN_WIDTH - 2 * D_MODEL) // half

    def pcol(c):
        return pl.BlockSpec((tm, half), lambda i, c=c: (i, c))

    def const(a):
        return pl.BlockSpec(a.shape, lambda i: (0, 0), pipeline_mode=pl.Buffered(1))

    def prompt(w):
        return pl.BlockSpec((tm, w), lambda i: (jnp.minimum(i, npt - 1), 0))

    def sample(w):
        return pl.BlockSpec((tm, w), lambda i: (jnp.maximum(i - npt, 0), 0))

    def source(n):
        lo, hi = bounds[n], bounds[n + 1]
        return pl.BlockSpec((tm, half), lambda i: (jnp.clip(i - lo, 0, hi - lo - 1), 0))

    sources = [source(n) for n in range(len(o_hg))]
    return pl.pallas_call(
        functools.partial(_mix_kernel, bounds=tuple(bounds)),
        grid=(npt + nst,),
        in_specs=sources + sources + [
            pcol(ga), pcol(ga + 1), pcol(ga + 2), pcol(ga + 3),
            prompt(D_MODEL), sample(D_MODEL),
            const(wbh), const(wbr), const(wout), const(norm_ffn), const(wr), const(br),
        ],
        out_specs=[
            pl.BlockSpec((tm, D_MODEL), lambda i: (i, 0)),
            pl.BlockSpec((tm * ROW_GROUP, LANES), lambda i: (i, 0)),
            pl.BlockSpec((tm, LANES), lambda i: (i, 0)),
        ],
        out_shape=[
            jax.ShapeDtypeStruct((T, D_MODEL), F32),
            jax.ShapeDtypeStruct((T * ROW_GROUP, LANES), BF16),
            jax.ShapeDtypeStruct((T, LANES), F32),
        ],
        compiler_params=_params(("parallel",)),
        name="mix",
    )(*o_hg, *o_r, P, P, P, P, xp, xs, wbh, wbr, wout, norm_ffn, wr, br)


def _plan_kernel(route_ref, pos_ref, te_ref, tail_ref, cum_scr, *, tb):
    T = route_ref.shape[0]
    lane_t = lax.broadcasted_iota(I32, (tb, LANES), 1).astype(F32)
    r = lax.broadcasted_iota(I32, (tb, tb), 0)
    c = lax.broadcasted_iota(I32, (tb, tb), 1)
    strict_lower = jnp.where(r > c, 1.0, 0.0).astype(BF16)

    def onehots(rows):
        rt = route_ref[rows, :]
        a1 = jnp.where(lane_t == rt[:, LANE_E1:LANE_E1 + 1], 1.0, 0.0)
        a2 = jnp.where(lane_t == rt[:, LANE_E2:LANE_E2 + 1], 1.0, 0.0)
        return a1, a2

    def rank_body(bi, carry):
        rows = pl.ds(pl.multiple_of(bi * tb, tb), tb)
        a1, a2 = onehots(rows)
        m = a1 + a2
        cum_scr[rows, :] = _dot(strict_lower, m.astype(BF16)) + carry
        return carry + jnp.sum(m, axis=0, keepdims=True)

    counts = lax.fori_loop(0, T // tb, rank_body, jnp.zeros((1, LANES), F32))
    ntiles = jnp.floor((counts + (MOE_TILE - 1)) * (1.0 / MOE_TILE))
    rr = lax.broadcasted_iota(I32, (LANES, LANES), 0)
    cc = lax.broadcasted_iota(I32, (LANES, LANES), 1)
    upper = jnp.where(rr < cc, 1.0, 0.0).astype(BF16)
    tile0 = _dot(jnp.broadcast_to(ntiles, (SUBLANES, LANES)).astype(BF16), upper)[0:1, :]
    slot0 = tile0 * MOE_TILE

    def pos_body(bi, carry):
        rows = pl.ds(pl.multiple_of(bi * tb, tb), tb)
        a1, a2 = onehots(rows)
        base = cum_scr[rows, :] + slot0
        p1 = jnp.sum(a1 * base, axis=1, keepdims=True)
        p2 = jnp.sum(a2 * base, axis=1, keepdims=True)
        pos_ref[rows, :] = (jnp.where(lane_t == 0.0, p1, 0.0) + jnp.where(lane_t == 1.0, p2, 0.0)).astype(I32)
        return carry

    lax.fori_loop(0, T // tb, pos_body, 0)
    tend = tile0 + ntiles
    lane = lax.broadcasted_iota(I32, (LANES, LANES), 1)
    tile_i = lax.broadcasted_iota(I32, (LANES, LANES), 0).astype(F32)
    is_expert = lane < N_EXPERTS
    te = jnp.sum(jnp.where(is_expert & (jnp.broadcast_to(tend, (LANES, LANES)) <= tile_i), 1.0, 0.0),
                 axis=1, keepdims=True)
    te_ref[...] = jnp.broadcast_to(jnp.minimum(te, N_EXPERTS - 1.0), (LANES, LANES)).astype(I32)
    lane1 = lax.broadcasted_iota(I32, (1, LANES), 1)
    total = jnp.sum(jnp.where(lane1 < N_EXPERTS, ntiles, 0.0), axis=1, keepdims=True)
    tail = jnp.where(ntiles > 0.0, (tend - 1.0) * MOE_TILE, -1.0)
    tail = jnp.where(lane1 < N_EXPERTS, tail, jnp.where(lane1 == N_EXPERTS, total, 0.0))
    tail_ref[...] = jnp.broadcast_to(tail, (SUBLANES, LANES)).astype(I32)


def _plan(route, tb):
    T = route.shape[0]
    return pl.pallas_call(
        functools.partial(_plan_kernel, tb=tb),
        grid=(1,),
        in_specs=[pl.BlockSpec((T, LANES), lambda i: (0, 0))],
        out_specs=[
            pl.BlockSpec((T, LANES), lambda i: (0, 0)),
            pl.BlockSpec((LANES, LANES), lambda i: (0, 0)),
            pl.BlockSpec((SUBLANES, LANES), lambda i: (0, 0)),
        ],
        out_shape=[
            jax.ShapeDtypeStruct((T, LANES), I32),
            jax.ShapeDtypeStruct((LANES, LANES), I32),
            jax.ShapeDtypeStruct((SUBLANES, LANES), I32),
        ],
        scratch_shapes=[pltpu.VMEM((T, LANES), F32)],
        compiler_params=_params(("arbitrary",)),
        name="moe_plan",
    )(route)


def _row_group(ref, idx):
    return ref.at[pl.ds(pl.multiple_of(idx * ROW_GROUP, ROW_GROUP), ROW_GROUP), :]


def _dispatch_kernel(pos_ref, tail_ref, hn_ref, xs_ref, zero_scr, zsem, sem):
    i = pl.program_id(0)
    tm = hn_ref.shape[0] // ROW_GROUP
    tile_rows = MOE_TILE * ROW_GROUP

    @pl.when(i == 0)
    def _():
        zero_scr[...] = jnp.zeros_like(zero_scr)
        n_used = tail_ref[N_EXPERTS]

        def zero_copy(first_slot):
            start = pl.multiple_of(first_slot * ROW_GROUP, tile_rows)
            return pltpu.make_async_copy(zero_scr, xs_ref.at[pl.ds(start, tile_rows), :], zsem)

        def tails(fn):
            def body(e, carry):
                pl.when(tail_ref[e] >= 0)(lambda: fn(zero_copy(jnp.maximum(tail_ref[e], 0))))
                return carry
            lax.fori_loop(0, N_EXPERTS, body, 0)

        def unused(fn):
            def body(t, carry):
                fn(zero_copy(t * MOE_TILE))
                return carry
            lax.fori_loop(n_used, xs_ref.shape[0] // tile_rows, body, 0)

        tails(lambda c: c.start())
        unused(lambda c: c.start())
        tails(lambda c: c.wait())
        unused(lambda c: c.wait())

    def copy(r, k):
        slot = pos_ref[(i * tm + r) * TOP_K + k]
        return pltpu.make_async_copy(_row_group(hn_ref, r), _row_group(xs_ref, slot), sem)

    def start(r, carry):
        for k in range(TOP_K):
            copy(r, k).start(priority=k)
        return carry

    lax.fori_loop(0, tm, start, 0, unroll=8)
    for _ in range(TOP_K):
        pltpu.make_async_copy(hn_ref, xs_ref.at[pl.ds(0, tm * ROW_GROUP), :], sem).wait()


def _dispatch(pos, tails, hn_rows, n_slots, tm):
    T = hn_rows.shape[0] // ROW_GROUP
    return pl.pallas_call(
        _dispatch_kernel,
        grid_spec=pltpu.PrefetchScalarGridSpec(
            num_scalar_prefetch=2,
            grid=(T // tm,),
            in_specs=[pl.BlockSpec((tm * ROW_GROUP, LANES), lambda i, *_: (i, 0))],
            out_specs=pl.BlockSpec(memory_space=pl.ANY),
            scratch_shapes=[pltpu.VMEM((MOE_TILE * ROW_GROUP, LANES), BF16),
                            pltpu.SemaphoreType.DMA, pltpu.SemaphoreType.DMA],
        ),
        out_shape=jax.ShapeDtypeStruct((n_slots * ROW_GROUP, LANES), BF16),
        compiler_params=_params(("arbitrary",)),
        name="moe_dispatch",
    )(pos, tails, hn_rows)


def _experts_kernel(te_ref, nt_ref, xs_ref, wg_hbm, wu_hbm, wd_hbm, ys_ref,
                    wg_buf, wu_buf, wd_buf, slot_ref, sem):
    i = pl.program_id(0)
    nt = nt_ref[0]
    e = te_ref[i]

    def fetch(expert, slot):
        return [pltpu.make_async_copy(hbm.at[expert], buf.at[slot], sem.at[slot, n])
                for n, (hbm, buf) in enumerate(((wg_hbm, wg_buf), (wu_hbm, wu_buf), (wd_hbm, wd_buf)))]

    @pl.when(i == 0)
    def _():
        slot_ref[1] = 0
        for c in fetch(e, 0):
            c.start(priority=1)

    @pl.when((i < nt) & ((i == 0) | (te_ref[jnp.maximum(i - 1, 0)] != e)))
    def _():
        slot = slot_ref[1]
        for c in fetch(e, slot):
            c.wait()
        nxt = lax.while_loop(lambda j: (j < nt) & (te_ref[jnp.minimum(j, nt - 1)] == e), lambda j: j + 1, i + 1)

        @pl.when(nxt < nt)
        def _():
            for c in fetch(te_ref[jnp.minimum(nxt, nt - 1)], 1 - slot):
                c.start(priority=1)

        slot_ref[0] = slot
        slot_ref[1] = 1 - slot

    @pl.when(i < nt)
    def _():
        slot = slot_ref[0]
        x = _from_token_tiles(xs_ref[...])[0]
        hmid = (_silu(_dot(x, wg_buf[slot].astype(BF16))) * _dot(x, wu_buf[slot].astype(BF16))).astype(BF16)
        ys_ref[...] = _to_token_tiles(_dot(hmid, wd_buf[slot].astype(BF16)))

    @pl.when(i >= nt)
    def _():
        ys_ref[...] = jnp.zeros_like(ys_ref)


def _experts(te, nt, xs_rows, w_ge, w_ue, w_de, n_tiles):
    M = MOE_TILE
    rows = pl.BlockSpec((M * ROW_GROUP, LANES), lambda i, te_ref, nt_ref: (jnp.minimum(i, nt_ref[0] - 1), 0))
    hbm = pl.BlockSpec(memory_space=pl.ANY)
    up, down = (D_MODEL, EXPERT_FF), (EXPERT_FF, D_MODEL)
    return pl.pallas_call(
        _experts_kernel,
        grid_spec=pltpu.PrefetchScalarGridSpec(
            num_scalar_prefetch=2,
            grid=(n_tiles,),
            in_specs=[rows, hbm, hbm, hbm],
            out_specs=pl.BlockSpec((M * ROW_GROUP, LANES), lambda i, te_ref, nt_ref: (i, 0)),
            scratch_shapes=[pltpu.VMEM((2,) + up, F32), pltpu.VMEM((2,) + up, F32), pltpu.VMEM((2,) + down, F32),
                            pltpu.SMEM((2,), I32), pltpu.SemaphoreType.DMA((2, 3))],
        ),
        out_shape=jax.ShapeDtypeStruct(xs_rows.shape, F32),
        compiler_params=_params(("arbitrary",)),
        name="moe_experts",
    )(te, nt, xs_rows, w_ge, w_ue, w_de)


def _final_kernel(pos_ref, h_ref, route_ref, gain_ref, ys_ref, y_ref, buf, sem, *, tile0):
    i = pl.program_id(0)
    n = pl.num_programs(0)
    tm = h_ref.shape[0]

    def copy(step, r, k, slot_buf):
        slot = pos_ref[((tile0 + step) * tm + r) * TOP_K + k]
        return pltpu.make_async_copy(_row_group(ys_ref, slot), _row_group(buf.at[slot_buf], r * TOP_K + k),
                                     sem.at[slot_buf])

    def start_all(step, slot_buf):
        def body(r, carry):
            for k in range(TOP_K):
                copy(step, r, k, slot_buf).start(priority=k)
            return carry
        lax.fori_loop(0, tm, body, 0, unroll=8)

    cur = i % 2
    pl.when(i == 0)(lambda: start_all(0, 0))
    pl.when(i + 1 < n)(lambda: start_all(i + 1, 1 - cur))
    pltpu.make_async_copy(ys_ref.at[pl.ds(0, buf.shape[1]), :], buf.at[cur], sem.at[cur]).wait()

    rt = route_ref[...]
    w1 = rt[:, LANE_W1:LANE_W1 + 1]
    w2 = rt[:, LANE_W2:LANE_W2 + 1]
    o1, o2 = _from_token_tiles(buf[cur], groups=TOP_K)
    v = h_ref[...] + (w1 * o1 + w2 * o2)
    y_ref[...] = _rms(v) * gain_ref[...]


def _final(pos, h, route, gain, ys_rows, tile0, n_tiles, tm):
    return pl.pallas_call(
        functools.partial(_final_kernel, tile0=tile0),
        grid_spec=pltpu.PrefetchScalarGridSpec(
            num_scalar_prefetch=1,
            grid=(n_tiles,),
            in_specs=[
                pl.BlockSpec((tm, D_MODEL), lambda i, *_: (tile0 + i, 0)),
                pl.BlockSpec((tm, LANES), lambda i, *_: (tile0 + i, 0)),
                pl.BlockSpec((1, D_MODEL), lambda i, *_: (0, 0)),
                pl.BlockSpec(memory_space=pl.ANY),
            ],
            out_specs=pl.BlockSpec((tm, D_MODEL), lambda i, *_: (i, 0)),
            scratch_shapes=[pltpu.VMEM((2, tm * TOP_K * ROW_GROUP, LANES), F32), pltpu.SemaphoreType.DMA((2,))],
        ),
        out_shape=jax.ShapeDtypeStruct((n_tiles * tm, D_MODEL), F32),
        compiler_params=_params(("arbitrary",)),
        name="final",
    )(pos, h, route, gain, ys_rows)


def _rope_tables(pos):
    half = RET_HEAD_QK // 2
    inv = 1.0 / (ROPE_BASE ** jnp.linspace(0.0, 1.0, half, dtype=F32))
    ang = jnp.repeat(pos[:, None] * inv[None, :], 2, axis=-1)
    cos, sin = jnp.cos(ang), jnp.sin(ang)
    even = (jnp.arange(RET_HEAD_QK) % 2 == 0)[None, :]
    sa = jnp.where(even, -sin, 0.0)
    sb = jnp.where(even, 0.0, sin)
    return cos, sa, sb


def _tile_sizes(Tp, Ts, Lp):
    g = math.gcd(Tp, Ts)
    return dict(
        xnorm=math.gcd(g, 512),
        proj=(Tp + Ts) // 4,
        mix=math.gcd(g, 256),
        plan=math.gcd(Tp + Ts, 512),
        dispatch=math.gcd(g, 1024),
        final=math.gcd(g, 256),
        scan=math.gcd(Lp, 128),
    )


def kernel(x_prompt, x_sample, state_hgrn, state_ret, norm_mix, w_in, hg_lb_logits, hg_norm,
           w_branch_hg, w_branch_ret, b_gate, w_out, norm_ffn, w_router_group, b_router_group,
           w_router_expert, b_router_expert, w_expert_gate, w_expert_up, w_expert_down, norm_final):
    assert w_in.shape == (1, D_MODEL, IN_WIDTH), "single-layer trunk only"
    Bp, Lp, _ = x_prompt.shape
    Bs, Ls, _ = x_sample.shape
    Tp, Ts = Bp * Lp, Bs * Ls
    T = Tp + Ts
    ts = _tile_sizes(Tp, Ts, Lp)
    xp = x_prompt.reshape(Tp, D_MODEL)
    xs = x_sample.reshape(Ts, D_MODEL)

    lower = jnp.cumsum(jax.nn.softmax(hg_lb_logits.astype(F32), axis=0), axis=0)[0]
    act = _act_coefficients(lower, b_gate[0])
    tab_p = _rope_tables(jnp.arange(Lp, dtype=F32))
    tab_s = _rope_tables(jnp.arange(Ls, dtype=F32) + jnp.float32(PAST_LEN))
    tabs = tuple(jnp.concatenate([jnp.tile(a, (Bp, 1)), jnp.tile(b, (Bs, 1))]) for a, b in zip(tab_p, tab_s))
    xn = _xnorm(xp, xs, norm_mix, ts["xnorm"])
    P = _in_proj(xn, w_in[0], act, tabs, ts["proj"])

    bb = math.gcd(Bs, 16)
    o_hg_p, shp = _hgrn_prompt(P, hg_norm, Bp, Lp, ts["scan"])
    o_hg_s, shs = _hgrn_sample(P, hg_norm, state_hgrn[0].astype(F32), Bs, Ls, bb, Tp)
    o_r_p, srp = _ret_prompt(P, Bp, Lp, ts["scan"])
    o_r_s, srs = _ret_sample(P, state_ret[0].astype(F32), Bs, Ls, bb, Tp)

    wbh = w_branch_hg[0].astype(BF16)
    wbr = w_branch_ret[0].astype(BF16)
    wout = w_out[0].astype(BF16)
    pad = LANES - N_GROUPS - N_EXPERTS
    wr = jnp.concatenate([w_router_group[0], w_router_expert[0], jnp.zeros((D_MODEL, pad), F32)], axis=1)
    wr_hi = wr.astype(BF16)
    wr_hilo = jnp.concatenate([wr_hi, (wr - wr_hi.astype(F32)).astype(BF16)], axis=1)
    br = jnp.concatenate([b_router_group[0], b_router_expert[0], jnp.zeros((pad,), F32)])[None, :]
    h, hn_rows, route = _mix(list(o_hg_p) + [o_hg_s], list(o_r_p) + [o_r_s], P, xp, xs, wbh, wbr, wout,
                             norm_ffn, wr_hilo, br, ts["mix"])

    n_tiles = -(-T * TOP_K // MOE_TILE) + N_EXPERTS
    assert n_tiles <= LANES
    pos_slab, te_slab, tail_slab = _plan(route, ts["plan"])
    pos = pos_slab[:, :TOP_K].reshape(-1)
    te = te_slab[:n_tiles, 0]
    tails = tail_slab[0, :N_EXPERTS + 1]
    nt = tail_slab[0, N_EXPERTS:N_EXPERTS + 1]
    xs_rows = _dispatch(pos, tails, hn_rows, n_tiles * MOE_TILE, ts["dispatch"])
    ys_rows = _experts(te, nt, xs_rows, w_expert_gate[0], w_expert_up[0], w_expert_down[0], n_tiles)

    tf = ts["final"]
    gain = norm_final[None, :]
    y_p = _final(pos, h, route, gain, ys_rows, 0, Tp // tf, tf)
    y_s = _final(pos, h, route, gain, ys_rows, Tp // tf, Ts // tf, tf)
    return (y_p.reshape(Bp, Lp, D_MODEL), y_s.reshape(Bs, Ls, D_MODEL),
            shp[None], srp[None], shs[None], srs[None])
```

```python
import functools
import math

import jax
import jax.numpy as jnp
from jax import lax
from jax.experimental import pallas as pl
from jax.experimental.pallas import tpu as pltpu

F32 = jnp.float32
BF16 = jnp.bfloat16
I32 = jnp.int32

D_MODEL = 2048
PAST_LEN = 16384
HG_HEAD_V = 128
HG_EXPAND = 128
HG_VAL_DIM = D_MODEL // 2
HG_HEADS = HG_VAL_DIM // HG_HEAD_V
HG_KEY_DIM = HG_HEADS * HG_EXPAND
RET_HEAD_V = 256
RET_HEAD_QK = 128
RET_V_DIM = D_MODEL // 2
RET_HEADS = RET_V_DIM // RET_HEAD_V
RET_QK_DIM = RET_HEADS * RET_HEAD_QK
IN_WIDTH = 2 * HG_KEY_DIM + 2 * HG_VAL_DIM + 2 * RET_QK_DIM + 2 * RET_V_DIM + 2 * D_MODEL
SCAN_CHUNK = 128
N_GROUPS = 4
EXPERTS_PER_GROUP = 8
N_EXPERTS = N_GROUPS * EXPERTS_PER_GROUP
TOP_K = 2
EXPERT_FF = D_MODEL // 4
ROPE_BASE = 10000.0
EPS = 1e-6

LANES = 128
SUBLANES = 8
VMEM_LIMIT = 56 * 1024 * 1024
PROJ_MC = 256
PROJ_TN = 512
SEG_HQ, SEG_HF, SEG_HI, SEG_HG, SEG_RQ, SEG_RK, SEG_RV, SEG_RG, SEG_GA = 0, 2, 4, 6, 8, 9, 10, 12, 14
SAFE_EXP_SPAN = 80.0
ROW_GROUP = D_MODEL // LANES
MOE_TILE = 320
LANE_E1, LANE_E2, LANE_W1, LANE_W2 = 0, 1, 2, 3
ROUTE_EXPERT_LANE = N_GROUPS


def _params(sem):
    return pltpu.CompilerParams(dimension_semantics=sem, vmem_limit_bytes=VMEM_LIMIT)


def _silu(z):
    return z * jax.nn.sigmoid(z)


def _dot(a, b):
    return jnp.dot(a, b, preferred_element_type=F32)


def _dot_nt(a, b):
    return lax.dot_general(a, b, (((1,), (1,)), ((), ())), preferred_element_type=F32)


def _dot_tn(a, b):
    return lax.dot_general(a, b, (((0,), (0,)), ((), ())), preferred_element_type=F32)


def _rms(x):
    return x * lax.rsqrt(jnp.mean(x * x, axis=-1, keepdims=True) + EPS)


def _to_token_tiles(x):
    M = x.shape[0]
    pieces = jnp.stack([x[:, s * LANES:(s + 1) * LANES] for s in range(ROW_GROUP)], axis=0)
    return jnp.swapaxes(pieces, 0, 1).reshape(M * ROW_GROUP, LANES)


def _from_token_tiles(rows, groups=1):
    M = rows.shape[0] // (groups * ROW_GROUP)
    t = jnp.swapaxes(rows.reshape(M, groups * ROW_GROUP, LANES), 0, 1)
    return [jnp.concatenate([t[g * ROW_GROUP + s] for s in range(ROW_GROUP)], axis=1) for g in range(groups)]


def _xnorm_kernel(xp_ref, xs_ref, gain_ref, o_ref, *, n_prompt_tiles):
    i = pl.program_id(0)

    def norm(x_ref):
        o_ref[...] = (_rms(x_ref[...]) * gain_ref[...]).astype(BF16)

    pl.when(i < n_prompt_tiles)(lambda: norm(xp_ref))
    pl.when(i >= n_prompt_tiles)(lambda: norm(xs_ref))


def _xnorm(xp, xs, gain, tm):
    npt, nst = xp.shape[0] // tm, xs.shape[0] // tm
    return pl.pallas_call(
        functools.partial(_xnorm_kernel, n_prompt_tiles=npt),
        grid=(npt + nst,),
        in_specs=[
            pl.BlockSpec((tm, D_MODEL), lambda i: (jnp.minimum(i, npt - 1), 0)),
            pl.BlockSpec((tm, D_MODEL), lambda i: (jnp.maximum(i - npt, 0), 0)),
            pl.BlockSpec((1, D_MODEL), lambda i: (0, 0)),
        ],
        out_specs=pl.BlockSpec((tm, D_MODEL), lambda i: (i, 0)),
        out_shape=jax.ShapeDtypeStruct(((npt + nst) * tm, D_MODEL), BF16),
        compiler_params=_params(("parallel",)),
        name="xnorm",
    )(xp, xs, gain)


ACT_ALPHA, ACT_BETA, ACT_GAMMA, ACT_DELTA, ACT_EPS = 0, 1, 2, 3, 4


def _act_coefficients(lower, b_gate):
    z = jnp.zeros((IN_WIDTH,), F32)
    seg = lambda a, b: slice(a * PROJ_TN, b * PROJ_TN)
    alpha = z.at[seg(SEG_HI, SEG_HG)].set(1.0).at[seg(SEG_RQ, SEG_RG)].set(1.0)
    beta = z.at[seg(SEG_GA, IN_WIDTH // PROJ_TN)].set(b_gate)
    gamma = (z.at[seg(SEG_HQ, SEG_HF)].set(HG_EXPAND ** -0.5)
             .at[seg(SEG_HG, SEG_RQ)].set(1.0).at[seg(SEG_RG, SEG_GA)].set(1.0))
    delta = z.at[seg(SEG_HF, SEG_HI)].set(1.0 - lower).at[seg(SEG_GA, IN_WIDTH // PROJ_TN)].set(1.0)
    eps = z.at[seg(SEG_HF, SEG_HI)].set(lower)
    return jnp.stack([alpha, beta, gamma, delta, eps, z, z, z])


def _in_proj_kernel(x_ref, w_ref, act_ref, cos_ref, sa_ref, sb_ref, o_ref):
    j = pl.program_id(1)
    w = w_ref[...].astype(BF16)
    row = lambda r: act_ref[r:r + 1, :]
    tm = x_ref.shape[0]
    mc = math.gcd(tm, PROJ_MC)
    for m in range(0, tm, mc):
        p = _dot(x_ref[m:m + mc, :], w)
        o_ref[m:m + mc, :] = (row(ACT_ALPHA) * p + row(ACT_EPS)
                              + jax.nn.sigmoid(p + row(ACT_BETA)) * (row(ACT_GAMMA) * p + row(ACT_DELTA)))

    @pl.when((j >= SEG_RQ) & (j < SEG_RV))
    def _():
        scale = jnp.where(j == SEG_RK, RET_HEAD_QK ** -0.5, 1.0).astype(F32)
        cos, sa, sb = cos_ref[...], sa_ref[...], sb_ref[...]
        for hh in range(PROJ_TN // LANES):
            cols = slice(hh * LANES, (hh + 1) * LANES)
            xs = o_ref[:, cols]
            r = xs * cos + pltpu.roll(xs, LANES - 1, 1) * sa + pltpu.roll(xs, 1, 1) * sb
            o_ref[:, cols] = r * scale


def _in_proj(xn, w_in, act, tabs, tm):
    T = xn.shape[0]
    tab = pl.BlockSpec((tm, LANES), lambda i, j: (i, 0))
    return pl.pallas_call(
        _in_proj_kernel,
        grid=(T // tm, IN_WIDTH // PROJ_TN),
        in_specs=[
            pl.BlockSpec((tm, D_MODEL), lambda i, j: (i, 0)),
            pl.BlockSpec((D_MODEL, PROJ_TN), lambda i, j: (0, j)),
            pl.BlockSpec((SUBLANES, PROJ_TN), lambda i, j: (0, j)),
            tab, tab, tab,
        ],
        out_specs=pl.BlockSpec((tm, PROJ_TN), lambda i, j: (i, j)),
        out_shape=jax.ShapeDtypeStruct((T, IN_WIDTH), F32),
        compiler_params=_params(("parallel", "arbitrary")),
        name="in_proj",
    )(xn, w_in, act, *tabs)


def _prefix_matrix(C, G):
    r = lax.broadcasted_iota(I32, (C, 3 * C), 0)
    c = lax.broadcasted_iota(I32, (C, 3 * C), 1) % C
    return jnp.where((c <= r) & (c // G == r // G), 1.0, 0.0).astype(BF16)


def _prefix_sum(g, pm):
    g1 = g.astype(BF16)
    r1 = g - g1.astype(F32)
    g2 = r1.astype(BF16)
    g3 = (r1 - g2.astype(F32)).astype(BF16)
    return _dot(pm, jnp.concatenate([g1, g2, g3], axis=0))


def _state_decay_column(d_row):
    r = lax.broadcasted_iota(I32, (LANES, LANES), 0)
    c = lax.broadcasted_iota(I32, (LANES, LANES), 1)
    return jnp.sum(jnp.where(r == c, jnp.broadcast_to(d_row, (LANES, LANES)), 0.0), axis=1, keepdims=True)


def _hgrn_chunk(q, f, g, v, states, pm, tmp_ref, NG, factorised):
    C, W = q.shape
    G, H = C // NG, W // LANES
    hs = [slice(h * LANES, (h + 1) * LANES) for h in range(H)]
    gs = [slice(n * G, (n + 1) * G) for n in range(NG)]
    k = 1.0 - f
    b = _prefix_sum(g, pm)
    b3 = b.reshape(NG, G, W)
    b_end = b3[:, G - 1:G, :]
    q3, k3 = q.reshape(NG, G, W), k.reshape(NG, G, W)
    vb = v.astype(BF16)
    if factorised:
        b_mid = b3[:, G // 2 - 1:G // 2, :]
        qm3 = q3 * jnp.exp(b3 - b_mid)
        km3 = k3 * jnp.exp(b_mid - b3)
        qb = (qm3 * jnp.exp(b_mid)).reshape(C, W)
        kd = (km3 * jnp.exp(b_end - b_mid)).reshape(C, W)
        qm = qm3.reshape(C, W).astype(BF16)
        km = km3.reshape(C, W).astype(BF16)
        row = lax.broadcasted_iota(I32, (C, C), 0)
        col = lax.broadcasted_iota(I32, (C, C), 1)
        amask = (row >= col) & (row // G == col // G)
        o_intra = []
        for h in range(H):
            att = jnp.where(amask, _dot_nt(qm[:, hs[h]], km[:, hs[h]]), 0.0)
            o_intra.append(_dot(att.astype(BF16), vb[:, hs[h]]))
    else:
        qb = (q3 * jnp.exp(b3)).reshape(C, W)
        kd = (k3 * jnp.exp(b_end - b3)).reshape(C, W)
        trow = lax.broadcasted_iota(I32, (G, LANES), 0)
        o_intra = []
        for h in range(H):
            parts = []
            for n in range(NG):
                bh, qh = b[gs[n], hs[h]], q[gs[n], hs[h]]
                tmp_ref[0] = bh
                tmp_ref[1] = k[gs[n], hs[h]]
                tmp_ref[2] = v[gs[n], hs[h]]

                def body(s, acc, bh=bh, qh=qh):
                    bs = tmp_ref[0, pl.ds(s, 1), :]
                    ks = tmp_ref[1, pl.ds(s, 1), :]
                    vs = tmp_ref[2, pl.ds(s, 1), :]
                    w = jnp.where(trow >= s, jnp.exp(jnp.minimum(bh - bs, 0.0)), 0.0)
                    return acc + jnp.sum(qh * ks * w, axis=1, keepdims=True) * vs

                parts.append(lax.fori_loop(0, G, body, jnp.zeros((G, LANES), F32)))
            o_intra.append(parts[0] if NG == 1 else jnp.concatenate(parts, axis=0))
    d = jnp.exp(b_end)
    outs, new_states = [], [[None] * H for _ in range(NG)]
    for h in range(H):
        inter = []
        for n in range(NG):
            S = states[n][h]
            inter.append(_dot(qb[gs[n], hs[h]].astype(BF16), S.astype(BF16)))
            upd = _dot_tn(kd[gs[n], hs[h]].astype(BF16), v[gs[n], hs[h]].astype(BF16))
            new_states[n][h] = _state_decay_column(d[n, :, hs[h]]) * S + upd
        inter = inter[0] if NG == 1 else jnp.concatenate(inter, axis=0)
        outs.append(inter + o_intra[h])
    return jnp.concatenate(outs, axis=1), new_states


def _decay_is_safe(g, G):
    R, W = g.shape
    return jnp.min(jnp.sum(g.reshape(R // G, G, W), axis=1)) >= -SAFE_EXP_SPAN


def _hgrn_epilogue(o_scr, gain_ref, gate_ref, o_ref):
    o_ref[...] = (_rms(o_scr[...]) * gain_ref[...] * gate_ref[...]).astype(o_ref.dtype)


def _hgrn_prompt_kernel(*refs, C, NB):
    ins, (gain_ref,), rest = refs[:4 * NB], refs[4 * NB:4 * NB + 1], refs[4 * NB + 1:]
    o_refs, (st_ref, s_scr, g_scr, o_scr, tmp_ref) = rest[:NB], rest[NB:]
    seq = [ins[4 * n:4 * n + 4] for n in range(NB)]
    l = pl.program_id(0)
    Lb = o_refs[0].shape[0]

    @pl.when(l == 0)
    def _():
        s_scr[...] = jnp.zeros_like(s_scr)

    for n in range(NB):
        g_scr[n] = jnp.log(seq[n][1][...])
    pm = _prefix_matrix(C, C)

    def run(factorised):
        def body(ci, carry):
            rows = pl.ds(pl.multiple_of(ci * C, C), C)
            args = [(seq[n][0][rows, :], seq[n][1][rows, :], g_scr[n, rows, :], seq[n][2][rows, :],
                     [[s_scr[n, h] for h in range(HG_HEADS)]]) for n in range(NB)]
            outs = [_hgrn_chunk(*a, pm, tmp_ref, 1, factorised) for a in args]
            for n, (o, new) in enumerate(outs):
                o_scr[n, rows, :] = o
                for h in range(HG_HEADS):
                    s_scr[n, h] = new[0][h]
            return carry
        lax.fori_loop(0, Lb // C, body, 0)

    safe = _decay_is_safe(g_scr[...].reshape(NB * Lb, -1), C // 2)
    pl.when(safe)(lambda: run(True))
    pl.when(jnp.logical_not(safe))(lambda: run(False))
    for n in range(NB):
        o_refs[n][...] = (_rms(o_scr[n]) * gain_ref[...] * seq[n][3][...]).astype(o_refs[n].dtype)

    @pl.when(l == pl.num_programs(0) - 1)
    def _():
        st_ref[...] = s_scr[...]


def _hgrn_prompt(P, gain, B, L, Lb):
    C = math.gcd(L, SCAN_CHUNK)
    nl = L // Lb
    W = HG_KEY_DIM
    blk = lambda b, seg: pl.BlockSpec((Lb, W), lambda l, b=b, seg=seg: (b * nl + l, seg))
    outs = pl.pallas_call(
        functools.partial(_hgrn_prompt_kernel, C=C, NB=B),
        grid=(nl,),
        in_specs=[blk(b, seg) for b in range(B) for seg in range(4)] + [pl.BlockSpec((1, W), lambda l: (0, 0))],
        out_specs=[pl.BlockSpec((Lb, W), lambda l: (l, 0)) for _ in range(B)]
        + [pl.BlockSpec((B, HG_HEADS, HG_EXPAND, HG_HEAD_V), lambda l: (0, 0, 0, 0))],
        out_shape=[jax.ShapeDtypeStruct((L, HG_VAL_DIM), BF16) for _ in range(B)]
        + [jax.ShapeDtypeStruct((B, HG_HEADS, HG_EXPAND, HG_HEAD_V), F32)],
        scratch_shapes=[pltpu.VMEM((B, HG_HEADS, HG_EXPAND, HG_HEAD_V), F32), pltpu.VMEM((B, Lb, W), F32),
                        pltpu.VMEM((B, Lb, W), F32), pltpu.VMEM((3, C, LANES), F32)],
        compiler_params=_params(("arbitrary",)),
        name="hgrn_prompt",
    )(*([P] * (4 * B)), gain)
    return outs[:B], outs[B]


def _hgrn_sample_kernel(q_ref, f_ref, v_ref, gate_ref, gain_ref, s_ref, o_ref, so_ref, o_scr, tmp_ref, *, G):
    NG = s_ref.shape[0]
    g = jnp.log(f_ref[...])
    pm = _prefix_matrix(NG * G, G)

    def run(factorised):
        states = [[s_ref[n, h] for h in range(HG_HEADS)] for n in range(NG)]
        o, new = _hgrn_chunk(q_ref[...], f_ref[...], g, v_ref[...], states, pm, tmp_ref, NG, factorised)
        o_scr[...] = o
        for n in range(NG):
            for h in range(HG_HEADS):
                so_ref[n, h] = new[n][h]

    safe = _decay_is_safe(g, G // 2)
    pl.when(safe)(lambda: run(True))
    pl.when(jnp.logical_not(safe))(lambda: run(False))
    _hgrn_epilogue(o_scr, gain_ref, gate_ref, o_ref)


def _hgrn_sample(P, gain, state, Bs, Ls, Bb, row0):
    W = HG_KEY_DIM
    rows = Bb * Ls
    blk0 = row0 // rows

    def col(seg):
        return pl.BlockSpec((rows, W), lambda i, seg=seg: (blk0 + i, seg))

    st = pl.BlockSpec((Bb, HG_HEADS, HG_EXPAND, HG_HEAD_V), lambda i: (i, 0, 0, 0))
    return pl.pallas_call(
        functools.partial(_hgrn_sample_kernel, G=Ls),
        grid=(Bs // Bb,),
        in_specs=[col(0), col(1), col(2), col(3), pl.BlockSpec((1, W), lambda i: (0, 0)), st],
        out_specs=[pl.BlockSpec((rows, W), lambda i: (i, 0)), st],
        out_shape=[
            jax.ShapeDtypeStruct((Bs * Ls, HG_VAL_DIM), BF16),
            jax.ShapeDtypeStruct((Bs, HG_HEADS, HG_EXPAND, HG_HEAD_V), F32),
        ],
        scratch_shapes=[pltpu.VMEM((rows, W), F32), pltpu.VMEM((3, Ls, LANES), F32)],
        compiler_params=_params(("parallel",)),
        name="hgrn_sample",
    )(P, P, P, P, gain, state)


def _ret_chunk(q, k, v, states, dm_ref, qw_ref, kw_ref, sd_ref):
    outs, new_states = [], []
    C = q.shape[0]
    qb, kb, vb = q.astype(BF16), k.astype(BF16), v.astype(BF16)
    for h in range(RET_HEADS):
        qs = slice(h * RET_HEAD_QK, (h + 1) * RET_HEAD_QK)
        vs = slice(h * RET_HEAD_V, (h + 1) * RET_HEAD_V)
        att = (_dot_nt(qb[:, qs], kb[:, qs]) * dm_ref[h]).astype(BF16)
        kw = (k[:, qs] * kw_ref[h]).astype(BF16)
        if C <= 2 * SUBLANES:
            both = _dot(jnp.concatenate([att, kw.T], axis=0), vb[:, vs])
            intra, upd = both[:C], both[C:]
        else:
            intra, upd = _dot(att, vb[:, vs]), _dot_tn(kw, vb[:, vs])
        o = intra + _dot((q[:, qs] * qw_ref[h]).astype(BF16), states[h].astype(BF16))
        new_states.append(sd_ref[h, 0:1, 0:1] * states[h] + upd)
        outs.append(o)
    return jnp.concatenate(outs, axis=1), new_states


def _ret_epilogue(o_scr, gate_ref, o_ref):
    for h in range(RET_HEADS):
        vs = slice(h * RET_HEAD_V, (h + 1) * RET_HEAD_V)
        o_ref[:, vs] = (_rms(o_scr[:, vs]) * gate_ref[:, vs]).astype(o_ref.dtype)


def _ret_prompt_kernel(*refs, NB):
    ins, (dm_ref, qw_ref, kw_ref, sd_ref), rest = refs[:4 * NB], refs[4 * NB:4 * NB + 4], refs[4 * NB + 4:]
    o_refs, (st_ref, s_scr, o_scr) = rest[:NB], rest[NB:]
    seq = [ins[4 * n:4 * n + 4] for n in range(NB)]
    l = pl.program_id(0)
    Lb = o_refs[0].shape[0]
    C = dm_ref.shape[1]

    @pl.when(l == 0)
    def _():
        s_scr[...] = jnp.zeros_like(s_scr)

    def body(ci, carry):
        rows = pl.ds(pl.multiple_of(ci * C, C), C)
        args = [(seq[n][0][rows, :], seq[n][1][rows, :], seq[n][2][rows, :],
                 [s_scr[n, h] for h in range(RET_HEADS)]) for n in range(NB)]
        outs = [_ret_chunk(*a, dm_ref, qw_ref, kw_ref, sd_ref) for a in args]
        for n, (o, new) in enumerate(outs):
            o_scr[n, rows, :] = o
            for h in range(RET_HEADS):
                s_scr[n, h] = new[h]
        return carry

    lax.fori_loop(0, Lb // C, body, 0)
    for n in range(NB):
        _ret_epilogue(o_scr.at[n], seq[n][3], o_refs[n])

    @pl.when(l == pl.num_programs(0) - 1)
    def _():
        st_ref[...] = s_scr[...]


def _ret_tables(C):
    log_gamma = jnp.log(1.0 - jnp.exp2(-5.0 - jnp.arange(RET_HEADS, dtype=F32)))
    idx = jnp.arange(C, dtype=F32)
    rel = idx[:, None] - idx[None, :]
    tri = jnp.tril(jnp.ones((C, C), dtype=bool))
    dmat = jnp.exp(jnp.where(tri[None], log_gamma[:, None, None] * rel[None], -jnp.inf))
    qw = jnp.exp(log_gamma[:, None] * (idx[None, :] + 1.0))[..., None]
    kw = jnp.exp(log_gamma[:, None] * (C - 1.0 - idx[None, :]))[..., None]
    sdec = jnp.exp(log_gamma * C)[:, None, None]
    bc = lambda a: jnp.broadcast_to(a, (RET_HEADS, a.shape[1], LANES))
    return dmat, bc(qw), bc(kw), jnp.broadcast_to(sdec, (RET_HEADS, SUBLANES, LANES))


_QK0 = (2 * HG_KEY_DIM + 2 * HG_VAL_DIM) // RET_QK_DIM
_V0 = (2 * HG_KEY_DIM + 2 * HG_VAL_DIM + 2 * RET_QK_DIM) // RET_V_DIM


def _full3(a):
    return pl.BlockSpec(a.shape, lambda *_: (0, 0, 0))


def _ret_prompt(P, B, L, Lb):
    C = math.gcd(L, 2 * SCAN_CHUNK)
    Lb = max(Lb, C)
    nl = L // Lb
    tabs = _ret_tables(C)
    st_shape = (B, RET_HEADS, RET_HEAD_QK, RET_HEAD_V)
    blk = lambda b, w, c: pl.BlockSpec((Lb, w), lambda l, b=b, c=c: (b * nl + l, c))
    per_seq = lambda b: [blk(b, RET_QK_DIM, _QK0), blk(b, RET_QK_DIM, _QK0 + 1),
                         blk(b, RET_V_DIM, _V0), blk(b, RET_V_DIM, _V0 + 1)]
    outs = pl.pallas_call(
        functools.partial(_ret_prompt_kernel, NB=B),
        grid=(nl,),
        in_specs=[spec for b in range(B) for spec in per_seq(b)] + [_full3(t) for t in tabs],
        out_specs=[pl.BlockSpec((Lb, RET_V_DIM), lambda l: (l, 0)) for _ in range(B)]
        + [pl.BlockSpec(st_shape, lambda l: (0, 0, 0, 0))],
        out_shape=[jax.ShapeDtypeStruct((L, RET_V_DIM), BF16) for _ in range(B)]
        + [jax.ShapeDtypeStruct(st_shape, F32)],
        scratch_shapes=[pltpu.VMEM(st_shape, F32), pltpu.VMEM((B, Lb, RET_V_DIM), F32)],
        compiler_params=_params(("arbitrary",)),
        name="ret_prompt",
    )(*([P] * (4 * B)), *tabs)
    return outs[:B], outs[B]


def _ret_sample_kernel(q_ref, k_ref, v_ref, gate_ref, s_ref, dm_ref, qw_ref, kw_ref, sd_ref,
                       o_ref, so_ref, o_scr, *, C):
    Bb = s_ref.shape[0]
    args = [(q_ref[bb * C:(bb + 1) * C, :], k_ref[bb * C:(bb + 1) * C, :], v_ref[bb * C:(bb + 1) * C, :],
             [s_ref[bb, h] for h in range(RET_HEADS)]) for bb in range(Bb)]
    outs = [_ret_chunk(*a, dm_ref, qw_ref, kw_ref, sd_ref) for a in args]
    for bb, (o, new) in enumerate(outs):
        o_scr[bb * C:(bb + 1) * C, :] = o
        for h in range(RET_HEADS):
            so_ref[bb, h] = new[h]
    _ret_epilogue(o_scr, gate_ref, o_ref)


def _ret_sample(P, state, Bs, Ls, Bb, row0):
    tabs = _ret_tables(Ls)
    rows = Bb * Ls
    blk0 = row0 // rows
    st = pl.BlockSpec((Bb, RET_HEADS, RET_HEAD_QK, RET_HEAD_V), lambda i: (i, 0, 0, 0))
    return pl.pallas_call(
        functools.partial(_ret_sample_kernel, C=Ls),
        grid=(Bs // Bb,),
        in_specs=[
            pl.BlockSpec((rows, RET_QK_DIM), lambda i: (blk0 + i, _QK0)),
            pl.BlockSpec((rows, RET_QK_DIM), lambda i: (blk0 + i, _QK0 + 1)),
            pl.BlockSpec((rows, RET_V_DIM), lambda i: (blk0 + i, _V0)),
            pl.BlockSpec((rows, RET_V_DIM), lambda i: (blk0 + i, _V0 + 1)),
            st,
        ] + [_full3(t) for t in tabs],
        out_specs=[pl.BlockSpec((rows, RET_V_DIM), lambda i: (i, 0)), st],
        out_shape=[
            jax.ShapeDtypeStruct((Bs * Ls, RET_V_DIM), BF16),
            jax.ShapeDtypeStruct((Bs, RET_HEADS, RET_HEAD_QK, RET_HEAD_V), F32),
        ],
        scratch_shapes=[pltpu.VMEM((rows, RET_V_DIM), F32)],
        compiler_params=_params(("parallel",)),
        name="ret_sample",
    )(P, P, P, P, state, *tabs)


def _split_dot(a, w_hilo):
    a_hi = a.astype(BF16)
    a_lo = (a - a_hi.astype(F32)).astype(BF16)
    hi = _dot(a_hi, w_hilo)
    return hi[:, :LANES] + (hi[:, LANES:] + _dot(a_lo, w_hilo[:, :LANES]))


def _route(logits):
    lane = lax.broadcasted_iota(I32, logits.shape, 1)
    neg = jnp.float32(-jnp.inf)
    big = jnp.int32(LANES)
    gmask = lane < N_GROUPS
    gl = jnp.where(gmask, logits, neg)
    gmax = jnp.max(gl, axis=1, keepdims=True)
    gu = jnp.where(gmask, jnp.exp(gl - gmax), 0.0)
    gp = gu / jnp.sum(gu, axis=1, keepdims=True)
    g_w = jnp.max(gp, axis=1, keepdims=True)
    g_i = jnp.min(jnp.where(gmask & (gp == g_w), lane, big), axis=1, keepdims=True)
    lo = ROUTE_EXPERT_LANE + g_i * EXPERTS_PER_GROUP
    emask = (lane >= lo) & (lane < lo + EXPERTS_PER_GROUP)
    el = jnp.where(emask, logits, neg)
    e1 = jnp.max(el, axis=1, keepdims=True)
    i1 = jnp.min(jnp.where(emask & (el == e1), lane, big), axis=1, keepdims=True)
    el2 = jnp.where(lane == i1, neg, el)
    e2 = jnp.max(el2, axis=1, keepdims=True)
    i2 = jnp.min(jnp.where(emask & (lane != i1) & (el2 == e2), lane, big), axis=1, keepdims=True)
    u2 = jnp.exp(e2 - e1)
    den = 1.0 + u2
    w1 = (1.0 / den) * g_w
    w2 = (u2 / den) * g_w
    id1 = (i1 - ROUTE_EXPERT_LANE).astype(F32)
    id2 = (i2 - ROUTE_EXPERT_LANE).astype(F32)
    return (jnp.where(lane == LANE_E1, id1, 0.0) + jnp.where(lane == LANE_E2, id2, 0.0)
            + jnp.where(lane == LANE_W1, w1, 0.0) + jnp.where(lane == LANE_W2, w2, 0.0))


def _pick(i, bounds, refs):
    val = refs[-1][...]
    for n in range(len(refs) - 2, -1, -1):
        val = jnp.where(i < bounds[n + 1], refs[n][...], val)
    return val


def _mix_kernel(*refs, bounds):
    ns = len(bounds) - 1
    ohg_refs, or_refs, refs = refs[:ns], refs[ns:2 * ns], refs[2 * ns:]
    (ga0_ref, ga1_ref, gb0_ref, gb1_ref, xp_ref, xs_ref, wbh_ref, wbr_ref, wout_ref, nffn_ref,
     wr_ref, br_ref, h_ref, hn_ref, route_ref) = refs
    i = pl.program_id(0)
    tm = h_ref.shape[0]
    is_prompt = i < bounds[-2]
    p_hg = _dot(_pick(i, bounds, ohg_refs), wbh_ref[...])
    p_r = _dot(_pick(i, bounds, or_refs), wbr_ref[...])
    half = D_MODEL // 2
    m0 = ga0_ref[...] * p_hg[:, :half] + gb0_ref[...] * p_r[:, :half]
    m1 = ga1_ref[...] * p_hg[:, half:] + gb1_ref[...] * p_r[:, half:]
    mixed = jnp.concatenate([m0, m1], axis=1).astype(BF16)
    x = jnp.where(is_prompt, xp_ref[...], xs_ref[...])
    h = x + _dot(mixed, wout_ref[...])
    h_ref[...] = h
    hn = _rms(h) * nffn_ref[...]
    hn_ref[...] = _to_token_tiles(hn.astype(BF16))
    route_ref[...] = _route(_split_dot(hn, wr_ref[...]) + br_ref[...])


def _mix(o_hg, o_r, P, xp, xs, wbh, wbr, wout, norm_ffn, wr, br, tm):
    npt, nst = xp.shape[0] // tm, xs.shape[0] // tm
    bounds = [0]
    for a in o_hg:
        bounds.append(bounds[-1] + a.shape[0] // tm)
    T = (npt + nst) * tm
    half = D_MODEL // 2
    ga = (IN_WIDTH - 2 * D_MODEL) // half

    def pcol(c):
        return pl.BlockSpec((tm, half), lambda i, c=c: (i, c))

    def const(a):
        return pl.BlockSpec(a.shape, lambda i: (0, 0), pipeline_mode=pl.Buffered(1))

    def prompt(w):
        return pl.BlockSpec((tm, w), lambda i: (jnp.minimum(i, npt - 1), 0))

    def sample(w):
        return pl.BlockSpec((tm, w), lambda i: (jnp.maximum(i - npt, 0), 0))

    def source(n):
        lo, hi = bounds[n], bounds[n + 1]
        return pl.BlockSpec((tm, half), lambda i: (jnp.clip(i - lo, 0, hi - lo - 1), 0))

    sources = [source(n) for n in range(len(o_hg))]
    return pl.pallas_call(
        functools.partial(_mix_kernel, bounds=tuple(bounds)),
        grid=(npt + nst,),
        in_specs=sources + sources + [
            pcol(ga), pcol(ga + 1), pcol(ga + 2), pcol(ga + 3),
            prompt(D_MODEL), sample(D_MODEL),
            const(wbh), const(wbr), const(wout), const(norm_ffn), const(wr), const(br),
        ],
        out_specs=[
            pl.BlockSpec((tm, D_MODEL), lambda i: (i, 0)),
            pl.BlockSpec((tm * ROW_GROUP, LANES), lambda i: (i, 0)),
            pl.BlockSpec((tm, LANES), lambda i: (i, 0)),
        ],
        out_shape=[
            jax.ShapeDtypeStruct((T, D_MODEL), F32),
            jax.ShapeDtypeStruct((T * ROW_GROUP, LANES), BF16),
            jax.ShapeDtypeStruct((T, LANES), F32),
        ],
        compiler_params=_params(("parallel",)),
        name="mix",
    )(*o_hg, *o_r, P, P, P, P, xp, xs, wbh, wbr, wout, norm_ffn, wr, br)


def _plan_kernel(route_ref, pos_ref, te_ref, tail_ref, cum_scr, *, tb):
    T = route_ref.shape[0]
    lane_t = lax.broadcasted_iota(I32, (tb, LANES), 1).astype(F32)
    r = lax.broadcasted_iota(I32, (tb, tb), 0)
    c = lax.broadcasted_iota(I32, (tb, tb), 1)
    strict_lower = jnp.where(r > c, 1.0, 0.0).astype(BF16)

    def onehots(rows):
        rt = route_ref[rows, :]
        a1 = jnp.where(lane_t == rt[:, LANE_E1:LANE_E1 + 1], 1.0, 0.0)
        a2 = jnp.where(lane_t == rt[:, LANE_E2:LANE_E2 + 1], 1.0, 0.0)
        return a1, a2

    def rank_body(bi, carry):
        rows = pl.ds(pl.multiple_of(bi * tb, tb), tb)
        a1, a2 = onehots(rows)
        m = a1 + a2
        cum_scr[rows, :] = _dot(strict_lower, m.astype(BF16)) + carry
        return carry + jnp.sum(m, axis=0, keepdims=True)

    counts = lax.fori_loop(0, T // tb, rank_body, jnp.zeros((1, LANES), F32))
    ntiles = jnp.zeros_like(counts)
    for kt in range(-(-T // MOE_TILE)):
        ntiles = ntiles + jnp.where(counts > float(kt * MOE_TILE), 1.0, 0.0)
    rr = lax.broadcasted_iota(I32, (LANES, LANES), 0)
    cc = lax.broadcasted_iota(I32, (LANES, LANES), 1)
    upper = jnp.where(rr < cc, 1.0, 0.0).astype(BF16)
    tile0 = _dot(jnp.broadcast_to(ntiles, (SUBLANES, LANES)).astype(BF16), upper)[0:1, :]
    slot0 = tile0 * MOE_TILE

    def pos_body(bi, carry):
        rows = pl.ds(pl.multiple_of(bi * tb, tb), tb)
        a1, a2 = onehots(rows)
        base = cum_scr[rows, :] + slot0
        p1 = jnp.sum(a1 * base, axis=1, keepdims=True)
        p2 = jnp.sum(a2 * base, axis=1, keepdims=True)
        pos_ref[rows, :] = (jnp.where(lane_t == 0.0, p1, 0.0) + jnp.where(lane_t == 1.0, p2, 0.0)).astype(I32)
        return carry

    lax.fori_loop(0, T // tb, pos_body, 0)
    tend = tile0 + ntiles
    lane = lax.broadcasted_iota(I32, (LANES, LANES), 1)
    tile_i = lax.broadcasted_iota(I32, (LANES, LANES), 0).astype(F32)
    is_expert = lane < N_EXPERTS
    te = jnp.sum(jnp.where(is_expert & (jnp.broadcast_to(tend, (LANES, LANES)) <= tile_i), 1.0, 0.0),
                 axis=1, keepdims=True)
    te_ref[...] = jnp.broadcast_to(jnp.minimum(te, N_EXPERTS - 1.0), (LANES, LANES)).astype(I32)
    lane1 = lax.broadcasted_iota(I32, (1, LANES), 1)
    total = jnp.sum(jnp.where(lane1 < N_EXPERTS, ntiles, 0.0), axis=1, keepdims=True)
    tail = jnp.where(ntiles > 0.0, (tend - 1.0) * MOE_TILE, -1.0)
    tail = jnp.where(lane1 < N_EXPERTS, tail, jnp.where(lane1 == N_EXPERTS, total, 0.0))
    tail_ref[...] = jnp.broadcast_to(tail, (SUBLANES, LANES)).astype(I32)


def _plan(route, tb):
    T = route.shape[0]
    return pl.pallas_call(
        functools.partial(_plan_kernel, tb=tb),
        grid=(1,),
        in_specs=[pl.BlockSpec((T, LANES), lambda i: (0, 0))],
        out_specs=[
            pl.BlockSpec((T, LANES), lambda i: (0, 0)),
            pl.BlockSpec((LANES, LANES), lambda i: (0, 0)),
            pl.BlockSpec((SUBLANES, LANES), lambda i: (0, 0)),
        ],
        out_shape=[
            jax.ShapeDtypeStruct((T, LANES), I32),
            jax.ShapeDtypeStruct((LANES, LANES), I32),
            jax.ShapeDtypeStruct((SUBLANES, LANES), I32),
        ],
        scratch_shapes=[pltpu.VMEM((T, LANES), F32)],
        compiler_params=_params(("arbitrary",)),
        name="moe_plan",
    )(route)


def _row_group(ref, idx):
    return ref.at[pl.ds(pl.multiple_of(idx * ROW_GROUP, ROW_GROUP), ROW_GROUP), :]


def _dispatch_kernel(pos_ref, tail_ref, hn_ref, xs_ref, zero_scr, zsem, sem):
    i = pl.program_id(0)
    tm = hn_ref.shape[0] // ROW_GROUP
    tile_rows = MOE_TILE * ROW_GROUP

    @pl.when(i == 0)
    def _():
        zero_scr[...] = jnp.zeros_like(zero_scr)
        n_used = tail_ref[N_EXPERTS]

        def zero_copy(first_slot):
            start = pl.multiple_of(first_slot * ROW_GROUP, tile_rows)
            return pltpu.make_async_copy(zero_scr, xs_ref.at[pl.ds(start, tile_rows), :], zsem)

        def tails(fn):
            def body(e, carry):
                pl.when(tail_ref[e] >= 0)(lambda: fn(zero_copy(jnp.maximum(tail_ref[e], 0))))
                return carry
            lax.fori_loop(0, N_EXPERTS, body, 0)

        def unused(fn):
            def body(t, carry):
                fn(zero_copy(t * MOE_TILE))
                return carry
            lax.fori_loop(n_used, xs_ref.shape[0] // tile_rows, body, 0)

        tails(lambda c: c.start())
        unused(lambda c: c.start())
        tails(lambda c: c.wait())
        unused(lambda c: c.wait())

    def copy(r, k):
        slot = pos_ref[(i * tm + r) * TOP_K + k]
        return pltpu.make_async_copy(_row_group(hn_ref, r), _row_group(xs_ref, slot), sem)

    def start(r, carry):
        for k in range(TOP_K):
            copy(r, k).start(priority=k)
        return carry

    lax.fori_loop(0, tm, start, 0, unroll=8)
    for _ in range(TOP_K):
        pltpu.make_async_copy(hn_ref, xs_ref.at[pl.ds(0, tm * ROW_GROUP), :], sem).wait()


def _dispatch(pos, tails, hn_rows, n_slots, tm):
    T = hn_rows.shape[0] // ROW_GROUP
    return pl.pallas_call(
        _dispatch_kernel,
        grid_spec=pltpu.PrefetchScalarGridSpec(
            num_scalar_prefetch=2,
            grid=(T // tm,),
            in_specs=[pl.BlockSpec((tm * ROW_GROUP, LANES), lambda i, *_: (i, 0))],
            out_specs=pl.BlockSpec(memory_space=pl.ANY),
            scratch_shapes=[pltpu.VMEM((MOE_TILE * ROW_GROUP, LANES), BF16),
                            pltpu.SemaphoreType.DMA, pltpu.SemaphoreType.DMA],
        ),
        out_shape=jax.ShapeDtypeStruct((n_slots * ROW_GROUP, LANES), BF16),
        compiler_params=_params(("arbitrary",)),
        name="moe_dispatch",
    )(pos, tails, hn_rows)


def _experts_kernel(te_ref, nt_ref, xs_ref, wg_hbm, wu_hbm, wd_hbm, ys_ref,
                    wg_buf, wu_buf, wd_buf, slot_ref, sem):
    i = pl.program_id(0)
    nt = nt_ref[0]
    e = te_ref[i]

    def fetch(expert, slot):
        return [pltpu.make_async_copy(hbm.at[expert], buf.at[slot], sem.at[slot, n])
                for n, (hbm, buf) in enumerate(((wg_hbm, wg_buf), (wu_hbm, wu_buf), (wd_hbm, wd_buf)))]

    @pl.when(i == 0)
    def _():
        slot_ref[1] = 0
        for c in fetch(e, 0):
            c.start(priority=1)

    @pl.when((i < nt) & ((i == 0) | (te_ref[jnp.maximum(i - 1, 0)] != e)))
    def _():
        slot = slot_ref[1]
        for c in fetch(e, slot):
            c.wait()
        nxt = lax.while_loop(lambda j: (j < nt) & (te_ref[jnp.minimum(j, nt - 1)] == e), lambda j: j + 1, i + 1)

        @pl.when(nxt < nt)
        def _():
            for c in fetch(te_ref[jnp.minimum(nxt, nt - 1)], 1 - slot):
                c.start(priority=1)

        slot_ref[0] = slot
        slot_ref[1] = 1 - slot

    @pl.when(i < nt)
    def _():
        slot = slot_ref[0]
        x = _from_token_tiles(xs_ref[...])[0]
        hmid = (_silu(_dot(x, wg_buf[slot].astype(BF16))) * _dot(x, wu_buf[slot].astype(BF16))).astype(BF16)
        ys_ref[...] = _to_token_tiles(_dot(hmid, wd_buf[slot].astype(BF16)))

    @pl.when(i >= nt)
    def _():
        ys_ref[...] = jnp.zeros_like(ys_ref)


def _experts(te, nt, xs_rows, w_ge, w_ue, w_de, n_tiles):
    M = MOE_TILE
    rows = pl.BlockSpec((M * ROW_GROUP, LANES), lambda i, te_ref, nt_ref: (jnp.minimum(i, nt_ref[0] - 1), 0))
    hbm = pl.BlockSpec(memory_space=pl.ANY)
    up, down = (D_MODEL, EXPERT_FF), (EXPERT_FF, D_MODEL)
    return pl.pallas_call(
        _experts_kernel,
        grid_spec=pltpu.PrefetchScalarGridSpec(
            num_scalar_prefetch=2,
            grid=(n_tiles,),
            in_specs=[rows, hbm, hbm, hbm],
            out_specs=pl.BlockSpec((M * ROW_GROUP, LANES), lambda i, te_ref, nt_ref: (i, 0)),
            scratch_shapes=[pltpu.VMEM((2,) + up, F32), pltpu.VMEM((2,) + up, F32), pltpu.VMEM((2,) + down, F32),
                            pltpu.SMEM((2,), I32), pltpu.SemaphoreType.DMA((2, 3))],
        ),
        out_shape=jax.ShapeDtypeStruct(xs_rows.shape, F32),
        compiler_params=_params(("arbitrary",)),
        name="moe_experts",
    )(te, nt, xs_rows, w_ge, w_ue, w_de)


def _final_kernel(pos_ref, h_ref, route_ref, gain_ref, ys_ref, y_ref, buf, sem, *, tile0):
    i = pl.program_id(0)
    n = pl.num_programs(0)
    tm = h_ref.shape[0]

    def copy(step, r, k, slot_buf):
        slot = pos_ref[((tile0 + step) * tm + r) * TOP_K + k]
        return pltpu.make_async_copy(_row_group(ys_ref, slot), _row_group(buf.at[slot_buf], r * TOP_K + k),
                                     sem.at[slot_buf])

    def start_all(step, slot_buf):
        def body(r, carry):
            for k in range(TOP_K):
                copy(step, r, k, slot_buf).start(priority=k)
            return carry
        lax.fori_loop(0, tm, body, 0, unroll=8)

    cur = i % 2
    pl.when(i == 0)(lambda: start_all(0, 0))
    pl.when(i + 1 < n)(lambda: start_all(i + 1, 1 - cur))
    pltpu.make_async_copy(ys_ref.at[pl.ds(0, buf.shape[1]), :], buf.at[cur], sem.at[cur]).wait()

    rt = route_ref[...]
    w1 = rt[:, LANE_W1:LANE_W1 + 1]
    w2 = rt[:, LANE_W2:LANE_W2 + 1]
    o1, o2 = _from_token_tiles(buf[cur], groups=TOP_K)
    v = h_ref[...] + (w1 * o1 + w2 * o2)
    y_ref[...] = _rms(v) * gain_ref[...]


def _final(pos, h, route, gain, ys_rows, tile0, n_tiles, tm):
    return pl.pallas_call(
        functools.partial(_final_kernel, tile0=tile0),
        grid_spec=pltpu.PrefetchScalarGridSpec(
            num_scalar_prefetch=1,
            grid=(n_tiles,),
            in_specs=[
                pl.BlockSpec((tm, D_MODEL), lambda i, *_: (tile0 + i, 0)),
                pl.BlockSpec((tm, LANES), lambda i, *_: (tile0 + i, 0)),
                pl.BlockSpec((1, D_MODEL), lambda i, *_: (0, 0)),
                pl.BlockSpec(memory_space=pl.ANY),
            ],
            out_specs=pl.BlockSpec((tm, D_MODEL), lambda i, *_: (i, 0)),
            scratch_shapes=[pltpu.VMEM((2, tm * TOP_K * ROW_GROUP, LANES), F32), pltpu.SemaphoreType.DMA((2,))],
        ),
        out_shape=jax.ShapeDtypeStruct((n_tiles * tm, D_MODEL), F32),
        compiler_params=_params(("arbitrary",)),
        name="final",
    )(pos, h, route, gain, ys_rows)


def _rope_tables(pos):
    half = RET_HEAD_QK // 2
    inv = 1.0 / (ROPE_BASE ** jnp.linspace(0.0, 1.0, half, dtype=F32))
    ang = jnp.repeat(pos[:, None] * inv[None, :], 2, axis=-1)
    cos, sin = jnp.cos(ang), jnp.sin(ang)
    even = (jnp.arange(RET_HEAD_QK) % 2 == 0)[None, :]
    sa = jnp.where(even, -sin, 0.0)
    sb = jnp.where(even, 0.0, sin)
    return cos, sa, sb


def _tile_sizes(Tp, Ts, Lp):
    g = math.gcd(Tp, Ts)
    return dict(
        xnorm=math.gcd(g, 512),
        proj=(Tp + Ts) // 4,
        mix=math.gcd(g, 256),
        plan=math.gcd(Tp + Ts, 512),
        dispatch=math.gcd(g, 1024),
        final=math.gcd(g, 256),
        scan=math.gcd(Lp, 128),
    )


def kernel(x_prompt, x_sample, state_hgrn, state_ret, norm_mix, w_in, hg_lb_logits, hg_norm,
           w_branch_hg, w_branch_ret, b_gate, w_out, norm_ffn, w_router_group, b_router_group,
           w_router_expert, b_router_expert, w_expert_gate, w_expert_up, w_expert_down, norm_final):
    assert w_in.shape == (1, D_MODEL, IN_WIDTH), "single-layer trunk only"
    Bp, Lp, _ = x_prompt.shape
    Bs, Ls, _ = x_sample.shape
    Tp, Ts = Bp * Lp, Bs * Ls
    T = Tp + Ts
    ts = _tile_sizes(Tp, Ts, Lp)
    xp = x_prompt.reshape(Tp, D_MODEL)
    xs = x_sample.reshape(Ts, D_MODEL)

    lower = jnp.cumsum(jax.nn.softmax(hg_lb_logits.astype(F32), axis=0), axis=0)[0]
    act = _act_coefficients(lower, b_gate[0])
    tab_p = _rope_tables(jnp.arange(Lp, dtype=F32))
    tab_s = _rope_tables(jnp.arange(Ls, dtype=F32) + jnp.float32(PAST_LEN))
    tabs = tuple(jnp.concatenate([jnp.tile(a, (Bp, 1)), jnp.tile(b, (Bs, 1))]) for a, b in zip(tab_p, tab_s))
    xn = _xnorm(xp, xs, norm_mix, ts["xnorm"])
    P = _in_proj(xn, w_in[0], act, tabs, ts["proj"])

    bb = math.gcd(Bs, 16)
    o_hg_p, shp = _hgrn_prompt(P, hg_norm, Bp, Lp, ts["scan"])
    o_hg_s, shs = _hgrn_sample(P, hg_norm, state_hgrn[0].astype(F32), Bs, Ls, bb, Tp)
    o_r_p, srp = _ret_prompt(P, Bp, Lp, ts["scan"])
    o_r_s, srs = _ret_sample(P, state_ret[0].astype(F32), Bs, Ls, bb, Tp)

    wbh = w_branch_hg[0].astype(BF16)
    wbr = w_branch_ret[0].astype(BF16)
    wout = w_out[0].astype(BF16)
    pad = LANES - N_GROUPS - N_EXPERTS
    wr = jnp.concatenate([w_router_group[0], w_router_expert[0], jnp.zeros((D_MODEL, pad), F32)], axis=1)
    wr_hi = wr.astype(BF16)
    wr_hilo = jnp.concatenate([wr_hi, (wr - wr_hi.astype(F32)).astype(BF16)], axis=1)
    br = jnp.concatenate([b_router_group[0], b_router_expert[0], jnp.zeros((pad,), F32)])[None, :]
    h, hn_rows, route = _mix(list(o_hg_p) + [o_hg_s], list(o_r_p) + [o_r_s], P, xp, xs, wbh, wbr, wout,
                             norm_ffn, wr_hilo, br, ts["mix"])

    n_tiles = -(-T * TOP_K // MOE_TILE) + N_EXPERTS
    assert n_tiles <= LANES
    pos_slab, te_slab, tail_slab = _plan(route, ts["plan"])
    pos = pos_slab[:, :TOP_K].reshape(-1)
    te = te_slab[:n_tiles, 0]
    tails = tail_slab[0, :N_EXPERTS + 1]
    nt = tail_slab[0, N_EXPERTS:N_EXPERTS + 1]
    xs_rows = _dispatch(pos, tails, hn_rows, n_tiles * MOE_TILE, ts["dispatch"])
    ys_rows = _experts(te, nt, xs_rows, w_expert_gate[0], w_expert_up[0], w_expert_down[0], n_tiles)

    tf = ts["final"]
    gain = norm_final[None, :]
    y_p = _final(pos, h, route, gain, ys_rows, 0, Tp // tf, tf)
    y_s = _final(pos, h, route, gain, ys_rows, Tp // tf, Ts // tf, tf)
    return (y_p.reshape(Bp, Lp, D_MODEL), y_s.reshape(Bs, Ls, D_MODEL),
            shp[None], srp[None], shs[None], srs[None])
```

```python
import functools
import math

import jax
import jax.numpy as jnp
from jax import lax
from jax.experimental import pallas as pl
from jax.experimental.pallas import tpu as pltpu

F32 = jnp.float32
BF16 = jnp.bfloat16
I32 = jnp.int32

D_MODEL = 2048
PAST_LEN = 16384
HG_HEAD_V = 128
HG_EXPAND = 128
HG_VAL_DIM = D_MODEL // 2
HG_HEADS = HG_VAL_DIM // HG_HEAD_V
HG_KEY_DIM = HG_HEADS * HG_EXPAND
RET_HEAD_V = 256
RET_HEAD_QK = 128
RET_V_DIM = D_MODEL // 2
RET_HEADS = RET_V_DIM // RET_HEAD_V
RET_QK_DIM = RET_HEADS * RET_HEAD_QK
IN_WIDTH = 2 * HG_KEY_DIM + 2 * HG_VAL_DIM + 2 * RET_QK_DIM + 2 * RET_V_DIM + 2 * D_MODEL
SCAN_CHUNK = 128
N_GROUPS = 4
EXPERTS_PER_GROUP = 8
N_EXPERTS = N_GROUPS * EXPERTS_PER_GROUP
TOP_K = 2
EXPERT_FF = D_MODEL // 4
ROPE_BASE = 10000.0
EPS = 1e-6

LANES = 128
SUBLANES = 8
VMEM_LIMIT = 56 * 1024 * 1024
PROJ_MC = 256
PROJ_TN = 512
SEG_HQ, SEG_HF, SEG_HI, SEG_HG, SEG_RQ, SEG_RK, SEG_RV, SEG_RG, SEG_GA = 0, 2, 4, 6, 8, 9, 10, 12, 14
SAFE_EXP_SPAN = 80.0
ROW_GROUP = D_MODEL // LANES
MOE_TILE = 320
WEIGHT_SLOTS = 3
LANE_E1, LANE_E2, LANE_W1, LANE_W2 = 0, 1, 2, 3
ROUTE_EXPERT_LANE = N_GROUPS


def _params(sem):
    return pltpu.CompilerParams(dimension_semantics=sem, vmem_limit_bytes=VMEM_LIMIT)


def _silu(z):
    return z * jax.nn.sigmoid(z)


def _dot(a, b):
    return jnp.dot(a, b, preferred_element_type=F32)


def _dot_nt(a, b):
    return lax.dot_general(a, b, (((1,), (1,)), ((), ())), preferred_element_type=F32)


def _dot_tn(a, b):
    return lax.dot_general(a, b, (((0,), (0,)), ((), ())), preferred_element_type=F32)


def _rms(x):
    return x * lax.rsqrt(jnp.mean(x * x, axis=-1, keepdims=True) + EPS)


def _to_token_tiles(x):
    M = x.shape[0]
    pieces = jnp.stack([x[:, s * LANES:(s + 1) * LANES] for s in range(ROW_GROUP)], axis=0)
    return jnp.swapaxes(pieces, 0, 1).reshape(M * ROW_GROUP, LANES)


def _from_token_tiles(rows, groups=1):
    M = rows.shape[0] // (groups * ROW_GROUP)
    t = jnp.swapaxes(rows.reshape(M, groups * ROW_GROUP, LANES), 0, 1)
    return [jnp.concatenate([t[g * ROW_GROUP + s] for s in range(ROW_GROUP)], axis=1) for g in range(groups)]


def _xnorm_kernel(xp_ref, xs_ref, gain_ref, o_ref, *, n_prompt_tiles):
    i = pl.program_id(0)

    def norm(x_ref):
        o_ref[...] = (_rms(x_ref[...]) * gain_ref[...]).astype(BF16)

    pl.when(i < n_prompt_tiles)(lambda: norm(xp_ref))
    pl.when(i >= n_prompt_tiles)(lambda: norm(xs_ref))


def _xnorm(xp, xs, gain, tm):
    npt, nst = xp.shape[0] // tm, xs.shape[0] // tm
    return pl.pallas_call(
        functools.partial(_xnorm_kernel, n_prompt_tiles=npt),
        grid=(npt + nst,),
        in_specs=[
            pl.BlockSpec((tm, D_MODEL), lambda i: (jnp.minimum(i, npt - 1), 0)),
            pl.BlockSpec((tm, D_MODEL), lambda i: (jnp.maximum(i - npt, 0), 0)),
            pl.BlockSpec((1, D_MODEL), lambda i: (0, 0)),
        ],
        out_specs=pl.BlockSpec((tm, D_MODEL), lambda i: (i, 0)),
        out_shape=jax.ShapeDtypeStruct(((npt + nst) * tm, D_MODEL), BF16),
        compiler_params=_params(("parallel",)),
        name="xnorm",
    )(xp, xs, gain)


ACT_ALPHA, ACT_BETA, ACT_GAMMA, ACT_DELTA, ACT_EPS = 0, 1, 2, 3, 4


def _act_coefficients(lower, b_gate):
    z = jnp.zeros((IN_WIDTH,), F32)
    seg = lambda a, b: slice(a * PROJ_TN, b * PROJ_TN)
    alpha = z.at[seg(SEG_HI, SEG_HG)].set(1.0).at[seg(SEG_RQ, SEG_RG)].set(1.0)
    beta = z.at[seg(SEG_GA, IN_WIDTH // PROJ_TN)].set(b_gate)
    gamma = (z.at[seg(SEG_HQ, SEG_HF)].set(HG_EXPAND ** -0.5)
             .at[seg(SEG_HG, SEG_RQ)].set(1.0).at[seg(SEG_RG, SEG_GA)].set(1.0))
    delta = z.at[seg(SEG_HF, SEG_HI)].set(1.0 - lower).at[seg(SEG_GA, IN_WIDTH // PROJ_TN)].set(1.0)
    eps = z.at[seg(SEG_HF, SEG_HI)].set(lower)
    return jnp.stack([alpha, beta, gamma, delta, eps, z, z, z])


def _in_proj_kernel(x_ref, w_ref, act_ref, cos_ref, sa_ref, sb_ref, o_ref):
    j = pl.program_id(1)
    w = w_ref[...].astype(BF16)
    row = lambda r: act_ref[r:r + 1, :]
    tm = x_ref.shape[0]
    mc = math.gcd(tm, PROJ_MC)
    for m in range(0, tm, mc):
        p = _dot(x_ref[m:m + mc, :], w)
        o_ref[m:m + mc, :] = (row(ACT_ALPHA) * p + row(ACT_EPS)
                              + jax.nn.sigmoid(p + row(ACT_BETA)) * (row(ACT_GAMMA) * p + row(ACT_DELTA)))

    @pl.when((j >= SEG_RQ) & (j < SEG_RV))
    def _():
        scale = jnp.where(j == SEG_RK, RET_HEAD_QK ** -0.5, 1.0).astype(F32)
        cos, sa, sb = cos_ref[...], sa_ref[...], sb_ref[...]
        for hh in range(PROJ_TN // LANES):
            cols = slice(hh * LANES, (hh + 1) * LANES)
            xs = o_ref[:, cols]
            r = xs * cos + pltpu.roll(xs, LANES - 1, 1) * sa + pltpu.roll(xs, 1, 1) * sb
            o_ref[:, cols] = r * scale


def _in_proj(xn, w_in, act, tabs, tm):
    T = xn.shape[0]
    tab = pl.BlockSpec((tm, LANES), lambda i, j: (i, 0))
    return pl.pallas_call(
        _in_proj_kernel,
        grid=(T // tm, IN_WIDTH // PROJ_TN),
        in_specs=[
            pl.BlockSpec((tm, D_MODEL), lambda i, j: (i, 0)),
            pl.BlockSpec((D_MODEL, PROJ_TN), lambda i, j: (0, j)),
            pl.BlockSpec((SUBLANES, PROJ_TN), lambda i, j: (0, j)),
            tab, tab, tab,
        ],
        out_specs=pl.BlockSpec((tm, PROJ_TN), lambda i, j: (i, j)),
        out_shape=jax.ShapeDtypeStruct((T, IN_WIDTH), F32),
        compiler_params=_params(("parallel", "arbitrary")),
        name="in_proj",
    )(xn, w_in, act, *tabs)


def _prefix_matrix(C, G):
    r = lax.broadcasted_iota(I32, (C, 3 * C), 0)
    c = lax.broadcasted_iota(I32, (C, 3 * C), 1) % C
    return jnp.where((c <= r) & (c // G == r // G), 1.0, 0.0).astype(BF16)


def _prefix_sum(g, pm):
    g1 = g.astype(BF16)
    r1 = g - g1.astype(F32)
    g2 = r1.astype(BF16)
    g3 = (r1 - g2.astype(F32)).astype(BF16)
    return _dot(pm, jnp.concatenate([g1, g2, g3], axis=0))


def _state_decay_column(d_row):
    r = lax.broadcasted_iota(I32, (LANES, LANES), 0)
    c = lax.broadcasted_iota(I32, (LANES, LANES), 1)
    return jnp.sum(jnp.where(r == c, jnp.broadcast_to(d_row, (LANES, LANES)), 0.0), axis=1, keepdims=True)


def _hgrn_chunk(q, f, g, v, states, pm, tmp_ref, NG, factorised):
    C, W = q.shape
    G, H = C // NG, W // LANES
    hs = [slice(h * LANES, (h + 1) * LANES) for h in range(H)]
    gs = [slice(n * G, (n + 1) * G) for n in range(NG)]
    k = 1.0 - f
    b = _prefix_sum(g, pm)
    b3 = b.reshape(NG, G, W)
    b_end = b3[:, G - 1:G, :]
    q3, k3 = q.reshape(NG, G, W), k.reshape(NG, G, W)
    vb = v.astype(BF16)
    if factorised:
        b_mid = b3[:, G // 2 - 1:G // 2, :]
        qm3 = q3 * jnp.exp(b3 - b_mid)
        km3 = k3 * jnp.exp(b_mid - b3)
        qb = (qm3 * jnp.exp(b_mid)).reshape(C, W)
        kd = (km3 * jnp.exp(b_end - b_mid)).reshape(C, W)
        qm = qm3.reshape(C, W).astype(BF16)
        km = km3.reshape(C, W).astype(BF16)
        row = lax.broadcasted_iota(I32, (C, C), 0)
        col = lax.broadcasted_iota(I32, (C, C), 1)
        amask = (row >= col) & (row // G == col // G)
        o_intra = []
        for h in range(H):
            att = jnp.where(amask, _dot_nt(qm[:, hs[h]], km[:, hs[h]]), 0.0)
            o_intra.append(_dot(att.astype(BF16), vb[:, hs[h]]))
    else:
        qb = (q3 * jnp.exp(b3)).reshape(C, W)
        kd = (k3 * jnp.exp(b_end - b3)).reshape(C, W)
        trow = lax.broadcasted_iota(I32, (G, LANES), 0)
        o_intra = []
        for h in range(H):
            parts = []
            for n in range(NG):
                bh, qh = b[gs[n], hs[h]], q[gs[n], hs[h]]
                tmp_ref[0] = bh
                tmp_ref[1] = k[gs[n], hs[h]]
                tmp_ref[2] = v[gs[n], hs[h]]

                def body(s, acc, bh=bh, qh=qh):
                    bs = tmp_ref[0, pl.ds(s, 1), :]
                    ks = tmp_ref[1, pl.ds(s, 1), :]
                    vs = tmp_ref[2, pl.ds(s, 1), :]
                    w = jnp.where(trow >= s, jnp.exp(jnp.minimum(bh - bs, 0.0)), 0.0)
                    return acc + jnp.sum(qh * ks * w, axis=1, keepdims=True) * vs

                parts.append(lax.fori_loop(0, G, body, jnp.zeros((G, LANES), F32)))
            o_intra.append(parts[0] if NG == 1 else jnp.concatenate(parts, axis=0))
    d = jnp.exp(b_end)
    outs, new_states = [], [[None] * H for _ in range(NG)]
    for h in range(H):
        inter = []
        for n in range(NG):
            S = states[n][h]
            inter.append(_dot(qb[gs[n], hs[h]].astype(BF16), S.astype(BF16)))
            upd = _dot_tn(kd[gs[n], hs[h]].astype(BF16), v[gs[n], hs[h]].astype(BF16))
            new_states[n][h] = _state_decay_column(d[n, :, hs[h]]) * S + upd
        inter = inter[0] if NG == 1 else jnp.concatenate(inter, axis=0)
        outs.append(inter + o_intra[h])
    return jnp.concatenate(outs, axis=1), new_states


def _decay_is_safe(g, G):
    R, W = g.shape
    return jnp.min(jnp.sum(g.reshape(R // G, G, W), axis=1)) >= -SAFE_EXP_SPAN


def _hgrn_epilogue(o_scr, gain_ref, gate_ref, o_ref):
    o_ref[...] = (_rms(o_scr[...]) * gain_ref[...] * gate_ref[...]).astype(o_ref.dtype)


def _hgrn_prompt_kernel(*refs, C, NB):
    ins, (gain_ref,), rest = refs[:4 * NB], refs[4 * NB:4 * NB + 1], refs[4 * NB + 1:]
    o_refs, (st_ref, s_scr, g_scr, o_scr, tmp_ref) = rest[:NB], rest[NB:]
    seq = [ins[4 * n:4 * n + 4] for n in range(NB)]
    l = pl.program_id(0)
    Lb = o_refs[0].shape[0]

    @pl.when(l == 0)
    def _():
        s_scr[...] = jnp.zeros_like(s_scr)

    for n in range(NB):
        g_scr[n] = jnp.log(seq[n][1][...])
    pm = _prefix_matrix(C, C)

    def run(factorised):
        def body(ci, carry):
            rows = pl.ds(pl.multiple_of(ci * C, C), C)
            args = [(seq[n][0][rows, :], seq[n][1][rows, :], g_scr[n, rows, :], seq[n][2][rows, :],
                     [[s_scr[n, h] for h in range(HG_HEADS)]]) for n in range(NB)]
            outs = [_hgrn_chunk(*a, pm, tmp_ref, 1, factorised) for a in args]
            for n, (o, new) in enumerate(outs):
                o_scr[n, rows, :] = o
                for h in range(HG_HEADS):
                    s_scr[n, h] = new[0][h]
            return carry
        lax.fori_loop(0, Lb // C, body, 0)

    safe = _decay_is_safe(g_scr[...].reshape(NB * Lb, -1), C // 2)
    pl.when(safe)(lambda: run(True))
    pl.when(jnp.logical_not(safe))(lambda: run(False))
    for n in range(NB):
        o_refs[n][...] = (_rms(o_scr[n]) * gain_ref[...] * seq[n][3][...]).astype(o_refs[n].dtype)

    @pl.when(l == pl.num_programs(0) - 1)
    def _():
        st_ref[...] = s_scr[...]


def _hgrn_prompt(P, gain, B, L, Lb):
    C = math.gcd(L, SCAN_CHUNK)
    nl = L // Lb
    W = HG_KEY_DIM
    blk = lambda b, seg: pl.BlockSpec((Lb, W), lambda l, b=b, seg=seg: (b * nl + l, seg))
    outs = pl.pallas_call(
        functools.partial(_hgrn_prompt_kernel, C=C, NB=B),
        grid=(nl,),
        in_specs=[blk(b, seg) for b in range(B) for seg in range(4)] + [pl.BlockSpec((1, W), lambda l: (0, 0))],
        out_specs=[pl.BlockSpec((Lb, W), lambda l: (l, 0)) for _ in range(B)]
        + [pl.BlockSpec((B, HG_HEADS, HG_EXPAND, HG_HEAD_V), lambda l: (0, 0, 0, 0))],
        out_shape=[jax.ShapeDtypeStruct((L, HG_VAL_DIM), BF16) for _ in range(B)]
        + [jax.ShapeDtypeStruct((B, HG_HEADS, HG_EXPAND, HG_HEAD_V), F32)],
        scratch_shapes=[pltpu.VMEM((B, HG_HEADS, HG_EXPAND, HG_HEAD_V), F32), pltpu.VMEM((B, Lb, W), F32),
                        pltpu.VMEM((B, Lb, W), F32), pltpu.VMEM((3, C, LANES), F32)],
        compiler_params=_params(("arbitrary",)),
        name="hgrn_prompt",
    )(*([P] * (4 * B)), gain)
    return outs[:B], outs[B]


def _hgrn_sample_kernel(q_ref, f_ref, v_ref, gate_ref, gain_ref, s_ref, o_ref, so_ref, o_scr, tmp_ref, *, G):
    NG = s_ref.shape[0]
    g = jnp.log(f_ref[...])
    pm = _prefix_matrix(NG * G, G)

    def run(factorised):
        states = [[s_ref[n, h] for h in range(HG_HEADS)] for n in range(NG)]
        o, new = _hgrn_chunk(q_ref[...], f_ref[...], g, v_ref[...], states, pm, tmp_ref, NG, factorised)
        o_scr[...] = o
        for n in range(NG):
            for h in range(HG_HEADS):
                so_ref[n, h] = new[n][h]

    safe = _decay_is_safe(g, G // 2)
    pl.when(safe)(lambda: run(True))
    pl.when(jnp.logical_not(safe))(lambda: run(False))
    _hgrn_epilogue(o_scr, gain_ref, gate_ref, o_ref)


def _hgrn_sample(P, gain, state, Bs, Ls, Bb, row0):
    W = HG_KEY_DIM
    rows = Bb * Ls
    blk0 = row0 // rows

    def col(seg):
        return pl.BlockSpec((rows, W), lambda i, seg=seg: (blk0 + i, seg))

    st = pl.BlockSpec((Bb, HG_HEADS, HG_EXPAND, HG_HEAD_V), lambda i: (i, 0, 0, 0))
    return pl.pallas_call(
        functools.partial(_hgrn_sample_kernel, G=Ls),
        grid=(Bs // Bb,),
        in_specs=[col(0), col(1), col(2), col(3), pl.BlockSpec((1, W), lambda i: (0, 0)), st],
        out_specs=[pl.BlockSpec((rows, W), lambda i: (i, 0)), st],
        out_shape=[
            jax.ShapeDtypeStruct((Bs * Ls, HG_VAL_DIM), BF16),
            jax.ShapeDtypeStruct((Bs, HG_HEADS, HG_EXPAND, HG_HEAD_V), F32),
        ],
        scratch_shapes=[pltpu.VMEM((rows, W), F32), pltpu.VMEM((3, Ls, LANES), F32)],
        compiler_params=_params(("parallel",)),
        name="hgrn_sample",
    )(P, P, P, P, gain, state)


def _ret_chunk(q, k, v, states, dm_ref, qw_ref, kw_ref, sd_ref):
    outs, new_states = [], []
    C = q.shape[0]
    qb, kb, vb = q.astype(BF16), k.astype(BF16), v.astype(BF16)
    for h in range(RET_HEADS):
        qs = slice(h * RET_HEAD_QK, (h + 1) * RET_HEAD_QK)
        vs = slice(h * RET_HEAD_V, (h + 1) * RET_HEAD_V)
        att = (_dot_nt(qb[:, qs], kb[:, qs]) * dm_ref[h]).astype(BF16)
        kw = (k[:, qs] * kw_ref[h]).astype(BF16)
        if C <= 2 * SUBLANES:
            both = _dot(jnp.concatenate([att, kw.T], axis=0), vb[:, vs])
            intra, upd = both[:C], both[C:]
        else:
            intra, upd = _dot(att, vb[:, vs]), _dot_tn(kw, vb[:, vs])
        o = intra + _dot((q[:, qs] * qw_ref[h]).astype(BF16), states[h].astype(BF16))
        new_states.append(sd_ref[h, 0:1, 0:1] * states[h] + upd)
        outs.append(o)
    return jnp.concatenate(outs, axis=1), new_states


def _ret_epilogue(o_scr, gate_ref, o_ref):
    for h in range(RET_HEADS):
        vs = slice(h * RET_HEAD_V, (h + 1) * RET_HEAD_V)
        o_ref[:, vs] = (_rms(o_scr[:, vs]) * gate_ref[:, vs]).astype(o_ref.dtype)


def _ret_prompt_kernel(*refs, NB):
    ins, (dm_ref, qw_ref, kw_ref, sd_ref), rest = refs[:4 * NB], refs[4 * NB:4 * NB + 4], refs[4 * NB + 4:]
    o_refs, (st_ref, s_scr, o_scr) = rest[:NB], rest[NB:]
    seq = [ins[4 * n:4 * n + 4] for n in range(NB)]
    l = pl.program_id(0)
    Lb = o_refs[0].shape[0]
    C = dm_ref.shape[1]

    @pl.when(l == 0)
    def _():
        s_scr[...] = jnp.zeros_like(s_scr)

    def body(ci, carry):
        rows = pl.ds(pl.multiple_of(ci * C, C), C)
        args = [(seq[n][0][rows, :], seq[n][1][rows, :], seq[n][2][rows, :],
                 [s_scr[n, h] for h in range(RET_HEADS)]) for n in range(NB)]
        outs = [_ret_chunk(*a, dm_ref, qw_ref, kw_ref, sd_ref) for a in args]
        for n, (o, new) in enumerate(outs):
            o_scr[n, rows, :] = o
            for h in range(RET_HEADS):
                s_scr[n, h] = new[h]
        return carry

    lax.fori_loop(0, Lb // C, body, 0)
    for n in range(NB):
        _ret_epilogue(o_scr.at[n], seq[n][3], o_refs[n])

    @pl.when(l == pl.num_programs(0) - 1)
    def _():
        st_ref[...] = s_scr[...]


def _ret_tables(C):
    log_gamma = jnp.log(1.0 - jnp.exp2(-5.0 - jnp.arange(RET_HEADS, dtype=F32)))
    idx = jnp.arange(C, dtype=F32)
    rel = idx[:, None] - idx[None, :]
    tri = jnp.tril(jnp.ones((C, C), dtype=bool))
    dmat = jnp.exp(jnp.where(tri[None], log_gamma[:, None, None] * rel[None], -jnp.inf))
    qw = jnp.exp(log_gamma[:, None] * (idx[None, :] + 1.0))[..., None]
    kw = jnp.exp(log_gamma[:, None] * (C - 1.0 - idx[None, :]))[..., None]
    sdec = jnp.exp(log_gamma * C)[:, None, None]
    bc = lambda a: jnp.broadcast_to(a, (RET_HEADS, a.shape[1], LANES))
    return dmat, bc(qw), bc(kw), jnp.broadcast_to(sdec, (RET_HEADS, SUBLANES, LANES))


_QK0 = (2 * HG_KEY_DIM + 2 * HG_VAL_DIM) // RET_QK_DIM
_V0 = (2 * HG_KEY_DIM + 2 * HG_VAL_DIM + 2 * RET_QK_DIM) // RET_V_DIM


def _full3(a):
    return pl.BlockSpec(a.shape, lambda *_: (0, 0, 0))


def _ret_prompt(P, B, L, Lb):
    C = math.gcd(L, 2 * SCAN_CHUNK)
    Lb = max(Lb, C)
    nl = L // Lb
    tabs = _ret_tables(C)
    st_shape = (B, RET_HEADS, RET_HEAD_QK, RET_HEAD_V)
    blk = lambda b, w, c: pl.BlockSpec((Lb, w), lambda l, b=b, c=c: (b * nl + l, c))
    per_seq = lambda b: [blk(b, RET_QK_DIM, _QK0), blk(b, RET_QK_DIM, _QK0 + 1),
                         blk(b, RET_V_DIM, _V0), blk(b, RET_V_DIM, _V0 + 1)]
    outs = pl.pallas_call(
        functools.partial(_ret_prompt_kernel, NB=B),
        grid=(nl,),
        in_specs=[spec for b in range(B) for spec in per_seq(b)] + [_full3(t) for t in tabs],
        out_specs=[pl.BlockSpec((Lb, RET_V_DIM), lambda l: (l, 0)) for _ in range(B)]
        + [pl.BlockSpec(st_shape, lambda l: (0, 0, 0, 0))],
        out_shape=[jax.ShapeDtypeStruct((L, RET_V_DIM), BF16) for _ in range(B)]
        + [jax.ShapeDtypeStruct(st_shape, F32)],
        scratch_shapes=[pltpu.VMEM(st_shape, F32), pltpu.VMEM((B, Lb, RET_V_DIM), F32)],
        compiler_params=_params(("arbitrary",)),
        name="ret_prompt",
    )(*([P] * (4 * B)), *tabs)
    return outs[:B], outs[B]


def _ret_sample_kernel(q_ref, k_ref, v_ref, gate_ref, s_ref, dm_ref, qw_ref, kw_ref, sd_ref,
                       o_ref, so_ref, o_scr, *, C):
    Bb = s_ref.shape[0]
    args = [(q_ref[bb * C:(bb + 1) * C, :], k_ref[bb * C:(bb + 1) * C, :], v_ref[bb * C:(bb + 1) * C, :],
             [s_ref[bb, h] for h in range(RET_HEADS)]) for bb in range(Bb)]
    outs = [_ret_chunk(*a, dm_ref, qw_ref, kw_ref, sd_ref) for a in args]
    for bb, (o, new) in enumerate(outs):
        o_scr[bb * C:(bb + 1) * C, :] = o
        for h in range(RET_HEADS):
            so_ref[bb, h] = new[h]
    _ret_epilogue(o_scr, gate_ref, o_ref)


def _ret_sample(P, state, Bs, Ls, Bb, row0):
    tabs = _ret_tables(Ls)
    rows = Bb * Ls
    blk0 = row0 // rows
    st = pl.BlockSpec((Bb, RET_HEADS, RET_HEAD_QK, RET_HEAD_V), lambda i: (i, 0, 0, 0))
    return pl.pallas_call(
        functools.partial(_ret_sample_kernel, C=Ls),
        grid=(Bs // Bb,),
        in_specs=[
            pl.BlockSpec((rows, RET_QK_DIM), lambda i: (blk0 + i, _QK0)),
            pl.BlockSpec((rows, RET_QK_DIM), lambda i: (blk0 + i, _QK0 + 1)),
            pl.BlockSpec((rows, RET_V_DIM), lambda i: (blk0 + i, _V0)),
            pl.BlockSpec((rows, RET_V_DIM), lambda i: (blk0 + i, _V0 + 1)),
            st,
        ] + [_full3(t) for t in tabs],
        out_specs=[pl.BlockSpec((rows, RET_V_DIM), lambda i: (i, 0)), st],
        out_shape=[
            jax.ShapeDtypeStruct((Bs * Ls, RET_V_DIM), BF16),
            jax.ShapeDtypeStruct((Bs, RET_HEADS, RET_HEAD_QK, RET_HEAD_V), F32),
        ],
        scratch_shapes=[pltpu.VMEM((rows, RET_V_DIM), F32)],
        compiler_params=_params(("parallel",)),
        name="ret_sample",
    )(P, P, P, P, state, *tabs)


def _split_dot(a, w_hilo):
    a_hi = a.astype(BF16)
    a_lo = (a - a_hi.astype(F32)).astype(BF16)
    hi = _dot(a_hi, w_hilo)
    return hi[:, :LANES] + (hi[:, LANES:] + _dot(a_lo, w_hilo[:, :LANES]))


def _route(logits):
    lane = lax.broadcasted_iota(I32, logits.shape, 1)
    neg = jnp.float32(-jnp.inf)
    big = jnp.int32(LANES)
    gmask = lane < N_GROUPS
    gl = jnp.where(gmask, logits, neg)
    gmax = jnp.max(gl, axis=1, keepdims=True)
    gu = jnp.where(gmask, jnp.exp(gl - gmax), 0.0)
    gp = gu / jnp.sum(gu, axis=1, keepdims=True)
    g_w = jnp.max(gp, axis=1, keepdims=True)
    g_i = jnp.min(jnp.where(gmask & (gp == g_w), lane, big), axis=1, keepdims=True)
    lo = ROUTE_EXPERT_LANE + g_i * EXPERTS_PER_GROUP
    emask = (lane >= lo) & (lane < lo + EXPERTS_PER_GROUP)
    el = jnp.where(emask, logits, neg)
    e1 = jnp.max(el, axis=1, keepdims=True)
    i1 = jnp.min(jnp.where(emask & (el == e1), lane, big), axis=1, keepdims=True)
    el2 = jnp.where(lane == i1, neg, el)
    e2 = jnp.max(el2, axis=1, keepdims=True)
    i2 = jnp.min(jnp.where(emask & (lane != i1) & (el2 == e2), lane, big), axis=1, keepdims=True)
    u2 = jnp.exp(e2 - e1)
    den = 1.0 + u2
    w1 = (1.0 / den) * g_w
    w2 = (u2 / den) * g_w
    id1 = (i1 - ROUTE_EXPERT_LANE).astype(F32)
    id2 = (i2 - ROUTE_EXPERT_LANE).astype(F32)
    return (jnp.where(lane == LANE_E1, id1, 0.0) + jnp.where(lane == LANE_E2, id2, 0.0)
            + jnp.where(lane == LANE_W1, w1, 0.0) + jnp.where(lane == LANE_W2, w2, 0.0))


def _pick(i, bounds, refs):
    val = refs[-1][...]
    for n in range(len(refs) - 2, -1, -1):
        val = jnp.where(i < bounds[n + 1], refs[n][...], val)
    return val


def _mix_kernel(*refs, bounds):
    ns = len(bounds) - 1
    ohg_refs, or_refs, refs = refs[:ns], refs[ns:2 * ns], refs[2 * ns:]
    (ga0_ref, ga1_ref, gb0_ref, gb1_ref, xp_ref, xs_ref, wbh_ref, wbr_ref, wout_ref, nffn_ref,
     wr_ref, br_ref, h_ref, hn_ref, route_ref) = refs
    i = pl.program_id(0)
    tm = h_ref.shape[0]
    is_prompt = i < bounds[-2]
    p_hg = _dot(_pick(i, bounds, ohg_refs), wbh_ref[...])
    p_r = _dot(_pick(i, bounds, or_refs), wbr_ref[...])
    half = D_MODEL // 2
    m0 = ga0_ref[...] * p_hg[:, :half] + gb0_ref[...] * p_r[:, :half]
    m1 = ga1_ref[...] * p_hg[:, half:] + gb1_ref[...] * p_r[:, half:]
    mixed = jnp.concatenate([m0, m1], axis=1).astype(BF16)
    x = jnp.where(is_prompt, xp_ref[...], xs_ref[...])
    h = x + _dot(mixed, wout_ref[...])
    h_ref[...] = h
    hn = _rms(h) * nffn_ref[...]
    hn_ref[...] = _to_token_tiles(hn.astype(BF16))
    route_ref[...] = _route(_split_dot(hn, wr_ref[...]) + br_ref[...])


def _mix(o_hg, o_r, P, xp, xs, wbh, wbr, wout, norm_ffn, wr, br, tm):
    npt, nst = xp.shape[0] // tm, xs.shape[0] // tm
    bounds = [0]
    for a in o_hg:
        bounds.append(bounds[-1] + a.shape[0] // tm)
    T = (npt + nst) * tm
    half = D_MODEL // 2
    ga = (IN_WIDTH - 2 * D_MODEL) // half

    def pcol(c):
        return pl.BlockSpec((tm, half), lambda i, c=c: (i, c))

    def const(a):
        return pl.BlockSpec(a.shape, lambda i: (0, 0), pipeline_mode=pl.Buffered(1))

    def prompt(w):
        return pl.BlockSpec((tm, w), lambda i: (jnp.minimum(i, npt - 1), 0))

    def sample(w):
        return pl.BlockSpec((tm, w), lambda i: (jnp.maximum(i - npt, 0), 0))

    def source(n):
        lo, hi = bounds[n], bounds[n + 1]
        return pl.BlockSpec((tm, half), lambda i: (jnp.clip(i - lo, 0, hi - lo - 1), 0))

    sources = [source(n) for n in range(len(o_hg))]
    return pl.pallas_call(
        functools.partial(_mix_kernel, bounds=tuple(bounds)),
        grid=(npt + nst,),
        in_specs=sources + sources + [
            pcol(ga), pcol(ga + 1), pcol(ga + 2), pcol(ga + 3),
            prompt(D_MODEL), sample(D_MODEL),
            const(wbh), const(wbr), const(wout), const(norm_ffn), const(wr), const(br),
        ],
        out_specs=[
            pl.BlockSpec((tm, D_MODEL), lambda i: (i, 0)),
            pl.BlockSpec((tm * ROW_GROUP, LANES), lambda i: (i, 0)),
            pl.BlockSpec((tm, LANES), lambda i: (i, 0)),
        ],
        out_shape=[
            jax.ShapeDtypeStruct((T, D_MODEL), F32),
            jax.ShapeDtypeStruct((T * ROW_GROUP, LANES), BF16),
            jax.ShapeDtypeStruct((T, LANES), F32),
        ],
        compiler_params=_params(("parallel",)),
        name="mix",
    )(*o_hg, *o_r, P, P, P, P, xp, xs, wbh, wbr, wout, norm_ffn, wr, br)


def _plan_kernel(route_ref, pos_ref, te_ref, tail_ref, cum_scr, *, tb):
    T = route_ref.shape[0]
    lane_t = lax.broadcasted_iota(I32, (tb, LANES), 1).astype(F32)
    r = lax.broadcasted_iota(I32, (tb, tb), 0)
    c = lax.broadcasted_iota(I32, (tb, tb), 1)
    strict_lower = jnp.where(r > c, 1.0, 0.0).astype(BF16)

    def onehots(rows):
        rt = route_ref[rows, :]
        a1 = jnp.where(lane_t == rt[:, LANE_E1:LANE_E1 + 1], 1.0, 0.0)
        a2 = jnp.where(lane_t == rt[:, LANE_E2:LANE_E2 + 1], 1.0, 0.0)
        return a1, a2

    def rank_body(bi, carry):
        rows = pl.ds(pl.multiple_of(bi * tb, tb), tb)
        a1, a2 = onehots(rows)
        m = a1 + a2
        cum_scr[rows, :] = _dot(strict_lower, m.astype(BF16)) + carry
        return carry + jnp.sum(m, axis=0, keepdims=True)

    counts = lax.fori_loop(0, T // tb, rank_body, jnp.zeros((1, LANES), F32))
    ntiles = jnp.zeros_like(counts)
    for kt in range(-(-T // MOE_TILE)):
        ntiles = ntiles + jnp.where(counts > float(kt * MOE_TILE), 1.0, 0.0)
    rr = lax.broadcasted_iota(I32, (LANES, LANES), 0)
    cc = lax.broadcasted_iota(I32, (LANES, LANES), 1)
    upper = jnp.where(rr < cc, 1.0, 0.0).astype(BF16)
    tile0 = _dot(jnp.broadcast_to(ntiles, (SUBLANES, LANES)).astype(BF16), upper)[0:1, :]
    slot0 = tile0 * MOE_TILE

    def pos_body(bi, carry):
        rows = pl.ds(pl.multiple_of(bi * tb, tb), tb)
        a1, a2 = onehots(rows)
        base = cum_scr[rows, :] + slot0
        p1 = jnp.sum(a1 * base, axis=1, keepdims=True)
        p2 = jnp.sum(a2 * base, axis=1, keepdims=True)
        pos_ref[rows, :] = (jnp.where(lane_t == 0.0, p1, 0.0) + jnp.where(lane_t == 1.0, p2, 0.0)).astype(I32)
        return carry

    lax.fori_loop(0, T // tb, pos_body, 0)
    tend = tile0 + ntiles
    lane = lax.broadcasted_iota(I32, (LANES, LANES), 1)
    tile_i = lax.broadcasted_iota(I32, (LANES, LANES), 0).astype(F32)
    is_expert = lane < N_EXPERTS
    te = jnp.sum(jnp.where(is_expert & (jnp.broadcast_to(tend, (LANES, LANES)) <= tile_i), 1.0, 0.0),
                 axis=1, keepdims=True)
    te_ref[...] = jnp.broadcast_to(jnp.minimum(te, N_EXPERTS - 1.0), (LANES, LANES)).astype(I32)
    lane1 = lax.broadcasted_iota(I32, (1, LANES), 1)
    total = jnp.sum(jnp.where(lane1 < N_EXPERTS, ntiles, 0.0), axis=1, keepdims=True)
    tail = jnp.where(ntiles > 0.0, (tend - 1.0) * MOE_TILE, -1.0)
    tail = jnp.where(lane1 < N_EXPERTS, tail, jnp.where(lane1 == N_EXPERTS, total, 0.0))
    tail_ref[...] = jnp.broadcast_to(tail, (SUBLANES, LANES)).astype(I32)


def _plan(route, tb):
    T = route.shape[0]
    return pl.pallas_call(
        functools.partial(_plan_kernel, tb=tb),
        grid=(1,),
        in_specs=[pl.BlockSpec((T, LANES), lambda i: (0, 0))],
        out_specs=[
            pl.BlockSpec((T, LANES), lambda i: (0, 0)),
            pl.BlockSpec((LANES, LANES), lambda i: (0, 0)),
            pl.BlockSpec((SUBLANES, LANES), lambda i: (0, 0)),
        ],
        out_shape=[
            jax.ShapeDtypeStruct((T, LANES), I32),
            jax.ShapeDtypeStruct((LANES, LANES), I32),
            jax.ShapeDtypeStruct((SUBLANES, LANES), I32),
        ],
        scratch_shapes=[pltpu.VMEM((T, LANES), F32)],
        compiler_params=_params(("arbitrary",)),
        name="moe_plan",
    )(route)


def _row_group(ref, idx):
    return ref.at[pl.ds(pl.multiple_of(idx * ROW_GROUP, ROW_GROUP), ROW_GROUP), :]


def _dispatch_kernel(pos_ref, tail_ref, hn_ref, xs_ref, zero_scr, zsem, sem):
    i = pl.program_id(0)
    tm = hn_ref.shape[0] // ROW_GROUP
    tile_rows = MOE_TILE * ROW_GROUP

    @pl.when(i == 0)
    def _():
        zero_scr[...] = jnp.zeros_like(zero_scr)
        n_used = tail_ref[N_EXPERTS]

        def zero_copy(first_slot):
            start = pl.multiple_of(first_slot * ROW_GROUP, tile_rows)
            return pltpu.make_async_copy(zero_scr, xs_ref.at[pl.ds(start, tile_rows), :], zsem)

        def tails(fn):
            def body(e, carry):
                pl.when(tail_ref[e] >= 0)(lambda: fn(zero_copy(jnp.maximum(tail_ref[e], 0))))
                return carry
            lax.fori_loop(0, N_EXPERTS, body, 0)

        def unused(fn):
            def body(t, carry):
                fn(zero_copy(t * MOE_TILE))
                return carry
            lax.fori_loop(n_used, xs_ref.shape[0] // tile_rows, body, 0)

        tails(lambda c: c.start())
        unused(lambda c: c.start())
        tails(lambda c: c.wait())
        unused(lambda c: c.wait())

    def copy(r, k):
        slot = pos_ref[(i * tm + r) * TOP_K + k]
        return pltpu.make_async_copy(_row_group(hn_ref, r), _row_group(xs_ref, slot), sem)

    def start(r, carry):
        for k in range(TOP_K):
            copy(r, k).start(priority=k)
        return carry

    lax.fori_loop(0, tm, start, 0, unroll=8)
    for _ in range(TOP_K):
        pltpu.make_async_copy(hn_ref, xs_ref.at[pl.ds(0, tm * ROW_GROUP), :], sem).wait()


def _dispatch(pos, tails, hn_rows, n_slots, tm):
    T = hn_rows.shape[0] // ROW_GROUP
    return pl.pallas_call(
        _dispatch_kernel,
        grid_spec=pltpu.PrefetchScalarGridSpec(
            num_scalar_prefetch=2,
            grid=(T // tm,),
            in_specs=[pl.BlockSpec((tm * ROW_GROUP, LANES), lambda i, *_: (i, 0))],
            out_specs=pl.BlockSpec(memory_space=pl.ANY),
            scratch_shapes=[pltpu.VMEM((MOE_TILE * ROW_GROUP, LANES), BF16),
                            pltpu.SemaphoreType.DMA, pltpu.SemaphoreType.DMA],
        ),
        out_shape=jax.ShapeDtypeStruct((n_slots * ROW_GROUP, LANES), BF16),
        compiler_params=_params(("arbitrary",)),
        name="moe_dispatch",
    )(pos, tails, hn_rows)


def _experts_kernel(te_ref, nt_ref, xs_ref, wg_hbm, wu_hbm, wd_hbm, ys_ref,
                    wg_buf, wu_buf, wd_buf, rank_ref, sem):
    i = pl.program_id(0)
    nt = nt_ref[0]
    e = te_ref[i]

    def fetch(expert, slot):
        return [pltpu.make_async_copy(hbm.at[expert], buf.at[slot], sem.at[slot, n])
                for n, (hbm, buf) in enumerate(((wg_hbm, wg_buf), (wu_hbm, wu_buf), (wd_hbm, wd_buf)))]

    def start_fetch(tile, slot):
        for c in fetch(te_ref[jnp.minimum(tile, nt - 1)], slot):
            c.start(priority=1)

    def next_change(tile):
        ex = te_ref[jnp.minimum(tile, nt - 1)]
        return lax.while_loop(lambda j: (j < nt) & (te_ref[jnp.minimum(j, nt - 1)] == ex), lambda j: j + 1, tile + 1)

    @pl.when(i == 0)
    def _():
        rank_ref[0] = -1
        start_fetch(0, 0)
        n1 = next_change(0)
        pl.when(n1 < nt)(lambda: start_fetch(n1, 1))

    @pl.when((i < nt) & ((i == 0) | (te_ref[jnp.maximum(i - 1, 0)] != e)))
    def _():
        rank = rank_ref[0] + 1
        rank_ref[0] = rank
        for c in fetch(e, rank % WEIGHT_SLOTS):
            c.wait()
        n1 = next_change(i)

        @pl.when(n1 < nt)
        def _():
            n2 = next_change(n1)
            pl.when(n2 < nt)(lambda: start_fetch(n2, (rank + 2) % WEIGHT_SLOTS))

    @pl.when(i < nt)
    def _():
        slot = rank_ref[0] % WEIGHT_SLOTS
        x = _from_token_tiles(xs_ref[...])[0]
        hmid = (_silu(_dot(x, wg_buf[slot].astype(BF16))) * _dot(x, wu_buf[slot].astype(BF16))).astype(BF16)
        ys_ref[...] = _to_token_tiles(_dot(hmid, wd_buf[slot].astype(BF16)))

    @pl.when(i >= nt)
    def _():
        ys_ref[...] = jnp.zeros_like(ys_ref)


def _experts(te, nt, xs_rows, w_ge, w_ue, w_de, n_tiles):
    M = MOE_TILE
    rows = pl.BlockSpec((M * ROW_GROUP, LANES), lambda i, te_ref, nt_ref: (jnp.minimum(i, nt_ref[0] - 1), 0))
    hbm = pl.BlockSpec(memory_space=pl.ANY)
    up, down = (D_MODEL, EXPERT_FF), (EXPERT_FF, D_MODEL)
    return pl.pallas_call(
        _experts_kernel,
        grid_spec=pltpu.PrefetchScalarGridSpec(
            num_scalar_prefetch=2,
            grid=(n_tiles,),
            in_specs=[rows, hbm, hbm, hbm],
            out_specs=pl.BlockSpec((M * ROW_GROUP, LANES), lambda i, te_ref, nt_ref: (i, 0)),
            scratch_shapes=[pltpu.VMEM((WEIGHT_SLOTS,) + up, F32), pltpu.VMEM((WEIGHT_SLOTS,) + up, F32),
                            pltpu.VMEM((WEIGHT_SLOTS,) + down, F32),
                            pltpu.SMEM((1,), I32), pltpu.SemaphoreType.DMA((WEIGHT_SLOTS, 3))],
        ),
        out_shape=jax.ShapeDtypeStruct(xs_rows.shape, F32),
        compiler_params=_params(("arbitrary",)),
        name="moe_experts",
    )(te, nt, xs_rows, w_ge, w_ue, w_de)


def _final_kernel(pos_ref, h_ref, route_ref, gain_ref, ys_ref, y_ref, buf, sem, *, tile0):
    i = pl.program_id(0)
    n = pl.num_programs(0)
    tm = h_ref.shape[0]

    def copy(step, r, k, slot_buf):
        slot = pos_ref[((tile0 + step) * tm + r) * TOP_K + k]
        return pltpu.make_async_copy(_row_group(ys_ref, slot), _row_group(buf.at[slot_buf], r * TOP_K + k),
                                     sem.at[slot_buf])

    def start_all(step, slot_buf):
        def body(r, carry):
            for k in range(TOP_K):
                copy(step, r, k, slot_buf).start(priority=k)
            return carry
        lax.fori_loop(0, tm, body, 0, unroll=8)

    cur = i % 2
    pl.when(i == 0)(lambda: start_all(0, 0))
    pl.when(i + 1 < n)(lambda: start_all(i + 1, 1 - cur))
    pltpu.make_async_copy(ys_ref.at[pl.ds(0, buf.shape[1]), :], buf.at[cur], sem.at[cur]).wait()

    rt = route_ref[...]
    w1 = rt[:, LANE_W1:LANE_W1 + 1]
    w2 = rt[:, LANE_W2:LANE_W2 + 1]
    o1, o2 = _from_token_tiles(buf[cur], groups=TOP_K)
    v = h_ref[...] + (w1 * o1 + w2 * o2)
    y_ref[...] = _rms(v) * gain_ref[...]


def _final(pos, h, route, gain, ys_rows, tile0, n_tiles, tm):
    return pl.pallas_call(
        functools.partial(_final_kernel, tile0=tile0),
        grid_spec=pltpu.PrefetchScalarGridSpec(
            num_scalar_prefetch=1,
            grid=(n_tiles,),
            in_specs=[
                pl.BlockSpec((tm, D_MODEL), lambda i, *_: (tile0 + i, 0)),
                pl.BlockSpec((tm, LANES), lambda i, *_: (tile0 + i, 0)),
                pl.BlockSpec((1, D_MODEL), lambda i, *_: (0, 0)),
                pl.BlockSpec(memory_space=pl.ANY),
            ],
            out_specs=pl.BlockSpec((tm, D_MODEL), lambda i, *_: (i, 0)),
            scratch_shapes=[pltpu.VMEM((2, tm * TOP_K * ROW_GROUP, LANES), F32), pltpu.SemaphoreType.DMA((2,))],
        ),
        out_shape=jax.ShapeDtypeStruct((n_tiles * tm, D_MODEL), F32),
        compiler_params=_params(("arbitrary",)),
        name="final",
    )(pos, h, route, gain, ys_rows)


def _rope_tables(pos):
    half = RET_HEAD_QK // 2
    inv = 1.0 / (ROPE_BASE ** jnp.linspace(0.0, 1.0, half, dtype=F32))
    ang = jnp.repeat(pos[:, None] * inv[None, :], 2, axis=-1)
    cos, sin = jnp.cos(ang), jnp.sin(ang)
    even = (jnp.arange(RET_HEAD_QK) % 2 == 0)[None, :]
    sa = jnp.where(even, -sin, 0.0)
    sb = jnp.where(even, 0.0, sin)
    return cos, sa, sb


def _tile_sizes(Tp, Ts, Lp):
    g = math.gcd(Tp, Ts)
    return dict(
        xnorm=math.gcd(g, 512),
        proj=(Tp + Ts) // 4,
        mix=math.gcd(g, 256),
        plan=math.gcd(Tp + Ts, 512),
        dispatch=math.gcd(g, 1024),
        final=math.gcd(g, 256),
        scan=math.gcd(Lp, 128),
    )


def kernel(x_prompt, x_sample, state_hgrn, state_ret, norm_mix, w_in, hg_lb_logits, hg_norm,
           w_branch_hg, w_branch_ret, b_gate, w_out, norm_ffn, w_router_group, b_router_group,
           w_router_expert, b_router_expert, w_expert_gate, w_expert_up, w_expert_down, norm_final):
    assert w_in.shape == (1, D_MODEL, IN_WIDTH), "single-layer trunk only"
    Bp, Lp, _ = x_prompt.shape
    Bs, Ls, _ = x_sample.shape
    Tp, Ts = Bp * Lp, Bs * Ls
    T = Tp + Ts
    ts = _tile_sizes(Tp, Ts, Lp)
    xp = x_prompt.reshape(Tp, D_MODEL)
    xs = x_sample.reshape(Ts, D_MODEL)

    lower = jnp.cumsum(jax.nn.softmax(hg_lb_logits.astype(F32), axis=0), axis=0)[0]
    act = _act_coefficients(lower, b_gate[0])
    tab_p = _rope_tables(jnp.arange(Lp, dtype=F32))
    tab_s = _rope_tables(jnp.arange(Ls, dtype=F32) + jnp.float32(PAST_LEN))
    tabs = tuple(jnp.concatenate([jnp.tile(a, (Bp, 1)), jnp.tile(b, (Bs, 1))]) for a, b in zip(tab_p, tab_s))
    xn = _xnorm(xp, xs, norm_mix, ts["xnorm"])
    P = _in_proj(xn, w_in[0], act, tabs, ts["proj"])

    bb = math.gcd(Bs, 16)
    o_hg_p, shp = _hgrn_prompt(P, hg_norm, Bp, Lp, ts["scan"])
    o_hg_s, shs = _hgrn_sample(P, hg_norm, state_hgrn[0].astype(F32), Bs, Ls, bb, Tp)
    o_r_p, srp = _ret_prompt(P, Bp, Lp, ts["scan"])
    o_r_s, srs = _ret_sample(P, state_ret[0].astype(F32), Bs, Ls, bb, Tp)

    wbh = w_branch_hg[0].astype(BF16)
    wbr = w_branch_ret[0].astype(BF16)
    wout = w_out[0].astype(BF16)
    pad = LANES - N_GROUPS - N_EXPERTS
    wr = jnp.concatenate([w_router_group[0], w_router_expert[0], jnp.zeros((D_MODEL, pad), F32)], axis=1)
    wr_hi = wr.astype(BF16)
    wr_hilo = jnp.concatenate([wr_hi, (wr - wr_hi.astype(F32)).astype(BF16)], axis=1)
    br = jnp.concatenate([b_router_group[0], b_router_expert[0], jnp.zeros((pad,), F32)])[None, :]
    h, hn_rows, route = _mix(list(o_hg_p) + [o_hg_s], list(o_r_p) + [o_r_s], P, xp, xs, wbh, wbr, wout,
                             norm_ffn, wr_hilo, br, ts["mix"])

    n_tiles = -(-T * TOP_K // MOE_TILE) + N_EXPERTS
    assert n_tiles <= LANES
    pos_slab, te_slab, tail_slab = _plan(route, ts["plan"])
    pos = pos_slab[:, :TOP_K].reshape(-1)
    te = te_slab[:n_tiles, 0]
    tails = tail_slab[0, :N_EXPERTS + 1]
    nt = tail_slab[0, N_EXPERTS:N_EXPERTS + 1]
    xs_rows = _dispatch(pos, tails, hn_rows, n_tiles * MOE_TILE, ts["dispatch"])
    ys_rows = _experts(te, nt, xs_rows, w_expert_gate[0], w_expert_up[0], w_expert_down[0], n_tiles)

    tf = ts["final"]
    gain = norm_final[None, :]
    y_p = _final(pos, h, route, gain, ys_rows, 0, Tp // tf, tf)
    y_s = _final(pos, h, route, gain, ys_rows, Tp // tf, Ts // tf, tf)
    return (y_p.reshape(Bp, Lp, D_MODEL), y_s.reshape(Bs, Ls, D_MODEL),
            shp[None], srp[None], shs[None], srs[None])
```

```python
import functools
import math

import jax
import jax.numpy as jnp
from jax import lax
from jax.experimental import pallas as pl
from jax.experimental.pallas import tpu as pltpu

F32 = jnp.float32
BF16 = jnp.bfloat16
I32 = jnp.int32

D_MODEL = 2048
PAST_LEN = 16384
HG_HEAD_V = 128
HG_EXPAND = 128
HG_VAL_DIM = D_MODEL // 2
HG_HEADS = HG_VAL_DIM // HG_HEAD_V
HG_KEY_DIM = HG_HEADS * HG_EXPAND
RET_HEAD_V = 256
RET_HEAD_QK = 128
RET_V_DIM = D_MODEL // 2
RET_HEADS = RET_V_DIM // RET_HEAD_V
RET_QK_DIM = RET_HEADS * RET_HEAD_QK
IN_WIDTH = 2 * HG_KEY_DIM + 2 * HG_VAL_DIM + 2 * RET_QK_DIM + 2 * RET_V_DIM + 2 * D_MODEL
SCAN_CHUNK = 128
N_GROUPS = 4
EXPERTS_PER_GROUP = 8
N_EXPERTS = N_GROUPS * EXPERTS_PER_GROUP
TOP_K = 2
EXPERT_FF = D_MODEL // 4
ROPE_BASE = 10000.0
EPS = 1e-6

LANES = 128
SUBLANES = 8
VMEM_LIMIT = 56 * 1024 * 1024
PROJ_MC = 256
PROJ_TN = 512
SEG_HQ, SEG_HF, SEG_HI, SEG_HG, SEG_RQ, SEG_RK, SEG_RV, SEG_RG, SEG_GA = 0, 2, 4, 6, 8, 9, 10, 12, 14
SAFE_EXP_SPAN = 80.0
ROW_GROUP = D_MODEL // LANES
MOE_TILE = 320
WEIGHT_SLOTS = 3
LANE_E1, LANE_E2, LANE_W1, LANE_W2 = 0, 1, 2, 3
ROUTE_EXPERT_LANE = N_GROUPS


def _params(sem):
    return pltpu.CompilerParams(dimension_semantics=sem, vmem_limit_bytes=VMEM_LIMIT)


def _silu(z):
    return z * jax.nn.sigmoid(z)


def _dot(a, b):
    return jnp.dot(a, b, preferred_element_type=F32)


def _dot_nt(a, b):
    return lax.dot_general(a, b, (((1,), (1,)), ((), ())), preferred_element_type=F32)


def _dot_tn(a, b):
    return lax.dot_general(a, b, (((0,), (0,)), ((), ())), preferred_element_type=F32)


def _rms(x):
    return x * lax.rsqrt(jnp.mean(x * x, axis=-1, keepdims=True) + EPS)


def _to_token_tiles(x):
    M = x.shape[0]
    pieces = jnp.stack([x[:, s * LANES:(s + 1) * LANES] for s in range(ROW_GROUP)], axis=0)
    return jnp.swapaxes(pieces, 0, 1).reshape(M * ROW_GROUP, LANES)


def _from_token_tiles(rows, groups=1):
    M = rows.shape[0] // (groups * ROW_GROUP)
    t = jnp.swapaxes(rows.reshape(M, groups * ROW_GROUP, LANES), 0, 1)
    return [jnp.concatenate([t[g * ROW_GROUP + s] for s in range(ROW_GROUP)], axis=1) for g in range(groups)]


def _xnorm_kernel(xp_ref, xs_ref, gain_ref, o_ref, *, n_prompt_tiles):
    i = pl.program_id(0)

    def norm(x_ref):
        o_ref[...] = (_rms(x_ref[...]) * gain_ref[...]).astype(BF16)

    pl.when(i < n_prompt_tiles)(lambda: norm(xp_ref))
    pl.when(i >= n_prompt_tiles)(lambda: norm(xs_ref))


def _xnorm(xp, xs, gain, tm):
    npt, nst = xp.shape[0] // tm, xs.shape[0] // tm
    return pl.pallas_call(
        functools.partial(_xnorm_kernel, n_prompt_tiles=npt),
        grid=(npt + nst,),
        in_specs=[
            pl.BlockSpec((tm, D_MODEL), lambda i: (jnp.minimum(i, npt - 1), 0)),
            pl.BlockSpec((tm, D_MODEL), lambda i: (jnp.maximum(i - npt, 0), 0)),
            pl.BlockSpec((1, D_MODEL), lambda i: (0, 0)),
        ],
        out_specs=pl.BlockSpec((tm, D_MODEL), lambda i: (i, 0)),
        out_shape=jax.ShapeDtypeStruct(((npt + nst) * tm, D_MODEL), BF16),
        compiler_params=_params(("parallel",)),
        name="xnorm",
    )(xp, xs, gain)


ACT_ALPHA, ACT_BETA, ACT_GAMMA, ACT_DELTA, ACT_EPS = 0, 1, 2, 3, 4


def _act_coefficients(lower, b_gate):
    z = jnp.zeros((IN_WIDTH,), F32)
    seg = lambda a, b: slice(a * PROJ_TN, b * PROJ_TN)
    alpha = z.at[seg(SEG_HI, SEG_HG)].set(1.0).at[seg(SEG_RQ, SEG_RG)].set(1.0)
    beta = z.at[seg(SEG_GA, IN_WIDTH // PROJ_TN)].set(b_gate)
    gamma = (z.at[seg(SEG_HQ, SEG_HF)].set(HG_EXPAND ** -0.5)
             .at[seg(SEG_HG, SEG_RQ)].set(1.0).at[seg(SEG_RG, SEG_GA)].set(1.0))
    delta = z.at[seg(SEG_HF, SEG_HI)].set(1.0 - lower).at[seg(SEG_GA, IN_WIDTH // PROJ_TN)].set(1.0)
    eps = z.at[seg(SEG_HF, SEG_HI)].set(lower)
    return jnp.stack([alpha, beta, gamma, delta, eps, z, z, z])


def _in_proj_kernel(x_ref, w_ref, act_ref, cos_ref, sa_ref, sb_ref, o_ref):
    j = pl.program_id(1)
    w = w_ref[...].astype(BF16)
    row = lambda r: act_ref[r:r + 1, :]
    tm = x_ref.shape[0]
    mc = math.gcd(tm, PROJ_MC)
    for m in range(0, tm, mc):
        p = _dot(x_ref[m:m + mc, :], w)
        o_ref[m:m + mc, :] = (row(ACT_ALPHA) * p + row(ACT_EPS)
                              + jax.nn.sigmoid(p + row(ACT_BETA)) * (row(ACT_GAMMA) * p + row(ACT_DELTA)))

    @pl.when((j >= SEG_RQ) & (j < SEG_RV))
    def _():
        scale = jnp.where(j == SEG_RK, RET_HEAD_QK ** -0.5, 1.0).astype(F32)
        cos, sa, sb = cos_ref[...], sa_ref[...], sb_ref[...]
        for hh in range(PROJ_TN // LANES):
            cols = slice(hh * LANES, (hh + 1) * LANES)
            xs = o_ref[:, cols]
            r = xs * cos + pltpu.roll(xs, LANES - 1, 1) * sa + pltpu.roll(xs, 1, 1) * sb
            o_ref[:, cols] = r * scale


def _in_proj(xn, w_in, act, tabs, tm):
    T = xn.shape[0]
    tab = pl.BlockSpec((tm, LANES), lambda i, j: (i, 0))
    return pl.pallas_call(
        _in_proj_kernel,
        grid=(T // tm, IN_WIDTH // PROJ_TN),
        in_specs=[
            pl.BlockSpec((tm, D_MODEL), lambda i, j: (i, 0)),
            pl.BlockSpec((D_MODEL, PROJ_TN), lambda i, j: (0, j)),
            pl.BlockSpec((SUBLANES, PROJ_TN), lambda i, j: (0, j)),
            tab, tab, tab,
        ],
        out_specs=pl.BlockSpec((tm, PROJ_TN), lambda i, j: (i, j)),
        out_shape=jax.ShapeDtypeStruct((T, IN_WIDTH), F32),
        compiler_params=_params(("parallel", "arbitrary")),
        name="in_proj",
    )(xn, w_in, act, *tabs)


def _prefix_matrix(C, G):
    r = lax.broadcasted_iota(I32, (C, 3 * C), 0)
    c = lax.broadcasted_iota(I32, (C, 3 * C), 1) % C
    return jnp.where((c <= r) & (c // G == r // G), 1.0, 0.0).astype(BF16)


def _prefix_sum(g, pm):
    g1 = g.astype(BF16)
    r1 = g - g1.astype(F32)
    g2 = r1.astype(BF16)
    g3 = (r1 - g2.astype(F32)).astype(BF16)
    return _dot(pm, jnp.concatenate([g1, g2, g3], axis=0))


def _state_decay_column(d_row):
    r = lax.broadcasted_iota(I32, (LANES, LANES), 0)
    c = lax.broadcasted_iota(I32, (LANES, LANES), 1)
    return jnp.sum(jnp.where(r == c, jnp.broadcast_to(d_row, (LANES, LANES)), 0.0), axis=1, keepdims=True)


def _hgrn_chunk(q, f, g, v, states, pm, tmp_ref, NG, factorised):
    C, W = q.shape
    G, H = C // NG, W // LANES
    hs = [slice(h * LANES, (h + 1) * LANES) for h in range(H)]
    gs = [slice(n * G, (n + 1) * G) for n in range(NG)]
    k = 1.0 - f
    b = _prefix_sum(g, pm)
    b3 = b.reshape(NG, G, W)
    b_end = b3[:, G - 1:G, :]
    q3, k3 = q.reshape(NG, G, W), k.reshape(NG, G, W)
    vb = v.astype(BF16)
    if factorised:
        b_mid = b3[:, G // 2 - 1:G // 2, :]
        qm3 = q3 * jnp.exp(b3 - b_mid)
        km3 = k3 * jnp.exp(b_mid - b3)
        qb = (qm3 * jnp.exp(b_mid)).reshape(C, W)
        kd = (km3 * jnp.exp(b_end - b_mid)).reshape(C, W)
        qm = qm3.reshape(C, W).astype(BF16)
        km = km3.reshape(C, W).astype(BF16)
        row = lax.broadcasted_iota(I32, (C, C), 0)
        col = lax.broadcasted_iota(I32, (C, C), 1)
        amask = (row >= col) & (row // G == col // G)
        o_intra = []
        for h in range(H):
            att = jnp.where(amask, _dot_nt(qm[:, hs[h]], km[:, hs[h]]), 0.0)
            o_intra.append(_dot(att.astype(BF16), vb[:, hs[h]]))
    else:
        qb = (q3 * jnp.exp(b3)).reshape(C, W)
        kd = (k3 * jnp.exp(b_end - b3)).reshape(C, W)
        trow = lax.broadcasted_iota(I32, (G, LANES), 0)
        o_intra = []
        for h in range(H):
            parts = []
            for n in range(NG):
                bh, qh = b[gs[n], hs[h]], q[gs[n], hs[h]]
                tmp_ref[0] = bh
                tmp_ref[1] = k[gs[n], hs[h]]
                tmp_ref[2] = v[gs[n], hs[h]]

                def body(s, acc, bh=bh, qh=qh):
                    bs = tmp_ref[0, pl.ds(s, 1), :]
                    ks = tmp_ref[1, pl.ds(s, 1), :]
                    vs = tmp_ref[2, pl.ds(s, 1), :]
                    w = jnp.where(trow >= s, jnp.exp(jnp.minimum(bh - bs, 0.0)), 0.0)
                    return acc + jnp.sum(qh * ks * w, axis=1, keepdims=True) * vs

                parts.append(lax.fori_loop(0, G, body, jnp.zeros((G, LANES), F32)))
            o_intra.append(parts[0] if NG == 1 else jnp.concatenate(parts, axis=0))
    d = jnp.exp(b_end)
    outs, new_states = [], [[None] * H for _ in range(NG)]
    for h in range(H):
        inter = []
        for n in range(NG):
            S = states[n][h]
            inter.append(_dot(qb[gs[n], hs[h]].astype(BF16), S.astype(BF16)))
            upd = _dot_tn(kd[gs[n], hs[h]].astype(BF16), v[gs[n], hs[h]].astype(BF16))
            new_states[n][h] = _state_decay_column(d[n, :, hs[h]]) * S + upd
        inter = inter[0] if NG == 1 else jnp.concatenate(inter, axis=0)
        outs.append(inter + o_intra[h])
    return jnp.concatenate(outs, axis=1), new_states


def _decay_is_safe(g, G):
    R, W = g.shape
    return jnp.min(jnp.sum(g.reshape(R // G, G, W), axis=1)) >= -SAFE_EXP_SPAN


def _hgrn_epilogue(o_scr, gain_ref, gate_ref, o_ref):
    o_ref[...] = (_rms(o_scr[...]) * gain_ref[...] * gate_ref[...]).astype(o_ref.dtype)


def _hgrn_prompt_kernel(*refs, C, NB):
    ins, (gain_ref,), rest = refs[:4 * NB], refs[4 * NB:4 * NB + 1], refs[4 * NB + 1:]
    o_refs, (st_ref, s_scr, g_scr, o_scr, tmp_ref) = rest[:NB], rest[NB:]
    seq = [ins[4 * n:4 * n + 4] for n in range(NB)]
    l = pl.program_id(0)
    Lb = o_refs[0].shape[0]

    @pl.when(l == 0)
    def _():
        s_scr[...] = jnp.zeros_like(s_scr)

    for n in range(NB):
        g_scr[n] = jnp.log(seq[n][1][...])
    pm = _prefix_matrix(C, C)

    def run(factorised):
        def body(ci, carry):
            rows = pl.ds(pl.multiple_of(ci * C, C), C)
            args = [(seq[n][0][rows, :], seq[n][1][rows, :], g_scr[n, rows, :], seq[n][2][rows, :],
                     [[s_scr[n, h] for h in range(HG_HEADS)]]) for n in range(NB)]
            outs = [_hgrn_chunk(*a, pm, tmp_ref, 1, factorised) for a in args]
            for n, (o, new) in enumerate(outs):
                o_scr[n, rows, :] = o
                for h in range(HG_HEADS):
                    s_scr[n, h] = new[0][h]
            return carry
        lax.fori_loop(0, Lb // C, body, 0)

    safe = _decay_is_safe(g_scr[...].reshape(NB * Lb, -1), C // 2)
    pl.when(safe)(lambda: run(True))
    pl.when(jnp.logical_not(safe))(lambda: run(False))
    for n in range(NB):
        o_refs[n][...] = (_rms(o_scr[n]) * gain_ref[...] * seq[n][3][...]).astype(o_refs[n].dtype)

    @pl.when(l == pl.num_programs(0) - 1)
    def _():
        st_ref[...] = s_scr[...]


def _hgrn_prompt(P, gain, B, L, Lb):
    C = math.gcd(L, SCAN_CHUNK)
    nl = L // Lb
    W = HG_KEY_DIM
    blk = lambda b, seg: pl.BlockSpec((Lb, W), lambda l, b=b, seg=seg: (b * nl + l, seg))
    outs = pl.pallas_call(
        functools.partial(_hgrn_prompt_kernel, C=C, NB=B),
        grid=(nl,),
        in_specs=[blk(b, seg) for b in range(B) for seg in range(4)] + [pl.BlockSpec((1, W), lambda l: (0, 0))],
        out_specs=[pl.BlockSpec((Lb, W), lambda l: (l, 0)) for _ in range(B)]
        + [pl.BlockSpec((B, HG_HEADS, HG_EXPAND, HG_HEAD_V), lambda l: (0, 0, 0, 0))],
        out_shape=[jax.ShapeDtypeStruct((L, HG_VAL_DIM), BF16) for _ in range(B)]
        + [jax.ShapeDtypeStruct((B, HG_HEADS, HG_EXPAND, HG_HEAD_V), F32)],
        scratch_shapes=[pltpu.VMEM((B, HG_HEADS, HG_EXPAND, HG_HEAD_V), F32), pltpu.VMEM((B, Lb, W), F32),
                        pltpu.VMEM((B, Lb, W), F32), pltpu.VMEM((3, C, LANES), F32)],
        compiler_params=_params(("arbitrary",)),
        name="hgrn_prompt",
    )(*([P] * (4 * B)), gain)
    return outs[:B], outs[B]


def _hgrn_sample_kernel(q_ref, f_ref, v_ref, gate_ref, gain_ref, s_ref, o_ref, so_ref, o_scr, tmp_ref, *, G):
    NG = s_ref.shape[0]
    g = jnp.log(f_ref[...])
    pm = _prefix_matrix(NG * G, G)

    def run(factorised):
        states = [[s_ref[n, h] for h in range(HG_HEADS)] for n in range(NG)]
        o, new = _hgrn_chunk(q_ref[...], f_ref[...], g, v_ref[...], states, pm, tmp_ref, NG, factorised)
        o_scr[...] = o
        for n in range(NG):
            for h in range(HG_HEADS):
                so_ref[n, h] = new[n][h]

    safe = _decay_is_safe(g, G // 2)
    pl.when(safe)(lambda: run(True))
    pl.when(jnp.logical_not(safe))(lambda: run(False))
    _hgrn_epilogue(o_scr, gain_ref, gate_ref, o_ref)


def _hgrn_sample(P, gain, state, Bs, Ls, Bb, row0):
    W = HG_KEY_DIM
    rows = Bb * Ls
    blk0 = row0 // rows

    def col(seg):
        return pl.BlockSpec((rows, W), lambda i, seg=seg: (blk0 + i, seg))

    st = pl.BlockSpec((Bb, HG_HEADS, HG_EXPAND, HG_HEAD_V), lambda i: (i, 0, 0, 0))
    return pl.pallas_call(
        functools.partial(_hgrn_sample_kernel, G=Ls),
        grid=(Bs // Bb,),
        in_specs=[col(0), col(1), col(2), col(3), pl.BlockSpec((1, W), lambda i: (0, 0)), st],
        out_specs=[pl.BlockSpec((rows, W), lambda i: (i, 0)), st],
        out_shape=[
            jax.ShapeDtypeStruct((Bs * Ls, HG_VAL_DIM), BF16),
            jax.ShapeDtypeStruct((Bs, HG_HEADS, HG_EXPAND, HG_HEAD_V), F32),
        ],
        scratch_shapes=[pltpu.VMEM((rows, W), F32), pltpu.VMEM((3, Ls, LANES), F32)],
        compiler_params=_params(("parallel",)),
        name="hgrn_sample",
    )(P, P, P, P, gain, state)


def _ret_chunk(q, k, v, states, dm_ref, qw_ref, kw_ref, sd_ref):
    outs, new_states = [], []
    C = q.shape[0]
    qb, kb, vb = q.astype(BF16), k.astype(BF16), v.astype(BF16)
    for h in range(RET_HEADS):
        qs = slice(h * RET_HEAD_QK, (h + 1) * RET_HEAD_QK)
        vs = slice(h * RET_HEAD_V, (h + 1) * RET_HEAD_V)
        att = (_dot_nt(qb[:, qs], kb[:, qs]) * dm_ref[h]).astype(BF16)
        kw = (k[:, qs] * kw_ref[h]).astype(BF16)
        if C <= 2 * SUBLANES:
            both = _dot(jnp.concatenate([att, kw.T], axis=0), vb[:, vs])
            intra, upd = both[:C], both[C:]
        else:
            intra, upd = _dot(att, vb[:, vs]), _dot_tn(kw, vb[:, vs])
        o = intra + _dot((q[:, qs] * qw_ref[h]).astype(BF16), states[h].astype(BF16))
        new_states.append(sd_ref[h, 0:1, 0:1] * states[h] + upd)
        outs.append(o)
    return jnp.concatenate(outs, axis=1), new_states


def _ret_epilogue(o_scr, gate_ref, o_ref):
    for h in range(RET_HEADS):
        vs = slice(h * RET_HEAD_V, (h + 1) * RET_HEAD_V)
        o_ref[:, vs] = (_rms(o_scr[:, vs]) * gate_ref[:, vs]).astype(o_ref.dtype)


def _ret_prompt_kernel(*refs, NB):
    ins, (dm_ref, qw_ref, kw_ref, sd_ref), rest = refs[:4 * NB], refs[4 * NB:4 * NB + 4], refs[4 * NB + 4:]
    o_refs, (st_ref, s_scr, o_scr) = rest[:NB], rest[NB:]
    seq = [ins[4 * n:4 * n + 4] for n in range(NB)]
    l = pl.program_id(0)
    Lb = o_refs[0].shape[0]
    C = dm_ref.shape[1]

    @pl.when(l == 0)
    def _():
        s_scr[...] = jnp.zeros_like(s_scr)

    def body(ci, carry):
        rows = pl.ds(pl.multiple_of(ci * C, C), C)
        args = [(seq[n][0][rows, :], seq[n][1][rows, :], seq[n][2][rows, :],
                 [s_scr[n, h] for h in range(RET_HEADS)]) for n in range(NB)]
        outs = [_ret_chunk(*a, dm_ref, qw_ref, kw_ref, sd_ref) for a in args]
        for n, (o, new) in enumerate(outs):
            o_scr[n, rows, :] = o
            for h in range(RET_HEADS):
                s_scr[n, h] = new[h]
        return carry

    lax.fori_loop(0, Lb // C, body, 0)
    for n in range(NB):
        _ret_epilogue(o_scr.at[n], seq[n][3], o_refs[n])

    @pl.when(l == pl.num_programs(0) - 1)
    def _():
        st_ref[...] = s_scr[...]


def _ret_tables(C):
    log_gamma = jnp.log(1.0 - jnp.exp2(-5.0 - jnp.arange(RET_HEADS, dtype=F32)))
    idx = jnp.arange(C, dtype=F32)
    rel = idx[:, None] - idx[None, :]
    tri = jnp.tril(jnp.ones((C, C), dtype=bool))
    dmat = jnp.exp(jnp.where(tri[None], log_gamma[:, None, None] * rel[None], -jnp.inf))
    qw = jnp.exp(log_gamma[:, None] * (idx[None, :] + 1.0))[..., None]
    kw = jnp.exp(log_gamma[:, None] * (C - 1.0 - idx[None, :]))[..., None]
    sdec = jnp.exp(log_gamma * C)[:, None, None]
    bc = lambda a: jnp.broadcast_to(a, (RET_HEADS, a.shape[1], LANES))
    return dmat, bc(qw), bc(kw), jnp.broadcast_to(sdec, (RET_HEADS, SUBLANES, LANES))


_QK0 = (2 * HG_KEY_DIM + 2 * HG_VAL_DIM) // RET_QK_DIM
_V0 = (2 * HG_KEY_DIM + 2 * HG_VAL_DIM + 2 * RET_QK_DIM) // RET_V_DIM


def _full3(a):
    return pl.BlockSpec(a.shape, lambda *_: (0, 0, 0))


def _ret_prompt(P, B, L, Lb):
    C = math.gcd(L, 2 * SCAN_CHUNK)
    Lb = max(Lb, C)
    nl = L // Lb
    tabs = _ret_tables(C)
    st_shape = (B, RET_HEADS, RET_HEAD_QK, RET_HEAD_V)
    blk = lambda b, w, c: pl.BlockSpec((Lb, w), lambda l, b=b, c=c: (b * nl + l, c))
    per_seq = lambda b: [blk(b, RET_QK_DIM, _QK0), blk(b, RET_QK_DIM, _QK0 + 1),
                         blk(b, RET_V_DIM, _V0), blk(b, RET_V_DIM, _V0 + 1)]
    outs = pl.pallas_call(
        functools.partial(_ret_prompt_kernel, NB=B),
        grid=(nl,),
        in_specs=[spec for b in range(B) for spec in per_seq(b)] + [_full3(t) for t in tabs],
        out_specs=[pl.BlockSpec((Lb, RET_V_DIM), lambda l: (l, 0)) for _ in range(B)]
        + [pl.BlockSpec(st_shape, lambda l: (0, 0, 0, 0))],
        out_shape=[jax.ShapeDtypeStruct((L, RET_V_DIM), BF16) for _ in range(B)]
        + [jax.ShapeDtypeStruct(st_shape, F32)],
        scratch_shapes=[pltpu.VMEM(st_shape, F32), pltpu.VMEM((B, Lb, RET_V_DIM), F32)],
        compiler_params=_params(("arbitrary",)),
        name="ret_prompt",
    )(*([P] * (4 * B)), *tabs)
    return outs[:B], outs[B]


def _ret_sample_kernel(q_ref, k_ref, v_ref, gate_ref, s_ref, dm_ref, qw_ref, kw_ref, sd_ref,
                       o_ref, so_ref, o_scr, *, C):
    Bb = s_ref.shape[0]
    args = [(q_ref[bb * C:(bb + 1) * C, :], k_ref[bb * C:(bb + 1) * C, :], v_ref[bb * C:(bb + 1) * C, :],
             [s_ref[bb, h] for h in range(RET_HEADS)]) for bb in range(Bb)]
    outs = [_ret_chunk(*a, dm_ref, qw_ref, kw_ref, sd_ref) for a in args]
    for bb, (o, new) in enumerate(outs):
        o_scr[bb * C:(bb + 1) * C, :] = o
        for h in range(RET_HEADS):
            so_ref[bb, h] = new[h]
    _ret_epilogue(o_scr, gate_ref, o_ref)


def _ret_sample(P, state, Bs, Ls, Bb, row0):
    tabs = _ret_tables(Ls)
    rows = Bb * Ls
    blk0 = row0 // rows
    st = pl.BlockSpec((Bb, RET_HEADS, RET_HEAD_QK, RET_HEAD_V), lambda i: (i, 0, 0, 0))
    return pl.pallas_call(
        functools.partial(_ret_sample_kernel, C=Ls),
        grid=(Bs // Bb,),
        in_specs=[
            pl.BlockSpec((rows, RET_QK_DIM), lambda i: (blk0 + i, _QK0)),
            pl.BlockSpec((rows, RET_QK_DIM), lambda i: (blk0 + i, _QK0 + 1)),
            pl.BlockSpec((rows, RET_V_DIM), lambda i: (blk0 + i, _V0)),
            pl.BlockSpec((rows, RET_V_DIM), lambda i: (blk0 + i, _V0 + 1)),
            st,
        ] + [_full3(t) for t in tabs],
        out_specs=[pl.BlockSpec((rows, RET_V_DIM), lambda i: (i, 0)), st],
        out_shape=[
            jax.ShapeDtypeStruct((Bs * Ls, RET_V_DIM), BF16),
            jax.ShapeDtypeStruct((Bs, RET_HEADS, RET_HEAD_QK, RET_HEAD_V), F32),
        ],
        scratch_shapes=[pltpu.VMEM((rows, RET_V_DIM), F32)],
        compiler_params=_params(("parallel",)),
        name="ret_sample",
    )(P, P, P, P, state, *tabs)


def _split_dot(a, w_hilo):
    a_hi = a.astype(BF16)
    a_lo = (a - a_hi.astype(F32)).astype(BF16)
    hi = _dot(a_hi, w_hilo)
    return hi[:, :LANES] + (hi[:, LANES:] + _dot(a_lo, w_hilo[:, :LANES]))


def _route(logits):
    lane = lax.broadcasted_iota(I32, logits.shape, 1)
    neg = jnp.float32(-jnp.inf)
    big = jnp.int32(LANES)
    gmask = lane < N_GROUPS
    gl = jnp.where(gmask, logits, neg)
    gmax = jnp.max(gl, axis=1, keepdims=True)
    gu = jnp.where(gmask, jnp.exp(gl - gmax), 0.0)
    gp = gu / jnp.sum(gu, axis=1, keepdims=True)
    g_w = jnp.max(gp, axis=1, keepdims=True)
    g_i = jnp.min(jnp.where(gmask & (gp == g_w), lane, big), axis=1, keepdims=True)
    lo = ROUTE_EXPERT_LANE + g_i * EXPERTS_PER_GROUP
    emask = (lane >= lo) & (lane < lo + EXPERTS_PER_GROUP)
    el = jnp.where(emask, logits, neg)
    e1 = jnp.max(el, axis=1, keepdims=True)
    i1 = jnp.min(jnp.where(emask & (el == e1), lane, big), axis=1, keepdims=True)
    el2 = jnp.where(lane == i1, neg, el)
    e2 = jnp.max(el2, axis=1, keepdims=True)
    i2 = jnp.min(jnp.where(emask & (lane != i1) & (el2 == e2), lane, big), axis=1, keepdims=True)
    u2 = jnp.exp(e2 - e1)
    den = 1.0 + u2
    w1 = (1.0 / den) * g_w
    w2 = (u2 / den) * g_w
    id1 = (i1 - ROUTE_EXPERT_LANE).astype(F32)
    id2 = (i2 - ROUTE_EXPERT_LANE).astype(F32)
    return (jnp.where(lane == LANE_E1, id1, 0.0) + jnp.where(lane == LANE_E2, id2, 0.0)
            + jnp.where(lane == LANE_W1, w1, 0.0) + jnp.where(lane == LANE_W2, w2, 0.0))


def _pick(i, bounds, refs):
    val = refs[-1][...]
    for n in range(len(refs) - 2, -1, -1):
        val = jnp.where(i < bounds[n + 1], refs[n][...], val)
    return val


def _mix_kernel(*refs, bounds):
    ns = len(bounds) - 1
    ohg_refs, or_refs, refs = refs[:ns], refs[ns:2 * ns], refs[2 * ns:]
    (ga0_ref, ga1_ref, gb0_ref, gb1_ref, xp_ref, xs_ref, wbh_ref, wbr_ref, wout_ref, nffn_ref,
     wr_ref, br_ref, h_ref, hn_ref, route_ref) = refs
    i = pl.program_id(0)
    tm = h_ref.shape[0]
    is_prompt = i < bounds[-2]
    p_hg = _dot(_pick(i, bounds, ohg_refs), wbh_ref[...])
    p_r = _dot(_pick(i, bounds, or_refs), wbr_ref[...])
    half = D_MODEL // 2
    m0 = ga0_ref[...] * p_hg[:, :half] + gb0_ref[...] * p_r[:, :half]
    m1 = ga1_ref[...] * p_hg[:, half:] + gb1_ref[...] * p_r[:, half:]
    mixed = jnp.concatenate([m0, m1], axis=1).astype(BF16)
    x = jnp.where(is_prompt, xp_ref[...], xs_ref[...])
    h = x + _dot(mixed, wout_ref[...])
    h_ref[...] = h
    hn = _rms(h) * nffn_ref[...]
    hn_ref[...] = _to_token_tiles(hn.astype(BF16))
    route_ref[...] = _route(_split_dot(hn, wr_ref[...]) + br_ref[...])


def _mix(o_hg, o_r, P, xp, xs, wbh, wbr, wout, norm_ffn, wr, br, tm):
    npt, nst = xp.shape[0] // tm, xs.shape[0] // tm
    bounds = [0]
    for a in o_hg:
        bounds.append(bounds[-1] + a.shape[0] // tm)
    T = (npt + nst) * tm
    half = D_MODEL // 2
    ga = (IN_WIDTH - 2 * D_MODEL) // half

    def pcol(c):
        return pl.BlockSpec((tm, half), lambda i, c=c: (i, c))

    def const(a):
        return pl.BlockSpec(a.shape, lambda i: (0, 0), pipeline_mode=pl.Buffered(1))

    def prompt(w):
        return pl.BlockSpec((tm, w), lambda i: (jnp.minimum(i, npt - 1), 0))

    def sample(w):
        return pl.BlockSpec((tm, w), lambda i: (jnp.maximum(i - npt, 0), 0))

    def source(n):
        lo, hi = bounds[n], bounds[n + 1]
        return pl.BlockSpec((tm, half), lambda i: (jnp.clip(i - lo, 0, hi - lo - 1), 0))

    sources = [source(n) for n in range(len(o_hg))]
    return pl.pallas_call(
        functools.partial(_mix_kernel, bounds=tuple(bounds)),
        grid=(npt + nst,),
        in_specs=sources + sources + [
            pcol(ga), pcol(ga + 1), pcol(ga + 2), pcol(ga + 3),
            prompt(D_MODEL), sample(D_MODEL),
            const(wbh), const(wbr), const(wout), const(norm_ffn), const(wr), const(br),
        ],
        out_specs=[
            pl.BlockSpec((tm, D_MODEL), lambda i: (i, 0)),
            pl.BlockSpec((tm * ROW_GROUP, LANES), lambda i: (i, 0)),
            pl.BlockSpec((tm, LANES), lambda i: (i, 0)),
        ],
        out_shape=[
            jax.ShapeDtypeStruct((T, D_MODEL), F32),
            jax.ShapeDtypeStruct((T * ROW_GROUP, LANES), BF16),
            jax.ShapeDtypeStruct((T, LANES), F32),
        ],
        compiler_params=_params(("parallel",)),
        name="mix",
    )(*o_hg, *o_r, P, P, P, P, xp, xs, wbh, wbr, wout, norm_ffn, wr, br)


def _plan_kernel(route_ref, pos_ref, te_ref, tail_ref, cum_scr, *, tb):
    T = route_ref.shape[0]
    lane_t = lax.broadcasted_iota(I32, (tb, LANES), 1).astype(F32)
    r = lax.broadcasted_iota(I32, (tb, tb), 0)
    c = lax.broadcasted_iota(I32, (tb, tb), 1)
    strict_lower = jnp.where(r > c, 1.0, 0.0).astype(BF16)

    def onehots(rows):
        rt = route_ref[rows, :]
        a1 = jnp.where(lane_t == rt[:, LANE_E1:LANE_E1 + 1], 1.0, 0.0)
        a2 = jnp.where(lane_t == rt[:, LANE_E2:LANE_E2 + 1], 1.0, 0.0)
        return a1, a2

    def rank_body(bi, carry):
        rows = pl.ds(pl.multiple_of(bi * tb, tb), tb)
        a1, a2 = onehots(rows)
        m = a1 + a2
        cum_scr[rows, :] = _dot(strict_lower, m.astype(BF16)) + carry
        return carry + jnp.sum(m, axis=0, keepdims=True)

    counts = lax.fori_loop(0, T // tb, rank_body, jnp.zeros((1, LANES), F32))
    ntiles = jnp.zeros_like(counts)
    for kt in range(-(-T // MOE_TILE)):
        ntiles = ntiles + jnp.where(counts > float(kt * MOE_TILE), 1.0, 0.0)
    rr = lax.broadcasted_iota(I32, (LANES, LANES), 0)
    cc = lax.broadcasted_iota(I32, (LANES, LANES), 1)
    upper = jnp.where(rr < cc, 1.0, 0.0).astype(BF16)
    tile0 = _dot(jnp.broadcast_to(ntiles, (SUBLANES, LANES)).astype(BF16), upper)[0:1, :]
    slot0 = tile0 * MOE_TILE

    def pos_body(bi, carry):
        rows = pl.ds(pl.multiple_of(bi * tb, tb), tb)
        a1, a2 = onehots(rows)
        base = cum_scr[rows, :] + slot0
        p1 = jnp.sum(a1 * base, axis=1, keepdims=True)
        p2 = jnp.sum(a2 * base, axis=1, keepdims=True)
        pos_ref[rows, :] = (jnp.where(lane_t == 0.0, p1, 0.0) + jnp.where(lane_t == 1.0, p2, 0.0)).astype(I32)
        return carry

    lax.fori_loop(0, T // tb, pos_body, 0)
    tend = tile0 + ntiles
    lane = lax.broadcasted_iota(I32, (LANES, LANES), 1)
    tile_i = lax.broadcasted_iota(I32, (LANES, LANES), 0).astype(F32)
    is_expert = lane < N_EXPERTS
    te = jnp.sum(jnp.where(is_expert & (jnp.broadcast_to(tend, (LANES, LANES)) <= tile_i), 1.0, 0.0),
                 axis=1, keepdims=True)
    te_ref[...] = jnp.broadcast_to(jnp.minimum(te, N_EXPERTS - 1.0), (LANES, LANES)).astype(I32)
    lane1 = lax.broadcasted_iota(I32, (1, LANES), 1)
    total = jnp.sum(jnp.where(lane1 < N_EXPERTS, ntiles, 0.0), axis=1, keepdims=True)
    tail = jnp.where(ntiles > 0.0, (tend - 1.0) * MOE_TILE, -1.0)
    tail = jnp.where(lane1 < N_EXPERTS, tail, jnp.where(lane1 == N_EXPERTS, total, 0.0))
    tail_ref[...] = jnp.broadcast_to(tail, (SUBLANES, LANES)).astype(I32)


def _plan(route, tb):
    T = route.shape[0]
    return pl.pallas_call(
        functools.partial(_plan_kernel, tb=tb),
        grid=(1,),
        in_specs=[pl.BlockSpec((T, LANES), lambda i: (0, 0))],
        out_specs=[
            pl.BlockSpec((T, LANES), lambda i: (0, 0)),
            pl.BlockSpec((LANES, LANES), lambda i: (0, 0)),
            pl.BlockSpec((SUBLANES, LANES), lambda i: (0, 0)),
        ],
        out_shape=[
            jax.ShapeDtypeStruct((T, LANES), I32),
            jax.ShapeDtypeStruct((LANES, LANES), I32),
            jax.ShapeDtypeStruct((SUBLANES, LANES), I32),
        ],
        scratch_shapes=[pltpu.VMEM((T, LANES), F32)],
        compiler_params=_params(("arbitrary",)),
        name="moe_plan",
    )(route)


def _row_group(ref, idx):
    return ref.at[pl.ds(pl.multiple_of(idx * ROW_GROUP, ROW_GROUP), ROW_GROUP), :]


def _dispatch_kernel(pos_ref, tail_ref, hn_ref, xs_ref, zero_scr, zsem, sem):
    i = pl.program_id(0)
    tm = hn_ref.shape[0] // ROW_GROUP
    tile_rows = MOE_TILE * ROW_GROUP

    @pl.when(i == 0)
    def _():
        zero_scr[...] = jnp.zeros_like(zero_scr)
        n_used = tail_ref[N_EXPERTS]

        def zero_copy(first_slot):
            start = pl.multiple_of(first_slot * ROW_GROUP, tile_rows)
            return pltpu.make_async_copy(zero_scr, xs_ref.at[pl.ds(start, tile_rows), :], zsem)

        def tails(fn):
            def body(e, carry):
                pl.when(tail_ref[e] >= 0)(lambda: fn(zero_copy(jnp.maximum(tail_ref[e], 0))))
                return carry
            lax.fori_loop(0, N_EXPERTS, body, 0)

        def unused(fn):
            def body(t, carry):
                fn(zero_copy(t * MOE_TILE))
                return carry
            lax.fori_loop(n_used, xs_ref.shape[0] // tile_rows, body, 0)

        tails(lambda c: c.start())
        unused(lambda c: c.start(priority=1))
        tails(lambda c: c.wait())
        unused(lambda c: c.wait())

    def copy(r, k):
        slot = pos_ref[(i * tm + r) * TOP_K + k]
        return pltpu.make_async_copy(_row_group(hn_ref, r), _row_group(xs_ref, slot), sem)

    def start(r, carry):
        for k in range(TOP_K):
            copy(r, k).start(priority=k)
        return carry

    lax.fori_loop(0, tm, start, 0, unroll=8)
    for _ in range(TOP_K):
        pltpu.make_async_copy(hn_ref, xs_ref.at[pl.ds(0, tm * ROW_GROUP), :], sem).wait()


def _dispatch(pos, tails, hn_rows, n_slots, tm):
    T = hn_rows.shape[0] // ROW_GROUP
    return pl.pallas_call(
        _dispatch_kernel,
        grid_spec=pltpu.PrefetchScalarGridSpec(
            num_scalar_prefetch=2,
            grid=(T // tm,),
            in_specs=[pl.BlockSpec((tm * ROW_GROUP, LANES), lambda i, *_: (i, 0))],
            out_specs=pl.BlockSpec(memory_space=pl.ANY),
            scratch_shapes=[pltpu.VMEM((MOE_TILE * ROW_GROUP, LANES), BF16),
                            pltpu.SemaphoreType.DMA, pltpu.SemaphoreType.DMA],
        ),
        out_shape=jax.ShapeDtypeStruct((n_slots * ROW_GROUP, LANES), BF16),
        compiler_params=_params(("arbitrary",)),
        name="moe_dispatch",
    )(pos, tails, hn_rows)


def _experts_kernel(te_ref, nt_ref, xs_ref, wg_hbm, wu_hbm, wd_hbm, ys_ref,
                    wg_buf, wu_buf, wd_buf, rank_ref, sem):
    i = pl.program_id(0)
    nt = nt_ref[0]
    e = te_ref[i]

    def fetch(expert, slot):
        return [pltpu.make_async_copy(hbm.at[expert], buf.at[slot], sem.at[slot, n])
                for n, (hbm, buf) in enumerate(((wg_hbm, wg_buf), (wu_hbm, wu_buf), (wd_hbm, wd_buf)))]

    def start_fetch(tile, slot):
        for c in fetch(te_ref[jnp.minimum(tile, nt - 1)], slot):
            c.start(priority=1)

    def next_change(tile):
        ex = te_ref[jnp.minimum(tile, nt - 1)]
        return lax.while_loop(lambda j: (j < nt) & (te_ref[jnp.minimum(j, nt - 1)] == ex), lambda j: j + 1, tile + 1)

    @pl.when(i == 0)
    def _():
        rank_ref[0] = -1
        start_fetch(0, 0)
        n1 = next_change(0)
        pl.when(n1 < nt)(lambda: start_fetch(n1, 1))

    @pl.when((i < nt) & ((i == 0) | (te_ref[jnp.maximum(i - 1, 0)] != e)))
    def _():
        rank = rank_ref[0] + 1
        rank_ref[0] = rank
        for c in fetch(e, rank % WEIGHT_SLOTS):
            c.wait()
        n1 = next_change(i)

        @pl.when(n1 < nt)
        def _():
            n2 = next_change(n1)
            pl.when(n2 < nt)(lambda: start_fetch(n2, (rank + 2) % WEIGHT_SLOTS))

    @pl.when(i < nt)
    def _():
        slot = rank_ref[0] % WEIGHT_SLOTS
        x = _from_token_tiles(xs_ref[...])[0]
        hmid = (_silu(_dot(x, wg_buf[slot].astype(BF16))) * _dot(x, wu_buf[slot].astype(BF16))).astype(BF16)
        ys_ref[...] = _to_token_tiles(_dot(hmid, wd_buf[slot].astype(BF16)))

    @pl.when(i >= nt)
    def _():
        ys_ref[...] = jnp.zeros_like(ys_ref)


def _experts(te, nt, xs_rows, w_ge, w_ue, w_de, n_tiles):
    M = MOE_TILE
    rows = pl.BlockSpec((M * ROW_GROUP, LANES), lambda i, te_ref, nt_ref: (jnp.minimum(i, nt_ref[0] - 1), 0))
    hbm = pl.BlockSpec(memory_space=pl.ANY)
    up, down = (D_MODEL, EXPERT_FF), (EXPERT_FF, D_MODEL)
    return pl.pallas_call(
        _experts_kernel,
        grid_spec=pltpu.PrefetchScalarGridSpec(
            num_scalar_prefetch=2,
            grid=(n_tiles,),
            in_specs=[rows, hbm, hbm, hbm],
            out_specs=pl.BlockSpec((M * ROW_GROUP, LANES), lambda i, te_ref, nt_ref: (i, 0)),
            scratch_shapes=[pltpu.VMEM((WEIGHT_SLOTS,) + up, F32), pltpu.VMEM((WEIGHT_SLOTS,) + up, F32),
                            pltpu.VMEM((WEIGHT_SLOTS,) + down, F32),
                            pltpu.SMEM((1,), I32), pltpu.SemaphoreType.DMA((WEIGHT_SLOTS, 3))],
        ),
        out_shape=jax.ShapeDtypeStruct(xs_rows.shape, F32),
        compiler_params=_params(("arbitrary",)),
        name="moe_experts",
    )(te, nt, xs_rows, w_ge, w_ue, w_de)


def _final_kernel(pos_ref, h_ref, route_ref, gain_ref, ys_ref, y_ref, buf, sem, *, tile0):
    i = pl.program_id(0)
    n = pl.num_programs(0)
    tm = h_ref.shape[0]

    def copy(step, r, k, slot_buf):
        slot = pos_ref[((tile0 + step) * tm + r) * TOP_K + k]
        return pltpu.make_async_copy(_row_group(ys_ref, slot), _row_group(buf.at[slot_buf], r * TOP_K + k),
                                     sem.at[slot_buf])

    def start_all(step, slot_buf):
        def body(r, carry):
            for k in range(TOP_K):
                copy(step, r, k, slot_buf).start(priority=k)
            return carry
        lax.fori_loop(0, tm, body, 0, unroll=8)

    cur = i % 2
    pl.when(i == 0)(lambda: start_all(0, 0))
    pl.when(i + 1 < n)(lambda: start_all(i + 1, 1 - cur))
    pltpu.make_async_copy(ys_ref.at[pl.ds(0, buf.shape[1]), :], buf.at[cur], sem.at[cur]).wait()

    rt = route_ref[...]
    w1 = rt[:, LANE_W1:LANE_W1 + 1]
    w2 = rt[:, LANE_W2:LANE_W2 + 1]
    o1, o2 = _from_token_tiles(buf[cur], groups=TOP_K)
    v = h_ref[...] + (w1 * o1 + w2 * o2)
    y_ref[...] = _rms(v) * gain_ref[...]


def _final(pos, h, route, gain, ys_rows, tile0, n_tiles, tm):
    return pl.pallas_call(
        functools.partial(_final_kernel, tile0=tile0),
        grid_spec=pltpu.PrefetchScalarGridSpec(
            num_scalar_prefetch=1,
            grid=(n_tiles,),
            in_specs=[
                pl.BlockSpec((tm, D_MODEL), lambda i, *_: (tile0 + i, 0)),
                pl.BlockSpec((tm, LANES), lambda i, *_: (tile0 + i, 0)),
                pl.BlockSpec((1, D_MODEL), lambda i, *_: (0, 0)),
                pl.BlockSpec(memory_space=pl.ANY),
            ],
            out_specs=pl.BlockSpec((tm, D_MODEL), lambda i, *_: (i, 0)),
            scratch_shapes=[pltpu.VMEM((2, tm * TOP_K * ROW_GROUP, LANES), F32), pltpu.SemaphoreType.DMA((2,))],
        ),
        out_shape=jax.ShapeDtypeStruct((n_tiles * tm, D_MODEL), F32),
        compiler_params=_params(("arbitrary",)),
        name="final",
    )(pos, h, route, gain, ys_rows)


def _rope_tables(pos):
    half = RET_HEAD_QK // 2
    inv = 1.0 / (ROPE_BASE ** jnp.linspace(0.0, 1.0, half, dtype=F32))
    ang = jnp.repeat(pos[:, None] * inv[None, :], 2, axis=-1)
    cos, sin = jnp.cos(ang), jnp.sin(ang)
    even = (jnp.arange(RET_HEAD_QK) % 2 == 0)[None, :]
    sa = jnp.where(even, -sin, 0.0)
    sb = jnp.where(even, 0.0, sin)
    return cos, sa, sb


def _tile_sizes(Tp, Ts, Lp):
    g = math.gcd(Tp, Ts)
    return dict(
        xnorm=math.gcd(g, 512),
        proj=(Tp + Ts) // 4,
        mix=math.gcd(g, 256),
        plan=math.gcd(Tp + Ts, 512),
        dispatch=math.gcd(g, 1024),
        final=math.gcd(g, 256),
        scan=math.gcd(Lp, 128),
    )


def kernel(x_prompt, x_sample, state_hgrn, state_ret, norm_mix, w_in, hg_lb_logits, hg_norm,
           w_branch_hg, w_branch_ret, b_gate, w_out, norm_ffn, w_router_group, b_router_group,
           w_router_expert, b_router_expert, w_expert_gate, w_expert_up, w_expert_down, norm_final):
    assert w_in.shape == (1, D_MODEL, IN_WIDTH), "single-layer trunk only"
    Bp, Lp, _ = x_prompt.shape
    Bs, Ls, _ = x_sample.shape
    Tp, Ts = Bp * Lp, Bs * Ls
    T = Tp + Ts
    ts = _tile_sizes(Tp, Ts, Lp)
    xp = x_prompt.reshape(Tp, D_MODEL)
    xs = x_sample.reshape(Ts, D_MODEL)

    lower = jnp.cumsum(jax.nn.softmax(hg_lb_logits.astype(F32), axis=0), axis=0)[0]
    act = _act_coefficients(lower, b_gate[0])
    tab_p = _rope_tables(jnp.arange(Lp, dtype=F32))
    tab_s = _rope_tables(jnp.arange(Ls, dtype=F32) + jnp.float32(PAST_LEN))
    tabs = tuple(jnp.concatenate([jnp.tile(a, (Bp, 1)), jnp.tile(b, (Bs, 1))]) for a, b in zip(tab_p, tab_s))
    xn = _xnorm(xp, xs, norm_mix, ts["xnorm"])
    P = _in_proj(xn, w_in[0], act, tabs, ts["proj"])

    bb = math.gcd(Bs, 16)
    o_hg_p, shp = _hgrn_prompt(P, hg_norm, Bp, Lp, ts["scan"])
    o_hg_s, shs = _hgrn_sample(P, hg_norm, state_hgrn[0].astype(F32), Bs, Ls, bb, Tp)
    o_r_p, srp = _ret_prompt(P, Bp, Lp, ts["scan"])
    o_r_s, srs = _ret_sample(P, state_ret[0].astype(F32), Bs, Ls, bb, Tp)

    wbh = w_branch_hg[0].astype(BF16)
    wbr = w_branch_ret[0].astype(BF16)
    wout = w_out[0].astype(BF16)
    pad = LANES - N_GROUPS - N_EXPERTS
    wr = jnp.concatenate([w_router_group[0], w_router_expert[0], jnp.zeros((D_MODEL, pad), F32)], axis=1)
    wr_hi = wr.astype(BF16)
    wr_hilo = jnp.concatenate([wr_hi, (wr - wr_hi.astype(F32)).astype(BF16)], axis=1)
    br = jnp.concatenate([b_router_group[0], b_router_expert[0], jnp.zeros((pad,), F32)])[None, :]
    h, hn_rows, route = _mix(list(o_hg_p) + [o_hg_s], list(o_r_p) + [o_r_s], P, xp, xs, wbh, wbr, wout,
                             norm_ffn, wr_hilo, br, ts["mix"])

    n_tiles = -(-T * TOP_K // MOE_TILE) + N_EXPERTS
    assert n_tiles <= LANES
    pos_slab, te_slab, tail_slab = _plan(route, ts["plan"])
    pos = pos_slab[:, :TOP_K].reshape(-1)
    te = te_slab[:n_tiles, 0]
    tails = tail_slab[0, :N_EXPERTS + 1]
    nt = tail_slab[0, N_EXPERTS:N_EXPERTS + 1]
    xs_rows = _dispatch(pos, tails, hn_rows, n_tiles * MOE_TILE, ts["dispatch"])
    ys_rows = _experts(te, nt, xs_rows, w_expert_gate[0], w_expert_up[0], w_expert_down[0], n_tiles)

    tf = ts["final"]
    gain = norm_final[None, :]
    y_p = _final(pos, h, route, gain, ys_rows, 0, Tp // tf, tf)
    y_s = _final(pos, h, route, gain, ys_rows, Tp // tf, Ts // tf, tf)
    return (y_p.reshape(Bp, Lp, D_MODEL), y_s.reshape(Bs, Ls, D_MODEL),
            shp[None], srp[None], shs[None], srs[None])
```

```python
import functools
import math

import jax
import jax.numpy as jnp
from jax import lax
from jax.experimental import pallas as pl
from jax.experimental.pallas import tpu as pltpu

F32 = jnp.float32
BF16 = jnp.bfloat16
I32 = jnp.int32

D_MODEL = 2048
PAST_LEN = 16384
HG_HEAD_V = 128
HG_EXPAND = 128
HG_VAL_DIM = D_MODEL // 2
HG_HEADS = HG_VAL_DIM // HG_HEAD_V
HG_KEY_DIM = HG_HEADS * HG_EXPAND
RET_HEAD_V = 256
RET_HEAD_QK = 128
RET_V_DIM = D_MODEL // 2
RET_HEADS = RET_V_DIM // RET_HEAD_V
RET_QK_DIM = RET_HEADS * RET_HEAD_QK
IN_WIDTH = 2 * HG_KEY_DIM + 2 * HG_VAL_DIM + 2 * RET_QK_DIM + 2 * RET_V_DIM + 2 * D_MODEL
SCAN_CHUNK = 128
N_GROUPS = 4
EXPERTS_PER_GROUP = 8
N_EXPERTS = N_GROUPS * EXPERTS_PER_GROUP
TOP_K = 2
EXPERT_FF = D_MODEL // 4
ROPE_BASE = 10000.0
EPS = 1e-6

LANES = 128
SUBLANES = 8
VMEM_LIMIT = 56 * 1024 * 1024
PROJ_MC = 256
PROJ_TN = 512
SEG_HQ, SEG_HF, SEG_HI, SEG_HG, SEG_RQ, SEG_RK, SEG_RV, SEG_RG, SEG_GA = 0, 2, 4, 6, 8, 9, 10, 12, 14
SAFE_EXP_SPAN = 80.0
ROW_GROUP = D_MODEL // LANES
MOE_TILE = 320
WEIGHT_SLOTS = 3
LANE_E1, LANE_E2, LANE_W1, LANE_W2 = 0, 1, 2, 3
ROUTE_EXPERT_LANE = N_GROUPS


def _params(sem):
    return pltpu.CompilerParams(dimension_semantics=sem, vmem_limit_bytes=VMEM_LIMIT)


def _silu(z):
    return z * jax.nn.sigmoid(z)


def _dot(a, b):
    return jnp.dot(a, b, preferred_element_type=F32)


def _dot_nt(a, b):
    return lax.dot_general(a, b, (((1,), (1,)), ((), ())), preferred_element_type=F32)


def _dot_tn(a, b):
    return lax.dot_general(a, b, (((0,), (0,)), ((), ())), preferred_element_type=F32)


def _rms(x):
    return x * lax.rsqrt(jnp.mean(x * x, axis=-1, keepdims=True) + EPS)


def _to_token_tiles(x):
    M = x.shape[0]
    pieces = jnp.stack([x[:, s * LANES:(s + 1) * LANES] for s in range(ROW_GROUP)], axis=0)
    return jnp.swapaxes(pieces, 0, 1).reshape(M * ROW_GROUP, LANES)


def _from_token_tiles(rows, groups=1):
    M = rows.shape[0] // (groups * ROW_GROUP)
    t = jnp.swapaxes(rows.reshape(M, groups * ROW_GROUP, LANES), 0, 1)
    return [jnp.concatenate([t[g * ROW_GROUP + s] for s in range(ROW_GROUP)], axis=1) for g in range(groups)]


def _xnorm_kernel(xp_ref, xs_ref, gain_ref, o_ref, *, n_prompt_tiles):
    i = pl.program_id(0)

    def norm(x_ref):
        o_ref[...] = (_rms(x_ref[...]) * gain_ref[...]).astype(BF16)

    pl.when(i < n_prompt_tiles)(lambda: norm(xp_ref))
    pl.when(i >= n_prompt_tiles)(lambda: norm(xs_ref))


def _xnorm(xp, xs, gain, tm):
    npt, nst = xp.shape[0] // tm, xs.shape[0] // tm
    return pl.pallas_call(
        functools.partial(_xnorm_kernel, n_prompt_tiles=npt),
        grid=(npt + nst,),
        in_specs=[
            pl.BlockSpec((tm, D_MODEL), lambda i: (jnp.minimum(i, npt - 1), 0)),
            pl.BlockSpec((tm, D_MODEL), lambda i: (jnp.maximum(i - npt, 0), 0)),
            pl.BlockSpec((1, D_MODEL), lambda i: (0, 0)),
        ],
        out_specs=pl.BlockSpec((tm, D_MODEL), lambda i: (i, 0)),
        out_shape=jax.ShapeDtypeStruct(((npt + nst) * tm, D_MODEL), BF16),
        compiler_params=_params(("parallel",)),
        name="xnorm",
    )(xp, xs, gain)


ACT_ALPHA, ACT_BETA, ACT_GAMMA, ACT_DELTA, ACT_EPS = 0, 1, 2, 3, 4


def _act_coefficients(lower, b_gate):
    z = jnp.zeros((IN_WIDTH,), F32)
    seg = lambda a, b: slice(a * PROJ_TN, b * PROJ_TN)
    alpha = (z.at[seg(SEG_HI, SEG_HG)].set(1.0).at[seg(SEG_RQ, SEG_RG)].set(1.0)
             .at[seg(SEG_RK, SEG_RV)].set(RET_HEAD_QK ** -0.5))
    beta = z.at[seg(SEG_GA, IN_WIDTH // PROJ_TN)].set(b_gate)
    gamma = (z.at[seg(SEG_HQ, SEG_HF)].set(HG_EXPAND ** -0.5)
             .at[seg(SEG_HG, SEG_RQ)].set(1.0).at[seg(SEG_RG, SEG_GA)].set(1.0))
    delta = z.at[seg(SEG_HF, SEG_HI)].set(1.0 - lower).at[seg(SEG_GA, IN_WIDTH // PROJ_TN)].set(1.0)
    eps = z.at[seg(SEG_HF, SEG_HI)].set(lower)
    return jnp.stack([alpha, beta, gamma, delta, eps, z, z, z])


def _in_proj_kernel(x_ref, w_ref, act_ref, o_ref):
    w = w_ref[...].astype(BF16)
    row = lambda r: act_ref[r:r + 1, :]
    tm = x_ref.shape[0]
    mc = math.gcd(tm, PROJ_MC)
    for m in range(0, tm, mc):
        p = _dot(x_ref[m:m + mc, :], w)
        o_ref[m:m + mc, :] = (row(ACT_ALPHA) * p + row(ACT_EPS)
                              + jax.nn.sigmoid(p + row(ACT_BETA)) * (row(ACT_GAMMA) * p + row(ACT_DELTA)))


def _in_proj(xn, w_in, act, tm):
    T = xn.shape[0]
    return pl.pallas_call(
        _in_proj_kernel,
        grid=(T // tm, IN_WIDTH // PROJ_TN),
        in_specs=[
            pl.BlockSpec((tm, D_MODEL), lambda i, j: (i, 0)),
            pl.BlockSpec((D_MODEL, PROJ_TN), lambda i, j: (0, j)),
            pl.BlockSpec((SUBLANES, PROJ_TN), lambda i, j: (0, j)),
        ],
        out_specs=pl.BlockSpec((tm, PROJ_TN), lambda i, j: (i, j)),
        out_shape=jax.ShapeDtypeStruct((T, IN_WIDTH), F32),
        compiler_params=_params(("parallel", "arbitrary")),
        name="in_proj",
    )(xn, w_in, act)


def _prefix_matrix(C, G):
    r = lax.broadcasted_iota(I32, (C, 3 * C), 0)
    c = lax.broadcasted_iota(I32, (C, 3 * C), 1) % C
    return jnp.where((c <= r) & (c // G == r // G), 1.0, 0.0).astype(BF16)


def _prefix_sum(g, pm):
    g1 = g.astype(BF16)
    r1 = g - g1.astype(F32)
    g2 = r1.astype(BF16)
    g3 = (r1 - g2.astype(F32)).astype(BF16)
    return _dot(pm, jnp.concatenate([g1, g2, g3], axis=0))


def _state_decay_column(d_row):
    r = lax.broadcasted_iota(I32, (LANES, LANES), 0)
    c = lax.broadcasted_iota(I32, (LANES, LANES), 1)
    return jnp.sum(jnp.where(r == c, jnp.broadcast_to(d_row, (LANES, LANES)), 0.0), axis=1, keepdims=True)


def _hgrn_chunk(q, f, g, v, states, pm, tmp_ref, NG, factorised):
    C, W = q.shape
    G, H = C // NG, W // LANES
    hs = [slice(h * LANES, (h + 1) * LANES) for h in range(H)]
    gs = [slice(n * G, (n + 1) * G) for n in range(NG)]
    k = 1.0 - f
    b = _prefix_sum(g, pm)
    b3 = b.reshape(NG, G, W)
    b_end = b3[:, G - 1:G, :]
    q3, k3 = q.reshape(NG, G, W), k.reshape(NG, G, W)
    vb = v.astype(BF16)
    if factorised:
        b_mid = b3[:, G // 2 - 1:G // 2, :]
        qm3 = q3 * jnp.exp(b3 - b_mid)
        km3 = k3 * jnp.exp(b_mid - b3)
        qb = (qm3 * jnp.exp(b_mid)).reshape(C, W)
        kd = (km3 * jnp.exp(b_end - b_mid)).reshape(C, W)
        qm = qm3.reshape(C, W).astype(BF16)
        km = km3.reshape(C, W).astype(BF16)
        row = lax.broadcasted_iota(I32, (C, C), 0)
        col = lax.broadcasted_iota(I32, (C, C), 1)
        amask = (row >= col) & (row // G == col // G)
        o_intra = []
        for h in range(H):
            att = jnp.where(amask, _dot_nt(qm[:, hs[h]], km[:, hs[h]]), 0.0)
            o_intra.append(_dot(att.astype(BF16), vb[:, hs[h]]))
    else:
        qb = (q3 * jnp.exp(b3)).reshape(C, W)
        kd = (k3 * jnp.exp(b_end - b3)).reshape(C, W)
        trow = lax.broadcasted_iota(I32, (G, LANES), 0)
        o_intra = []
        for h in range(H):
            parts = []
            for n in range(NG):
                bh, qh = b[gs[n], hs[h]], q[gs[n], hs[h]]
                tmp_ref[0] = bh
                tmp_ref[1] = k[gs[n], hs[h]]
                tmp_ref[2] = v[gs[n], hs[h]]

                def body(s, acc, bh=bh, qh=qh):
                    bs = tmp_ref[0, pl.ds(s, 1), :]
                    ks = tmp_ref[1, pl.ds(s, 1), :]
                    vs = tmp_ref[2, pl.ds(s, 1), :]
                    w = jnp.where(trow >= s, jnp.exp(jnp.minimum(bh - bs, 0.0)), 0.0)
                    return acc + jnp.sum(qh * ks * w, axis=1, keepdims=True) * vs

                parts.append(lax.fori_loop(0, G, body, jnp.zeros((G, LANES), F32)))
            o_intra.append(parts[0] if NG == 1 else jnp.concatenate(parts, axis=0))
    d = jnp.exp(b_end)
    outs, new_states = [], [[None] * H for _ in range(NG)]
    for h in range(H):
        inter = []
        for n in range(NG):
            S = states[n][h]
            inter.append(_dot(qb[gs[n], hs[h]].astype(BF16), S.astype(BF16)))
            upd = _dot_tn(kd[gs[n], hs[h]].astype(BF16), v[gs[n], hs[h]].astype(BF16))
            new_states[n][h] = _state_decay_column(d[n, :, hs[h]]) * S + upd
        inter = inter[0] if NG == 1 else jnp.concatenate(inter, axis=0)
        outs.append(inter + o_intra[h])
    return jnp.concatenate(outs, axis=1), new_states


def _decay_is_safe(g, G):
    R, W = g.shape
    return jnp.min(jnp.sum(g.reshape(R // G, G, W), axis=1)) >= -SAFE_EXP_SPAN


def _hgrn_epilogue(o_scr, gain_ref, gate_ref, o_ref):
    o_ref[...] = (_rms(o_scr[...]) * gain_ref[...] * gate_ref[...]).astype(o_ref.dtype)


def _hgrn_prompt_kernel(*refs, C, NB):
    ins, (gain_ref,), rest = refs[:4 * NB], refs[4 * NB:4 * NB + 1], refs[4 * NB + 1:]
    o_refs, (st_ref, s_scr, g_scr, o_scr, tmp_ref) = rest[:NB], rest[NB:]
    seq = [ins[4 * n:4 * n + 4] for n in range(NB)]
    l = pl.program_id(0)
    Lb = o_refs[0].shape[0]

    @pl.when(l == 0)
    def _():
        s_scr[...] = jnp.zeros_like(s_scr)

    for n in range(NB):
        g_scr[n] = jnp.log(seq[n][1][...])
    pm = _prefix_matrix(C, C)

    def run(factorised):
        def body(ci, carry):
            rows = pl.ds(pl.multiple_of(ci * C, C), C)
            args = [(seq[n][0][rows, :], seq[n][1][rows, :], g_scr[n, rows, :], seq[n][2][rows, :],
                     [[s_scr[n, h] for h in range(HG_HEADS)]]) for n in range(NB)]
            outs = [_hgrn_chunk(*a, pm, tmp_ref, 1, factorised) for a in args]
            for n, (o, new) in enumerate(outs):
                o_scr[n, rows, :] = o
                for h in range(HG_HEADS):
                    s_scr[n, h] = new[0][h]
            return carry
        lax.fori_loop(0, Lb // C, body, 0)

    safe = _decay_is_safe(g_scr[...].reshape(NB * Lb, -1), C // 2)
    pl.when(safe)(lambda: run(True))
    pl.when(jnp.logical_not(safe))(lambda: run(False))
    for n in range(NB):
        o_refs[n][...] = (_rms(o_scr[n]) * gain_ref[...] * seq[n][3][...]).astype(o_refs[n].dtype)

    @pl.when(l == pl.num_programs(0) - 1)
    def _():
        st_ref[...] = s_scr[...]


def _hgrn_prompt(P, gain, B, L, Lb):
    C = math.gcd(L, SCAN_CHUNK)
    nl = L // Lb
    W = HG_KEY_DIM
    blk = lambda b, seg: pl.BlockSpec((Lb, W), lambda l, b=b, seg=seg: (b * nl + l, seg))
    outs = pl.pallas_call(
        functools.partial(_hgrn_prompt_kernel, C=C, NB=B),
        grid=(nl,),
        in_specs=[blk(b, seg) for b in range(B) for seg in range(4)] + [pl.BlockSpec((1, W), lambda l: (0, 0))],
        out_specs=[pl.BlockSpec((Lb, W), lambda l: (l, 0)) for _ in range(B)]
        + [pl.BlockSpec((B, HG_HEADS, HG_EXPAND, HG_HEAD_V), lambda l: (0, 0, 0, 0))],
        out_shape=[jax.ShapeDtypeStruct((L, HG_VAL_DIM), BF16) for _ in range(B)]
        + [jax.ShapeDtypeStruct((B, HG_HEADS, HG_EXPAND, HG_HEAD_V), F32)],
        scratch_shapes=[pltpu.VMEM((B, HG_HEADS, HG_EXPAND, HG_HEAD_V), F32), pltpu.VMEM((B, Lb, W), F32),
                        pltpu.VMEM((B, Lb, W), F32), pltpu.VMEM((3, C, LANES), F32)],
        compiler_params=_params(("arbitrary",)),
        name="hgrn_prompt",
    )(*([P] * (4 * B)), gain)
    return outs[:B], outs[B]


def _hgrn_sample_kernel(q_ref, f_ref, v_ref, gate_ref, gain_ref, s_ref, o_ref, so_ref, o_scr, tmp_ref, *, G):
    NG = s_ref.shape[0]
    g = jnp.log(f_ref[...])
    pm = _prefix_matrix(NG * G, G)

    def run(factorised):
        states = [[s_ref[n, h] for h in range(HG_HEADS)] for n in range(NG)]
        o, new = _hgrn_chunk(q_ref[...], f_ref[...], g, v_ref[...], states, pm, tmp_ref, NG, factorised)
        o_scr[...] = o
        for n in range(NG):
            for h in range(HG_HEADS):
                so_ref[n, h] = new[n][h]

    safe = _decay_is_safe(g, G // 2)
    pl.when(safe)(lambda: run(True))
    pl.when(jnp.logical_not(safe))(lambda: run(False))
    _hgrn_epilogue(o_scr, gain_ref, gate_ref, o_ref)


def _hgrn_sample(P, gain, state, Bs, Ls, Bb, row0):
    W = HG_KEY_DIM
    rows = Bb * Ls
    blk0 = row0 // rows

    def col(seg):
        return pl.BlockSpec((rows, W), lambda i, seg=seg: (blk0 + i, seg))

    st = pl.BlockSpec((Bb, HG_HEADS, HG_EXPAND, HG_HEAD_V), lambda i: (i, 0, 0, 0))
    return pl.pallas_call(
        functools.partial(_hgrn_sample_kernel, G=Ls),
        grid=(Bs // Bb,),
        in_specs=[col(0), col(1), col(2), col(3), pl.BlockSpec((1, W), lambda i: (0, 0)), st],
        out_specs=[pl.BlockSpec((rows, W), lambda i: (i, 0)), st],
        out_shape=[
            jax.ShapeDtypeStruct((Bs * Ls, HG_VAL_DIM), BF16),
            jax.ShapeDtypeStruct((Bs, HG_HEADS, HG_EXPAND, HG_HEAD_V), F32),
        ],
        scratch_shapes=[pltpu.VMEM((rows, W), F32), pltpu.VMEM((3, Ls, LANES), F32)],
        compiler_params=_params(("parallel",)),
        name="hgrn_sample",
    )(P, P, P, P, gain, state)


def _rotary(x, cos, sa, sb):
    heads = []
    for h in range(x.shape[1] // LANES):
        xs = x[:, h * LANES:(h + 1) * LANES]
        heads.append(xs * cos + pltpu.roll(xs, LANES - 1, 1) * sa + pltpu.roll(xs, 1, 1) * sb)
    return jnp.concatenate(heads, axis=1)


def _ret_chunk(q, k, v, states, dm_ref, qw_ref, kw_ref, sd_ref):
    outs, new_states = [], []
    C = q.shape[0]
    qb, kb, vb = q.astype(BF16), k.astype(BF16), v.astype(BF16)
    for h in range(RET_HEADS):
        qs = slice(h * RET_HEAD_QK, (h + 1) * RET_HEAD_QK)
        vs = slice(h * RET_HEAD_V, (h + 1) * RET_HEAD_V)
        att = (_dot_nt(qb[:, qs], kb[:, qs]) * dm_ref[h]).astype(BF16)
        kw = (k[:, qs] * kw_ref[h]).astype(BF16)
        if C <= 2 * SUBLANES:
            both = _dot(jnp.concatenate([att, kw.T], axis=0), vb[:, vs])
            intra, upd = both[:C], both[C:]
        else:
            intra, upd = _dot(att, vb[:, vs]), _dot_tn(kw, vb[:, vs])
        o = intra + _dot((q[:, qs] * qw_ref[h]).astype(BF16), states[h].astype(BF16))
        new_states.append(sd_ref[h, 0:1, 0:1] * states[h] + upd)
        outs.append(o)
    return jnp.concatenate(outs, axis=1), new_states


def _ret_epilogue(o_scr, gate_ref, o_ref):
    for h in range(RET_HEADS):
        vs = slice(h * RET_HEAD_V, (h + 1) * RET_HEAD_V)
        o_ref[:, vs] = (_rms(o_scr[:, vs]) * gate_ref[:, vs]).astype(o_ref.dtype)


def _ret_prompt_kernel(*refs, NB):
    ins, (dm_ref, qw_ref, kw_ref, sd_ref), rest = refs[:4 * NB], refs[4 * NB:4 * NB + 4], refs[4 * NB + 4:]
    rope, rest = rest[:3], rest[3:]
    o_refs, (st_ref, s_scr, o_scr) = rest[:NB], rest[NB:]
    seq = [ins[4 * n:4 * n + 4] for n in range(NB)]
    l = pl.program_id(0)
    Lb = o_refs[0].shape[0]
    C = dm_ref.shape[1]

    @pl.when(l == 0)
    def _():
        s_scr[...] = jnp.zeros_like(s_scr)

    def body(ci, carry):
        rows = pl.ds(pl.multiple_of(ci * C, C), C)
        tabs = [t[rows, :] for t in rope]
        args = [(_rotary(seq[n][0][rows, :], *tabs), _rotary(seq[n][1][rows, :], *tabs), seq[n][2][rows, :],
                 [s_scr[n, h] for h in range(RET_HEADS)]) for n in range(NB)]
        outs = [_ret_chunk(*a, dm_ref, qw_ref, kw_ref, sd_ref) for a in args]
        for n, (o, new) in enumerate(outs):
            o_scr[n, rows, :] = o
            for h in range(RET_HEADS):
                s_scr[n, h] = new[h]
        return carry

    lax.fori_loop(0, Lb // C, body, 0)
    for n in range(NB):
        _ret_epilogue(o_scr.at[n], seq[n][3], o_refs[n])

    @pl.when(l == pl.num_programs(0) - 1)
    def _():
        st_ref[...] = s_scr[...]


def _ret_tables(C):
    log_gamma = jnp.log(1.0 - jnp.exp2(-5.0 - jnp.arange(RET_HEADS, dtype=F32)))
    idx = jnp.arange(C, dtype=F32)
    rel = idx[:, None] - idx[None, :]
    tri = jnp.tril(jnp.ones((C, C), dtype=bool))
    dmat = jnp.exp(jnp.where(tri[None], log_gamma[:, None, None] * rel[None], -jnp.inf))
    qw = jnp.exp(log_gamma[:, None] * (idx[None, :] + 1.0))[..., None]
    kw = jnp.exp(log_gamma[:, None] * (C - 1.0 - idx[None, :]))[..., None]
    sdec = jnp.exp(log_gamma * C)[:, None, None]
    bc = lambda a: jnp.broadcast_to(a, (RET_HEADS, a.shape[1], LANES))
    return dmat, bc(qw), bc(kw), jnp.broadcast_to(sdec, (RET_HEADS, SUBLANES, LANES))


_QK0 = (2 * HG_KEY_DIM + 2 * HG_VAL_DIM) // RET_QK_DIM
_V0 = (2 * HG_KEY_DIM + 2 * HG_VAL_DIM + 2 * RET_QK_DIM) // RET_V_DIM


def _full3(a):
    return pl.BlockSpec(a.shape, lambda *_: (0, 0, 0))


def _ret_prompt(P, rope, B, L, Lb):
    C = math.gcd(L, 2 * SCAN_CHUNK)
    Lb = max(Lb, C)
    nl = L // Lb
    tabs = _ret_tables(C)
    st_shape = (B, RET_HEADS, RET_HEAD_QK, RET_HEAD_V)
    blk = lambda b, w, c: pl.BlockSpec((Lb, w), lambda l, b=b, c=c: (b * nl + l, c))
    per_seq = lambda b: [blk(b, RET_QK_DIM, _QK0), blk(b, RET_QK_DIM, _QK0 + 1),
                         blk(b, RET_V_DIM, _V0), blk(b, RET_V_DIM, _V0 + 1)]
    outs = pl.pallas_call(
        functools.partial(_ret_prompt_kernel, NB=B),
        grid=(nl,),
        in_specs=[spec for b in range(B) for spec in per_seq(b)] + [_full3(t) for t in tabs]
        + [pl.BlockSpec((Lb, LANES), lambda l: (l, 0))] * 3,
        out_specs=[pl.BlockSpec((Lb, RET_V_DIM), lambda l: (l, 0)) for _ in range(B)]
        + [pl.BlockSpec(st_shape, lambda l: (0, 0, 0, 0))],
        out_shape=[jax.ShapeDtypeStruct((L, RET_V_DIM), BF16) for _ in range(B)]
        + [jax.ShapeDtypeStruct(st_shape, F32)],
        scratch_shapes=[pltpu.VMEM(st_shape, F32), pltpu.VMEM((B, Lb, RET_V_DIM), F32)],
        compiler_params=_params(("arbitrary",)),
        name="ret_prompt",
    )(*([P] * (4 * B)), *tabs, *rope)
    return outs[:B], outs[B]


def _ret_sample_kernel(q_ref, k_ref, v_ref, gate_ref, s_ref, dm_ref, qw_ref, kw_ref, sd_ref,
                       cos_ref, sa_ref, sb_ref, o_ref, so_ref, o_scr, *, C):
    Bb = s_ref.shape[0]
    tabs = (cos_ref[...], sa_ref[...], sb_ref[...])
    args = [(_rotary(q_ref[bb * C:(bb + 1) * C, :], *tabs), _rotary(k_ref[bb * C:(bb + 1) * C, :], *tabs),
             v_ref[bb * C:(bb + 1) * C, :], [s_ref[bb, h] for h in range(RET_HEADS)]) for bb in range(Bb)]
    outs = [_ret_chunk(*a, dm_ref, qw_ref, kw_ref, sd_ref) for a in args]
    for bb, (o, new) in enumerate(outs):
        o_scr[bb * C:(bb + 1) * C, :] = o
        for h in range(RET_HEADS):
            so_ref[bb, h] = new[h]
    _ret_epilogue(o_scr, gate_ref, o_ref)


def _ret_sample(P, rope, state, Bs, Ls, Bb, row0):
    tabs = _ret_tables(Ls)
    rows = Bb * Ls
    blk0 = row0 // rows
    st = pl.BlockSpec((Bb, RET_HEADS, RET_HEAD_QK, RET_HEAD_V), lambda i: (i, 0, 0, 0))
    return pl.pallas_call(
        functools.partial(_ret_sample_kernel, C=Ls),
        grid=(Bs // Bb,),
        in_specs=[
            pl.BlockSpec((rows, RET_QK_DIM), lambda i: (blk0 + i, _QK0)),
            pl.BlockSpec((rows, RET_QK_DIM), lambda i: (blk0 + i, _QK0 + 1)),
            pl.BlockSpec((rows, RET_V_DIM), lambda i: (blk0 + i, _V0)),
            pl.BlockSpec((rows, RET_V_DIM), lambda i: (blk0 + i, _V0 + 1)),
            st,
        ] + [_full3(t) for t in tabs] + [pl.BlockSpec((Ls, LANES), lambda i: (0, 0))] * 3,
        out_specs=[pl.BlockSpec((rows, RET_V_DIM), lambda i: (i, 0)), st],
        out_shape=[
            jax.ShapeDtypeStruct((Bs * Ls, RET_V_DIM), BF16),
            jax.ShapeDtypeStruct((Bs, RET_HEADS, RET_HEAD_QK, RET_HEAD_V), F32),
        ],
        scratch_shapes=[pltpu.VMEM((rows, RET_V_DIM), F32)],
        compiler_params=_params(("parallel",)),
        name="ret_sample",
    )(P, P, P, P, state, *tabs, *rope)


def _split_dot(a, w_hilo):
    a_hi = a.astype(BF16)
    a_lo = (a - a_hi.astype(F32)).astype(BF16)
    hi = _dot(a_hi, w_hilo)
    return hi[:, :LANES] + (hi[:, LANES:] + _dot(a_lo, w_hilo[:, :LANES]))


def _route(logits):
    lane = lax.broadcasted_iota(I32, logits.shape, 1)
    neg = jnp.float32(-jnp.inf)
    big = jnp.int32(LANES)
    gmask = lane < N_GROUPS
    gl = jnp.where(gmask, logits, neg)
    gmax = jnp.max(gl, axis=1, keepdims=True)
    gu = jnp.where(gmask, jnp.exp(gl - gmax), 0.0)
    gp = gu / jnp.sum(gu, axis=1, keepdims=True)
    g_w = jnp.max(gp, axis=1, keepdims=True)
    g_i = jnp.min(jnp.where(gmask & (gp == g_w), lane, big), axis=1, keepdims=True)
    lo = ROUTE_EXPERT_LANE + g_i * EXPERTS_PER_GROUP
    emask = (lane >= lo) & (lane < lo + EXPERTS_PER_GROUP)
    el = jnp.where(emask, logits, neg)
    e1 = jnp.max(el, axis=1, keepdims=True)
    i1 = jnp.min(jnp.where(emask & (el == e1), lane, big), axis=1, keepdims=True)
    el2 = jnp.where(lane == i1, neg, el)
    e2 = jnp.max(el2, axis=1, keepdims=True)
    i2 = jnp.min(jnp.where(emask & (lane != i1) & (el2 == e2), lane, big), axis=1, keepdims=True)
    u2 = jnp.exp(e2 - e1)
    den = 1.0 + u2
    w1 = (1.0 / den) * g_w
    w2 = (u2 / den) * g_w
    id1 = (i1 - ROUTE_EXPERT_LANE).astype(F32)
    id2 = (i2 - ROUTE_EXPERT_LANE).astype(F32)
    return (jnp.where(lane == LANE_E1, id1, 0.0) + jnp.where(lane == LANE_E2, id2, 0.0)
            + jnp.where(lane == LANE_W1, w1, 0.0) + jnp.where(lane == LANE_W2, w2, 0.0))


def _pick(i, bounds, refs):
    val = refs[-1][...]
    for n in range(len(refs) - 2, -1, -1):
        val = jnp.where(i < bounds[n + 1], refs[n][...], val)
    return val


def _mix_kernel(*refs, bounds):
    ns = len(bounds) - 1
    ohg_refs, or_refs, refs = refs[:ns], refs[ns:2 * ns], refs[2 * ns:]
    (ga0_ref, ga1_ref, gb0_ref, gb1_ref, xp_ref, xs_ref, wbh_ref, wbr_ref, wout_ref, nffn_ref,
     wr_ref, br_ref, h_ref, hn_ref, route_ref) = refs
    i = pl.program_id(0)
    tm = h_ref.shape[0]
    is_prompt = i < bounds[-2]
    p_hg = _dot(_pick(i, bounds, ohg_refs), wbh_ref[...])
    p_r = _dot(_pick(i, bounds, or_refs), wbr_ref[...])
    half = D_MODEL // 2
    m0 = ga0_ref[...] * p_hg[:, :half] + gb0_ref[...] * p_r[:, :half]
    m1 = ga1_ref[...] * p_hg[:, half:] + gb1_ref[...] * p_r[:, half:]
    mixed = jnp.concatenate([m0, m1], axis=1).astype(BF16)
    x = jnp.where(is_prompt, xp_ref[...], xs_ref[...])
    h = x + _dot(mixed, wout_ref[...])
    h_ref[...] = h
    hn = _rms(h) * nffn_ref[...]
    hn_ref[...] = _to_token_tiles(hn.astype(BF16))
    route_ref[...] = _route(_split_dot(hn, wr_ref[...]) + br_ref[...])


def _mix(o_hg, o_r, P, xp, xs, wbh, wbr, wout, norm_ffn, wr, br, tm):
    npt, nst = xp.shape[0] // tm, xs.shape[0] // tm
    bounds = [0]
    for a in o_hg:
        bounds.append(bounds[-1] + a.shape[0] // tm)
    T = (npt + nst) * tm
    half = D_MODEL // 2
    ga = (IN_WIDTH - 2 * D_MODEL) // half

    def pcol(c):
        return pl.BlockSpec((tm, half), lambda i, c=c: (i, c))

    def const(a):
        return pl.BlockSpec(a.shape, lambda i: (0, 0), pipeline_mode=pl.Buffered(1))

    def prompt(w):
        return pl.BlockSpec((tm, w), lambda i: (jnp.minimum(i, npt - 1), 0))

    def sample(w):
        return pl.BlockSpec((tm, w), lambda i: (jnp.maximum(i - npt, 0), 0))

    def source(n):
        lo, hi = bounds[n], bounds[n + 1]
        return pl.BlockSpec((tm, half), lambda i: (jnp.clip(i - lo, 0, hi - lo - 1), 0))

    sources = [source(n) for n in range(len(o_hg))]
    return pl.pallas_call(
        functools.partial(_mix_kernel, bounds=tuple(bounds)),
        grid=(npt + nst,),
        in_specs=sources + sources + [
            pcol(ga), pcol(ga + 1), pcol(ga + 2), pcol(ga + 3),
            prompt(D_MODEL), sample(D_MODEL),
            const(wbh), const(wbr), const(wout), const(norm_ffn), const(wr), const(br),
        ],
        out_specs=[
            pl.BlockSpec((tm, D_MODEL), lambda i: (i, 0)),
            pl.BlockSpec((tm * ROW_GROUP, LANES), lambda i: (i, 0)),
            pl.BlockSpec((tm, LANES), lambda i: (i, 0)),
        ],
        out_shape=[
            jax.ShapeDtypeStruct((T, D_MODEL), F32),
            jax.ShapeDtypeStruct((T * ROW_GROUP, LANES), BF16),
            jax.ShapeDtypeStruct((T, LANES), F32),
        ],
        compiler_params=_params(("parallel",)),
        name="mix",
    )(*o_hg, *o_r, P, P, P, P, xp, xs, wbh, wbr, wout, norm_ffn, wr, br)


def _plan_kernel(route_ref, pos_ref, te_ref, tail_ref, cum_scr, *, tb):
    T = route_ref.shape[0]
    lane_t = lax.broadcasted_iota(I32, (tb, LANES), 1).astype(F32)
    r = lax.broadcasted_iota(I32, (tb, tb), 0)
    c = lax.broadcasted_iota(I32, (tb, tb), 1)
    strict_lower = jnp.where(r > c, 1.0, 0.0).astype(BF16)

    def onehots(rows):
        rt = route_ref[rows, :]
        a1 = jnp.where(lane_t == rt[:, LANE_E1:LANE_E1 + 1], 1.0, 0.0)
        a2 = jnp.where(lane_t == rt[:, LANE_E2:LANE_E2 + 1], 1.0, 0.0)
        return a1, a2

    def rank_body(bi, carry):
        rows = pl.ds(pl.multiple_of(bi * tb, tb), tb)
        a1, a2 = onehots(rows)
        m = a1 + a2
        cum_scr[rows, :] = _dot(strict_lower, m.astype(BF16)) + carry
        return carry + jnp.sum(m, axis=0, keepdims=True)

    counts = lax.fori_loop(0, T // tb, rank_body, jnp.zeros((1, LANES), F32))
    ntiles = jnp.zeros_like(counts)
    for kt in range(-(-T // MOE_TILE)):
        ntiles = ntiles + jnp.where(counts > float(kt * MOE_TILE), 1.0, 0.0)
    rr = lax.broadcasted_iota(I32, (LANES, LANES), 0)
    cc = lax.broadcasted_iota(I32, (LANES, LANES), 1)
    upper = jnp.where(rr < cc, 1.0, 0.0).astype(BF16)
    tile0 = _dot(jnp.broadcast_to(ntiles, (SUBLANES, LANES)).astype(BF16), upper)[0:1, :]
    slot0 = tile0 * MOE_TILE

    def pos_body(bi, carry):
        rows = pl.ds(pl.multiple_of(bi * tb, tb), tb)
        a1, a2 = onehots(rows)
        base = cum_scr[rows, :] + slot0
        p1 = jnp.sum(a1 * base, axis=1, keepdims=True)
        p2 = jnp.sum(a2 * base, axis=1, keepdims=True)
        pos_ref[rows, :] = (jnp.where(lane_t == 0.0, p1, 0.0) + jnp.where(lane_t == 1.0, p2, 0.0)).astype(I32)
        return carry

    lax.fori_loop(0, T // tb, pos_body, 0)
    tend = tile0 + ntiles
    lane = lax.broadcasted_iota(I32, (LANES, LANES), 1)
    tile_i = lax.broadcasted_iota(I32, (LANES, LANES), 0).astype(F32)
    is_expert = lane < N_EXPERTS
    te = jnp.sum(jnp.where(is_expert & (jnp.broadcast_to(tend, (LANES, LANES)) <= tile_i), 1.0, 0.0),
                 axis=1, keepdims=True)
    te_ref[...] = jnp.broadcast_to(jnp.minimum(te, N_EXPERTS - 1.0), (LANES, LANES)).astype(I32)
    lane1 = lax.broadcasted_iota(I32, (1, LANES), 1)
    total = jnp.sum(jnp.where(lane1 < N_EXPERTS, ntiles, 0.0), axis=1, keepdims=True)
    tail = jnp.where(ntiles > 0.0, (tend - 1.0) * MOE_TILE, -1.0)
    tail = jnp.where(lane1 < N_EXPERTS, tail, jnp.where(lane1 == N_EXPERTS, total, 0.0))
    tail_ref[...] = jnp.broadcast_to(tail, (SUBLANES, LANES)).astype(I32)


def _plan(route, tb):
    T = route.shape[0]
    return pl.pallas_call(
        functools.partial(_plan_kernel, tb=tb),
        grid=(1,),
        in_specs=[pl.BlockSpec((T, LANES), lambda i: (0, 0))],
        out_specs=[
            pl.BlockSpec((T, LANES), lambda i: (0, 0)),
            pl.BlockSpec((LANES, LANES), lambda i: (0, 0)),
            pl.BlockSpec((SUBLANES, LANES), lambda i: (0, 0)),
        ],
        out_shape=[
            jax.ShapeDtypeStruct((T, LANES), I32),
            jax.ShapeDtypeStruct((LANES, LANES), I32),
            jax.ShapeDtypeStruct((SUBLANES, LANES), I32),
        ],
        scratch_shapes=[pltpu.VMEM((T, LANES), F32)],
        compiler_params=_params(("arbitrary",)),
        name="moe_plan",
    )(route)


def _row_group(ref, idx):
    return ref.at[pl.ds(pl.multiple_of(idx * ROW_GROUP, ROW_GROUP), ROW_GROUP), :]


def _dispatch_kernel(pos_ref, tail_ref, hn_ref, xs_ref, zero_scr, zsem, sem):
    i = pl.program_id(0)
    tm = hn_ref.shape[0] // ROW_GROUP
    tile_rows = MOE_TILE * ROW_GROUP

    @pl.when(i == 0)
    def _():
        zero_scr[...] = jnp.zeros_like(zero_scr)
        n_used = tail_ref[N_EXPERTS]

        def zero_copy(first_slot):
            start = pl.multiple_of(first_slot * ROW_GROUP, tile_rows)
            return pltpu.make_async_copy(zero_scr, xs_ref.at[pl.ds(start, tile_rows), :], zsem)

        def tails(fn):
            def body(e, carry):
                pl.when(tail_ref[e] >= 0)(lambda: fn(zero_copy(jnp.maximum(tail_ref[e], 0))))
                return carry
            lax.fori_loop(0, N_EXPERTS, body, 0)

        def unused(fn):
            def body(t, carry):
                fn(zero_copy(t * MOE_TILE))
                return carry
            lax.fori_loop(n_used, xs_ref.shape[0] // tile_rows, body, 0)

        tails(lambda c: c.start())
        unused(lambda c: c.start())
        tails(lambda c: c.wait())
        unused(lambda c: c.wait())

    def copy(r, k):
        slot = pos_ref[(i * tm + r) * TOP_K + k]
        return pltpu.make_async_copy(_row_group(hn_ref, r), _row_group(xs_ref, slot), sem)

    def start(r, carry):
        for k in range(TOP_K):
            copy(r, k).start(priority=k)
        return carry

    lax.fori_loop(0, tm, start, 0, unroll=8)
    for _ in range(TOP_K):
        pltpu.make_async_copy(hn_ref, xs_ref.at[pl.ds(0, tm * ROW_GROUP), :], sem).wait()


def _dispatch(pos, tails, hn_rows, n_slots, tm):
    T = hn_rows.shape[0] // ROW_GROUP
    return pl.pallas_call(
        _dispatch_kernel,
        grid_spec=pltpu.PrefetchScalarGridSpec(
            num_scalar_prefetch=2,
            grid=(T // tm,),
            in_specs=[pl.BlockSpec((tm * ROW_GROUP, LANES), lambda i, *_: (i, 0))],
            out_specs=pl.BlockSpec(memory_space=pl.ANY),
            scratch_shapes=[pltpu.VMEM((MOE_TILE * ROW_GROUP, LANES), BF16),
                            pltpu.SemaphoreType.DMA, pltpu.SemaphoreType.DMA],
        ),
        out_shape=jax.ShapeDtypeStruct((n_slots * ROW_GROUP, LANES), BF16),
        compiler_params=_params(("arbitrary",)),
        name="moe_dispatch",
    )(pos, tails, hn_rows)


def _experts_kernel(te_ref, nt_ref, xs_ref, wg_hbm, wu_hbm, wd_hbm, ys_ref,
                    wg_buf, wu_buf, wd_buf, rank_ref, sem):
    i = pl.program_id(0)
    nt = nt_ref[0]
    e = te_ref[i]

    def fetch(expert, slot):
        return [pltpu.make_async_copy(hbm.at[expert], buf.at[slot], sem.at[slot, n])
                for n, (hbm, buf) in enumerate(((wg_hbm, wg_buf), (wu_hbm, wu_buf), (wd_hbm, wd_buf)))]

    def start_fetch(tile, slot):
        for c in fetch(te_ref[jnp.minimum(tile, nt - 1)], slot):
            c.start(priority=1)

    def next_change(tile):
        ex = te_ref[jnp.minimum(tile, nt - 1)]
        return lax.while_loop(lambda j: (j < nt) & (te_ref[jnp.minimum(j, nt - 1)] == ex), lambda j: j + 1, tile + 1)

    @pl.when(i == 0)
    def _():
        rank_ref[0] = -1
        start_fetch(0, 0)
        n1 = next_change(0)
        pl.when(n1 < nt)(lambda: start_fetch(n1, 1))

    @pl.when((i < nt) & ((i == 0) | (te_ref[jnp.maximum(i - 1, 0)] != e)))
    def _():
        rank = rank_ref[0] + 1
        rank_ref[0] = rank
        for c in fetch(e, rank % WEIGHT_SLOTS):
            c.wait()
        n1 = next_change(i)

        @pl.when(n1 < nt)
        def _():
            n2 = next_change(n1)
            pl.when(n2 < nt)(lambda: start_fetch(n2, (rank + 2) % WEIGHT_SLOTS))

    @pl.when(i < nt)
    def _():
        slot = rank_ref[0] % WEIGHT_SLOTS
        x = _from_token_tiles(xs_ref[...])[0]
        hmid = (_silu(_dot(x, wg_buf[slot].astype(BF16))) * _dot(x, wu_buf[slot].astype(BF16))).astype(BF16)
        ys_ref[...] = _to_token_tiles(_dot(hmid, wd_buf[slot].astype(BF16)))

    @pl.when(i >= nt)
    def _():
        ys_ref[...] = jnp.zeros_like(ys_ref)


def _experts(te, nt, xs_rows, w_ge, w_ue, w_de, n_tiles):
    M = MOE_TILE
    rows = pl.BlockSpec((M * ROW_GROUP, LANES), lambda i, te_ref, nt_ref: (jnp.minimum(i, nt_ref[0] - 1), 0))
    hbm = pl.BlockSpec(memory_space=pl.ANY)
    up, down = (D_MODEL, EXPERT_FF), (EXPERT_FF, D_MODEL)
    return pl.pallas_call(
        _experts_kernel,
        grid_spec=pltpu.PrefetchScalarGridSpec(
            num_scalar_prefetch=2,
            grid=(n_tiles,),
            in_specs=[rows, hbm, hbm, hbm],
            out_specs=pl.BlockSpec((M * ROW_GROUP, LANES), lambda i, te_ref, nt_ref: (i, 0)),
            scratch_shapes=[pltpu.VMEM((WEIGHT_SLOTS,) + up, F32), pltpu.VMEM((WEIGHT_SLOTS,) + up, F32),
                            pltpu.VMEM((WEIGHT_SLOTS,) + down, F32),
                            pltpu.SMEM((1,), I32), pltpu.SemaphoreType.DMA((WEIGHT_SLOTS, 3))],
        ),
        out_shape=jax.ShapeDtypeStruct(xs_rows.shape, F32),
        compiler_params=_params(("arbitrary",)),
        name="moe_experts",
    )(te, nt, xs_rows, w_ge, w_ue, w_de)


def _final_kernel(pos_ref, h_ref, route_ref, gain_ref, ys_ref, y_ref, buf, sem, *, tile0):
    i = pl.program_id(0)
    n = pl.num_programs(0)
    tm = h_ref.shape[0]

    def copy(step, r, k, slot_buf):
        slot = pos_ref[((tile0 + step) * tm + r) * TOP_K + k]
        return pltpu.make_async_copy(_row_group(ys_ref, slot), _row_group(buf.at[slot_buf], r * TOP_K + k),
                                     sem.at[slot_buf])

    def start_all(step, slot_buf):
        def body(r, carry):
            for k in range(TOP_K):
                copy(step, r, k, slot_buf).start(priority=k)
            return carry
        lax.fori_loop(0, tm, body, 0, unroll=8)

    cur = i % 2
    pl.when(i == 0)(lambda: start_all(0, 0))
    pl.when(i + 1 < n)(lambda: start_all(i + 1, 1 - cur))
    pltpu.make_async_copy(ys_ref.at[pl.ds(0, buf.shape[1]), :], buf.at[cur], sem.at[cur]).wait()

    rt = route_ref[...]
    w1 = rt[:, LANE_W1:LANE_W1 + 1]
    w2 = rt[:, LANE_W2:LANE_W2 + 1]
    o1, o2 = _from_token_tiles(buf[cur], groups=TOP_K)
    v = h_ref[...] + (w1 * o1 + w2 * o2)
    y_ref[...] = _rms(v) * gain_ref[...]


def _final(pos, h, route, gain, ys_rows, tile0, n_tiles, tm):
    return pl.pallas_call(
        functools.partial(_final_kernel, tile0=tile0),
        grid_spec=pltpu.PrefetchScalarGridSpec(
            num_scalar_prefetch=1,
            grid=(n_tiles,),
            in_specs=[
                pl.BlockSpec((tm, D_MODEL), lambda i, *_: (tile0 + i, 0)),
                pl.BlockSpec((tm, LANES), lambda i, *_: (tile0 + i, 0)),
                pl.BlockSpec((1, D_MODEL), lambda i, *_: (0, 0)),
                pl.BlockSpec(memory_space=pl.ANY),
            ],
            out_specs=pl.BlockSpec((tm, D_MODEL), lambda i, *_: (i, 0)),
            scratch_shapes=[pltpu.VMEM((2, tm * TOP_K * ROW_GROUP, LANES), F32), pltpu.SemaphoreType.DMA((2,))],
        ),
        out_shape=jax.ShapeDtypeStruct((n_tiles * tm, D_MODEL), F32),
        compiler_params=_params(("arbitrary",)),
        name="final",
    )(pos, h, route, gain, ys_rows)


def _rope_tables(pos):
    half = RET_HEAD_QK // 2
    inv = 1.0 / (ROPE_BASE ** jnp.linspace(0.0, 1.0, half, dtype=F32))
    ang = jnp.repeat(pos[:, None] * inv[None, :], 2, axis=-1)
    cos, sin = jnp.cos(ang), jnp.sin(ang)
    even = (jnp.arange(RET_HEAD_QK) % 2 == 0)[None, :]
    sa = jnp.where(even, -sin, 0.0)
    sb = jnp.where(even, 0.0, sin)
    return cos, sa, sb


def _tile_sizes(Tp, Ts, Lp):
    g = math.gcd(Tp, Ts)
    return dict(
        xnorm=math.gcd(g, 512),
        proj=(Tp + Ts) // 4,
        mix=math.gcd(g, 256),
        plan=math.gcd(Tp + Ts, 512),
        dispatch=math.gcd(g, 1024),
        final=math.gcd(g, 256),
        scan=math.gcd(Lp, 128),
    )


def kernel(x_prompt, x_sample, state_hgrn, state_ret, norm_mix, w_in, hg_lb_logits, hg_norm,
           w_branch_hg, w_branch_ret, b_gate, w_out, norm_ffn, w_router_group, b_router_group,
           w_router_expert, b_router_expert, w_expert_gate, w_expert_up, w_expert_down, norm_final):
    assert w_in.shape == (1, D_MODEL, IN_WIDTH), "single-layer trunk only"
    Bp, Lp, _ = x_prompt.shape
    Bs, Ls, _ = x_sample.shape
    Tp, Ts = Bp * Lp, Bs * Ls
    T = Tp + Ts
    ts = _tile_sizes(Tp, Ts, Lp)
    xp = x_prompt.reshape(Tp, D_MODEL)
    xs = x_sample.reshape(Ts, D_MODEL)

    lower = jnp.cumsum(jax.nn.softmax(hg_lb_logits.astype(F32), axis=0), axis=0)[0]
    act = _act_coefficients(lower, b_gate[0])
    rope_p = _rope_tables(jnp.arange(Lp, dtype=F32))
    rope_s = _rope_tables(jnp.arange(Ls, dtype=F32) + jnp.float32(PAST_LEN))
    xn = _xnorm(xp, xs, norm_mix, ts["xnorm"])
    P = _in_proj(xn, w_in[0], act, ts["proj"])

    bb = math.gcd(Bs, 16)
    o_hg_p, shp = _hgrn_prompt(P, hg_norm, Bp, Lp, ts["scan"])
    o_hg_s, shs = _hgrn_sample(P, hg_norm, state_hgrn[0].astype(F32), Bs, Ls, bb, Tp)
    o_r_p, srp = _ret_prompt(P, rope_p, Bp, Lp, ts["scan"])
    o_r_s, srs = _ret_sample(P, rope_s, state_ret[0].astype(F32), Bs, Ls, bb, Tp)

    wbh = w_branch_hg[0].astype(BF16)
    wbr = w_branch_ret[0].astype(BF16)
    wout = w_out[0].astype(BF16)
    pad = LANES - N_GROUPS - N_EXPERTS
    wr = jnp.concatenate([w_router_group[0], w_router_expert[0], jnp.zeros((D_MODEL, pad), F32)], axis=1)
    wr_hi = wr.astype(BF16)
    wr_hilo = jnp.concatenate([wr_hi, (wr - wr_hi.astype(F32)).astype(BF16)], axis=1)
    br = jnp.concatenate([b_router_group[0], b_router_expert[0], jnp.zeros((pad,), F32)])[None, :]
    h, hn_rows, route = _mix(list(o_hg_p) + [o_hg_s], list(o_r_p) + [o_r_s], P, xp, xs, wbh, wbr, wout,
                             norm_ffn, wr_hilo, br, ts["mix"])

    n_tiles = -(-T * TOP_K // MOE_TILE) + N_EXPERTS
    assert n_tiles <= LANES
    pos_slab, te_slab, tail_slab = _plan(route, ts["plan"])
    pos = pos_slab[:, :TOP_K].reshape(-1)
    te = te_slab[:n_tiles, 0]
    tails = tail_slab[0, :N_EXPERTS + 1]
    nt = tail_slab[0, N_EXPERTS:N_EXPERTS + 1]
    xs_rows = _dispatch(pos, tails, hn_rows, n_tiles * MOE_TILE, ts["dispatch"])
    ys_rows = _experts(te, nt, xs_rows, w_expert_gate[0], w_expert_up[0], w_expert_down[0], n_tiles)

    tf = ts["final"]
    gain = norm_final[None, :]
    y_p = _final(pos, h, route, gain, ys_rows, 0, Tp // tf, tf)
    y_s = _final(pos, h, route, gain, ys_rows, Tp // tf, Ts // tf, tf)
    return (y_p.reshape(Bp, Lp, D_MODEL), y_s.reshape(Bs, Ls, D_MODEL),
            shp[None], srp[None], shs[None], srs[None])
```

```python
import functools
import math

import jax
import jax.numpy as jnp
from jax import lax
from jax.experimental import pallas as pl
from jax.experimental.pallas import tpu as pltpu

F32 = jnp.float32
BF16 = jnp.bfloat16
I32 = jnp.int32

D_MODEL = 2048
PAST_LEN = 16384
HG_HEAD_V = 128
HG_EXPAND = 128
HG_VAL_DIM = D_MODEL // 2
HG_HEADS = HG_VAL_DIM // HG_HEAD_V
HG_KEY_DIM = HG_HEADS * HG_EXPAND
RET_HEAD_V = 256
RET_HEAD_QK = 128
RET_V_DIM = D_MODEL // 2
RET_HEADS = RET_V_DIM // RET_HEAD_V
RET_QK_DIM = RET_HEADS * RET_HEAD_QK
IN_WIDTH = 2 * HG_KEY_DIM + 2 * HG_VAL_DIM + 2 * RET_QK_DIM + 2 * RET_V_DIM + 2 * D_MODEL
SCAN_CHUNK = 128
N_GROUPS = 4
EXPERTS_PER_GROUP = 8
N_EXPERTS = N_GROUPS * EXPERTS_PER_GROUP
TOP_K = 2
EXPERT_FF = D_MODEL // 4
ROPE_BASE = 10000.0
EPS = 1e-6

LANES = 128
SUBLANES = 8
VMEM_LIMIT = 56 * 1024 * 1024
PROJ_MC = 256
PROJ_TN = 512
SEG_HQ, SEG_HF, SEG_HI, SEG_HG, SEG_RQ, SEG_RK, SEG_RV, SEG_RG, SEG_GA = 0, 2, 4, 6, 8, 9, 10, 12, 14
SAFE_EXP_SPAN = 80.0
ROW_GROUP = D_MODEL // LANES
MOE_TILE = 320
WEIGHT_SLOTS = 3
LANE_E1, LANE_E2, LANE_W1, LANE_W2 = 0, 1, 2, 3
ROUTE_EXPERT_LANE = N_GROUPS


def _params(sem):
    return pltpu.CompilerParams(dimension_semantics=sem, vmem_limit_bytes=VMEM_LIMIT)


def _silu(z):
    return z * jax.nn.sigmoid(z)


def _dot(a, b):
    return jnp.dot(a, b, preferred_element_type=F32)


def _dot_nt(a, b):
    return lax.dot_general(a, b, (((1,), (1,)), ((), ())), preferred_element_type=F32)


def _dot_tn(a, b):
    return lax.dot_general(a, b, (((0,), (0,)), ((), ())), preferred_element_type=F32)


def _rms(x):
    return x * lax.rsqrt(jnp.mean(x * x, axis=-1, keepdims=True) + EPS)


def _to_token_tiles(x):
    M = x.shape[0]
    pieces = jnp.stack([x[:, s * LANES:(s + 1) * LANES] for s in range(ROW_GROUP)], axis=0)
    return jnp.swapaxes(pieces, 0, 1).reshape(M * ROW_GROUP, LANES)


def _from_token_tiles(rows, groups=1):
    M = rows.shape[0] // (groups * ROW_GROUP)
    t = jnp.swapaxes(rows.reshape(M, groups * ROW_GROUP, LANES), 0, 1)
    return [jnp.concatenate([t[g * ROW_GROUP + s] for s in range(ROW_GROUP)], axis=1) for g in range(groups)]


def _xnorm_kernel(xp_ref, xs_ref, gain_ref, o_ref, *, n_prompt_tiles):
    i = pl.program_id(0)

    def norm(x_ref):
        o_ref[...] = (_rms(x_ref[...]) * gain_ref[...]).astype(BF16)

    pl.when(i < n_prompt_tiles)(lambda: norm(xp_ref))
    pl.when(i >= n_prompt_tiles)(lambda: norm(xs_ref))


def _xnorm(xp, xs, gain, tm):
    npt, nst = xp.shape[0] // tm, xs.shape[0] // tm
    return pl.pallas_call(
        functools.partial(_xnorm_kernel, n_prompt_tiles=npt),
        grid=(npt + nst,),
        in_specs=[
            pl.BlockSpec((tm, D_MODEL), lambda i: (jnp.minimum(i, npt - 1), 0)),
            pl.BlockSpec((tm, D_MODEL), lambda i: (jnp.maximum(i - npt, 0), 0)),
            pl.BlockSpec((1, D_MODEL), lambda i: (0, 0)),
        ],
        out_specs=pl.BlockSpec((tm, D_MODEL), lambda i: (i, 0)),
        out_shape=jax.ShapeDtypeStruct(((npt + nst) * tm, D_MODEL), BF16),
        compiler_params=_params(("parallel",)),
        name="xnorm",
    )(xp, xs, gain)


ACT_ALPHA, ACT_BETA, ACT_GAMMA, ACT_DELTA, ACT_EPS = 0, 1, 2, 3, 4


def _act_coefficients(lower, b_gate):
    z = jnp.zeros((IN_WIDTH,), F32)
    seg = lambda a, b: slice(a * PROJ_TN, b * PROJ_TN)
    alpha = (z.at[seg(SEG_HI, SEG_HG)].set(1.0).at[seg(SEG_RQ, SEG_RG)].set(1.0)
             .at[seg(SEG_RK, SEG_RV)].set(RET_HEAD_QK ** -0.5))
    beta = z.at[seg(SEG_GA, IN_WIDTH // PROJ_TN)].set(b_gate)
    gamma = (z.at[seg(SEG_HQ, SEG_HF)].set(HG_EXPAND ** -0.5)
             .at[seg(SEG_HG, SEG_RQ)].set(1.0).at[seg(SEG_RG, SEG_GA)].set(1.0))
    delta = z.at[seg(SEG_HF, SEG_HI)].set(1.0 - lower).at[seg(SEG_GA, IN_WIDTH // PROJ_TN)].set(1.0)
    eps = z.at[seg(SEG_HF, SEG_HI)].set(lower)
    return jnp.stack([alpha, beta, gamma, delta, eps, z, z, z])


def _in_proj_kernel(x_ref, w_ref, act_ref, o_ref):
    w = w_ref[...].astype(BF16)
    row = lambda r: act_ref[r:r + 1, :]
    tm = x_ref.shape[0]
    mc = math.gcd(tm, PROJ_MC)
    for m in range(0, tm, mc):
        p = _dot(x_ref[m:m + mc, :], w)
        o_ref[m:m + mc, :] = (row(ACT_ALPHA) * p + row(ACT_EPS)
                              + jax.nn.sigmoid(p + row(ACT_BETA)) * (row(ACT_GAMMA) * p + row(ACT_DELTA)))


def _in_proj(xn, w_in, act, tm):
    T = xn.shape[0]
    return pl.pallas_call(
        _in_proj_kernel,
        grid=(T // tm, IN_WIDTH // PROJ_TN),
        in_specs=[
            pl.BlockSpec((tm, D_MODEL), lambda i, j: (i, 0)),
            pl.BlockSpec((D_MODEL, PROJ_TN), lambda i, j: (0, j)),
            pl.BlockSpec((SUBLANES, PROJ_TN), lambda i, j: (0, j)),
        ],
        out_specs=pl.BlockSpec((tm, PROJ_TN), lambda i, j: (i, j)),
        out_shape=jax.ShapeDtypeStruct((T, IN_WIDTH), F32),
        compiler_params=_params(("parallel", "arbitrary")),
        name="in_proj",
    )(xn, w_in, act)


def _prefix_matrix(C, G):
    r = lax.broadcasted_iota(I32, (C, 3 * C), 0)
    c = lax.broadcasted_iota(I32, (C, 3 * C), 1) % C
    return jnp.where((c <= r) & (c // G == r // G), 1.0, 0.0).astype(BF16)


def _prefix_sum(g, pm):
    g1 = g.astype(BF16)
    r1 = g - g1.astype(F32)
    g2 = r1.astype(BF16)
    g3 = (r1 - g2.astype(F32)).astype(BF16)
    return _dot(pm, jnp.concatenate([g1, g2, g3], axis=0))


def _state_decay_column(d_row):
    r = lax.broadcasted_iota(I32, (LANES, LANES), 0)
    c = lax.broadcasted_iota(I32, (LANES, LANES), 1)
    return jnp.sum(jnp.where(r == c, jnp.broadcast_to(d_row, (LANES, LANES)), 0.0), axis=1, keepdims=True)


def _hgrn_chunk(q, f, g, v, states, pm, tmp_ref, NG, factorised):
    C, W = q.shape
    G, H = C // NG, W // LANES
    hs = [slice(h * LANES, (h + 1) * LANES) for h in range(H)]
    gs = [slice(n * G, (n + 1) * G) for n in range(NG)]
    k = 1.0 - f
    b = _prefix_sum(g, pm)
    b3 = b.reshape(NG, G, W)
    b_end = b3[:, G - 1:G, :]
    q3, k3 = q.reshape(NG, G, W), k.reshape(NG, G, W)
    vb = v.astype(BF16)
    if factorised:
        b_mid = b3[:, G // 2 - 1:G // 2, :]
        qm3 = q3 * jnp.exp(b3 - b_mid)
        km3 = k3 * jnp.exp(b_mid - b3)
        qb = (qm3 * jnp.exp(b_mid)).reshape(C, W)
        kd = (km3 * jnp.exp(b_end - b_mid)).reshape(C, W)
        qm = qm3.reshape(C, W).astype(BF16)
        km = km3.reshape(C, W).astype(BF16)
        row = lax.broadcasted_iota(I32, (C, C), 0)
        col = lax.broadcasted_iota(I32, (C, C), 1)
        amask = (row >= col) & (row // G == col // G)
        o_intra = []
        for h in range(H):
            att = jnp.where(amask, _dot_nt(qm[:, hs[h]], km[:, hs[h]]), 0.0)
            o_intra.append(_dot(att.astype(BF16), vb[:, hs[h]]))
    else:
        qb = (q3 * jnp.exp(b3)).reshape(C, W)
        kd = (k3 * jnp.exp(b_end - b3)).reshape(C, W)
        trow = lax.broadcasted_iota(I32, (G, LANES), 0)
        o_intra = []
        for h in range(H):
            parts = []
            for n in range(NG):
                bh, qh = b[gs[n], hs[h]], q[gs[n], hs[h]]
                tmp_ref[0] = bh
                tmp_ref[1] = k[gs[n], hs[h]]
                tmp_ref[2] = v[gs[n], hs[h]]

                def body(s, acc, bh=bh, qh=qh):
                    bs = tmp_ref[0, pl.ds(s, 1), :]
                    ks = tmp_ref[1, pl.ds(s, 1), :]
                    vs = tmp_ref[2, pl.ds(s, 1), :]
                    w = jnp.where(trow >= s, jnp.exp(jnp.minimum(bh - bs, 0.0)), 0.0)
                    return acc + jnp.sum(qh * ks * w, axis=1, keepdims=True) * vs

                parts.append(lax.fori_loop(0, G, body, jnp.zeros((G, LANES), F32)))
            o_intra.append(parts[0] if NG == 1 else jnp.concatenate(parts, axis=0))
    d = jnp.exp(b_end)
    outs, new_states = [], [[None] * H for _ in range(NG)]
    for h in range(H):
        inter = []
        for n in range(NG):
            S = states[n][h]
            inter.append(_dot(qb[gs[n], hs[h]].astype(BF16), S.astype(BF16)))
            upd = _dot_tn(kd[gs[n], hs[h]].astype(BF16), v[gs[n], hs[h]].astype(BF16))
            new_states[n][h] = _state_decay_column(d[n, :, hs[h]]) * S + upd
        inter = inter[0] if NG == 1 else jnp.concatenate(inter, axis=0)
        outs.append(inter + o_intra[h])
    return jnp.concatenate(outs, axis=1), new_states


def _decay_is_safe(g, G):
    R, W = g.shape
    return jnp.min(jnp.sum(g.reshape(R // G, G, W), axis=1)) >= -SAFE_EXP_SPAN


def _hgrn_epilogue(o_scr, gain_ref, gate_ref, o_ref):
    o_ref[...] = (_rms(o_scr[...]) * gain_ref[...] * gate_ref[...]).astype(o_ref.dtype)


def _hgrn_prompt_kernel(*refs, C, NB):
    ins, (gain_ref,), rest = refs[:4 * NB], refs[4 * NB:4 * NB + 1], refs[4 * NB + 1:]
    o_refs, (st_ref, s_scr, g_scr, o_scr, tmp_ref) = rest[:NB], rest[NB:]
    seq = [ins[4 * n:4 * n + 4] for n in range(NB)]
    l = pl.program_id(0)
    Lb = o_refs[0].shape[0]

    @pl.when(l == 0)
    def _():
        s_scr[...] = jnp.zeros_like(s_scr)

    for n in range(NB):
        g_scr[n] = jnp.log(seq[n][1][...])
    pm = _prefix_matrix(C, C)

    def run(factorised):
        def body(ci, carry):
            rows = pl.ds(pl.multiple_of(ci * C, C), C)
            args = [(seq[n][0][rows, :], seq[n][1][rows, :], g_scr[n, rows, :], seq[n][2][rows, :],
                     [[s_scr[n, h] for h in range(HG_HEADS)]]) for n in range(NB)]
            outs = [_hgrn_chunk(*a, pm, tmp_ref, 1, factorised) for a in args]
            for n, (o, new) in enumerate(outs):
                o_scr[n, rows, :] = o
                for h in range(HG_HEADS):
                    s_scr[n, h] = new[0][h]
            return carry
        lax.fori_loop(0, Lb // C, body, 0)

    safe = _decay_is_safe(g_scr[...].reshape(NB * Lb, -1), C // 2)
    pl.when(safe)(lambda: run(True))
    pl.when(jnp.logical_not(safe))(lambda: run(False))
    for n in range(NB):
        o_refs[n][...] = (_rms(o_scr[n]) * gain_ref[...] * seq[n][3][...]).astype(o_refs[n].dtype)

    @pl.when(l == pl.num_programs(0) - 1)
    def _():
        st_ref[...] = s_scr[...]


def _hgrn_prompt(P, gain, B, L, Lb):
    C = math.gcd(L, SCAN_CHUNK)
    nl = L // Lb
    W = HG_KEY_DIM
    blk = lambda b, seg: pl.BlockSpec((Lb, W), lambda l, b=b, seg=seg: (b * nl + l, seg))
    outs = pl.pallas_call(
        functools.partial(_hgrn_prompt_kernel, C=C, NB=B),
        grid=(nl,),
        in_specs=[blk(b, seg) for b in range(B) for seg in range(4)] + [pl.BlockSpec((1, W), lambda l: (0, 0))],
        out_specs=[pl.BlockSpec((Lb, W), lambda l: (l, 0)) for _ in range(B)]
        + [pl.BlockSpec((B, HG_HEADS, HG_EXPAND, HG_HEAD_V), lambda l: (0, 0, 0, 0))],
        out_shape=[jax.ShapeDtypeStruct((L, HG_VAL_DIM), BF16) for _ in range(B)]
        + [jax.ShapeDtypeStruct((B, HG_HEADS, HG_EXPAND, HG_HEAD_V), F32)],
        scratch_shapes=[pltpu.VMEM((B, HG_HEADS, HG_EXPAND, HG_HEAD_V), F32), pltpu.VMEM((B, Lb, W), F32),
                        pltpu.VMEM((B, Lb, W), F32), pltpu.VMEM((3, C, LANES), F32)],
        compiler_params=_params(("arbitrary",)),
        name="hgrn_prompt",
    )(*([P] * (4 * B)), gain)
    return outs[:B], outs[B]


def _hgrn_sample_kernel(q_ref, f_ref, v_ref, gate_ref, gain_ref, s_ref, o_ref, so_ref, o_scr, tmp_ref, *, G):
    NG = s_ref.shape[0]
    g = jnp.log(f_ref[...])
    pm = _prefix_matrix(NG * G, G)

    def run(factorised):
        states = [[s_ref[n, h] for h in range(HG_HEADS)] for n in range(NG)]
        o, new = _hgrn_chunk(q_ref[...], f_ref[...], g, v_ref[...], states, pm, tmp_ref, NG, factorised)
        o_scr[...] = o
        for n in range(NG):
            for h in range(HG_HEADS):
                so_ref[n, h] = new[n][h]

    safe = _decay_is_safe(g, G // 2)
    pl.when(safe)(lambda: run(True))
    pl.when(jnp.logical_not(safe))(lambda: run(False))
    _hgrn_epilogue(o_scr, gain_ref, gate_ref, o_ref)


def _hgrn_sample(P, gain, state, Bs, Ls, Bb, row0):
    W = HG_KEY_DIM
    rows = Bb * Ls
    blk0 = row0 // rows

    def col(seg):
        return pl.BlockSpec((rows, W), lambda i, seg=seg: (blk0 + i, seg))

    st = pl.BlockSpec((Bb, HG_HEADS, HG_EXPAND, HG_HEAD_V), lambda i: (i, 0, 0, 0))
    return pl.pallas_call(
        functools.partial(_hgrn_sample_kernel, G=Ls),
        grid=(Bs // Bb,),
        in_specs=[col(0), col(1), col(2), col(3), pl.BlockSpec((1, W), lambda i: (0, 0)), st],
        out_specs=[pl.BlockSpec((rows, W), lambda i: (i, 0)), st],
        out_shape=[
            jax.ShapeDtypeStruct((Bs * Ls, HG_VAL_DIM), BF16),
            jax.ShapeDtypeStruct((Bs, HG_HEADS, HG_EXPAND, HG_HEAD_V), F32),
        ],
        scratch_shapes=[pltpu.VMEM((rows, W), F32), pltpu.VMEM((3, Ls, LANES), F32)],
        compiler_params=_params(("parallel",)),
        name="hgrn_sample",
    )(P, P, P, P, gain, state)


def _rotary(x, cos, sa, sb):
    heads = []
    for h in range(x.shape[1] // LANES):
        xs = x[:, h * LANES:(h + 1) * LANES]
        heads.append(xs * cos + pltpu.roll(xs, LANES - 1, 1) * sa + pltpu.roll(xs, 1, 1) * sb)
    return jnp.concatenate(heads, axis=1)


def _ret_chunk(q, k, v, states, dm_ref, qw_ref, kw_ref, sd_ref):
    outs, new_states = [], []
    C = q.shape[0]
    qb, kb, vb = q.astype(BF16), k.astype(BF16), v.astype(BF16)
    for h in range(RET_HEADS):
        qs = slice(h * RET_HEAD_QK, (h + 1) * RET_HEAD_QK)
        vs = slice(h * RET_HEAD_V, (h + 1) * RET_HEAD_V)
        att = (_dot_nt(qb[:, qs], kb[:, qs]) * dm_ref[h]).astype(BF16)
        kw = (k[:, qs] * kw_ref[h]).astype(BF16)
        if C <= 2 * SUBLANES:
            both = _dot(jnp.concatenate([att, kw.T], axis=0), vb[:, vs])
            intra, upd = both[:C], both[C:]
        else:
            intra, upd = _dot(att, vb[:, vs]), _dot_tn(kw, vb[:, vs])
        o = intra + _dot((q[:, qs] * qw_ref[h]).astype(BF16), states[h].astype(BF16))
        new_states.append(sd_ref[h, 0:1, 0:1] * states[h] + upd)
        outs.append(o)
    return jnp.concatenate(outs, axis=1), new_states


def _ret_epilogue(o_scr, gate_ref, o_ref):
    for h in range(RET_HEADS):
        vs = slice(h * RET_HEAD_V, (h + 1) * RET_HEAD_V)
        o_ref[:, vs] = (_rms(o_scr[:, vs]) * gate_ref[:, vs]).astype(o_ref.dtype)


def _ret_prompt_kernel(*refs, NB):
    ins, (dm_ref, qw_ref, kw_ref, sd_ref), rest = refs[:4 * NB], refs[4 * NB:4 * NB + 4], refs[4 * NB + 4:]
    rope, rest = rest[:3], rest[3:]
    o_refs, (st_ref, s_scr, o_scr) = rest[:NB], rest[NB:]
    seq = [ins[4 * n:4 * n + 4] for n in range(NB)]
    l = pl.program_id(0)
    Lb = o_refs[0].shape[0]
    C = dm_ref.shape[1]

    @pl.when(l == 0)
    def _():
        s_scr[...] = jnp.zeros_like(s_scr)

    def body(ci, carry):
        rows = pl.ds(pl.multiple_of(ci * C, C), C)
        tabs = [t[rows, :] for t in rope]
        args = [(_rotary(seq[n][0][rows, :], *tabs), _rotary(seq[n][1][rows, :], *tabs), seq[n][2][rows, :],
                 [s_scr[n, h] for h in range(RET_HEADS)]) for n in range(NB)]
        outs = [_ret_chunk(*a, dm_ref, qw_ref, kw_ref, sd_ref) for a in args]
        for n, (o, new) in enumerate(outs):
            o_scr[n, rows, :] = o
            for h in range(RET_HEADS):
                s_scr[n, h] = new[h]
        return carry

    lax.fori_loop(0, Lb // C, body, 0)
    for n in range(NB):
        _ret_epilogue(o_scr.at[n], seq[n][3], o_refs[n])

    @pl.when(l == pl.num_programs(0) - 1)
    def _():
        st_ref[...] = s_scr[...]


def _ret_tables(C):
    log_gamma = jnp.log(1.0 - jnp.exp2(-5.0 - jnp.arange(RET_HEADS, dtype=F32)))
    idx = jnp.arange(C, dtype=F32)
    rel = idx[:, None] - idx[None, :]
    tri = jnp.tril(jnp.ones((C, C), dtype=bool))
    dmat = jnp.exp(jnp.where(tri[None], log_gamma[:, None, None] * rel[None], -jnp.inf))
    qw = jnp.exp(log_gamma[:, None] * (idx[None, :] + 1.0))[..., None]
    kw = jnp.exp(log_gamma[:, None] * (C - 1.0 - idx[None, :]))[..., None]
    sdec = jnp.exp(log_gamma * C)[:, None, None]
    bc = lambda a: jnp.broadcast_to(a, (RET_HEADS, a.shape[1], LANES))
    return dmat, bc(qw), bc(kw), jnp.broadcast_to(sdec, (RET_HEADS, SUBLANES, LANES))


_QK0 = (2 * HG_KEY_DIM + 2 * HG_VAL_DIM) // RET_QK_DIM
_V0 = (2 * HG_KEY_DIM + 2 * HG_VAL_DIM + 2 * RET_QK_DIM) // RET_V_DIM


def _full3(a):
    return pl.BlockSpec(a.shape, lambda *_: (0, 0, 0))


def _ret_prompt(P, rope, B, L, Lb):
    C = math.gcd(L, 2 * SCAN_CHUNK)
    Lb = max(Lb, C)
    nl = L // Lb
    tabs = _ret_tables(C)
    st_shape = (B, RET_HEADS, RET_HEAD_QK, RET_HEAD_V)
    blk = lambda b, w, c: pl.BlockSpec((Lb, w), lambda l, b=b, c=c: (b * nl + l, c))
    per_seq = lambda b: [blk(b, RET_QK_DIM, _QK0), blk(b, RET_QK_DIM, _QK0 + 1),
                         blk(b, RET_V_DIM, _V0), blk(b, RET_V_DIM, _V0 + 1)]
    outs = pl.pallas_call(
        functools.partial(_ret_prompt_kernel, NB=B),
        grid=(nl,),
        in_specs=[spec for b in range(B) for spec in per_seq(b)] + [_full3(t) for t in tabs]
        + [pl.BlockSpec((Lb, LANES), lambda l: (l, 0))] * 3,
        out_specs=[pl.BlockSpec((Lb, RET_V_DIM), lambda l: (l, 0)) for _ in range(B)]
        + [pl.BlockSpec(st_shape, lambda l: (0, 0, 0, 0))],
        out_shape=[jax.ShapeDtypeStruct((L, RET_V_DIM), BF16) for _ in range(B)]
        + [jax.ShapeDtypeStruct(st_shape, F32)],
        scratch_shapes=[pltpu.VMEM(st_shape, F32), pltpu.VMEM((B, Lb, RET_V_DIM), F32)],
        compiler_params=_params(("arbitrary",)),
        name="ret_prompt",
    )(*([P] * (4 * B)), *tabs, *rope)
    return outs[:B], outs[B]


def _ret_sample_kernel(q_ref, k_ref, v_ref, gate_ref, s_ref, dm_ref, qw_ref, kw_ref, sd_ref,
                       cos_ref, sa_ref, sb_ref, o_ref, so_ref, o_scr, *, C):
    Bb = s_ref.shape[0]
    tabs = (cos_ref[...], sa_ref[...], sb_ref[...])
    args = [(_rotary(q_ref[bb * C:(bb + 1) * C, :], *tabs), _rotary(k_ref[bb * C:(bb + 1) * C, :], *tabs),
             v_ref[bb * C:(bb + 1) * C, :], [s_ref[bb, h] for h in range(RET_HEADS)]) for bb in range(Bb)]
    outs = [_ret_chunk(*a, dm_ref, qw_ref, kw_ref, sd_ref) for a in args]
    for bb, (o, new) in enumerate(outs):
        o_scr[bb * C:(bb + 1) * C, :] = o
        for h in range(RET_HEADS):
            so_ref[bb, h] = new[h]
    _ret_epilogue(o_scr, gate_ref, o_ref)


def _ret_sample(P, rope, state, Bs, Ls, Bb, row0):
    tabs = _ret_tables(Ls)
    rows = Bb * Ls
    blk0 = row0 // rows
    st = pl.BlockSpec((Bb, RET_HEADS, RET_HEAD_QK, RET_HEAD_V), lambda i: (i, 0, 0, 0))
    return pl.pallas_call(
        functools.partial(_ret_sample_kernel, C=Ls),
        grid=(Bs // Bb,),
        in_specs=[
            pl.BlockSpec((rows, RET_QK_DIM), lambda i: (blk0 + i, _QK0)),
            pl.BlockSpec((rows, RET_QK_DIM), lambda i: (blk0 + i, _QK0 + 1)),
            pl.BlockSpec((rows, RET_V_DIM), lambda i: (blk0 + i, _V0)),
            pl.BlockSpec((rows, RET_V_DIM), lambda i: (blk0 + i, _V0 + 1)),
            st,
        ] + [_full3(t) for t in tabs] + [pl.BlockSpec((Ls, LANES), lambda i: (0, 0))] * 3,
        out_specs=[pl.BlockSpec((rows, RET_V_DIM), lambda i: (i, 0)), st],
        out_shape=[
            jax.ShapeDtypeStruct((Bs * Ls, RET_V_DIM), BF16),
            jax.ShapeDtypeStruct((Bs, RET_HEADS, RET_HEAD_QK, RET_HEAD_V), F32),
        ],
        scratch_shapes=[pltpu.VMEM((rows, RET_V_DIM), F32)],
        compiler_params=_params(("parallel",)),
        name="ret_sample",
    )(P, P, P, P, state, *tabs, *rope)


def _split_dot(a, w_hilo):
    a_hi = a.astype(BF16)
    a_lo = (a - a_hi.astype(F32)).astype(BF16)
    hi = _dot(a_hi, w_hilo)
    return hi[:, :LANES] + (hi[:, LANES:] + _dot(a_lo, w_hilo[:, :LANES]))


def _route(logits):
    lane = lax.broadcasted_iota(I32, logits.shape, 1)
    neg = jnp.float32(-jnp.inf)
    big = jnp.int32(LANES)
    gmask = lane < N_GROUPS
    gl = jnp.where(gmask, logits, neg)
    gmax = jnp.max(gl, axis=1, keepdims=True)
    gu = jnp.where(gmask, jnp.exp(gl - gmax), 0.0)
    gp = gu / jnp.sum(gu, axis=1, keepdims=True)
    g_w = jnp.max(gp, axis=1, keepdims=True)
    g_i = jnp.min(jnp.where(gmask & (gp == g_w), lane, big), axis=1, keepdims=True)
    lo = ROUTE_EXPERT_LANE + g_i * EXPERTS_PER_GROUP
    emask = (lane >= lo) & (lane < lo + EXPERTS_PER_GROUP)
    el = jnp.where(emask, logits, neg)
    e1 = jnp.max(el, axis=1, keepdims=True)
    i1 = jnp.min(jnp.where(emask & (el == e1), lane, big), axis=1, keepdims=True)
    el2 = jnp.where(lane == i1, neg, el)
    e2 = jnp.max(el2, axis=1, keepdims=True)
    i2 = jnp.min(jnp.where(emask & (lane != i1) & (el2 == e2), lane, big), axis=1, keepdims=True)
    u2 = jnp.exp(e2 - e1)
    den = 1.0 + u2
    w1 = (1.0 / den) * g_w
    w2 = (u2 / den) * g_w
    id1 = (i1 - ROUTE_EXPERT_LANE).astype(F32)
    id2 = (i2 - ROUTE_EXPERT_LANE).astype(F32)
    return (jnp.where(lane == LANE_E1, id1, 0.0) + jnp.where(lane == LANE_E2, id2, 0.0)
            + jnp.where(lane == LANE_W1, w1, 0.0) + jnp.where(lane == LANE_W2, w2, 0.0))


def _pick(i, bounds, refs):
    val = refs[-1][...]
    for n in range(len(refs) - 2, -1, -1):
        val = jnp.where(i < bounds[n + 1], refs[n][...], val)
    return val


def _mix_kernel(*refs, bounds):
    ns = len(bounds) - 1
    ohg_refs, or_refs, refs = refs[:ns], refs[ns:2 * ns], refs[2 * ns:]
    (ga0_ref, ga1_ref, gb0_ref, gb1_ref, xp_ref, xs_ref, wbh_ref, wbr_ref, wout_ref, nffn_ref,
     wr_ref, br_ref, h_ref, hn_ref, route_ref) = refs
    i = pl.program_id(0)
    tm = h_ref.shape[0]
    is_prompt = i < bounds[-2]
    p_hg = _dot(_pick(i, bounds, ohg_refs), wbh_ref[...])
    p_r = _dot(_pick(i, bounds, or_refs), wbr_ref[...])
    half = D_MODEL // 2
    m0 = ga0_ref[...] * p_hg[:, :half] + gb0_ref[...] * p_r[:, :half]
    m1 = ga1_ref[...] * p_hg[:, half:] + gb1_ref[...] * p_r[:, half:]
    mixed = jnp.concatenate([m0, m1], axis=1).astype(BF16)
    x = jnp.where(is_prompt, xp_ref[...], xs_ref[...])
    h = x + _dot(mixed, wout_ref[...])
    h_ref[...] = h
    hn = _rms(h) * nffn_ref[...]
    hn_ref[...] = _to_token_tiles(hn.astype(BF16))
    route_ref[...] = _route(_split_dot(hn, wr_ref[...]) + br_ref[...])


def _mix(o_hg, o_r, P, xp, xs, wbh, wbr, wout, norm_ffn, wr, br, tm):
    npt, nst = xp.shape[0] // tm, xs.shape[0] // tm
    bounds = [0]
    for a in o_hg:
        bounds.append(bounds[-1] + a.shape[0] // tm)
    T = (npt + nst) * tm
    half = D_MODEL // 2
    ga = (IN_WIDTH - 2 * D_MODEL) // half

    def pcol(c):
        return pl.BlockSpec((tm, half), lambda i, c=c: (i, c))

    def const(a):
        return pl.BlockSpec(a.shape, lambda i: (0, 0), pipeline_mode=pl.Buffered(1))

    def prompt(w):
        return pl.BlockSpec((tm, w), lambda i: (jnp.minimum(i, npt - 1), 0))

    def sample(w):
        return pl.BlockSpec((tm, w), lambda i: (jnp.maximum(i - npt, 0), 0))

    def source(n):
        lo, hi = bounds[n], bounds[n + 1]
        return pl.BlockSpec((tm, half), lambda i: (jnp.clip(i - lo, 0, hi - lo - 1), 0))

    sources = [source(n) for n in range(len(o_hg))]
    return pl.pallas_call(
        functools.partial(_mix_kernel, bounds=tuple(bounds)),
        grid=(npt + nst,),
        in_specs=sources + sources + [
            pcol(ga), pcol(ga + 1), pcol(ga + 2), pcol(ga + 3),
            prompt(D_MODEL), sample(D_MODEL),
            const(wbh), const(wbr), const(wout), const(norm_ffn), const(wr), const(br),
        ],
        out_specs=[
            pl.BlockSpec((tm, D_MODEL), lambda i: (i, 0)),
            pl.BlockSpec((tm * ROW_GROUP, LANES), lambda i: (i, 0)),
            pl.BlockSpec((tm, LANES), lambda i: (i, 0)),
        ],
        out_shape=[
            jax.ShapeDtypeStruct((T, D_MODEL), F32),
            jax.ShapeDtypeStruct((T * ROW_GROUP, LANES), BF16),
            jax.ShapeDtypeStruct((T, LANES), F32),
        ],
        compiler_params=_params(("parallel",)),
        name="mix",
    )(*o_hg, *o_r, P, P, P, P, xp, xs, wbh, wbr, wout, norm_ffn, wr, br)


def _plan_kernel(route_ref, pos_ref, te_ref, tail_ref, cum_scr, *, tb):
    T = route_ref.shape[0]
    lane_t = lax.broadcasted_iota(I32, (tb, LANES), 1).astype(F32)
    r = lax.broadcasted_iota(I32, (tb, tb), 0)
    c = lax.broadcasted_iota(I32, (tb, tb), 1)
    strict_lower = jnp.where(r > c, 1.0, 0.0).astype(BF16)

    def onehots(rows):
        rt = route_ref[rows, :]
        a1 = jnp.where(lane_t == rt[:, LANE_E1:LANE_E1 + 1], 1.0, 0.0)
        a2 = jnp.where(lane_t == rt[:, LANE_E2:LANE_E2 + 1], 1.0, 0.0)
        return a1, a2

    def rank_body(bi, carry):
        rows = pl.ds(pl.multiple_of(bi * tb, tb), tb)
        a1, a2 = onehots(rows)
        m = a1 + a2
        cum_scr[rows, :] = _dot(strict_lower, m.astype(BF16)) + carry
        return carry + jnp.sum(m, axis=0, keepdims=True)

    counts = lax.fori_loop(0, T // tb, rank_body, jnp.zeros((1, LANES), F32))
    ntiles = jnp.zeros_like(counts)
    for kt in range(-(-T // MOE_TILE)):
        ntiles = ntiles + jnp.where(counts > float(kt * MOE_TILE), 1.0, 0.0)
    rr = lax.broadcasted_iota(I32, (LANES, LANES), 0)
    cc = lax.broadcasted_iota(I32, (LANES, LANES), 1)
    upper = jnp.where(rr < cc, 1.0, 0.0).astype(BF16)
    tile0 = _dot(jnp.broadcast_to(ntiles, (SUBLANES, LANES)).astype(BF16), upper)[0:1, :]
    slot0 = tile0 * MOE_TILE

    def pos_body(bi, carry):
        rows = pl.ds(pl.multiple_of(bi * tb, tb), tb)
        a1, a2 = onehots(rows)
        base = cum_scr[rows, :] + slot0
        p1 = jnp.sum(a1 * base, axis=1, keepdims=True)
        p2 = jnp.sum(a2 * base, axis=1, keepdims=True)
        pos_ref[rows, :] = (jnp.where(lane_t == 0.0, p1, 0.0) + jnp.where(lane_t == 1.0, p2, 0.0)).astype(I32)
        return carry

    lax.fori_loop(0, T // tb, pos_body, 0)
    tend = tile0 + ntiles
    lane = lax.broadcasted_iota(I32, (LANES, LANES), 1)
    tile_i = lax.broadcasted_iota(I32, (LANES, LANES), 0).astype(F32)
    is_expert = lane < N_EXPERTS
    te = jnp.sum(jnp.where(is_expert & (jnp.broadcast_to(tend, (LANES, LANES)) <= tile_i), 1.0, 0.0),
                 axis=1, keepdims=True)
    te_ref[...] = jnp.broadcast_to(jnp.minimum(te, N_EXPERTS - 1.0), (LANES, LANES)).astype(I32)
    lane1 = lax.broadcasted_iota(I32, (1, LANES), 1)
    total = jnp.sum(jnp.where(lane1 < N_EXPERTS, ntiles, 0.0), axis=1, keepdims=True)
    tail = jnp.where(ntiles > 0.0, (tend - 1.0) * MOE_TILE, -1.0)
    tail = jnp.where(lane1 < N_EXPERTS, tail, jnp.where(lane1 == N_EXPERTS, total, 0.0))
    tail_ref[...] = jnp.broadcast_to(tail, (SUBLANES, LANES)).astype(I32)


def _plan(route, tb):
    T = route.shape[0]
    return pl.pallas_call(
        functools.partial(_plan_kernel, tb=tb),
        grid=(1,),
        in_specs=[pl.BlockSpec((T, LANES), lambda i: (0, 0))],
        out_specs=[
            pl.BlockSpec((T, LANES), lambda i: (0, 0)),
            pl.BlockSpec((LANES, LANES), lambda i: (0, 0)),
            pl.BlockSpec((SUBLANES, LANES), lambda i: (0, 0)),
        ],
        out_shape=[
            jax.ShapeDtypeStruct((T, LANES), I32),
            jax.ShapeDtypeStruct((LANES, LANES), I32),
            jax.ShapeDtypeStruct((SUBLANES, LANES), I32),
        ],
        scratch_shapes=[pltpu.VMEM((T, LANES), F32)],
        compiler_params=_params(("arbitrary",)),
        name="moe_plan",
    )(route)


def _row_group(ref, idx):
    return ref.at[pl.ds(pl.multiple_of(idx * ROW_GROUP, ROW_GROUP), ROW_GROUP), :]


def _dispatch_kernel(pos_ref, tail_ref, hn_ref, xs_ref, zero_scr, zsem, sem):
    i = pl.program_id(0)
    tm = hn_ref.shape[0] // ROW_GROUP
    tile_rows = MOE_TILE * ROW_GROUP

    @pl.when(i == 0)
    def _():
        zero_scr[...] = jnp.zeros_like(zero_scr)
        n_used = tail_ref[N_EXPERTS]

        def zero_copy(first_slot):
            start = pl.multiple_of(first_slot * ROW_GROUP, tile_rows)
            return pltpu.make_async_copy(zero_scr, xs_ref.at[pl.ds(start, tile_rows), :], zsem)

        def tails(fn):
            def body(e, carry):
                pl.when(tail_ref[e] >= 0)(lambda: fn(zero_copy(jnp.maximum(tail_ref[e], 0))))
                return carry
            lax.fori_loop(0, N_EXPERTS, body, 0)

        def unused(fn):
            def body(t, carry):
                fn(zero_copy(t * MOE_TILE))
                return carry
            lax.fori_loop(n_used, xs_ref.shape[0] // tile_rows, body, 0)

        tails(lambda c: c.start())
        unused(lambda c: c.start())
        tails(lambda c: c.wait())
        unused(lambda c: c.wait())

    def copy(r, k):
        slot = pos_ref[(i * tm + r) * TOP_K + k]
        return pltpu.make_async_copy(_row_group(hn_ref, r), _row_group(xs_ref, slot), sem)

    def start(r, carry):
        for k in range(TOP_K):
            copy(r, k).start(priority=k)
        return carry

    lax.fori_loop(0, tm, start, 0, unroll=8)
    for _ in range(TOP_K):
        pltpu.make_async_copy(hn_ref, xs_ref.at[pl.ds(0, tm * ROW_GROUP), :], sem).wait()


def _dispatch(pos, tails, hn_rows, n_slots, tm):
    T = hn_rows.shape[0] // ROW_GROUP
    return pl.pallas_call(
        _dispatch_kernel,
        grid_spec=pltpu.PrefetchScalarGridSpec(
            num_scalar_prefetch=2,
            grid=(T // tm,),
            in_specs=[pl.BlockSpec((tm * ROW_GROUP, LANES), lambda i, *_: (i, 0))],
            out_specs=pl.BlockSpec(memory_space=pl.ANY),
            scratch_shapes=[pltpu.VMEM((MOE_TILE * ROW_GROUP, LANES), BF16),
                            pltpu.SemaphoreType.DMA, pltpu.SemaphoreType.DMA],
        ),
        out_shape=jax.ShapeDtypeStruct((n_slots * ROW_GROUP, LANES), BF16),
        compiler_params=_params(("arbitrary",)),
        name="moe_dispatch",
    )(pos, tails, hn_rows)


def _experts_kernel(te_ref, nt_ref, xs_ref, wg_hbm, wu_hbm, wd_hbm, ys_ref,
                    wg_buf, wu_buf, wd_buf, rank_ref, sem):
    i = pl.program_id(0)
    nt = nt_ref[0]
    e = te_ref[i]

    def fetch(expert, slot):
        return [pltpu.make_async_copy(hbm.at[expert], buf.at[slot], sem.at[slot, n])
                for n, (hbm, buf) in enumerate(((wg_hbm, wg_buf), (wu_hbm, wu_buf), (wd_hbm, wd_buf)))]

    def start_fetch(tile, slot):
        for c in fetch(te_ref[jnp.minimum(tile, nt - 1)], slot):
            c.start(priority=1)

    def next_change(tile):
        ex = te_ref[jnp.minimum(tile, nt - 1)]
        return lax.while_loop(lambda j: (j < nt) & (te_ref[jnp.minimum(j, nt - 1)] == ex), lambda j: j + 1, tile + 1)

    @pl.when(i == 0)
    def _():
        rank_ref[0] = -1
        start_fetch(0, 0)
        n1 = next_change(0)
        pl.when(n1 < nt)(lambda: start_fetch(n1, 1))

    @pl.when((i < nt) & ((i == 0) | (te_ref[jnp.maximum(i - 1, 0)] != e)))
    def _():
        rank = rank_ref[0] + 1
        rank_ref[0] = rank
        for c in fetch(e, rank % WEIGHT_SLOTS):
            c.wait()
        n1 = next_change(i)

        @pl.when(n1 < nt)
        def _():
            n2 = next_change(n1)
            pl.when(n2 < nt)(lambda: start_fetch(n2, (rank + 2) % WEIGHT_SLOTS))

    @pl.when(i < nt)
    def _():
        slot = rank_ref[0] % WEIGHT_SLOTS
        x = _from_token_tiles(xs_ref[...])[0]
        hmid = (_silu(_dot(x, wg_buf[slot].astype(BF16))) * _dot(x, wu_buf[slot].astype(BF16))).astype(BF16)
        ys_ref[...] = _to_token_tiles(_dot(hmid, wd_buf[slot].astype(BF16)))

    @pl.when(i >= nt)
    def _():
        ys_ref[...] = jnp.zeros_like(ys_ref)


def _experts(te, nt, xs_rows, w_ge, w_ue, w_de, n_tiles):
    M = MOE_TILE
    rows = pl.BlockSpec((M * ROW_GROUP, LANES), lambda i, te_ref, nt_ref: (jnp.minimum(i, nt_ref[0] - 1), 0))
    hbm = pl.BlockSpec(memory_space=pl.ANY)
    up, down = (D_MODEL, EXPERT_FF), (EXPERT_FF, D_MODEL)
    return pl.pallas_call(
        _experts_kernel,
        grid_spec=pltpu.PrefetchScalarGridSpec(
            num_scalar_prefetch=2,
            grid=(n_tiles,),
            in_specs=[rows, hbm, hbm, hbm],
            out_specs=pl.BlockSpec((M * ROW_GROUP, LANES), lambda i, te_ref, nt_ref: (i, 0)),
            scratch_shapes=[pltpu.VMEM((WEIGHT_SLOTS,) + up, F32), pltpu.VMEM((WEIGHT_SLOTS,) + up, F32),
                            pltpu.VMEM((WEIGHT_SLOTS,) + down, F32),
                            pltpu.SMEM((1,), I32), pltpu.SemaphoreType.DMA((WEIGHT_SLOTS, 3))],
        ),
        out_shape=jax.ShapeDtypeStruct(xs_rows.shape, F32),
        compiler_params=_params(("arbitrary",)),
        name="moe_experts",
    )(te, nt, xs_rows, w_ge, w_ue, w_de)


def _final_kernel(pos_ref, h_ref, route_ref, gain_ref, ys_ref, y_ref, buf, sem, *, tile0):
    i = pl.program_id(0)
    n = pl.num_programs(0)
    tm = h_ref.shape[0]

    def copy(step, r, k, slot_buf):
        slot = pos_ref[((tile0 + step) * tm + r) * TOP_K + k]
        return pltpu.make_async_copy(_row_group(ys_ref, slot), _row_group(buf.at[slot_buf], r * TOP_K + k),
                                     sem.at[slot_buf])

    def start_all(step, slot_buf):
        def body(r, carry):
            for k in range(TOP_K):
                copy(step, r, k, slot_buf).start(priority=k)
            return carry
        lax.fori_loop(0, tm, body, 0, unroll=8)

    cur = i % 2
    pl.when(i == 0)(lambda: start_all(0, 0))
    pl.when(i + 1 < n)(lambda: start_all(i + 1, 1 - cur))
    pltpu.make_async_copy(ys_ref.at[pl.ds(0, buf.shape[1]), :], buf.at[cur], sem.at[cur]).wait()

    rt = route_ref[...]
    w1 = rt[:, LANE_W1:LANE_W1 + 1]
    w2 = rt[:, LANE_W2:LANE_W2 + 1]
    o1, o2 = _from_token_tiles(buf[cur], groups=TOP_K)
    v = h_ref[...] + (w1 * o1 + w2 * o2)
    y_ref[...] = _rms(v) * gain_ref[...]


def _final(pos, h, route, gain, ys_rows, tile0, n_tiles, tm):
    return pl.pallas_call(
        functools.partial(_final_kernel, tile0=tile0),
        grid_spec=pltpu.PrefetchScalarGridSpec(
            num_scalar_prefetch=1,
            grid=(n_tiles,),
            in_specs=[
                pl.BlockSpec((tm, D_MODEL), lambda i, *_: (tile0 + i, 0)),
                pl.BlockSpec((tm, LANES), lambda i, *_: (tile0 + i, 0)),
                pl.BlockSpec((1, D_MODEL), lambda i, *_: (0, 0)),
                pl.BlockSpec(memory_space=pl.ANY),
            ],
            out_specs=pl.BlockSpec((tm, D_MODEL), lambda i, *_: (i, 0)),
            scratch_shapes=[pltpu.VMEM((2, tm * TOP_K * ROW_GROUP, LANES), F32), pltpu.SemaphoreType.DMA((2,))],
        ),
        out_shape=jax.ShapeDtypeStruct((n_tiles * tm, D_MODEL), F32),
        compiler_params=_params(("arbitrary",)),
        name="final",
    )(pos, h, route, gain, ys_rows)


def _rope_tables(pos):
    half = RET_HEAD_QK // 2
    inv = 1.0 / (ROPE_BASE ** jnp.linspace(0.0, 1.0, half, dtype=F32))
    ang = jnp.repeat(pos[:, None] * inv[None, :], 2, axis=-1)
    cos, sin = jnp.cos(ang), jnp.sin(ang)
    even = (jnp.arange(RET_HEAD_QK) % 2 == 0)[None, :]
    sa = jnp.where(even, -sin, 0.0)
    sb = jnp.where(even, 0.0, sin)
    return cos, sa, sb


def _tile_sizes(Tp, Ts, Lp):
    g = math.gcd(Tp, Ts)
    return dict(
        xnorm=math.gcd(g, 512),
        proj=(Tp + Ts) // 3,
        mix=math.gcd(g, 256),
        plan=math.gcd(Tp + Ts, 512),
        dispatch=math.gcd(g, 1024),
        final=math.gcd(g, 256),
        scan=math.gcd(Lp, 128),
    )


def kernel(x_prompt, x_sample, state_hgrn, state_ret, norm_mix, w_in, hg_lb_logits, hg_norm,
           w_branch_hg, w_branch_ret, b_gate, w_out, norm_ffn, w_router_group, b_router_group,
           w_router_expert, b_router_expert, w_expert_gate, w_expert_up, w_expert_down, norm_final):
    assert w_in.shape == (1, D_MODEL, IN_WIDTH), "single-layer trunk only"
    Bp, Lp, _ = x_prompt.shape
    Bs, Ls, _ = x_sample.shape
    Tp, Ts = Bp * Lp, Bs * Ls
    T = Tp + Ts
    ts = _tile_sizes(Tp, Ts, Lp)
    xp = x_prompt.reshape(Tp, D_MODEL)
    xs = x_sample.reshape(Ts, D_MODEL)

    lower = jnp.cumsum(jax.nn.softmax(hg_lb_logits.astype(F32), axis=0), axis=0)[0]
    act = _act_coefficients(lower, b_gate[0])
    rope_p = _rope_tables(jnp.arange(Lp, dtype=F32))
    rope_s = _rope_tables(jnp.arange(Ls, dtype=F32) + jnp.float32(PAST_LEN))
    xn = _xnorm(xp, xs, norm_mix, ts["xnorm"])
    P = _in_proj(xn, w_in[0], act, ts["proj"])

    bb = math.gcd(Bs, 16)
    o_hg_p, shp = _hgrn_prompt(P, hg_norm, Bp, Lp, ts["scan"])
    o_hg_s, shs = _hgrn_sample(P, hg_norm, state_hgrn[0].astype(F32), Bs, Ls, bb, Tp)
    o_r_p, srp = _ret_prompt(P, rope_p, Bp, Lp, ts["scan"])
    o_r_s, srs = _ret_sample(P, rope_s, state_ret[0].astype(F32), Bs, Ls, bb, Tp)

    wbh = w_branch_hg[0].astype(BF16)
    wbr = w_branch_ret[0].astype(BF16)
    wout = w_out[0].astype(BF16)
    pad = LANES - N_GROUPS - N_EXPERTS
    wr = jnp.concatenate([w_router_group[0], w_router_expert[0], jnp.zeros((D_MODEL, pad), F32)], axis=1)
    wr_hi = wr.astype(BF16)
    wr_hilo = jnp.concatenate([wr_hi, (wr - wr_hi.astype(F32)).astype(BF16)], axis=1)
    br = jnp.concatenate([b_router_group[0], b_router_expert[0], jnp.zeros((pad,), F32)])[None, :]
    h, hn_rows, route = _mix(list(o_hg_p) + [o_hg_s], list(o_r_p) + [o_r_s], P, xp, xs, wbh, wbr, wout,
                             norm_ffn, wr_hilo, br, ts["mix"])

    n_tiles = -(-T * TOP_K // MOE_TILE) + N_EXPERTS
    assert n_tiles <= LANES
    pos_slab, te_slab, tail_slab = _plan(route, ts["plan"])
    pos = pos_slab[:, :TOP_K].reshape(-1)
    te = te_slab[:n_tiles, 0]
    tails = tail_slab[0, :N_EXPERTS + 1]
    nt = tail_slab[0, N_EXPERTS:N_EXPERTS + 1]
    xs_rows = _dispatch(pos, tails, hn_rows, n_tiles * MOE_TILE, ts["dispatch"])
    ys_rows = _experts(te, nt, xs_rows, w_expert_gate[0], w_expert_up[0], w_expert_down[0], n_tiles)

    tf = ts["final"]
    gain = norm_final[None, :]
    y_p = _final(pos, h, route, gain, ys_rows, 0, Tp // tf, tf)
    y_s = _final(pos, h, route, gain, ys_rows, Tp // tf, Ts // tf, tf)
    return (y_p.reshape(Bp, Lp, D_MODEL), y_s.reshape(Bs, Ls, D_MODEL),
            shp[None], srp[None], shs[None], srs[None])
```
